```python
import jax
import jax.numpy as jnp
from jax import lax
import numpy as np


D_MODEL = 1024
BATCH = 8
SEQ = 2048
DEPTH = 4

HEAD_DIM = 64
N_HEADS_A = D_MODEL // HEAD_DIM // 2
N_GROUPS_B = D_MODEL // HEAD_DIM // 2
WIDTH_A = N_HEADS_A * HEAD_DIM
WIDTH_B = N_GROUPS_B * HEAD_DIM
MIX_WIDTH = WIDTH_A + WIDTH_B
MIX_IN_WIDTH = 3 * WIDTH_A + 2 * WIDTH_B
DILATED_PATTERNS = ((128, 1), (512, 4), (2048, 16))
SGU_CHUNK = 128
RWKV_HEADS = D_MODEL // HEAD_DIM
D_DECAY_LORA = 64
D_AAA_LORA = 64
D_MV_LORA = 32
D_GATE_LORA = 160
RWKV_GN_EPS = 64e-5
N_EXPERTS = 16
N_EXPERT_GROUPS = 4
EXPERTS_PER_GROUP = N_EXPERTS // N_EXPERT_GROUPS
TOP_K = 2
D_EXPERT = 256
N_EVEN = (DEPTH + 1) // 2
N_ODD = DEPTH // 2
RMS_EPS = 1e-6

kernel_name = 'hybrid_dilated_sgu_rwkv7_grouped_moe'


def _rmsnorm(x):
    xf = x.astype(jnp.float32)
    return (xf * lax.rsqrt(jnp.mean(xf * xf, axis=-1, keepdims=True) + RMS_EPS)).astype(x.dtype)


def _ada(c_act, w, b):
    shift, scale, gate = jnp.split(c_act @ w + b, 3, axis=-1)
    return shift[:, None], scale[:, None], gate[:, None]


def _from_sub(t, S, L):
    B, R = t.shape[0], t.shape[1]
    rest = t.shape[4:]
    t = t.reshape((B, R, -1) + rest)[:, :, :L]
    return jnp.swapaxes(t, 1, 2).reshape((B, L * R) + rest)[:, :S]


def _dilated_branch(q, k, v, dilation, span):
    B, S, H, Dh = q.shape
    L = -(-S // dilation)
    n_blk = -(-L // span)
    Lp = n_blk * span

    def to_sub(t):
        t = jnp.pad(t, ((0, 0), (0, L * dilation - S), (0, 0), (0, 0)))
        t = jnp.swapaxes(t.reshape(B, L, dilation, H, Dh), 1, 2)
        t = jnp.pad(t, ((0, 0), (0, 0), (0, Lp - L), (0, 0), (0, 0)))
        return t.reshape(B, dilation, n_blk, span, H, Dh)

    def with_prev(t):
        prev = jnp.pad(t, ((0, 0), (0, 0), (1, 0), (0, 0), (0, 0), (0, 0)))[:, :, :-1]
        return jnp.concatenate([prev, t], axis=3)

    qb = to_sub(q)
    kw = with_prev(to_sub(k))
    vw = with_prev(to_sub(v))
    s = jnp.einsum('brnqhd,brnkhd->brnhqk', qb, kw).astype(jnp.float32) * (Dh ** -0.5)
    q_pos = jnp.arange(span)[:, None] + span
    k_pos = jnp.arange(2 * span)[None, :]
    dist = q_pos - k_pos
    band = (dist >= 0) & (dist <= span)
    no_prev = (jnp.arange(n_blk)[:, None, None] == 0) & (k_pos[None] < span)
    valid = band[None] & ~no_prev
    s = jnp.where(valid[:, None], s, -jnp.inf)
    m = jnp.max(s, axis=-1, keepdims=True)
    p = jnp.exp(s - m)
    den = jnp.sum(p, axis=-1, keepdims=True)
    o = jnp.einsum('brnhqk,brnkhd->brnhqd', p, vw.astype(jnp.float32)) / den
    lse = (m + jnp.log(den))[..., 0]
    o = jnp.swapaxes(o, 3, 4)
    lse = jnp.swapaxes(lse, 3, 4)
    return _from_sub(o, S, L), _from_sub(lse, S, L)


def _dilated_attention(q, k, v):
    outs, lses = [], []
    for window, dilation in DILATED_PATTERNS:
        o, lse = _dilated_branch(q, k, v, dilation, window // dilation)
        outs.append(o)
        lses.append(lse)
    wts = jax.nn.softmax(jnp.stack(lses), axis=0)
    o = jnp.sum(wts[..., None] * jnp.stack(outs), axis=0)
    return o.astype(q.dtype)


def _spatial_gating(u, z, norm_g, w_s, b_s):
    B, S, _ = u.shape
    u = jax.nn.gelu(u, approximate=False)
    z = _rmsnorm(jax.nn.gelu(z, approximate=False)) * norm_g
    zc = z.reshape(B, S // SGU_CHUNK, SGU_CHUNK, N_GROUPS_B, HEAD_DIM)
    causal = jnp.tril(jnp.ones((SGU_CHUNK, SGU_CHUNK), dtype=bool))
    w = jnp.where(causal, w_s, 0)
    mixed = jnp.einsum('gij,bcjgd->bcigd', w, zc) + b_s.T[:, :, None]
    return u * mixed.reshape(B, S, WIDTH_B)


def _even_mixer(h, w_in, w_out, sgu_norm_g, sgu_w, sgu_b):
    B, S, _ = h.shape
    proj = h @ w_in
    q, k, v, u, z = jnp.split(proj, [WIDTH_A, 2 * WIDTH_A, 3 * WIDTH_A, 3 * WIDTH_A + WIDTH_B], axis=-1)
    heads = lambda t: t.reshape(B, S, N_HEADS_A, HEAD_DIM)
    o_a = _dilated_attention(heads(q), heads(k), heads(v)).reshape(B, S, WIDTH_A)
    o_b = _spatial_gating(u, z, sgu_norm_g, sgu_w, sgu_b)
    return jnp.concatenate([o_a, o_b], axis=-1) @ w_out


def _wkv7(r, w_log, k, v, a, b):
    B, T, H, N = r.shape
    f32 = jnp.float32
    decay = jnp.exp(-jnp.exp(w_log.astype(f32)))
    seq = tuple(jnp.moveaxis(t.astype(f32), 1, 0) for t in (r, decay, k, v, a, b))

    def step(state, inp):
        r_t, w_t, k_t, v_t, a_t, b_t = inp
        sa = jnp.einsum('bhvk,bhk->bhv', state, a_t)
        state = (state * w_t[:, :, None, :] + sa[..., None] * b_t[:, :, None, :]
                 + v_t[..., None] * k_t[:, :, None, :])
        return state, jnp.einsum('bhvk,bhk->bhv', state, r_t)

    _, y = lax.scan(step, jnp.zeros((B, H, N, N), f32), seq)
    return jnp.moveaxis(y, 0, 1)


def _rwkv7_time_mix(h, v_first, vres, mu, w_rkv, w_o, w0, w1, w2, a0, a1, a2,
                    g1, g2, k_k, k_a, r_k, ln_w, ln_b):
    B, T, C = h.shape
    H, N = RWKV_HEADS, HEAD_DIM
    xx = jnp.pad(h, ((0, 0), (1, 0), (0, 0)))[:, :-1] - h
    xr, xw, xk, xv, xa, xg = [h + xx * mu[i] for i in range(6)]
    r = xr @ w_rkv[0]
    k = xk @ w_rkv[1]
    v = xv @ w_rkv[2]
    w = -jax.nn.softplus(-(w0 + jnp.tanh(xw @ w1) @ w2)) - 0.5
    if vres is None:
        v_first = v
    else:
        v0, v1, v2 = vres
        v = v + (v_first - v) * jax.nn.sigmoid(v0 + (xv @ v1) @ v2)
    a = jax.nn.sigmoid(a0 + (xa @ a1) @ a2)
    g = jax.nn.sigmoid(xg @ g1) @ g2
    heads = lambda t: t.reshape(B, T, H, N)
    kk = heads(k * k_k).astype(jnp.float32)
    kk = kk / jnp.maximum(jnp.linalg.norm(kk, axis=-1, keepdims=True), 1e-12)
    k = k * (1 + (a - 1) * k_a)
    rh, kh, vh, ah = heads(r), heads(k), heads(v), heads(a).astype(jnp.float32)
    y = _wkv7(rh, heads(w), kh, vh, -kk, kk * ah)
    mean = jnp.mean(y, axis=-1, keepdims=True)
    var = jnp.mean(jnp.square(y - mean), axis=-1, keepdims=True)
    y = ((y - mean) * lax.rsqrt(var + RWKV_GN_EPS)).reshape(B, T, C).astype(h.dtype)
    y = y * ln_w + ln_b
    y = y + (jnp.sum(rh * kh * r_k, axis=-1, keepdims=True) * vh).reshape(B, T, C)
    return (y * g) @ w_o, v_first


def _moe(h, router_w, router_b, w1, w3, w2):
    B, S, D = h.shape
    t = h.reshape(-1, D)
    n_tok = t.shape[0]
    s = jax.nn.sigmoid(t.astype(jnp.float32) @ router_w.astype(jnp.float32))
    sel = (s + router_b.astype(jnp.float32)).reshape(n_tok, N_EXPERT_GROUPS, EXPERTS_PER_GROUP)
    group_score = jnp.sum(lax.top_k(sel, TOP_K)[0], axis=-1)
    g_idx = jnp.argmax(group_score, axis=-1)
    in_group = sel[jnp.arange(n_tok), g_idx]
    _, local = lax.top_k(in_group, TOP_K)
    e_idx = g_idx[:, None] * EXPERTS_PER_GROUP + local
    wts = jnp.take_along_axis(s, e_idx, axis=1)
    wts = wts / jnp.sum(wts, axis=-1, keepdims=True)
    gates = jnp.sum(jax.nn.one_hot(e_idx, N_EXPERTS, dtype=jnp.float32) * wts[..., None], axis=1)
    gates = gates.astype(t.dtype)
    y = jnp.zeros_like(t)
    for e in range(N_EXPERTS):
        hid = jax.nn.silu(t @ w1[e]) * (t @ w3[e])
        y = y + gates[:, e:e + 1] * (hid @ w2[e])
    return y.reshape(B, S, D)


def setup_inputs(seed: int = 0) -> dict:
    key = jax.random.key(seed)
    ks = iter(jax.random.split(key, 48))
    nrm = lambda shape, scale: jax.random.normal(next(ks), shape, jnp.float32) * scale
    D = D_MODEL
    inp = {}
    inp['x'] = nrm((BATCH, SEQ, D), 1.0)
    inp['c'] = nrm((BATCH, D), 1.0)
    inp['ada_w'] = nrm((DEPTH, 2, D, 3 * D), 0.5 * D ** -0.5)
    inp['ada_b'] = nrm((DEPTH, 2, 3 * D), 0.02)
    inp['mix_w_in'] = nrm((N_EVEN, D, MIX_IN_WIDTH), D ** -0.5)
    inp['mix_w_out'] = nrm((N_EVEN, MIX_WIDTH, D), MIX_WIDTH ** -0.5)
    inp['sgu_norm_g'] = 1.0 + nrm((N_EVEN, WIDTH_B), 0.05)
    inp['sgu_w'] = nrm((N_EVEN, N_GROUPS_B, SGU_CHUNK, SGU_CHUNK), SGU_CHUNK ** -0.5)
    inp['sgu_b'] = 1.0 + nrm((N_EVEN, N_GROUPS_B, SGU_CHUNK), 0.1)
    inp['rwkv_mu'] = jax.random.uniform(next(ks), (N_ODD, 6, D), jnp.float32)
    inp['rwkv_w_rkv'] = nrm((N_ODD, 3, D, D), D ** -0.5)
    inp['rwkv_w_o'] = nrm((N_ODD, D, D), D ** -0.5)
    inp['rwkv_w0'] = jax.random.uniform(next(ks), (N_ODD, D), jnp.float32, -6.5, -1.5)
    inp['rwkv_w1'] = nrm((N_ODD, D, D_DECAY_LORA), D ** -0.5)
    inp['rwkv_w2'] = nrm((N_ODD, D_DECAY_LORA, D), 0.1 * D_DECAY_LORA ** -0.5)
    inp['rwkv_a0'] = nrm((N_ODD, D), 0.1)
    inp['rwkv_a1'] = nrm((N_ODD, D, D_AAA_LORA), D ** -0.5)
    inp['rwkv_a2'] = nrm((N_ODD, D_AAA_LORA, D), D_AAA_LORA ** -0.5)
    inp['rwkv_v0'] = 1.0 + nrm((N_ODD - 1, D), 0.1)
    inp['rwkv_v1'] = nrm((N_ODD - 1, D, D_MV_LORA), D ** -0.5)
    inp['rwkv_v2'] = nrm((N_ODD - 1, D_MV_LORA, D), D_MV_LORA ** -0.5)
    inp['rwkv_g1'] = nrm((N_ODD, D, D_GATE_LORA), D ** -0.5)
    inp['rwkv_g2'] = nrm((N_ODD, D_GATE_LORA, D), D_GATE_LORA ** -0.5)
    inp['rwkv_k_k'] = 0.85 + nrm((N_ODD, D), 0.05)
    inp['rwkv_k_a'] = 1.0 + nrm((N_ODD, D), 0.05)
    inp['rwkv_r_k'] = nrm((N_ODD, RWKV_HEADS, HEAD_DIM), 0.1)
    inp['rwkv_ln_w'] = 1.0 + nrm((N_ODD, D), 0.05)
    inp['rwkv_ln_b'] = nrm((N_ODD, D), 0.02)
    inp['router_w'] = nrm((D, N_EXPERTS), D ** -0.5)
    inp['router_b'] = nrm((N_EXPERTS,), 0.01)
    inp['moe_w1'] = nrm((DEPTH, N_EXPERTS, D, D_EXPERT), D ** -0.5)
    inp['moe_w3'] = nrm((DEPTH, N_EXPERTS, D, D_EXPERT), D ** -0.5)
    inp['moe_w2'] = nrm((DEPTH, N_EXPERTS, D_EXPERT, D), D_EXPERT ** -0.5)
    inp['final_norm_g'] = 1.0 + nrm((D,), 0.05)
    return inp


def reference(x, c, ada_w, ada_b, mix_w_in, mix_w_out, sgu_norm_g, sgu_w, sgu_b,
              rwkv_mu, rwkv_w_rkv, rwkv_w_o, rwkv_w0, rwkv_w1, rwkv_w2,
              rwkv_a0, rwkv_a1, rwkv_a2, rwkv_v0, rwkv_v1, rwkv_v2,
              rwkv_g1, rwkv_g2, rwkv_k_k, rwkv_k_a, rwkv_r_k, rwkv_ln_w, rwkv_ln_b,
              router_w, router_b, moe_w1, moe_w3, moe_w2, final_norm_g):
    c_act = jax.nn.silu(c)
    v_first = None
    for layer in range(DEPTH):
        i = layer // 2
        shift, scale, gate = _ada(c_act, ada_w[layer, 0], ada_b[layer, 0])
        h = _rmsnorm(x) * (1 + scale) + shift
        if layer % 2 == 0:
            y = _even_mixer(h, mix_w_in[i], mix_w_out[i], sgu_norm_g[i], sgu_w[i], sgu_b[i])
        else:
            vres = None if i == 0 else (rwkv_v0[i - 1], rwkv_v1[i - 1], rwkv_v2[i - 1])
            y, v_first = _rwkv7_time_mix(
                h, v_first, vres, rwkv_mu[i], rwkv_w_rkv[i], rwkv_w_o[i],
                rwkv_w0[i], rwkv_w1[i], rwkv_w2[i], rwkv_a0[i], rwkv_a1[i], rwkv_a2[i],
                rwkv_g1[i], rwkv_g2[i], rwkv_k_k[i], rwkv_k_a[i], rwkv_r_k[i],
                rwkv_ln_w[i], rwkv_ln_b[i])
        x = x + gate * y
        shift, scale, gate = _ada(c_act, ada_w[layer, 1], ada_b[layer, 1])
        h = _rmsnorm(x) * (1 + scale) + shift
        x = x + gate * _moe(h, router_w, router_b, moe_w1[layer], moe_w3[layer], moe_w2[layer])
    return _rmsnorm(x) * final_norm_g
```

```python
import functools

import jax
import jax.numpy as jnp
from jax import lax
from jax.experimental import pallas as pl
from jax.experimental.pallas import tpu as pltpu

F32 = jnp.float32
BF16 = jnp.bfloat16
HIGHEST = lax.Precision.HIGHEST

D_MODEL = 1024
HEAD_DIM = 64
WIDTH_A = 512
WIDTH_B = 512
MIX_IN_WIDTH = 2560
ATTN_SPAN = 128
ATTN_DILATIONS = (1, 4, 16)
SGU_CHUNK = 128
N_GROUPS_B = 8
RWKV_HEADS = 16
RWKV_GN_EPS = 64e-5
WKV_CHUNK = 64
N_EXPERTS = 16
N_EXPERT_GROUPS = 4
EXPERTS_PER_GROUP = 4
D_EXPERT = 256
RMS_EPS = 1e-6
LANES = 128
VMEM_LIMIT = 56 * 1024 * 1024

NT_DIMS = (((1,), (1,)), ((), ()))
TN_DIMS = (((0,), (0,)), ((), ()))


def _params(*sem):
    return pltpu.CompilerParams(dimension_semantics=sem, vmem_limit_bytes=VMEM_LIMIT)


def _norm_mod(x, mod_ref):
    ms = jnp.mean(x * x, axis=-1, keepdims=True)
    return (x * lax.rsqrt(ms + RMS_EPS)) * (1.0 + mod_ref[0, 1:2, :]) + mod_ref[0, 0:1, :]


def _gelu(x):
    return 0.5 * x * (1.0 + lax.erf(x * 0.7071067811865476))


def _ada_body(c_ref, w_ref, b_ref, o_ref):
    c = c_ref[...]
    c_act = c * jax.nn.sigmoid(c)
    o_ref[0] = jnp.dot(c_act, w_ref[0], preferred_element_type=F32, precision=HIGHEST) + b_ref[0]


def _ada_call(c, ada_w, ada_b):
    n_pair = ada_w.shape[0] * ada_w.shape[1]
    batch = c.shape[0]
    w = ada_w.reshape(n_pair, D_MODEL, 3 * D_MODEL)
    b = ada_b.reshape(n_pair, 1, 3 * D_MODEL)
    out = pl.pallas_call(
        _ada_body,
        grid=(n_pair, 3),
        in_specs=[
            pl.BlockSpec((batch, D_MODEL), lambda p, j: (0, 0)),
            pl.BlockSpec((1, D_MODEL, D_MODEL), lambda p, j: (p, 0, j)),
            pl.BlockSpec((1, 1, D_MODEL), lambda p, j: (p, 0, j)),
        ],
        out_specs=pl.BlockSpec((1, batch, D_MODEL), lambda p, j: (p, 0, j)),
        out_shape=jax.ShapeDtypeStruct((n_pair, batch, 3 * D_MODEL), F32),
        compiler_params=_params("arbitrary", "arbitrary"),
        name="ada_mod",
    )(c, w, b)
    return out.reshape(n_pair, batch, 3, D_MODEL)


def _proj_body(x_ref, mod_ref, w_ref, o_ref):
    h = _norm_mod(x_ref[0], mod_ref)
    o_ref[0] = jnp.dot(h.astype(BF16), w_ref[...], preferred_element_type=F32)


def _proj_call(x, mod, w_in, ts=512):
    batch, seq, _ = x.shape
    return pl.pallas_call(
        _proj_body,
        grid=(batch, seq // ts),
        in_specs=[
            pl.BlockSpec((1, ts, D_MODEL), lambda b, i: (b, i, 0)),
            pl.BlockSpec((1, 3, D_MODEL), lambda b, i: (b, 0, 0)),
            pl.BlockSpec((D_MODEL, MIX_IN_WIDTH), lambda b, i: (0, 0)),
        ],
        out_specs=pl.BlockSpec((1, ts, MIX_IN_WIDTH), lambda b, i: (b, i, 0)),
        out_shape=jax.ShapeDtypeStruct((batch, seq, MIX_IN_WIDTH), F32),
        compiler_params=_params("arbitrary", "arbitrary"),
        name="mix_in_proj",
    )(x, mod, w_in)


def _attn_body(q_ref, k_ref, v_ref, o_ref, qs, ks, vs, ob, lb):
    seq = qs.shape[0]
    span = ATTN_SPAN
    row = lax.broadcasted_iota(jnp.int32, (span, span), 0)
    col = lax.broadcasted_iota(jnp.int32, (span, span), 1)
    cur_mask = col <= row
    prev_mask = col >= row

    def rows(start, dil):
        if dil == 1:
            return pl.ds(pl.multiple_of(start, span), span)
        return pl.ds(start, span, stride=dil)

    def band_block(p, dil, q_start, prev_start, has_prev):
        qi = rows(q_start, dil)
        q = qs[qi, :].astype(BF16)
        kc = ks[qi, :].astype(BF16)
        vc = vs[qi, :].astype(BF16)
        s_c = lax.dot_general(q, kc, NT_DIMS, preferred_element_type=F32)
        s_c = jnp.where(cur_mask, s_c, -jnp.inf)
        m = jnp.max(s_c, axis=-1, keepdims=True)
        if has_prev is not None:
            pi = rows(prev_start, dil)
            kp = ks[pi, :].astype(BF16)
            vp = vs[pi, :].astype(BF16)
            s_p = lax.dot_general(q, kp, NT_DIMS, preferred_element_type=F32)
            s_p = jnp.where(jnp.logical_and(prev_mask, has_prev), s_p, -jnp.inf)
            m = jnp.maximum(m, jnp.max(s_p, axis=-1, keepdims=True))
        p_c = jnp.exp(s_c - m)
        den = jnp.sum(p_c, axis=-1, keepdims=True)
        acc = jnp.dot(p_c.astype(BF16), vc, preferred_element_type=F32)
        if has_prev is not None:
            p_p = jnp.exp(s_p - m)
            den = den + jnp.sum(p_p, axis=-1, keepdims=True)
            acc = acc + jnp.dot(p_p.astype(BF16), vp, preferred_element_type=F32)
        ob[p, qi, :] = acc / den
        lb[p, qi, :] = jnp.broadcast_to(m + jnp.log(den), (span, HEAD_DIM))

    for hh in range(2):
        lanes = slice(HEAD_DIM * hh, HEAD_DIM * (hh + 1))
        qs[...] = q_ref[0, :, lanes] * (HEAD_DIM ** -0.5)
        ks[...] = k_ref[0, :, lanes]
        vs[...] = v_ref[0, :, lanes]

        for p, dil in enumerate(ATTN_DILATIONS):
            sub_len = seq // dil
            n_blk = sub_len // span

            def step(idx, carry, p=p, dil=dil, n_blk=n_blk):
                r = idx % dil
                n = idx // dil
                q_start = n * (span * dil) + r
                if n_blk == 1:
                    band_block(p, dil, q_start, None, None)
                else:
                    prev_start = jnp.maximum(n - 1, 0) * (span * dil) + r
                    band_block(p, dil, q_start, prev_start, n > 0)
                return carry

            lax.fori_loop(0, dil * n_blk, step, 0)

        l0, l1, l2 = lb[0], lb[1], lb[2]
        m = jnp.maximum(jnp.maximum(l0, l1), l2)
        w0, w1, w2 = jnp.exp(l0 - m), jnp.exp(l1 - m), jnp.exp(l2 - m)
        o = (w0 * ob[0] + w1 * ob[1] + w2 * ob[2]) / (w0 + w1 + w2)
        o_ref[0, :, lanes] = o


def _attn_call(proj):
    batch, seq, _ = proj.shape
    n_pair = WIDTH_A // LANES
    blk = lambda off: pl.BlockSpec((1, seq, LANES), lambda b, h: (b, 0, off + h))
    return pl.pallas_call(
        _attn_body,
        grid=(batch, n_pair),
        in_specs=[blk(0), blk(n_pair), blk(2 * n_pair)],
        out_specs=pl.BlockSpec((1, seq, LANES), lambda b, h: (b, 0, h)),
        out_shape=jax.ShapeDtypeStruct((batch, seq, WIDTH_A), F32),
        scratch_shapes=[
            pltpu.VMEM((seq, HEAD_DIM), F32),
            pltpu.VMEM((seq, HEAD_DIM), F32),
            pltpu.VMEM((seq, HEAD_DIM), F32),
            pltpu.VMEM((3, seq, HEAD_DIM), F32),
            pltpu.VMEM((3, seq, HEAD_DIM), F32),
        ],
        compiler_params=_params("arbitrary", "arbitrary"),
        name="dilated_attn",
    )(proj, proj, proj)


def _sgu_out_body(u_ref, z_ref, oa_ref, x_ref, mod_ref, ng_ref, sw_ref, sb_ref, wo_ref, o_ref, mix):
    ts = u_ref.shape[1]
    u = _gelu(u_ref[0])
    z = _gelu(z_ref[0])
    z = z * lax.rsqrt(jnp.mean(z * z, axis=-1, keepdims=True) + RMS_EPS) * ng_ref[...]
    zb = z.astype(BF16)
    row = lax.broadcasted_iota(jnp.int32, (SGU_CHUNK, SGU_CHUNK), 0)
    col = lax.broadcasted_iota(jnp.int32, (SGU_CHUNK, SGU_CHUNK), 1)
    causal = col <= row
    for g in range(N_GROUPS_B):
        w_g = jnp.where(causal, sw_ref[g], 0.0).astype(BF16)
        for cc in range(ts // SGU_CHUNK):
            rs = slice(cc * SGU_CHUNK, (cc + 1) * SGU_CHUNK)
            ls = slice(g * HEAD_DIM, (g + 1) * HEAD_DIM)
            mix[rs, ls] = jnp.dot(w_g, zb[rs, ls], preferred_element_type=F32)
    bias = jnp.concatenate([sb_ref[...]] * (ts // SGU_CHUNK), axis=0)
    o_b = u * (mix[...] + bias)
    y = jnp.dot(oa_ref[0].astype(BF16), wo_ref[0:WIDTH_A, :], preferred_element_type=F32)
    y = y + jnp.dot(o_b.astype(BF16), wo_ref[WIDTH_A:, :], preferred_element_type=F32)
    o_ref[0] = x_ref[0] + mod_ref[0, 2:3, :] * y


def _sgu_out_call(proj, o_a, x, mod, norm_g, sgu_w, sgu_bias_tile, w_out, ts=512):
    batch, seq, _ = x.shape
    u_blk = 3 * WIDTH_A // WIDTH_B
    full = lambda shape: pl.BlockSpec(shape, lambda b, i: (0,) * len(shape))
    return pl.pallas_call(
        _sgu_out_body,
        grid=(batch, seq // ts),
        in_specs=[
            pl.BlockSpec((1, ts, WIDTH_B), lambda b, i: (b, i, u_blk)),
            pl.BlockSpec((1, ts, WIDTH_B), lambda b, i: (b, i, u_blk + 1)),
            pl.BlockSpec((1, ts, WIDTH_A), lambda b, i: (b, i, 0)),
            pl.BlockSpec((1, ts, D_MODEL), lambda b, i: (b, i, 0)),
            pl.BlockSpec((1, 3, D_MODEL), lambda b, i: (b, 0, 0)),
            full((1, WIDTH_B)),
            full((N_GROUPS_B, SGU_CHUNK, SGU_CHUNK)),
            full((SGU_CHUNK, WIDTH_B)),
            full((WIDTH_A + WIDTH_B, D_MODEL)),
        ],
        out_specs=pl.BlockSpec((1, ts, D_MODEL), lambda b, i: (b, i, 0)),
        out_shape=jax.ShapeDtypeStruct(x.shape, F32),
        scratch_shapes=[pltpu.VMEM((ts, WIDTH_B), F32)],
        compiler_params=_params("arbitrary", "arbitrary"),
        name="sgu_out_proj",
    )(proj, proj, o_a, x, mod, norm_g, sgu_w, sgu_bias_tile, w_out)


def _rwkv_pre_body(has_vres, *refs):
    if has_vres:
        (x_ref, mod_ref, mu_ref, wr_ref, wk_ref, wv_ref, w0_ref, w1_ref, w2_ref, a0_ref, a1_ref, a2_ref,
         g1_ref, g2_ref, kk_ref, ka_ref, vf_ref, v0_ref, v1_ref, v2_ref,
         r_out, w_out, k_out, v_out, kk_out, a_out, g_out, carry) = refs
    else:
        (x_ref, mod_ref, mu_ref, wr_ref, wk_ref, wv_ref, w0_ref, w1_ref, w2_ref, a0_ref, a1_ref, a2_ref,
         g1_ref, g2_ref, kk_ref, ka_ref,
         r_out, w_out, k_out, v_out, kk_out, a_out, g_out, carry) = refs

    @pl.when(pl.program_id(1) == 0)
    def _():
        carry[...] = jnp.zeros_like(carry)

    h = _norm_mod(x_ref[0], mod_ref)
    ts = h.shape[0]
    first = lax.broadcasted_iota(jnp.int32, h.shape, 0) == 0
    h_prev = jnp.where(first, carry[0:1, :], pltpu.roll(h, 1, 0))
    carry[0:1, :] = h[ts - 1:ts, :]
    xx = h_prev - h

    def mixed(i):
        return (h + xx * mu_ref[i:i + 1, :]).astype(BF16)

    def mm(a, w_ref):
        return jnp.dot(a, w_ref[...], preferred_element_type=F32)

    xr, xw, xk, xv, xa, xg = [mixed(i) for i in range(6)]
    r = mm(xr, wr_ref)
    k = mm(xk, wk_ref)
    v = mm(xv, wv_ref)
    z = w0_ref[...] + mm(jnp.tanh(mm(xw, w1_ref)).astype(BF16), w2_ref)
    w_log = -(jnp.maximum(-z, 0.0) + jnp.log(1.0 + jnp.exp(-jnp.abs(z)))) - 0.5
    if has_vres:
        mix_v = jax.nn.sigmoid(v0_ref[...] + mm(mm(xv, v1_ref).astype(BF16), v2_ref))
        v = v + (vf_ref[0] - v) * mix_v
    a = jax.nn.sigmoid(a0_ref[...] + mm(mm(xa, a1_ref).astype(BF16), a2_ref))
    g = mm(jax.nn.sigmoid(mm(xg, g1_ref)).astype(BF16), g2_ref)
    r_out[0] = r
    w_out[0] = w_log
    kk_out[0] = k * kk_ref[...]
    k_out[0] = k * (1.0 + (a - 1.0) * ka_ref[...])
    v_out[0] = v
    a_out[0] = a
    g_out[0] = g


def _rwkv_pre_call(x, mod, weights, v_first, vres, ts=256):
    batch, seq, _ = x.shape
    tok = pl.BlockSpec((1, ts, D_MODEL), lambda b, i: (b, i, 0))
    full = lambda a: pl.BlockSpec(a.shape, lambda b, i: (0,) * a.ndim)
    ins = [x, mod] + list(weights)
    specs = [tok, pl.BlockSpec((1, 3, D_MODEL), lambda b, i: (b, 0, 0))] + [full(a) for a in weights]
    if vres is not None:
        ins += [v_first] + list(vres)
        specs += [tok] + [full(a) for a in vres]
    out_sds = jax.ShapeDtypeStruct(x.shape, F32)
    return pl.pallas_call(
        functools.partial(_rwkv_pre_body, vres is not None),
        grid=(batch, seq // ts),
        in_specs=specs,
        out_specs=[tok] * 7,
        out_shape=[out_sds] * 7,
        scratch_shapes=[pltpu.VMEM((8, D_MODEL), F32)],
        compiler_params=_params("arbitrary", "arbitrary"),
        name="rwkv_pre",
    )(*ins)


def _wkv_body(r_ref, w_ref, k_ref, v_ref, kk_ref, a_ref, lnw_ref, lnb_ref, rk_ref, o_ref, state):
    C = WKV_CHUNK
    n_chunk = r_ref.shape[1] // C

    @pl.when(pl.program_id(2) == 0)
    def _():
        state[...] = jnp.zeros_like(state)

    row = lax.broadcasted_iota(jnp.int32, (C, C), 0)
    col = lax.broadcasted_iota(jnp.int32, (C, C), 1)
    lower = col <= row
    strict = col < row
    tri = jnp.where(lower, 1.0, 0.0).astype(F32)
    eye = jnp.where(col == row, 1.0, 0.0).astype(F32)

    def nt(a, b):
        return lax.dot_general(a.astype(BF16), b.astype(BF16), NT_DIMS, preferred_element_type=F32)

    def nn(a, b):
        return jnp.dot(a.astype(BF16), b.astype(BF16), preferred_element_type=F32)

    def tn(a, b):
        return lax.dot_general(a.astype(BF16), b.astype(BF16), TN_DIMS, preferred_element_type=F32)

    def chunk(c, carry):
        rs = pl.ds(pl.multiple_of(c * C, C), C)
        for hh in range(2):
            ls = slice(HEAD_DIM * hh, HEAD_DIM * (hh + 1))
            r = r_ref[0, rs, ls]
            k = k_ref[0, rs, ls]
            v = v_ref[0, rs, ls]
            kk = kk_ref[0, rs, ls]
            a_gate = a_ref[0, rs, ls]
            kk = kk / jnp.maximum(jnp.sqrt(jnp.sum(kk * kk, axis=-1, keepdims=True)), 1e-12)
            a = -kk
            b = kk * a_gate
            log_w = -jnp.exp(w_ref[0, rs, ls])
            cum = jnp.dot(tri, log_w, preferred_element_type=F32, precision=HIGHEST)
            g_in = jnp.exp(cum)
            g_inv = jnp.exp(-cum)
            g_last = g_in[C - 1:C, :]
            a_t = a * jnp.exp(cum - log_w)
            r_t = r * g_in
            b_t = b * g_inv
            k_t = k * g_inv
            a_ab = jnp.where(strict, nt(a_t, b_t), 0.0)
            a_ak = jnp.where(strict, nt(a_t, k_t), 0.0)
            a_rb = jnp.where(lower, nt(r_t, b_t), 0.0)
            a_rk = jnp.where(lower, nt(r_t, k_t), 0.0)
            inv = eye + a_ab
            pw = a_ab
            for _ in range(5):
                pw = nn(pw, pw)
                inv = inv + nn(inv, pw)
            s0 = state[hh]
            u = nn(inv, nt(a_t, s0) + nn(a_ak, v))
            y = nt(r_t, s0) + nn(a_rb, u) + nn(a_rk, v)
            state[hh] = s0 * g_last + tn(u, b_t * g_last) + tn(v, k_t * g_last)
            mean = jnp.mean(y, axis=-1, keepdims=True)
            yc = y - mean
            var = jnp.mean(yc * yc, axis=-1, keepdims=True)
            yn = yc * lax.rsqrt(var + RWKV_GN_EPS) * lnw_ref[:, ls] + lnb_ref[:, ls]
            bonus = jnp.sum(r * k * rk_ref[:, ls], axis=-1, keepdims=True) * v
            o_ref[0, rs, ls] = yn + bonus
        return carry

    lax.fori_loop(0, n_chunk, chunk, 0)


def _wkv_call(r, w_log, k, v, kk, a, ln_w, ln_b, r_k, tc=512):
    batch, seq, _ = r.shape
    n_pair = D_MODEL // LANES
    tok = pl.BlockSpec((1, tc, LANES), lambda b, h, t: (b, t, h))
    vec = pl.BlockSpec((1, LANES), lambda b, h, t: (0, h))
    return pl.pallas_call(
        _wkv_body,
        grid=(batch, n_pair, seq // tc),
        in_specs=[tok] * 6 + [vec] * 3,
        out_specs=tok,
        out_shape=jax.ShapeDtypeStruct(r.shape, F32),
        scratch_shapes=[pltpu.VMEM((2, HEAD_DIM, HEAD_DIM), F32)],
        compiler_params=_params("arbitrary", "arbitrary", "arbitrary"),
        name="wkv7_chunked",
    )(r, w_log, k, v, kk, a, ln_w, ln_b, r_k)


def _rwkv_post_body(y_ref, g_ref, x_ref, mod_ref, wo_ref, o_ref):
    yg = (y_ref[0] * g_ref[0]).astype(BF16)
    o_ref[0] = x_ref[0] + mod_ref[0, 2:3, :] * jnp.dot(yg, wo_ref[...], preferred_element_type=F32)


def _rwkv_post_call(y, g, x, mod, w_o, ts=512):
    batch, seq, _ = x.shape
    tok = pl.BlockSpec((1, ts, D_MODEL), lambda b, i: (b, i, 0))
    return pl.pallas_call(
        _rwkv_post_body,
        grid=(batch, seq // ts),
        in_specs=[tok, tok, tok, pl.BlockSpec((1, 3, D_MODEL), lambda b, i: (b, 0, 0)),
                  pl.BlockSpec((D_MODEL, D_MODEL), lambda b, i: (0, 0))],
        out_specs=tok,
        out_shape=jax.ShapeDtypeStruct(x.shape, F32),
        compiler_params=_params("arbitrary", "arbitrary"),
        name="rwkv_out_proj",
    )(y, g, x, mod, w_o)


def _moe_body(x_ref, mod_ref, rw_ref, rb_ref, w13_ref, w2_ref, o_ref, h_scr, gate_scr, acc):
    e = pl.program_id(2)
    tm = x_ref.shape[1]

    @pl.when(e == 0)
    def _():
        h = _norm_mod(x_ref[0], mod_ref)
        h_scr[...] = h.astype(BF16)
        logits = lax.dot_general(rw_ref[...], h, NT_DIMS, preferred_element_type=F32, precision=HIGHEST)
        s = jax.nn.sigmoid(logits)
        sel = s + rb_ref[...]
        n_g, epg = N_EXPERT_GROUPS, EXPERTS_PER_GROUP
        s_rows = [s[i:i + 1, :] for i in range(N_EXPERTS)]
        sel_rows = [sel[i:i + 1, :] for i in range(N_EXPERTS)]
        scores = []
        for g in range(n_g):
            grp = sel_rows[g * epg:(g + 1) * epg]
            best = None
            for i in range(epg):
                for j in range(i + 1, epg):
                    pair = grp[i] + grp[j]
                    best = pair if best is None else jnp.maximum(best, pair)
            scores.append(best)
        g_idx = jnp.zeros_like(scores[0], dtype=jnp.int32)
        top = scores[0]
        for g in range(1, n_g):
            better = scores[g] > top
            g_idx = jnp.where(better, g, g_idx)
            top = jnp.where(better, scores[g], top)

        def pick(rows_, j):
            out = rows_[j]
            for g in range(1, n_g):
                out = jnp.where(g_idx == g, rows_[g * epg + j], out)
            return out

        in_sel = [pick(sel_rows, j) for j in range(epg)]
        in_s = [pick(s_rows, j) for j in range(epg)]
        chosen = []
        for j in range(epg):
            rank = jnp.zeros_like(g_idx)
            for i in range(epg):
                if i == j:
                    continue
                ahead = (in_sel[i] >= in_sel[j]) if i < j else (in_sel[i] > in_sel[j])
                rank = rank + jnp.where(ahead, 1, 0)
            chosen.append(rank < 2)
        den = sum(jnp.where(chosen[j], in_s[j], 0.0) for j in range(epg))
        gate_rows = []
        for ex in range(N_EXPERTS):
            g, j = divmod(ex, epg)
            on = jnp.logical_and(chosen[j], g_idx == g)
            gate_rows.append(jnp.where(on, in_s[j] / den, 0.0))
        pad = jnp.zeros((LANES - N_EXPERTS, tm), F32)
        gate_t = jnp.concatenate(gate_rows + [pad], axis=0).T
        for ex in range(N_EXPERTS):
            gate_scr[ex] = gate_t[:, ex:ex + 1]
        acc[...] = jnp.zeros_like(acc)

    hid = jnp.dot(h_scr[...], w13_ref[0, 0], preferred_element_type=F32)
    h1 = hid[:, :D_EXPERT]
    act = h1 * jax.nn.sigmoid(h1) * hid[:, D_EXPERT:] * gate_scr[e]
    acc[...] += jnp.dot(act.astype(BF16), w2_ref[0, 0], preferred_element_type=F32)

    @pl.when(e == N_EXPERTS - 1)
    def _():
        o_ref[0] = x_ref[0] + mod_ref[0, 2:3, :] * acc[...]


def _moe_call(x, mod, router_wt, router_b, w13, w2, layer, tm=1024):
    batch, seq, _ = x.shape
    tok = pl.BlockSpec((1, tm, D_MODEL), lambda b, i, e: (b, i, 0))
    return pl.pallas_call(
        _moe_body,
        grid=(batch, seq // tm, N_EXPERTS),
        in_specs=[
            tok,
            pl.BlockSpec((1, 3, D_MODEL), lambda b, i, e: (b, 0, 0)),
            pl.BlockSpec((N_EXPERTS, D_MODEL), lambda b, i, e: (0, 0)),
            pl.BlockSpec((N_EXPERTS, 1), lambda b, i, e: (0, 0)),
            pl.BlockSpec((1, 1, D_MODEL, 2 * D_EXPERT), lambda b, i, e: (layer, e, 0, 0)),
            pl.BlockSpec((1, 1, D_EXPERT, D_MODEL), lambda b, i, e: (layer, e, 0, 0)),
        ],
        out_specs=tok,
        out_shape=jax.ShapeDtypeStruct(x.shape, F32),
        scratch_shapes=[
            pltpu.VMEM((tm, D_MODEL), BF16),
            pltpu.VMEM((N_EXPERTS, tm, 1), F32),
            pltpu.VMEM((tm, D_MODEL), F32),
        ],
        compiler_params=_params("arbitrary", "arbitrary", "arbitrary"),
        name="moe",
    )(x, mod, router_wt, router_b, w13, w2)


def _final_body(x_ref, g_ref, o_ref):
    x = x_ref[0]
    o_ref[0] = x * lax.rsqrt(jnp.mean(x * x, axis=-1, keepdims=True) + RMS_EPS) * g_ref[...]


def _final_call(x, g, ts=1024):
    batch, seq, _ = x.shape
    tok = pl.BlockSpec((1, ts, D_MODEL), lambda b, i: (b, i, 0))
    return pl.pallas_call(
        _final_body,
        grid=(batch, seq // ts),
        in_specs=[tok, pl.BlockSpec((1, D_MODEL), lambda b, i: (0, 0))],
        out_specs=tok,
        out_shape=jax.ShapeDtypeStruct(x.shape, F32),
        compiler_params=_params("arbitrary", "arbitrary"),
        name="final_norm",
    )(x, g)


def _pad_cols(w):
    n = w.shape[-1]
    return jnp.pad(w, ((0, 0), (0, -n % LANES)))


def _pad_rows(w):
    n = w.shape[0]
    return jnp.pad(w, ((0, -n % LANES), (0, 0)))


def kernel(x, c, ada_w, ada_b, mix_w_in, mix_w_out, sgu_norm_g, sgu_w, sgu_b, rwkv_mu, rwkv_w_rkv, rwkv_w_o, rwkv_w0, rwkv_w1, rwkv_w2, rwkv_a0, rwkv_a1, rwkv_a2, rwkv_v0, rwkv_v1, rwkv_v2, rwkv_g1, rwkv_g2, rwkv_k_k, rwkv_k_a, rwkv_r_k, rwkv_ln_w, rwkv_ln_b, router_w, router_b, moe_w1, moe_w3, moe_w2, final_norm_g):
    depth = ada_w.shape[0]
    row = lambda t: t.reshape(1, -1)
    mods = _ada_call(c, ada_w, ada_b)
    w13 = jnp.concatenate([moe_w1, moe_w3], axis=-1).astype(BF16)
    w2 = moe_w2.astype(BF16)
    router_wt = router_w.T
    router_bc = router_b.reshape(N_EXPERTS, 1)
    v_first = None
    for layer in range(depth):
        i = layer // 2
        mod = mods[2 * layer]
        if layer % 2 == 0:
            proj = _proj_call(x, mod, mix_w_in[i].astype(BF16))
            o_a = _attn_call(proj)
            bias_tile = jnp.repeat(sgu_b[i].T, HEAD_DIM, axis=1)
            x = _sgu_out_call(proj, o_a, x, mod, row(sgu_norm_g[i]), sgu_w[i], bias_tile,
                              mix_w_out[i].astype(BF16))
        else:
            weights = [
                rwkv_mu[i],
                rwkv_w_rkv[i, 0].astype(BF16), rwkv_w_rkv[i, 1].astype(BF16), rwkv_w_rkv[i, 2].astype(BF16),
                row(rwkv_w0[i]), _pad_cols(rwkv_w1[i]).astype(BF16), _pad_rows(rwkv_w2[i]).astype(BF16),
                row(rwkv_a0[i]), _pad_cols(rwkv_a1[i]).astype(BF16), _pad_rows(rwkv_a2[i]).astype(BF16),
                _pad_cols(rwkv_g1[i]).astype(BF16), _pad_rows(rwkv_g2[i]).astype(BF16),
                row(rwkv_k_k[i]), row(rwkv_k_a[i]),
            ]
            vres = None
            if i > 0:
                vres = [row(rwkv_v0[i - 1]), _pad_cols(rwkv_v1[i - 1]).astype(BF16),
                        _pad_rows(rwkv_v2[i - 1]).astype(BF16)]
            r, w_log, k, v, kk, a, g = _rwkv_pre_call(x, mod, weights, v_first, vres)
            if i == 0:
                v_first = v
            y = _wkv_call(r, w_log, k, v, kk, a, row(rwkv_ln_w[i]), row(rwkv_ln_b[i]), row(rwkv_r_k[i]))
            x = _rwkv_post_call(y, g, x, mod, rwkv_w_o[i].astype(BF16))
        x = _moe_call(x, mods[2 * layer + 1], router_wt, router_bc, w13, w2, layer)
    return _final_call(x, row(final_norm_g))
```

```python
import functools

import jax
import jax.numpy as jnp
from jax import lax
from jax.experimental import pallas as pl
from jax.experimental.pallas import tpu as pltpu

F32 = jnp.float32
BF16 = jnp.bfloat16
HIGHEST = lax.Precision.HIGHEST

D_MODEL = 1024
HEAD_DIM = 64
WIDTH_A = 512
WIDTH_B = 512
MIX_IN_WIDTH = 2560
ATTN_SPAN = 128
ATTN_DILATIONS = (1, 4, 16)
SGU_CHUNK = 128
N_GROUPS_B = 8
RWKV_HEADS = 16
RWKV_GN_EPS = 64e-5
WKV_CHUNK = 64
N_EXPERTS = 16
N_EXPERT_GROUPS = 4
EXPERTS_PER_GROUP = 4
D_EXPERT = 256
RMS_EPS = 1e-6
LANES = 128
VMEM_LIMIT = 56 * 1024 * 1024

NT_DIMS = (((1,), (1,)), ((), ()))
TN_DIMS = (((0,), (0,)), ((), ()))


def _params(*sem):
    return pltpu.CompilerParams(dimension_semantics=sem, vmem_limit_bytes=VMEM_LIMIT)


def _norm_mod(x, mod_ref):
    ms = jnp.mean(x * x, axis=-1, keepdims=True)
    return (x * lax.rsqrt(ms + RMS_EPS)) * (1.0 + mod_ref[0, 1:2, :]) + mod_ref[0, 0:1, :]


def _gelu(x):
    return 0.5 * x * (1.0 + lax.erf(x * 0.7071067811865476))


def _ada_body(c_ref, w_ref, b_ref, o_ref):
    c = c_ref[...]
    c_act = c * jax.nn.sigmoid(c)
    o_ref[0] = jnp.dot(c_act, w_ref[0], preferred_element_type=F32, precision=HIGHEST) + b_ref[0]


def _ada_call(c, ada_w, ada_b):
    n_pair = ada_w.shape[0] * ada_w.shape[1]
    batch = c.shape[0]
    w = ada_w.reshape(n_pair, D_MODEL, 3 * D_MODEL)
    b = ada_b.reshape(n_pair, 1, 3 * D_MODEL)
    out = pl.pallas_call(
        _ada_body,
        grid=(n_pair, 3),
        in_specs=[
            pl.BlockSpec((batch, D_MODEL), lambda p, j: (0, 0)),
            pl.BlockSpec((1, D_MODEL, D_MODEL), lambda p, j: (p, 0, j)),
            pl.BlockSpec((1, 1, D_MODEL), lambda p, j: (p, 0, j)),
        ],
        out_specs=pl.BlockSpec((1, batch, D_MODEL), lambda p, j: (p, 0, j)),
        out_shape=jax.ShapeDtypeStruct((n_pair, batch, 3 * D_MODEL), F32),
        compiler_params=_params("arbitrary", "arbitrary"),
        name="ada_mod",
    )(c, w, b)
    return out.reshape(n_pair, batch, 3, D_MODEL)


def _proj_body(x_ref, mod_ref, w_ref, o_ref):
    h = _norm_mod(x_ref[0], mod_ref)
    o_ref[0] = jnp.dot(h.astype(BF16), w_ref[...], preferred_element_type=F32)


def _proj_call(x, mod, w_in, ts=512):
    batch, seq, _ = x.shape
    return pl.pallas_call(
        _proj_body,
        grid=(batch, seq // ts),
        in_specs=[
            pl.BlockSpec((1, ts, D_MODEL), lambda b, i: (b, i, 0)),
            pl.BlockSpec((1, 3, D_MODEL), lambda b, i: (b, 0, 0)),
            pl.BlockSpec((D_MODEL, MIX_IN_WIDTH), lambda b, i: (0, 0)),
        ],
        out_specs=pl.BlockSpec((1, ts, MIX_IN_WIDTH), lambda b, i: (b, i, 0)),
        out_shape=jax.ShapeDtypeStruct((batch, seq, MIX_IN_WIDTH), F32),
        compiler_params=_params("arbitrary", "arbitrary"),
        name="mix_in_proj",
    )(x, mod, w_in)


def _attn_body(q_ref, k_ref, v_ref, o_ref, qs, ks, vs, ob, lb):
    seq = qs.shape[0]
    span = ATTN_SPAN
    row = lax.broadcasted_iota(jnp.int32, (span, span), 0)
    col = lax.broadcasted_iota(jnp.int32, (span, span), 1)
    cur_mask = col <= row
    prev_mask = col >= row

    def rows(start, dil):
        if dil == 1:
            return pl.ds(pl.multiple_of(start, span), span)
        return pl.ds(start, span, stride=dil)

    def band_block(p, dil, q_start, prev_start, has_prev):
        qi = rows(q_start, dil)
        q = qs[qi, :].astype(BF16)
        kc = ks[qi, :].astype(BF16)
        vc = vs[qi, :].astype(BF16)
        s_c = lax.dot_general(q, kc, NT_DIMS, preferred_element_type=F32)
        s_c = jnp.where(cur_mask, s_c, -jnp.inf)
        m = jnp.max(s_c, axis=-1, keepdims=True)
        if has_prev is not None:
            pi = rows(prev_start, dil)
            kp = ks[pi, :].astype(BF16)
            vp = vs[pi, :].astype(BF16)
            s_p = lax.dot_general(q, kp, NT_DIMS, preferred_element_type=F32)
            s_p = jnp.where(jnp.logical_and(prev_mask, has_prev), s_p, -jnp.inf)
            m = jnp.maximum(m, jnp.max(s_p, axis=-1, keepdims=True))
        p_c = jnp.exp(s_c - m)
        den = jnp.sum(p_c, axis=-1, keepdims=True)
        acc = jnp.dot(p_c.astype(BF16), vc, preferred_element_type=F32)
        if has_prev is not None:
            p_p = jnp.exp(s_p - m)
            den = den + jnp.sum(p_p, axis=-1, keepdims=True)
            acc = acc + jnp.dot(p_p.astype(BF16), vp, preferred_element_type=F32)
        ob[p, qi, :] = acc / den
        lb[p, qi, :] = jnp.broadcast_to(m + jnp.log(den), (span, HEAD_DIM))

    for hh in range(2):
        lanes = slice(HEAD_DIM * hh, HEAD_DIM * (hh + 1))
        qs[...] = q_ref[0, :, lanes] * (HEAD_DIM ** -0.5)
        ks[...] = k_ref[0, :, lanes]
        vs[...] = v_ref[0, :, lanes]

        for p, dil in enumerate(ATTN_DILATIONS):
            sub_len = seq // dil
            n_blk = sub_len // span

            def step(idx, carry, p=p, dil=dil, n_blk=n_blk):
                r = idx % dil
                n = idx // dil
                q_start = n * (span * dil) + r
                if n_blk == 1:
                    band_block(p, dil, q_start, None, None)
                else:
                    prev_start = jnp.maximum(n - 1, 0) * (span * dil) + r
                    band_block(p, dil, q_start, prev_start, n > 0)
                return carry

            lax.fori_loop(0, dil * n_blk, step, 0)

        l0, l1, l2 = lb[0], lb[1], lb[2]
        m = jnp.maximum(jnp.maximum(l0, l1), l2)
        w0, w1, w2 = jnp.exp(l0 - m), jnp.exp(l1 - m), jnp.exp(l2 - m)
        o = (w0 * ob[0] + w1 * ob[1] + w2 * ob[2]) / (w0 + w1 + w2)
        o_ref[0, :, lanes] = o


def _attn_call(proj):
    batch, seq, _ = proj.shape
    n_pair = WIDTH_A // LANES
    blk = lambda off: pl.BlockSpec((1, seq, LANES), lambda b, h: (b, 0, off + h))
    return pl.pallas_call(
        _attn_body,
        grid=(batch, n_pair),
        in_specs=[blk(0), blk(n_pair), blk(2 * n_pair)],
        out_specs=pl.BlockSpec((1, seq, LANES), lambda b, h: (b, 0, h)),
        out_shape=jax.ShapeDtypeStruct((batch, seq, WIDTH_A), F32),
        scratch_shapes=[
            pltpu.VMEM((seq, HEAD_DIM), F32),
            pltpu.VMEM((seq, HEAD_DIM), F32),
            pltpu.VMEM((seq, HEAD_DIM), F32),
            pltpu.VMEM((3, seq, HEAD_DIM), F32),
            pltpu.VMEM((3, seq, HEAD_DIM), F32),
        ],
        compiler_params=_params("arbitrary", "arbitrary"),
        name="dilated_attn",
    )(proj, proj, proj)


def _sgu_out_body(u_ref, z_ref, oa_ref, x_ref, mod_ref, ng_ref, sw_ref, sb_ref, wo_ref, o_ref, mix):
    ts = u_ref.shape[1]
    u = _gelu(u_ref[0])
    z = _gelu(z_ref[0])
    z = z * lax.rsqrt(jnp.mean(z * z, axis=-1, keepdims=True) + RMS_EPS) * ng_ref[...]
    zb = z.astype(BF16)
    row = lax.broadcasted_iota(jnp.int32, (SGU_CHUNK, SGU_CHUNK), 0)
    col = lax.broadcasted_iota(jnp.int32, (SGU_CHUNK, SGU_CHUNK), 1)
    causal = col <= row
    for g in range(N_GROUPS_B):
        w_g = jnp.where(causal, sw_ref[g], 0.0).astype(BF16)
        for cc in range(ts // SGU_CHUNK):
            rs = slice(cc * SGU_CHUNK, (cc + 1) * SGU_CHUNK)
            ls = slice(g * HEAD_DIM, (g + 1) * HEAD_DIM)
            mix[rs, ls] = jnp.dot(w_g, zb[rs, ls], preferred_element_type=F32)
    bias = jnp.concatenate([sb_ref[...]] * (ts // SGU_CHUNK), axis=0)
    o_b = u * (mix[...] + bias)
    y = jnp.dot(oa_ref[0].astype(BF16), wo_ref[0:WIDTH_A, :], preferred_element_type=F32)
    y = y + jnp.dot(o_b.astype(BF16), wo_ref[WIDTH_A:, :], preferred_element_type=F32)
    o_ref[0] = x_ref[0] + mod_ref[0, 2:3, :] * y


def _sgu_out_call(proj, o_a, x, mod, norm_g, sgu_w, sgu_bias_tile, w_out, ts=512):
    batch, seq, _ = x.shape
    u_blk = 3 * WIDTH_A // WIDTH_B
    full = lambda shape: pl.BlockSpec(shape, lambda b, i: (0,) * len(shape))
    return pl.pallas_call(
        _sgu_out_body,
        grid=(batch, seq // ts),
        in_specs=[
            pl.BlockSpec((1, ts, WIDTH_B), lambda b, i: (b, i, u_blk)),
            pl.BlockSpec((1, ts, WIDTH_B), lambda b, i: (b, i, u_blk + 1)),
            pl.BlockSpec((1, ts, WIDTH_A), lambda b, i: (b, i, 0)),
            pl.BlockSpec((1, ts, D_MODEL), lambda b, i: (b, i, 0)),
            pl.BlockSpec((1, 3, D_MODEL), lambda b, i: (b, 0, 0)),
            full((1, WIDTH_B)),
            full((N_GROUPS_B, SGU_CHUNK, SGU_CHUNK)),
            full((SGU_CHUNK, WIDTH_B)),
            full((WIDTH_A + WIDTH_B, D_MODEL)),
        ],
        out_specs=pl.BlockSpec((1, ts, D_MODEL), lambda b, i: (b, i, 0)),
        out_shape=jax.ShapeDtypeStruct(x.shape, F32),
        scratch_shapes=[pltpu.VMEM((ts, WIDTH_B), F32)],
        compiler_params=_params("arbitrary", "arbitrary"),
        name="sgu_out_proj",
    )(proj, proj, o_a, x, mod, norm_g, sgu_w, sgu_bias_tile, w_out)


def _rwkv_pre_body(has_vres, *refs):
    if has_vres:
        (x_ref, mod_ref, mu_ref, wr_ref, wk_ref, wv_ref, w0_ref, w1_ref, w2_ref, a0_ref, a1_ref, a2_ref,
         g1_ref, g2_ref, kk_ref, ka_ref, vf_ref, v0_ref, v1_ref, v2_ref,
         r_out, w_out, k_out, v_out, kk_out, a_out, g_out, carry) = refs
    else:
        (x_ref, mod_ref, mu_ref, wr_ref, wk_ref, wv_ref, w0_ref, w1_ref, w2_ref, a0_ref, a1_ref, a2_ref,
         g1_ref, g2_ref, kk_ref, ka_ref,
         r_out, w_out, k_out, v_out, kk_out, a_out, g_out, carry) = refs

    @pl.when(pl.program_id(1) == 0)
    def _():
        carry[...] = jnp.zeros_like(carry)

    h = _norm_mod(x_ref[0], mod_ref)
    ts = h.shape[0]
    first = lax.broadcasted_iota(jnp.int32, h.shape, 0) == 0
    h_prev = jnp.where(first, carry[0:1, :], pltpu.roll(h, 1, 0))
    carry[0:1, :] = h[ts - 1:ts, :]
    xx = h_prev - h

    def mixed(i):
        return (h + xx * mu_ref[i:i + 1, :]).astype(BF16)

    def mm(a, w_ref):
        return jnp.dot(a, w_ref[...], preferred_element_type=F32)

    xr, xw, xk, xv, xa, xg = [mixed(i) for i in range(6)]
    r = mm(xr, wr_ref)
    k = mm(xk, wk_ref)
    v = mm(xv, wv_ref)
    z = w0_ref[...] + mm(jnp.tanh(mm(xw, w1_ref)).astype(BF16), w2_ref)
    w_log = -(jnp.maximum(-z, 0.0) + jnp.log(1.0 + jnp.exp(-jnp.abs(z)))) - 0.5
    if has_vres:
        mix_v = jax.nn.sigmoid(v0_ref[...] + mm(mm(xv, v1_ref).astype(BF16), v2_ref))
        v = v + (vf_ref[0] - v) * mix_v
    a = jax.nn.sigmoid(a0_ref[...] + mm(mm(xa, a1_ref).astype(BF16), a2_ref))
    g = mm(jax.nn.sigmoid(mm(xg, g1_ref)).astype(BF16), g2_ref)
    r_out[0] = r
    w_out[0] = w_log
    kk_out[0] = k * kk_ref[...]
    k_out[0] = k * (1.0 + (a - 1.0) * ka_ref[...])
    v_out[0] = v
    a_out[0] = a
    g_out[0] = g


def _rwkv_pre_call(x, mod, weights, v_first, vres, ts=256):
    batch, seq, _ = x.shape
    tok = pl.BlockSpec((1, ts, D_MODEL), lambda b, i: (b, i, 0))
    full = lambda a: pl.BlockSpec(a.shape, lambda b, i: (0,) * a.ndim)
    ins = [x, mod] + list(weights)
    specs = [tok, pl.BlockSpec((1, 3, D_MODEL), lambda b, i: (b, 0, 0))] + [full(a) for a in weights]
    if vres is not None:
        ins += [v_first] + list(vres)
        specs += [tok] + [full(a) for a in vres]
    out_sds = jax.ShapeDtypeStruct(x.shape, F32)
    return pl.pallas_call(
        functools.partial(_rwkv_pre_body, vres is not None),
        grid=(batch, seq // ts),
        in_specs=specs,
        out_specs=[tok] * 7,
        out_shape=[out_sds] * 7,
        scratch_shapes=[pltpu.VMEM((8, D_MODEL), F32)],
        compiler_params=_params("arbitrary", "arbitrary"),
        name="rwkv_pre",
    )(*ins)


def _split_dot(x, ones):
    hi = x.astype(BF16)
    lo = (x - hi.astype(F32)).astype(BF16)
    rows = x.shape[0]
    both = jnp.dot(jnp.concatenate([hi, lo], axis=0), ones, preferred_element_type=F32)
    return both[:rows] + both[rows:]


def _wkv_body(r_ref, w_ref, k_ref, v_ref, kk_ref, a_ref, lnw_ref, lnb_ref, rk_ref, o_ref, state):
    C = WKV_CHUNK
    W = LANES
    n_chunk = r_ref.shape[1] // C

    @pl.when(pl.program_id(2) == 0)
    def _():
        state[...] = jnp.zeros_like(state)

    def iota(shape, dim):
        return lax.broadcasted_iota(jnp.int32, shape, dim)

    head0 = iota((C, W), 1) < HEAD_DIM
    row2, col2 = iota((C, 2 * C), 0), iota((C, 2 * C), 1) % C
    strict2 = col2 < row2
    lower2 = col2 <= row2
    tri = jnp.where(iota((C, C), 1) <= iota((C, C), 0), 1.0, 0.0).astype(BF16)
    eye = jnp.where(iota((C, C), 1) == iota((C, C), 0), 1.0, 0.0).astype(F32)
    rw, cw = iota((W, W), 0), iota((W, W), 1)
    same_head = (rw < HEAD_DIM) == (cw < HEAD_DIM)
    ones_bd = jnp.where(same_head, 1.0, 0.0).astype(BF16)
    diag_w = cw == rw
    zeros_cw = jnp.zeros((C, W), F32)

    def bdot(a, b):
        return jnp.dot(a.astype(BF16), b.astype(BF16), preferred_element_type=F32)

    def pick(x0, x1):
        return jnp.where(head0, x0, x1)

    chunks = range(n_chunk)
    heads = range(2)
    rows = lambda t, c: t[c * C:(c + 1) * C]
    mine = (head0, jnp.logical_not(head0))

    r = r_ref[0]
    k = k_ref[0]
    v = v_ref[0]
    kk = kk_ref[0]
    kk = kk / jnp.maximum(jnp.sqrt(_split_dot(kk * kk, ones_bd)), 1e-12)
    b = kk * a_ref[0]
    bonus = _split_dot(r * k * rk_ref[...], ones_bd) * v
    log_w = -jnp.exp(w_ref[0])
    hi = log_w.astype(BF16)
    lo = (log_w - hi.astype(F32)).astype(BF16)
    hilo = jnp.concatenate([hi, lo], axis=1)
    cum2 = [jnp.dot(tri, rows(hilo, c), preferred_element_type=F32) for c in chunks]
    cum = jnp.concatenate([c2[:, :W] + c2[:, W:] for c2 in cum2], axis=0)
    g_in = jnp.exp(cum)
    g_inv = jnp.exp(-cum)
    a_t = -kk * jnp.exp(cum - log_w)
    r_t = r * g_in
    b_t = b * g_inv
    k_t = k * g_inv
    g_last = [g_in[c * C + C - 1:(c + 1) * C, :] for c in chunks]
    bk = [jnp.concatenate([rows(b_t, c), rows(k_t, c)], axis=0) for c in chunks]
    bk_b = [t.astype(BF16) for t in bk]
    zv = [jnp.concatenate([zeros_cw, rows(v, c)], axis=0).astype(BF16) for c in chunks]
    ar = [[jnp.concatenate([jnp.where(mine[h], rows(a_t, c), 0.0), jnp.where(mine[h], rows(r_t, c), 0.0)],
                           axis=0).astype(BF16) for h in heads] for c in chunks]
    gram = [[lax.dot_general(ar[c][h], bk_b[c], NT_DIMS, preferred_element_type=F32) for h in heads]
            for c in chunks]
    top = [[jnp.where(strict2, gram[c][h][:C], 0.0) for h in heads] for c in chunks]
    g_top = [[top[c][h].astype(BF16) for h in heads] for c in chunks]
    g_bot = [[jnp.where(lower2, gram[c][h][C:], 0.0).astype(BF16) for h in heads] for c in chunks]
    n1 = [[top[c][h][:, :C] for h in heads] for c in chunks]
    inv = [[eye + n1[c][h] for h in heads] for c in chunks]
    pw = [[bdot(n1[c][h], n1[c][h]) for h in heads] for c in chunks]
    levels = C.bit_length() - 2
    for lvl in range(levels):
        if lvl < levels - 1:
            st = [[bdot(jnp.concatenate([pw[c][h], inv[c][h]], axis=0), pw[c][h]) for h in heads] for c in chunks]
            pw = [[st[c][h][:C] for h in heads] for c in chunks]
            inv = [[inv[c][h] + st[c][h][C:] for h in heads] for c in chunks]
        else:
            inv = [[inv[c][h] + bdot(inv[c][h], pw[c][h]) for h in heads] for c in chunks]
    t_inv = [[inv[c][h].astype(BF16) for h in heads] for c in chunks]
    xs = [[jnp.dot(g_top[c][h], zv[c], preferred_element_type=F32) for h in heads] for c in chunks]
    ax = [jnp.concatenate([rows(a_t, c), pick(*xs[c])], axis=1).astype(BF16) for c in chunks]
    pp = [[jnp.dot(t_inv[c][h], ax[c], preferred_element_type=F32) for h in heads] for c in chunks]
    rhs = []
    for c in chunks:
        p1 = pick(pp[c][0][:, :W], pp[c][1][:, :W])
        p2 = pick(pp[c][0][:, W:], pp[c][1][:, W:])
        rhs.append(jnp.concatenate([jnp.concatenate([p1, p2], axis=1),
                                    jnp.concatenate([zeros_cw, rows(v, c)], axis=1)], axis=0).astype(BF16))
    qq = [[jnp.dot(g_bot[c][h], rhs[c], preferred_element_type=F32) for h in heads] for c in chunks]
    mm = [lax.dot_general((bk[c] * g_last[c]).astype(BF16), rhs[c], TN_DIMS, preferred_element_type=F32)
          for c in chunks]
    q1 = [rows(r_t, c) + pick(qq[c][0][:, :W], qq[c][1][:, :W]) for c in chunks]
    q2 = [pick(qq[c][0][:, W:], qq[c][1][:, W:]) for c in chunks]
    m1 = [(jnp.where(same_head, mm[c][:, :W], 0.0)
           + jnp.where(diag_w, jnp.broadcast_to(g_last[c], (W, W)), 0.0)).astype(BF16) for c in chunks]
    m2 = [jnp.where(same_head, mm[c][:, W:], 0.0) for c in chunks]
    h_in = [state[...]]
    for c in chunks:
        h_in.append(jnp.dot(m1[c], h_in[c].astype(BF16), preferred_element_type=F32) + m2[c])
    state[...] = h_in[n_chunk]
    y = jnp.concatenate([bdot(q1[c], h_in[c]) + q2[c] for c in chunks], axis=0)
    mean = _split_dot(y, ones_bd) * (1.0 / HEAD_DIM)
    yc = y - mean
    var = _split_dot(yc * yc, ones_bd) * (1.0 / HEAD_DIM)
    yn = yc * lax.rsqrt(var + RWKV_GN_EPS) * lnw_ref[...] + lnb_ref[...]
    o_ref[0] = yn + bonus


def _wkv_call(r, w_log, k, v, kk, a, ln_w, ln_b, r_k, tc=512):
    batch, seq, _ = r.shape
    n_pair = D_MODEL // LANES
    tok = pl.BlockSpec((1, tc, LANES), lambda b, h, t: (b, t, h))
    vec = pl.BlockSpec((1, LANES), lambda b, h, t: (0, h))
    return pl.pallas_call(
        _wkv_body,
        grid=(batch, n_pair, seq // tc),
        in_specs=[tok] * 6 + [vec] * 3,
        out_specs=tok,
        out_shape=jax.ShapeDtypeStruct(r.shape, F32),
        scratch_shapes=[pltpu.VMEM((LANES, LANES), F32)],
        compiler_params=_params("arbitrary", "arbitrary", "arbitrary"),
        name="wkv7_chunked",
    )(r, w_log, k, v, kk, a, ln_w, ln_b, r_k)


def _rwkv_post_body(y_ref, g_ref, x_ref, mod_ref, wo_ref, o_ref):
    yg = (y_ref[0] * g_ref[0]).astype(BF16)
    o_ref[0] = x_ref[0] + mod_ref[0, 2:3, :] * jnp.dot(yg, wo_ref[...], preferred_element_type=F32)


def _rwkv_post_call(y, g, x, mod, w_o, ts=512):
    batch, seq, _ = x.shape
    tok = pl.BlockSpec((1, ts, D_MODEL), lambda b, i: (b, i, 0))
    return pl.pallas_call(
        _rwkv_post_body,
        grid=(batch, seq // ts),
        in_specs=[tok, tok, tok, pl.BlockSpec((1, 3, D_MODEL), lambda b, i: (b, 0, 0)),
                  pl.BlockSpec((D_MODEL, D_MODEL), lambda b, i: (0, 0))],
        out_specs=tok,
        out_shape=jax.ShapeDtypeStruct(x.shape, F32),
        compiler_params=_params("arbitrary", "arbitrary"),
        name="rwkv_out_proj",
    )(y, g, x, mod, w_o)


def _moe_body(x_ref, mod_ref, rw_ref, rb_ref, w13_ref, w2_ref, o_ref, h_scr, gate_scr, acc):
    e = pl.program_id(2)
    tm = x_ref.shape[1]

    @pl.when(e == 0)
    def _():
        h = _norm_mod(x_ref[0], mod_ref)
        h_scr[...] = h.astype(BF16)
        logits = lax.dot_general(rw_ref[...], h, NT_DIMS, preferred_element_type=F32, precision=HIGHEST)
        s = jax.nn.sigmoid(logits)
        sel = s + rb_ref[...]
        n_g, epg = N_EXPERT_GROUPS, EXPERTS_PER_GROUP
        s_rows = [s[i:i + 1, :] for i in range(N_EXPERTS)]
        sel_rows = [sel[i:i + 1, :] for i in range(N_EXPERTS)]
        scores = []
        for g in range(n_g):
            grp = sel_rows[g * epg:(g + 1) * epg]
            best = None
            for i in range(epg):
                for j in range(i + 1, epg):
                    pair = grp[i] + grp[j]
                    best = pair if best is None else jnp.maximum(best, pair)
            scores.append(best)
        g_idx = jnp.zeros_like(scores[0], dtype=jnp.int32)
        top = scores[0]
        for g in range(1, n_g):
            better = scores[g] > top
            g_idx = jnp.where(better, g, g_idx)
            top = jnp.where(better, scores[g], top)

        def pick(rows_, j):
            out = rows_[j]
            for g in range(1, n_g):
                out = jnp.where(g_idx == g, rows_[g * epg + j], out)
            return out

        in_sel = [pick(sel_rows, j) for j in range(epg)]
        in_s = [pick(s_rows, j) for j in range(epg)]
        chosen = []
        for j in range(epg):
            rank = jnp.zeros_like(g_idx)
            for i in range(epg):
                if i == j:
                    continue
                ahead = (in_sel[i] >= in_sel[j]) if i < j else (in_sel[i] > in_sel[j])
                rank = rank + jnp.where(ahead, 1, 0)
            chosen.append(rank < 2)
        den = sum(jnp.where(chosen[j], in_s[j], 0.0) for j in range(epg))
        gate_rows = []
        for ex in range(N_EXPERTS):
            g, j = divmod(ex, epg)
            on = jnp.logical_and(chosen[j], g_idx == g)
            gate_rows.append(jnp.where(on, in_s[j] / den, 0.0))
        pad = jnp.zeros((LANES - N_EXPERTS, tm), F32)
        gate_t = jnp.concatenate(gate_rows + [pad], axis=0).T
        for ex in range(N_EXPERTS):
            gate_scr[ex] = gate_t[:, ex:ex + 1]
        acc[...] = jnp.zeros_like(acc)

    hid = jnp.dot(h_scr[...], w13_ref[0, 0], preferred_element_type=F32)
    h1 = hid[:, :D_EXPERT]
    act = h1 * jax.nn.sigmoid(h1) * hid[:, D_EXPERT:] * gate_scr[e]
    acc[...] += jnp.dot(act.astype(BF16), w2_ref[0, 0], preferred_element_type=F32)

    @pl.when(e == N_EXPERTS - 1)
    def _():
        o_ref[0] = x_ref[0] + mod_ref[0, 2:3, :] * acc[...]


def _moe_call(x, mod, router_wt, router_b, w13, w2, layer, tm=1024):
    batch, seq, _ = x.shape
    tok = pl.BlockSpec((1, tm, D_MODEL), lambda b, i, e: (b, i, 0))
    return pl.pallas_call(
        _moe_body,
        grid=(batch, seq // tm, N_EXPERTS),
        in_specs=[
            tok,
            pl.BlockSpec((1, 3, D_MODEL), lambda b, i, e: (b, 0, 0)),
            pl.BlockSpec((N_EXPERTS, D_MODEL), lambda b, i, e: (0, 0)),
            pl.BlockSpec((N_EXPERTS, 1), lambda b, i, e: (0, 0)),
            pl.BlockSpec((1, 1, D_MODEL, 2 * D_EXPERT), lambda b, i, e: (layer, e, 0, 0)),
            pl.BlockSpec((1, 1, D_EXPERT, D_MODEL), lambda b, i, e: (layer, e, 0, 0)),
        ],
        out_specs=tok,
        out_shape=jax.ShapeDtypeStruct(x.shape, F32),
        scratch_shapes=[
            pltpu.VMEM((tm, D_MODEL), BF16),
            pltpu.VMEM((N_EXPERTS, tm, 1), F32),
            pltpu.VMEM((tm, D_MODEL), F32),
        ],
        compiler_params=_params("arbitrary", "arbitrary", "arbitrary"),
        name="moe",
    )(x, mod, router_wt, router_b, w13, w2)


def _final_body(x_ref, g_ref, o_ref):
    x = x_ref[0]
    o_ref[0] = x * lax.rsqrt(jnp.mean(x * x, axis=-1, keepdims=True) + RMS_EPS) * g_ref[...]


def _final_call(x, g, ts=1024):
    batch, seq, _ = x.shape
    tok = pl.BlockSpec((1, ts, D_MODEL), lambda b, i: (b, i, 0))
    return pl.pallas_call(
        _final_body,
        grid=(batch, seq // ts),
        in_specs=[tok, pl.BlockSpec((1, D_MODEL), lambda b, i: (0, 0))],
        out_specs=tok,
        out_shape=jax.ShapeDtypeStruct(x.shape, F32),
        compiler_params=_params("arbitrary", "arbitrary"),
        name="final_norm",
    )(x, g)


def _pad_cols(w):
    n = w.shape[-1]
    return jnp.pad(w, ((0, 0), (0, -n % LANES)))


def _pad_rows(w):
    n = w.shape[0]
    return jnp.pad(w, ((0, -n % LANES), (0, 0)))


def kernel(x, c, ada_w, ada_b, mix_w_in, mix_w_out, sgu_norm_g, sgu_w, sgu_b, rwkv_mu, rwkv_w_rkv, rwkv_w_o, rwkv_w0, rwkv_w1, rwkv_w2, rwkv_a0, rwkv_a1, rwkv_a2, rwkv_v0, rwkv_v1, rwkv_v2, rwkv_g1, rwkv_g2, rwkv_k_k, rwkv_k_a, rwkv_r_k, rwkv_ln_w, rwkv_ln_b, router_w, router_b, moe_w1, moe_w3, moe_w2, final_norm_g):
    depth = ada_w.shape[0]
    row = lambda t: t.reshape(1, -1)
    mods = _ada_call(c, ada_w, ada_b)
    w13 = jnp.concatenate([moe_w1, moe_w3], axis=-1).astype(BF16)
    w2 = moe_w2.astype(BF16)
    router_wt = router_w.T
    router_bc = router_b.reshape(N_EXPERTS, 1)
    v_first = None
    for layer in range(depth):
        i = layer // 2
        mod = mods[2 * layer]
        if layer % 2 == 0:
            proj = _proj_call(x, mod, mix_w_in[i].astype(BF16))
            o_a = _attn_call(proj)
            bias_tile = jnp.repeat(sgu_b[i].T, HEAD_DIM, axis=1)
            x = _sgu_out_call(proj, o_a, x, mod, row(sgu_norm_g[i]), sgu_w[i], bias_tile,
                              mix_w_out[i].astype(BF16))
        else:
            weights = [
                rwkv_mu[i],
                rwkv_w_rkv[i, 0].astype(BF16), rwkv_w_rkv[i, 1].astype(BF16), rwkv_w_rkv[i, 2].astype(BF16),
                row(rwkv_w0[i]), _pad_cols(rwkv_w1[i]).astype(BF16), _pad_rows(rwkv_w2[i]).astype(BF16),
                row(rwkv_a0[i]), _pad_cols(rwkv_a1[i]).astype(BF16), _pad_rows(rwkv_a2[i]).astype(BF16),
                _pad_cols(rwkv_g1[i]).astype(BF16), _pad_rows(rwkv_g2[i]).astype(BF16),
                row(rwkv_k_k[i]), row(rwkv_k_a[i]),
            ]
            vres = None
            if i > 0:
                vres = [row(rwkv_v0[i - 1]), _pad_cols(rwkv_v1[i - 1]).astype(BF16),
                        _pad_rows(rwkv_v2[i - 1]).astype(BF16)]
            r, w_log, k, v, kk, a, g = _rwkv_pre_call(x, mod, weights, v_first, vres)
            if i == 0:
                v_first = v
            y = _wkv_call(r, w_log, k, v, kk, a, row(rwkv_ln_w[i]), row(rwkv_ln_b[i]), row(rwkv_r_k[i]))
            x = _rwkv_post_call(y, g, x, mod, rwkv_w_o[i].astype(BF16))
        x = _moe_call(x, mods[2 * layer + 1], router_wt, router_bc, w13, w2, layer)
    return _final_call(x, row(final_norm_g))
```

```python
import functools

import jax
import jax.numpy as jnp
from jax import lax
from jax.experimental import pallas as pl
from jax.experimental.pallas import tpu as pltpu

F32 = jnp.float32
BF16 = jnp.bfloat16
HIGHEST = lax.Precision.HIGHEST

D_MODEL = 1024
HEAD_DIM = 64
WIDTH_A = 512
WIDTH_B = 512
MIX_IN_WIDTH = 2560
ATTN_SPAN = 128
ATTN_DILATIONS = (1, 4, 16)
SGU_CHUNK = 128
N_GROUPS_B = 8
RWKV_HEADS = 16
RWKV_GN_EPS = 64e-5
WKV_CHUNK = 64
N_EXPERTS = 16
N_EXPERT_GROUPS = 4
EXPERTS_PER_GROUP = 4
D_EXPERT = 256
RMS_EPS = 1e-6
LANES = 128
VMEM_LIMIT = 56 * 1024 * 1024

NT_DIMS = (((1,), (1,)), ((), ()))
TN_DIMS = (((0,), (0,)), ((), ()))


def _params(*sem):
    return pltpu.CompilerParams(dimension_semantics=sem, vmem_limit_bytes=VMEM_LIMIT)


def _norm_mod(x, mod_ref):
    ms = jnp.mean(x * x, axis=-1, keepdims=True)
    return (x * lax.rsqrt(ms + RMS_EPS)) * (1.0 + mod_ref[0, 1:2, :]) + mod_ref[0, 0:1, :]


def _gelu(x):
    return 0.5 * x * (1.0 + lax.erf(x * 0.7071067811865476))


def _ada_body(c_ref, w_ref, b_ref, o_ref):
    c = c_ref[...]
    c_act = c * jax.nn.sigmoid(c)
    o_ref[0] = jnp.dot(c_act, w_ref[0], preferred_element_type=F32, precision=HIGHEST) + b_ref[0]


def _ada_call(c, ada_w, ada_b):
    n_pair = ada_w.shape[0] * ada_w.shape[1]
    batch = c.shape[0]
    w = ada_w.reshape(n_pair, D_MODEL, 3 * D_MODEL)
    b = ada_b.reshape(n_pair, 1, 3 * D_MODEL)
    out = pl.pallas_call(
        _ada_body,
        grid=(n_pair, 3),
        in_specs=[
            pl.BlockSpec((batch, D_MODEL), lambda p, j: (0, 0)),
            pl.BlockSpec((1, D_MODEL, D_MODEL), lambda p, j: (p, 0, j)),
            pl.BlockSpec((1, 1, D_MODEL), lambda p, j: (p, 0, j)),
        ],
        out_specs=pl.BlockSpec((1, batch, D_MODEL), lambda p, j: (p, 0, j)),
        out_shape=jax.ShapeDtypeStruct((n_pair, batch, 3 * D_MODEL), F32),
        compiler_params=_params("arbitrary", "arbitrary"),
        name="ada_mod",
    )(c, w, b)
    return out.reshape(n_pair, batch, 3, D_MODEL)


def _proj_body(x_ref, mod_ref, w_ref, o_ref):
    h = _norm_mod(x_ref[0], mod_ref)
    o_ref[0] = jnp.dot(h.astype(BF16), w_ref[...], preferred_element_type=F32)


def _proj_call(x, mod, w_in, ts=512):
    batch, seq, _ = x.shape
    return pl.pallas_call(
        _proj_body,
        grid=(batch, seq // ts),
        in_specs=[
            pl.BlockSpec((1, ts, D_MODEL), lambda b, i: (b, i, 0)),
            pl.BlockSpec((1, 3, D_MODEL), lambda b, i: (b, 0, 0)),
            pl.BlockSpec((D_MODEL, MIX_IN_WIDTH), lambda b, i: (0, 0)),
        ],
        out_specs=pl.BlockSpec((1, ts, MIX_IN_WIDTH), lambda b, i: (b, i, 0)),
        out_shape=jax.ShapeDtypeStruct((batch, seq, MIX_IN_WIDTH), F32),
        compiler_params=_params("arbitrary", "arbitrary"),
        name="mix_in_proj",
    )(x, mod, w_in)


def _attn_body(q_ref, k_ref, v_ref, o_ref, ob, lb):
    span = ATTN_SPAN
    W = LANES
    block_group = 4

    def iota(shape, dim):
        return lax.broadcasted_iota(jnp.int32, shape, dim)

    head0 = iota((span, W), 1) < HEAD_DIM
    mine = (head0, jnp.logical_not(head0))
    row2, col2 = iota((span, 2 * span), 0), iota((span, 2 * span), 1)
    mask_prev_cur = jnp.where(col2 < span, col2 - row2, row2 - (col2 - span)) >= 0
    mask_cur = iota((span, span), 1) <= iota((span, span), 0)

    def rows(start, n_rows, dil):
        if dil == 1:
            return pl.ds(pl.multiple_of(start, span), n_rows)
        return pl.ds(start, n_rows, stride=dil)

    def group(p, dil, q_starts, with_prev):
        blocks = range(len(q_starts))
        heads = range(2)
        qb, kcat, vaug, masks = [], [], [], []
        for j in blocks:
            n_keys = 2 * span if with_prev[j] else span
            k_start = q_starts[j] - span * dil if with_prev[j] else q_starts[j]
            qb.append(q_ref[0, rows(q_starts[j], span, dil), :] * (HEAD_DIM ** -0.5))
            kcat.append(k_ref[0, rows(k_start, n_keys, dil), :].astype(BF16))
            vb = v_ref[0, rows(k_start, n_keys, dil), :].astype(BF16)
            vaug.append(jnp.concatenate([vb, jnp.ones((n_keys, W), BF16)], axis=1))
            masks.append(mask_prev_cur if with_prev[j] else mask_cur)
        qh = [[jnp.where(mine[h], qb[j], 0.0).astype(BF16) for h in heads] for j in blocks]
        s = [[lax.dot_general(qh[j][h], kcat[j], NT_DIMS, preferred_element_type=F32) for h in heads] for j in blocks]
        s = [[jnp.where(masks[j], s[j][h], -jnp.inf) for h in heads] for j in blocks]
        m = [[jnp.max(s[j][h], axis=-1, keepdims=True) for h in heads] for j in blocks]
        pr = [[jnp.exp(s[j][h] - m[j][h]).astype(BF16) for h in heads] for j in blocks]
        ad = [[jnp.dot(pr[j][h], vaug[j], preferred_element_type=F32) for h in heads] for j in blocks]
        for j in blocks:
            den = [ad[j][h][:, W:] for h in heads]
            o = jnp.where(head0, ad[j][0][:, :W] / den[0], ad[j][1][:, :W] / den[1])
            lse = jnp.where(head0, m[j][0] + jnp.log(den[0]), m[j][1] + jnp.log(den[1]))
            qi = rows(q_starts[j], span, dil)
            ob[p, qi, :] = o
            lb[p, qi, :] = lse

    seq = q_ref.shape[1]
    for p, dil in enumerate(ATTN_DILATIONS):
        n_blk = seq // dil // span
        if n_blk == 1:
            def step(g, carry, p=p, dil=dil):
                group(p, dil, [g * block_group + j for j in range(block_group)], [False] * block_group)
                return carry
            lax.fori_loop(0, dil // block_group, step, 0)
        else:
            per_seq = n_blk // block_group

            def first(r, carry, p=p, dil=dil):
                group(p, dil, [r + j * span * dil for j in range(block_group)],
                      [False] + [True] * (block_group - 1))
                return carry

            def later(idx, carry, p=p, dil=dil, per_seq=per_seq):
                r = idx // (per_seq - 1)
                g = idx % (per_seq - 1) + 1
                group(p, dil, [r + (g * block_group + j) * span * dil for j in range(block_group)],
                      [True] * block_group)
                return carry

            lax.fori_loop(0, dil, first, 0)
            if per_seq > 1:
                lax.fori_loop(0, dil * (per_seq - 1), later, 0)

    l0, l1, l2 = lb[0], lb[1], lb[2]
    m = jnp.maximum(jnp.maximum(l0, l1), l2)
    w0, w1, w2 = jnp.exp(l0 - m), jnp.exp(l1 - m), jnp.exp(l2 - m)
    o_ref[0] = (w0 * ob[0] + w1 * ob[1] + w2 * ob[2]) / (w0 + w1 + w2)


def _attn_call(proj):
    batch, seq, _ = proj.shape
    n_pair = WIDTH_A // LANES
    blk = lambda off: pl.BlockSpec((1, seq, LANES), lambda b, h: (b, 0, off + h))
    return pl.pallas_call(
        _attn_body,
        grid=(batch, n_pair),
        in_specs=[blk(0), blk(n_pair), blk(2 * n_pair)],
        out_specs=pl.BlockSpec((1, seq, LANES), lambda b, h: (b, 0, h)),
        out_shape=jax.ShapeDtypeStruct((batch, seq, WIDTH_A), F32),
        scratch_shapes=[
            pltpu.VMEM((3, seq, LANES), F32),
            pltpu.VMEM((3, seq, LANES), F32),
        ],
        compiler_params=_params("arbitrary", "arbitrary"),
        name="dilated_attn",
    )(proj, proj, proj)


def _sgu_out_body(u_ref, z_ref, oa_ref, x_ref, mod_ref, ng_ref, sw_ref, sb_ref, wo_ref, o_ref, mix):
    ts = u_ref.shape[1]
    u = _gelu(u_ref[0])
    z = _gelu(z_ref[0])
    z = z * lax.rsqrt(jnp.mean(z * z, axis=-1, keepdims=True) + RMS_EPS) * ng_ref[...]
    zb = z.astype(BF16)
    row = lax.broadcasted_iota(jnp.int32, (SGU_CHUNK, SGU_CHUNK), 0)
    col = lax.broadcasted_iota(jnp.int32, (SGU_CHUNK, SGU_CHUNK), 1)
    causal = col <= row
    for g in range(N_GROUPS_B):
        w_g = jnp.where(causal, sw_ref[g], 0.0).astype(BF16)
        for cc in range(ts // SGU_CHUNK):
            rs = slice(cc * SGU_CHUNK, (cc + 1) * SGU_CHUNK)
            ls = slice(g * HEAD_DIM, (g + 1) * HEAD_DIM)
            mix[rs, ls] = jnp.dot(w_g, zb[rs, ls], preferred_element_type=F32)
    bias = jnp.concatenate([sb_ref[...]] * (ts // SGU_CHUNK), axis=0)
    o_b = u * (mix[...] + bias)
    y = jnp.dot(oa_ref[0].astype(BF16), wo_ref[0:WIDTH_A, :], preferred_element_type=F32)
    y = y + jnp.dot(o_b.astype(BF16), wo_ref[WIDTH_A:, :], preferred_element_type=F32)
    o_ref[0] = x_ref[0] + mod_ref[0, 2:3, :] * y


def _sgu_out_call(proj, o_a, x, mod, norm_g, sgu_w, sgu_bias_tile, w_out, ts=512):
    batch, seq, _ = x.shape
    u_blk = 3 * WIDTH_A // WIDTH_B
    full = lambda shape: pl.BlockSpec(shape, lambda b, i: (0,) * len(shape))
    return pl.pallas_call(
        _sgu_out_body,
        grid=(batch, seq // ts),
        in_specs=[
            pl.BlockSpec((1, ts, WIDTH_B), lambda b, i: (b, i, u_blk)),
            pl.BlockSpec((1, ts, WIDTH_B), lambda b, i: (b, i, u_blk + 1)),
            pl.BlockSpec((1, ts, WIDTH_A), lambda b, i: (b, i, 0)),
            pl.BlockSpec((1, ts, D_MODEL), lambda b, i: (b, i, 0)),
            pl.BlockSpec((1, 3, D_MODEL), lambda b, i: (b, 0, 0)),
            full((1, WIDTH_B)),
            full((N_GROUPS_B, SGU_CHUNK, SGU_CHUNK)),
            full((SGU_CHUNK, WIDTH_B)),
            full((WIDTH_A + WIDTH_B, D_MODEL)),
        ],
        out_specs=pl.BlockSpec((1, ts, D_MODEL), lambda b, i: (b, i, 0)),
        out_shape=jax.ShapeDtypeStruct(x.shape, F32),
        scratch_shapes=[pltpu.VMEM((ts, WIDTH_B), F32)],
        compiler_params=_params("arbitrary", "arbitrary"),
        name="sgu_out_proj",
    )(proj, proj, o_a, x, mod, norm_g, sgu_w, sgu_bias_tile, w_out)


def _rwkv_pre_body(has_vres, *refs):
    if has_vres:
        (x_ref, mod_ref, mu_ref, wr_ref, wk_ref, wv_ref, w0_ref, w1_ref, w2_ref, a0_ref, a1_ref, a2_ref,
         g1_ref, g2_ref, kk_ref, ka_ref, vf_ref, v0_ref, v1_ref, v2_ref,
         r_out, w_out, k_out, v_out, kk_out, a_out, g_out, carry) = refs
    else:
        (x_ref, mod_ref, mu_ref, wr_ref, wk_ref, wv_ref, w0_ref, w1_ref, w2_ref, a0_ref, a1_ref, a2_ref,
         g1_ref, g2_ref, kk_ref, ka_ref,
         r_out, w_out, k_out, v_out, kk_out, a_out, g_out, carry) = refs

    @pl.when(pl.program_id(1) == 0)
    def _():
        carry[...] = jnp.zeros_like(carry)

    h = _norm_mod(x_ref[0], mod_ref)
    ts = h.shape[0]
    first = lax.broadcasted_iota(jnp.int32, h.shape, 0) == 0
    h_prev = jnp.where(first, carry[0:1, :], pltpu.roll(h, 1, 0))
    carry[0:1, :] = h[ts - 1:ts, :]
    xx = h_prev - h

    def mixed(i):
        return (h + xx * mu_ref[i:i + 1, :]).astype(BF16)

    def mm(a, w_ref):
        return jnp.dot(a, w_ref[...], preferred_element_type=F32)

    xr, xw, xk, xv, xa, xg = [mixed(i) for i in range(6)]
    r = mm(xr, wr_ref)
    k = mm(xk, wk_ref)
    v = mm(xv, wv_ref)
    z = w0_ref[...] + mm(jnp.tanh(mm(xw, w1_ref)).astype(BF16), w2_ref)
    w_log = -(jnp.maximum(-z, 0.0) + jnp.log(1.0 + jnp.exp(-jnp.abs(z)))) - 0.5
    if has_vres:
        mix_v = jax.nn.sigmoid(v0_ref[...] + mm(mm(xv, v1_ref).astype(BF16), v2_ref))
        v = v + (vf_ref[0] - v) * mix_v
    a = jax.nn.sigmoid(a0_ref[...] + mm(mm(xa, a1_ref).astype(BF16), a2_ref))
    g = mm(jax.nn.sigmoid(mm(xg, g1_ref)).astype(BF16), g2_ref)
    r_out[0] = r
    w_out[0] = w_log
    kk_out[0] = k * kk_ref[...]
    k_out[0] = k * (1.0 + (a - 1.0) * ka_ref[...])
    v_out[0] = v
    a_out[0] = a
    g_out[0] = g


def _rwkv_pre_call(x, mod, weights, v_first, vres, ts=256):
    batch, seq, _ = x.shape
    tok = pl.BlockSpec((1, ts, D_MODEL), lambda b, i: (b, i, 0))
    full = lambda a: pl.BlockSpec(a.shape, lambda b, i: (0,) * a.ndim)
    ins = [x, mod] + list(weights)
    specs = [tok, pl.BlockSpec((1, 3, D_MODEL), lambda b, i: (b, 0, 0))] + [full(a) for a in weights]
    if vres is not None:
        ins += [v_first] + list(vres)
        specs += [tok] + [full(a) for a in vres]
    out_sds = jax.ShapeDtypeStruct(x.shape, F32)
    return pl.pallas_call(
        functools.partial(_rwkv_pre_body, vres is not None),
        grid=(batch, seq // ts),
        in_specs=specs,
        out_specs=[tok] * 7,
        out_shape=[out_sds] * 7,
        scratch_shapes=[pltpu.VMEM((8, D_MODEL), F32)],
        compiler_params=_params("arbitrary", "arbitrary"),
        name="rwkv_pre",
    )(*ins)


def _split_dot(x, ones):
    hi = x.astype(BF16)
    lo = (x - hi.astype(F32)).astype(BF16)
    rows = x.shape[0]
    both = jnp.dot(jnp.concatenate([hi, lo], axis=0), ones, preferred_element_type=F32)
    return both[:rows] + both[rows:]


def _wkv_body(r_ref, w_ref, k_ref, v_ref, kk_ref, a_ref, lnw_ref, lnb_ref, rk_ref, o_ref, state):
    C = WKV_CHUNK
    W = LANES
    n_chunk = r_ref.shape[1] // C

    @pl.when(pl.program_id(2) == 0)
    def _():
        state[...] = jnp.zeros_like(state)

    def iota(shape, dim):
        return lax.broadcasted_iota(jnp.int32, shape, dim)

    head0 = iota((C, W), 1) < HEAD_DIM
    row2, col2 = iota((C, 2 * C), 0), iota((C, 2 * C), 1) % C
    strict2 = col2 < row2
    lower2 = col2 <= row2
    tri = jnp.where(iota((C, C), 1) <= iota((C, C), 0), 1.0, 0.0).astype(BF16)
    eye = jnp.where(iota((C, C), 1) == iota((C, C), 0), 1.0, 0.0).astype(F32)
    rw, cw = iota((W, W), 0), iota((W, W), 1)
    same_head = (rw < HEAD_DIM) == (cw < HEAD_DIM)
    ones_bd = jnp.where(same_head, 1.0, 0.0).astype(BF16)
    diag_w = cw == rw
    zeros_cw = jnp.zeros((C, W), F32)

    def bdot(a, b):
        return jnp.dot(a.astype(BF16), b.astype(BF16), preferred_element_type=F32)

    def pick(x0, x1):
        return jnp.where(head0, x0, x1)

    chunks = range(n_chunk)
    heads = range(2)
    rows = lambda t, c: t[c * C:(c + 1) * C]
    mine = (head0, jnp.logical_not(head0))

    r = r_ref[0]
    k = k_ref[0]
    v = v_ref[0]
    kk = kk_ref[0]
    kk = kk / jnp.maximum(jnp.sqrt(_split_dot(kk * kk, ones_bd)), 1e-12)
    b = kk * a_ref[0]
    bonus = _split_dot(r * k * rk_ref[...], ones_bd) * v
    log_w = -jnp.exp(w_ref[0])
    hi = log_w.astype(BF16)
    lo = (log_w - hi.astype(F32)).astype(BF16)
    hilo = jnp.concatenate([hi, lo], axis=1)
    cum2 = [jnp.dot(tri, rows(hilo, c), preferred_element_type=F32) for c in chunks]
    cum = jnp.concatenate([c2[:, :W] + c2[:, W:] for c2 in cum2], axis=0)
    g_in = jnp.exp(cum)
    g_inv = jnp.exp(-cum)
    a_t = -kk * jnp.exp(cum - log_w)
    r_t = r * g_in
    b_t = b * g_inv
    k_t = k * g_inv
    g_last = [g_in[c * C + C - 1:(c + 1) * C, :] for c in chunks]
    bk = [jnp.concatenate([rows(b_t, c), rows(k_t, c)], axis=0) for c in chunks]
    bk_b = [t.astype(BF16) for t in bk]
    zv = [jnp.concatenate([zeros_cw, rows(v, c)], axis=0).astype(BF16) for c in chunks]
    ar = [[jnp.concatenate([jnp.where(mine[h], rows(a_t, c), 0.0), jnp.where(mine[h], rows(r_t, c), 0.0)],
                           axis=0).astype(BF16) for h in heads] for c in chunks]
    gram = [[lax.dot_general(ar[c][h], bk_b[c], NT_DIMS, preferred_element_type=F32) for h in heads]
            for c in chunks]
    top = [[jnp.where(strict2, gram[c][h][:C], 0.0) for h in heads] for c in chunks]
    g_top = [[top[c][h].astype(BF16) for h in heads] for c in chunks]
    g_bot = [[jnp.where(lower2, gram[c][h][C:], 0.0).astype(BF16) for h in heads] for c in chunks]
    n1 = [[top[c][h][:, :C] for h in heads] for c in chunks]
    inv = [[eye + n1[c][h] for h in heads] for c in chunks]
    pw = [[bdot(n1[c][h], n1[c][h]) for h in heads] for c in chunks]
    levels = C.bit_length() - 2
    for lvl in range(levels):
        if lvl < levels - 1:
            st = [[bdot(jnp.concatenate([pw[c][h], inv[c][h]], axis=0), pw[c][h]) for h in heads] for c in chunks]
            pw = [[st[c][h][:C] for h in heads] for c in chunks]
            inv = [[inv[c][h] + st[c][h][C:] for h in heads] for c in chunks]
        else:
            inv = [[inv[c][h] + bdot(inv[c][h], pw[c][h]) for h in heads] for c in chunks]
    t_inv = [[inv[c][h].astype(BF16) for h in heads] for c in chunks]
    xs = [[jnp.dot(g_top[c][h], zv[c], preferred_element_type=F32) for h in heads] for c in chunks]
    ax = [jnp.concatenate([rows(a_t, c), pick(*xs[c])], axis=1).astype(BF16) for c in chunks]
    pp = [[jnp.dot(t_inv[c][h], ax[c], preferred_element_type=F32) for h in heads] for c in chunks]
    rhs = []
    for c in chunks:
        p1 = pick(pp[c][0][:, :W], pp[c][1][:, :W])
        p2 = pick(pp[c][0][:, W:], pp[c][1][:, W:])
        rhs.append(jnp.concatenate([jnp.concatenate([p1, p2], axis=1),
                                    jnp.concatenate([zeros_cw, rows(v, c)], axis=1)], axis=0).astype(BF16))
    qq = [[jnp.dot(g_bot[c][h], rhs[c], preferred_element_type=F32) for h in heads] for c in chunks]
    mm = [lax.dot_general((bk[c] * g_last[c]).astype(BF16), rhs[c], TN_DIMS, preferred_element_type=F32)
          for c in chunks]
    q1 = [rows(r_t, c) + pick(qq[c][0][:, :W], qq[c][1][:, :W]) for c in chunks]
    q2 = [pick(qq[c][0][:, W:], qq[c][1][:, W:]) for c in chunks]
    m1 = [(jnp.where(same_head, mm[c][:, :W], 0.0)
           + jnp.where(diag_w, jnp.broadcast_to(g_last[c], (W, W)), 0.0)).astype(BF16) for c in chunks]
    m2 = [jnp.where(same_head, mm[c][:, W:], 0.0) for c in chunks]
    h_in = [state[...]]
    for c in chunks:
        h_in.append(jnp.dot(m1[c], h_in[c].astype(BF16), preferred_element_type=F32) + m2[c])
    state[...] = h_in[n_chunk]
    y = jnp.concatenate([bdot(q1[c], h_in[c]) + q2[c] for c in chunks], axis=0)
    mean = _split_dot(y, ones_bd) * (1.0 / HEAD_DIM)
    yc = y - mean
    var = _split_dot(yc * yc, ones_bd) * (1.0 / HEAD_DIM)
    yn = yc * lax.rsqrt(var + RWKV_GN_EPS) * lnw_ref[...] + lnb_ref[...]
    o_ref[0] = yn + bonus


def _wkv_call(r, w_log, k, v, kk, a, ln_w, ln_b, r_k, tc=512):
    batch, seq, _ = r.shape
    n_pair = D_MODEL // LANES
    tok = pl.BlockSpec((1, tc, LANES), lambda b, h, t: (b, t, h))
    vec = pl.BlockSpec((1, LANES), lambda b, h, t: (0, h))
    return pl.pallas_call(
        _wkv_body,
        grid=(batch, n_pair, seq // tc),
        in_specs=[tok] * 6 + [vec] * 3,
        out_specs=tok,
        out_shape=jax.ShapeDtypeStruct(r.shape, F32),
        scratch_shapes=[pltpu.VMEM((LANES, LANES), F32)],
        compiler_params=_params("arbitrary", "arbitrary", "arbitrary"),
        name="wkv7_chunked",
    )(r, w_log, k, v, kk, a, ln_w, ln_b, r_k)


def _rwkv_post_body(y_ref, g_ref, x_ref, mod_ref, wo_ref, o_ref):
    yg = (y_ref[0] * g_ref[0]).astype(BF16)
    o_ref[0] = x_ref[0] + mod_ref[0, 2:3, :] * jnp.dot(yg, wo_ref[...], preferred_element_type=F32)


def _rwkv_post_call(y, g, x, mod, w_o, ts=512):
    batch, seq, _ = x.shape
    tok = pl.BlockSpec((1, ts, D_MODEL), lambda b, i: (b, i, 0))
    return pl.pallas_call(
        _rwkv_post_body,
        grid=(batch, seq // ts),
        in_specs=[tok, tok, tok, pl.BlockSpec((1, 3, D_MODEL), lambda b, i: (b, 0, 0)),
                  pl.BlockSpec((D_MODEL, D_MODEL), lambda b, i: (0, 0))],
        out_specs=tok,
        out_shape=jax.ShapeDtypeStruct(x.shape, F32),
        compiler_params=_params("arbitrary", "arbitrary"),
        name="rwkv_out_proj",
    )(y, g, x, mod, w_o)


def _moe_body(x_ref, mod_ref, rw_ref, rb_ref, w13_ref, w2_ref, o_ref, h_scr, gate_scr, acc):
    e = pl.program_id(2)
    tm = x_ref.shape[1]

    @pl.when(e == 0)
    def _():
        h = _norm_mod(x_ref[0], mod_ref)
        h_scr[...] = h.astype(BF16)
        logits = lax.dot_general(rw_ref[...], h, NT_DIMS, preferred_element_type=F32, precision=HIGHEST)
        s = jax.nn.sigmoid(logits)
        sel = s + rb_ref[...]
        n_g, epg = N_EXPERT_GROUPS, EXPERTS_PER_GROUP
        s_rows = [s[i:i + 1, :] for i in range(N_EXPERTS)]
        sel_rows = [sel[i:i + 1, :] for i in range(N_EXPERTS)]
        scores = []
        for g in range(n_g):
            grp = sel_rows[g * epg:(g + 1) * epg]
            best = None
            for i in range(epg):
                for j in range(i + 1, epg):
                    pair = grp[i] + grp[j]
                    best = pair if best is None else jnp.maximum(best, pair)
            scores.append(best)
        g_idx = jnp.zeros_like(scores[0], dtype=jnp.int32)
        top = scores[0]
        for g in range(1, n_g):
            better = scores[g] > top
            g_idx = jnp.where(better, g, g_idx)
            top = jnp.where(better, scores[g], top)

        def pick(rows_, j):
            out = rows_[j]
            for g in range(1, n_g):
                out = jnp.where(g_idx == g, rows_[g * epg + j], out)
            return out

        in_sel = [pick(sel_rows, j) for j in range(epg)]
        in_s = [pick(s_rows, j) for j in range(epg)]
        chosen = []
        for j in range(epg):
            rank = jnp.zeros_like(g_idx)
            for i in range(epg):
                if i == j:
                    continue
                ahead = (in_sel[i] >= in_sel[j]) if i < j else (in_sel[i] > in_sel[j])
                rank = rank + jnp.where(ahead, 1, 0)
            chosen.append(rank < 2)
        den = sum(jnp.where(chosen[j], in_s[j], 0.0) for j in range(epg))
        gate_rows = []
        for ex in range(N_EXPERTS):
            g, j = divmod(ex, epg)
            on = jnp.logical_and(chosen[j], g_idx == g)
            gate_rows.append(jnp.where(on, in_s[j] / den, 0.0))
        pad = jnp.zeros((LANES - N_EXPERTS, tm), F32)
        gate_t = jnp.concatenate(gate_rows + [pad], axis=0).T
        for ex in range(N_EXPERTS):
            gate_scr[ex] = gate_t[:, ex:ex + 1]
        acc[...] = jnp.zeros_like(acc)

    hid = jnp.dot(h_scr[...], w13_ref[0, 0], preferred_element_type=F32)
    h1 = hid[:, :D_EXPERT]
    act = h1 * jax.nn.sigmoid(h1) * hid[:, D_EXPERT:] * gate_scr[e]
    acc[...] += jnp.dot(act.astype(BF16), w2_ref[0, 0], preferred_element_type=F32)

    @pl.when(e == N_EXPERTS - 1)
    def _():
        o_ref[0] = x_ref[0] + mod_ref[0, 2:3, :] * acc[...]


def _moe_call(x, mod, router_wt, router_b, w13, w2, layer, tm=1024):
    batch, seq, _ = x.shape
    tok = pl.BlockSpec((1, tm, D_MODEL), lambda b, i, e: (b, i, 0))
    return pl.pallas_call(
        _moe_body,
        grid=(batch, seq // tm, N_EXPERTS),
        in_specs=[
            tok,
            pl.BlockSpec((1, 3, D_MODEL), lambda b, i, e: (b, 0, 0)),
            pl.BlockSpec((N_EXPERTS, D_MODEL), lambda b, i, e: (0, 0)),
            pl.BlockSpec((N_EXPERTS, 1), lambda b, i, e: (0, 0)),
            pl.BlockSpec((1, 1, D_MODEL, 2 * D_EXPERT), lambda b, i, e: (layer, e, 0, 0)),
            pl.BlockSpec((1, 1, D_EXPERT, D_MODEL), lambda b, i, e: (layer, e, 0, 0)),
        ],
        out_specs=tok,
        out_shape=jax.ShapeDtypeStruct(x.shape, F32),
        scratch_shapes=[
            pltpu.VMEM((tm, D_MODEL), BF16),
            pltpu.VMEM((N_EXPERTS, tm, 1), F32),
            pltpu.VMEM((tm, D_MODEL), F32),
        ],
        compiler_params=_params("arbitrary", "arbitrary", "arbitrary"),
        name="moe",
    )(x, mod, router_wt, router_b, w13, w2)


def _final_body(x_ref, g_ref, o_ref):
    x = x_ref[0]
    o_ref[0] = x * lax.rsqrt(jnp.mean(x * x, axis=-1, keepdims=True) + RMS_EPS) * g_ref[...]


def _final_call(x, g, ts=1024):
    batch, seq, _ = x.shape
    tok = pl.BlockSpec((1, ts, D_MODEL), lambda b, i: (b, i, 0))
    return pl.pallas_call(
        _final_body,
        grid=(batch, seq // ts),
        in_specs=[tok, pl.BlockSpec((1, D_MODEL), lambda b, i: (0, 0))],
        out_specs=tok,
        out_shape=jax.ShapeDtypeStruct(x.shape, F32),
        compiler_params=_params("arbitrary", "arbitrary"),
        name="final_norm",
    )(x, g)


def _pad_cols(w):
    n = w.shape[-1]
    return jnp.pad(w, ((0, 0), (0, -n % LANES)))


def _pad_rows(w):
    n = w.shape[0]
    return jnp.pad(w, ((0, -n % LANES), (0, 0)))


def kernel(x, c, ada_w, ada_b, mix_w_in, mix_w_out, sgu_norm_g, sgu_w, sgu_b, rwkv_mu, rwkv_w_rkv, rwkv_w_o, rwkv_w0, rwkv_w1, rwkv_w2, rwkv_a0, rwkv_a1, rwkv_a2, rwkv_v0, rwkv_v1, rwkv_v2, rwkv_g1, rwkv_g2, rwkv_k_k, rwkv_k_a, rwkv_r_k, rwkv_ln_w, rwkv_ln_b, router_w, router_b, moe_w1, moe_w3, moe_w2, final_norm_g):
    depth = ada_w.shape[0]
    row = lambda t: t.reshape(1, -1)
    mods = _ada_call(c, ada_w, ada_b)
    w13 = jnp.concatenate([moe_w1, moe_w3], axis=-1).astype(BF16)
    w2 = moe_w2.astype(BF16)
    router_wt = router_w.T
    router_bc = router_b.reshape(N_EXPERTS, 1)
    v_first = None
    for layer in range(depth):
        i = layer // 2
        mod = mods[2 * layer]
        if layer % 2 == 0:
            proj = _proj_call(x, mod, mix_w_in[i].astype(BF16))
            o_a = _attn_call(proj)
            bias_tile = jnp.repeat(sgu_b[i].T, HEAD_DIM, axis=1)
            x = _sgu_out_call(proj, o_a, x, mod, row(sgu_norm_g[i]), sgu_w[i], bias_tile,
                              mix_w_out[i].astype(BF16))
        else:
            weights = [
                rwkv_mu[i],
                rwkv_w_rkv[i, 0].astype(BF16), rwkv_w_rkv[i, 1].astype(BF16), rwkv_w_rkv[i, 2].astype(BF16),
                row(rwkv_w0[i]), _pad_cols(rwkv_w1[i]).astype(BF16), _pad_rows(rwkv_w2[i]).astype(BF16),
                row(rwkv_a0[i]), _pad_cols(rwkv_a1[i]).astype(BF16), _pad_rows(rwkv_a2[i]).astype(BF16),
                _pad_cols(rwkv_g1[i]).astype(BF16), _pad_rows(rwkv_g2[i]).astype(BF16),
                row(rwkv_k_k[i]), row(rwkv_k_a[i]),
            ]
            vres = None
            if i > 0:
                vres = [row(rwkv_v0[i - 1]), _pad_cols(rwkv_v1[i - 1]).astype(BF16),
                        _pad_rows(rwkv_v2[i - 1]).astype(BF16)]
            r, w_log, k, v, kk, a, g = _rwkv_pre_call(x, mod, weights, v_first, vres)
            if i == 0:
                v_first = v
            y = _wkv_call(r, w_log, k, v, kk, a, row(rwkv_ln_w[i]), row(rwkv_ln_b[i]), row(rwkv_r_k[i]))
            x = _rwkv_post_call(y, g, x, mod, rwkv_w_o[i].astype(BF16))
        x = _moe_call(x, mods[2 * layer + 1], router_wt, router_bc, w13, w2, layer)
    return _final_call(x, row(final_norm_g))
```

```python
import functools

import jax
import jax.numpy as jnp
from jax import lax
from jax.experimental import pallas as pl
from jax.experimental.pallas import tpu as pltpu

F32 = jnp.float32
BF16 = jnp.bfloat16
HIGHEST = lax.Precision.HIGHEST

D_MODEL = 1024
HEAD_DIM = 64
WIDTH_A = 512
WIDTH_B = 512
MIX_IN_WIDTH = 2560
ATTN_SPAN = 128
ATTN_DILATIONS = (1, 4, 16)
SGU_CHUNK = 128
N_GROUPS_B = 8
RWKV_HEADS = 16
RWKV_GN_EPS = 64e-5
WKV_CHUNK = 64
N_EXPERTS = 16
N_EXPERT_GROUPS = 4
EXPERTS_PER_GROUP = 4
D_EXPERT = 256
RMS_EPS = 1e-6
LANES = 128
VMEM_LIMIT = 56 * 1024 * 1024

NT_DIMS = (((1,), (1,)), ((), ()))
TN_DIMS = (((0,), (0,)), ((), ()))


def _params(*sem):
    return pltpu.CompilerParams(dimension_semantics=sem, vmem_limit_bytes=VMEM_LIMIT)


def _norm_mod(x, mod_ref):
    ms = jnp.mean(x * x, axis=-1, keepdims=True)
    return (x * lax.rsqrt(ms + RMS_EPS)) * (1.0 + mod_ref[0, 1:2, :]) + mod_ref[0, 0:1, :]


def _gelu(x):
    return 0.5 * x * (1.0 + lax.erf(x * 0.7071067811865476))


def _ada_body(c_ref, w_ref, b_ref, o_ref):
    c = c_ref[...]
    c_act = c * jax.nn.sigmoid(c)
    o_ref[0] = jnp.dot(c_act, w_ref[0], preferred_element_type=F32, precision=HIGHEST) + b_ref[0]


def _ada_call(c, ada_w, ada_b):
    n_pair = ada_w.shape[0] * ada_w.shape[1]
    batch = c.shape[0]
    w = ada_w.reshape(n_pair, D_MODEL, 3 * D_MODEL)
    b = ada_b.reshape(n_pair, 1, 3 * D_MODEL)
    out = pl.pallas_call(
        _ada_body,
        grid=(n_pair, 3),
        in_specs=[
            pl.BlockSpec((batch, D_MODEL), lambda p, j: (0, 0)),
            pl.BlockSpec((1, D_MODEL, D_MODEL), lambda p, j: (p, 0, j)),
            pl.BlockSpec((1, 1, D_MODEL), lambda p, j: (p, 0, j)),
        ],
        out_specs=pl.BlockSpec((1, batch, D_MODEL), lambda p, j: (p, 0, j)),
        out_shape=jax.ShapeDtypeStruct((n_pair, batch, 3 * D_MODEL), F32),
        compiler_params=_params("arbitrary", "arbitrary"),
        name="ada_mod",
    )(c, w, b)
    return out.reshape(n_pair, batch, 3, D_MODEL)


def _proj_body(x_ref, mod_ref, w_ref, o_ref):
    h = _norm_mod(x_ref[0], mod_ref)
    o_ref[0] = jnp.dot(h.astype(BF16), w_ref[...], preferred_element_type=F32)


def _proj_call(x, mod, w_in, ts=512):
    batch, seq, _ = x.shape
    return pl.pallas_call(
        _proj_body,
        grid=(batch, seq // ts),
        in_specs=[
            pl.BlockSpec((1, ts, D_MODEL), lambda b, i: (b, i, 0)),
            pl.BlockSpec((1, 3, D_MODEL), lambda b, i: (b, 0, 0)),
            pl.BlockSpec((D_MODEL, MIX_IN_WIDTH), lambda b, i: (0, 0)),
        ],
        out_specs=pl.BlockSpec((1, ts, MIX_IN_WIDTH), lambda b, i: (b, i, 0)),
        out_shape=jax.ShapeDtypeStruct((batch, seq, MIX_IN_WIDTH), F32),
        compiler_params=_params("arbitrary", "arbitrary"),
        name="mix_in_proj",
    )(x, mod, w_in)


def _attn_body(q_ref, k_ref, v_ref, o_ref, ob, lb):
    span = ATTN_SPAN
    W = LANES
    block_group = 4

    def iota(shape, dim):
        return lax.broadcasted_iota(jnp.int32, shape, dim)

    head0 = iota((span, W), 1) < HEAD_DIM
    mine = (head0, jnp.logical_not(head0))
    row2, col2 = iota((span, 2 * span), 0), iota((span, 2 * span), 1)
    mask_prev_cur = jnp.where(col2 < span, col2 - row2, row2 - (col2 - span)) >= 0
    mask_cur = iota((span, span), 1) <= iota((span, span), 0)

    def rows(start, n_rows, dil):
        if dil == 1:
            return pl.ds(pl.multiple_of(start, span), n_rows)
        return pl.ds(start, n_rows, stride=dil)

    def group(p, dil, q_starts, with_prev):
        blocks = range(len(q_starts))
        heads = range(2)
        qb, kcat, vaug, masks = [], [], [], []
        for j in blocks:
            n_keys = 2 * span if with_prev[j] else span
            k_start = q_starts[j] - span * dil if with_prev[j] else q_starts[j]
            qb.append(q_ref[0, rows(q_starts[j], span, dil), :] * (HEAD_DIM ** -0.5))
            kcat.append(k_ref[0, rows(k_start, n_keys, dil), :].astype(BF16))
            vb = v_ref[0, rows(k_start, n_keys, dil), :].astype(BF16)
            vaug.append(jnp.concatenate([vb, jnp.ones((n_keys, W), BF16)], axis=1))
            masks.append(mask_prev_cur if with_prev[j] else mask_cur)
        qh = [[jnp.where(mine[h], qb[j], 0.0).astype(BF16) for h in heads] for j in blocks]
        s = [[lax.dot_general(qh[j][h], kcat[j], NT_DIMS, preferred_element_type=F32) for h in heads] for j in blocks]
        s = [[jnp.where(masks[j], s[j][h], -jnp.inf) for h in heads] for j in blocks]
        m = [[jnp.max(s[j][h], axis=-1, keepdims=True) for h in heads] for j in blocks]
        pr = [[jnp.exp(s[j][h] - m[j][h]).astype(BF16) for h in heads] for j in blocks]
        ad = [[jnp.dot(pr[j][h], vaug[j], preferred_element_type=F32) for h in heads] for j in blocks]
        for j in blocks:
            den = [ad[j][h][:, W:] for h in heads]
            o = jnp.where(head0, ad[j][0][:, :W] / den[0], ad[j][1][:, :W] / den[1])
            lse = jnp.where(head0, m[j][0] + jnp.log(den[0]), m[j][1] + jnp.log(den[1]))
            qi = rows(q_starts[j], span, dil)
            ob[p, qi, :] = o
            lb[p, qi, :] = lse

    seq = q_ref.shape[1]
    for p, dil in enumerate(ATTN_DILATIONS):
        n_blk = seq // dil // span
        if n_blk == 1:
            def step(g, carry, p=p, dil=dil):
                group(p, dil, [g * block_group + j for j in range(block_group)], [False] * block_group)
                return carry
            lax.fori_loop(0, dil // block_group, step, 0)
        else:
            per_seq = n_blk // block_group

            def first(r, carry, p=p, dil=dil):
                group(p, dil, [r + j * span * dil for j in range(block_group)],
                      [False] + [True] * (block_group - 1))
                return carry

            def later(idx, carry, p=p, dil=dil, per_seq=per_seq):
                r = idx // (per_seq - 1)
                g = idx % (per_seq - 1) + 1
                group(p, dil, [r + (g * block_group + j) * span * dil for j in range(block_group)],
                      [True] * block_group)
                return carry

            lax.fori_loop(0, dil, first, 0)
            if per_seq > 1:
                lax.fori_loop(0, dil * (per_seq - 1), later, 0)

    l0, l1, l2 = lb[0], lb[1], lb[2]
    m = jnp.maximum(jnp.maximum(l0, l1), l2)
    w0, w1, w2 = jnp.exp(l0 - m), jnp.exp(l1 - m), jnp.exp(l2 - m)
    o_ref[0] = (w0 * ob[0] + w1 * ob[1] + w2 * ob[2]) / (w0 + w1 + w2)


def _attn_call(proj):
    batch, seq, _ = proj.shape
    n_pair = WIDTH_A // LANES
    blk = lambda off: pl.BlockSpec((1, seq, LANES), lambda b, h: (b, 0, off + h))
    return pl.pallas_call(
        _attn_body,
        grid=(batch, n_pair),
        in_specs=[blk(0), blk(n_pair), blk(2 * n_pair)],
        out_specs=pl.BlockSpec((1, seq, LANES), lambda b, h: (b, 0, h)),
        out_shape=jax.ShapeDtypeStruct((batch, seq, WIDTH_A), F32),
        scratch_shapes=[
            pltpu.VMEM((3, seq, LANES), F32),
            pltpu.VMEM((3, seq, LANES), F32),
        ],
        compiler_params=_params("arbitrary", "arbitrary"),
        name="dilated_attn",
    )(proj, proj, proj)


def _sgu_out_body(u_ref, z_ref, oa_ref, x_ref, mod_ref, ng_ref, sw_ref, sb_ref, wo_ref, o_ref, mix):
    ts = u_ref.shape[1]
    u = _gelu(u_ref[0])
    z = _gelu(z_ref[0])
    z = z * lax.rsqrt(jnp.mean(z * z, axis=-1, keepdims=True) + RMS_EPS) * ng_ref[...]
    zb = z.astype(BF16)
    row = lax.broadcasted_iota(jnp.int32, (SGU_CHUNK, SGU_CHUNK), 0)
    col = lax.broadcasted_iota(jnp.int32, (SGU_CHUNK, SGU_CHUNK), 1)
    causal = col <= row
    for g in range(N_GROUPS_B):
        w_g = jnp.where(causal, sw_ref[g], 0.0).astype(BF16)
        for cc in range(ts // SGU_CHUNK):
            rs = slice(cc * SGU_CHUNK, (cc + 1) * SGU_CHUNK)
            ls = slice(g * HEAD_DIM, (g + 1) * HEAD_DIM)
            mix[rs, ls] = jnp.dot(w_g, zb[rs, ls], preferred_element_type=F32)
    bias = jnp.concatenate([sb_ref[...]] * (ts // SGU_CHUNK), axis=0)
    o_b = u * (mix[...] + bias)
    y = jnp.dot(oa_ref[0].astype(BF16), wo_ref[0:WIDTH_A, :], preferred_element_type=F32)
    y = y + jnp.dot(o_b.astype(BF16), wo_ref[WIDTH_A:, :], preferred_element_type=F32)
    o_ref[0] = x_ref[0] + mod_ref[0, 2:3, :] * y


def _sgu_out_call(proj, o_a, x, mod, norm_g, sgu_w, sgu_bias_tile, w_out, ts=512):
    batch, seq, _ = x.shape
    u_blk = 3 * WIDTH_A // WIDTH_B
    full = lambda shape: pl.BlockSpec(shape, lambda b, i: (0,) * len(shape))
    return pl.pallas_call(
        _sgu_out_body,
        grid=(batch, seq // ts),
        in_specs=[
            pl.BlockSpec((1, ts, WIDTH_B), lambda b, i: (b, i, u_blk)),
            pl.BlockSpec((1, ts, WIDTH_B), lambda b, i: (b, i, u_blk + 1)),
            pl.BlockSpec((1, ts, WIDTH_A), lambda b, i: (b, i, 0)),
            pl.BlockSpec((1, ts, D_MODEL), lambda b, i: (b, i, 0)),
            pl.BlockSpec((1, 3, D_MODEL), lambda b, i: (b, 0, 0)),
            full((1, WIDTH_B)),
            full((N_GROUPS_B, SGU_CHUNK, SGU_CHUNK)),
            full((SGU_CHUNK, WIDTH_B)),
            full((WIDTH_A + WIDTH_B, D_MODEL)),
        ],
        out_specs=pl.BlockSpec((1, ts, D_MODEL), lambda b, i: (b, i, 0)),
        out_shape=jax.ShapeDtypeStruct(x.shape, F32),
        scratch_shapes=[pltpu.VMEM((ts, WIDTH_B), F32)],
        compiler_params=_params("arbitrary", "arbitrary"),
        name="sgu_out_proj",
    )(proj, proj, o_a, x, mod, norm_g, sgu_w, sgu_bias_tile, w_out)


def _rwkv_pre_body(has_vres, *refs):
    if has_vres:
        (x_ref, mod_ref, mu_ref, wr_ref, wk_ref, wv_ref, w0_ref, w1_ref, w2_ref, a0_ref, a1_ref, a2_ref,
         g1_ref, g2_ref, kk_ref, ka_ref, vf_ref, v0_ref, v1_ref, v2_ref,
         r_out, w_out, k_out, v_out, kk_out, a_out, g_out, carry) = refs
    else:
        (x_ref, mod_ref, mu_ref, wr_ref, wk_ref, wv_ref, w0_ref, w1_ref, w2_ref, a0_ref, a1_ref, a2_ref,
         g1_ref, g2_ref, kk_ref, ka_ref,
         r_out, w_out, k_out, v_out, kk_out, a_out, g_out, carry) = refs

    @pl.when(pl.program_id(1) == 0)
    def _():
        carry[...] = jnp.zeros_like(carry)

    h = _norm_mod(x_ref[0], mod_ref)
    ts = h.shape[0]
    first = lax.broadcasted_iota(jnp.int32, h.shape, 0) == 0
    h_prev = jnp.where(first, carry[0:1, :], pltpu.roll(h, 1, 0))
    carry[0:1, :] = h[ts - 1:ts, :]
    xx = h_prev - h

    def mixed(i):
        return (h + xx * mu_ref[i:i + 1, :]).astype(BF16)

    def mm(a, w_ref):
        return jnp.dot(a, w_ref[...], preferred_element_type=F32)

    xr, xw, xk, xv, xa, xg = [mixed(i) for i in range(6)]
    r = mm(xr, wr_ref)
    k = mm(xk, wk_ref)
    v = mm(xv, wv_ref)
    z = w0_ref[...] + mm(jnp.tanh(mm(xw, w1_ref)).astype(BF16), w2_ref)
    w_log = -(jnp.maximum(-z, 0.0) + jnp.log(1.0 + jnp.exp(-jnp.abs(z)))) - 0.5
    if has_vres:
        mix_v = jax.nn.sigmoid(v0_ref[...] + mm(mm(xv, v1_ref).astype(BF16), v2_ref))
        v = v + (vf_ref[0].astype(F32) - v) * mix_v
    a = jax.nn.sigmoid(a0_ref[...] + mm(mm(xa, a1_ref).astype(BF16), a2_ref))
    g = mm(jax.nn.sigmoid(mm(xg, g1_ref)).astype(BF16), g2_ref)
    r_out[0] = r.astype(BF16)
    w_out[0] = w_log
    kk_out[0] = (k * kk_ref[...]).astype(BF16)
    k_out[0] = (k * (1.0 + (a - 1.0) * ka_ref[...])).astype(BF16)
    v_out[0] = v.astype(BF16)
    a_out[0] = a.astype(BF16)
    g_out[0] = g.astype(BF16)


def _rwkv_pre_call(x, mod, weights, v_first, vres, ts=512):
    batch, seq, _ = x.shape
    tok = pl.BlockSpec((1, ts, D_MODEL), lambda b, i: (b, i, 0))
    full = lambda a: pl.BlockSpec(a.shape, lambda b, i: (0,) * a.ndim, pipeline_mode=pl.Buffered(1))
    ins = [x, mod] + list(weights)
    specs = [tok, pl.BlockSpec((1, 3, D_MODEL), lambda b, i: (b, 0, 0))] + [full(a) for a in weights]
    if vres is not None:
        ins += [v_first] + list(vres)
        specs += [tok] + [full(a) for a in vres]
    out_dtypes = [BF16, F32, BF16, BF16, BF16, BF16, BF16]
    return pl.pallas_call(
        functools.partial(_rwkv_pre_body, vres is not None),
        grid=(batch, seq // ts),
        in_specs=specs,
        out_specs=[tok] * 7,
        out_shape=[jax.ShapeDtypeStruct(x.shape, dt) for dt in out_dtypes],
        scratch_shapes=[pltpu.VMEM((8, D_MODEL), F32)],
        compiler_params=_params("arbitrary", "arbitrary"),
        name="rwkv_pre",
    )(*ins)


def _split_dot(x, ones):
    hi = x.astype(BF16)
    lo = (x - hi.astype(F32)).astype(BF16)
    rows = x.shape[0]
    both = jnp.dot(jnp.concatenate([hi, lo], axis=0), ones, preferred_element_type=F32)
    return both[:rows] + both[rows:]


def _wkv_body(r_ref, w_ref, k_ref, v_ref, kk_ref, a_ref, lnw_ref, lnb_ref, rk_ref, o_ref, state):
    C = WKV_CHUNK
    W = LANES
    n_chunk = r_ref.shape[1] // C

    @pl.when(pl.program_id(2) == 0)
    def _():
        state[...] = jnp.zeros_like(state)

    def iota(shape, dim):
        return lax.broadcasted_iota(jnp.int32, shape, dim)

    head0 = iota((C, W), 1) < HEAD_DIM
    row2, col2 = iota((C, 2 * C), 0), iota((C, 2 * C), 1) % C
    strict2 = col2 < row2
    lower2 = col2 <= row2
    tri = jnp.where(iota((C, C), 1) <= iota((C, C), 0), 1.0, 0.0).astype(BF16)
    eye = jnp.where(iota((C, C), 1) == iota((C, C), 0), 1.0, 0.0).astype(F32)
    rw, cw = iota((W, W), 0), iota((W, W), 1)
    same_head = (rw < HEAD_DIM) == (cw < HEAD_DIM)
    ones_bd = jnp.where(same_head, 1.0, 0.0).astype(BF16)
    diag_w = cw == rw
    zeros_cw = jnp.zeros((C, W), F32)

    def bdot(a, b):
        return jnp.dot(a.astype(BF16), b.astype(BF16), preferred_element_type=F32)

    def pick(x0, x1):
        return jnp.where(head0, x0, x1)

    chunks = range(n_chunk)
    heads = range(2)
    rows = lambda t, c: t[c * C:(c + 1) * C]
    mine = (head0, jnp.logical_not(head0))

    r = r_ref[0].astype(F32)
    k = k_ref[0].astype(F32)
    v = v_ref[0].astype(F32)
    kk = kk_ref[0].astype(F32)
    kk = kk / jnp.maximum(jnp.sqrt(_split_dot(kk * kk, ones_bd)), 1e-12)
    b = kk * a_ref[0].astype(F32)
    bonus = _split_dot(r * k * rk_ref[...], ones_bd) * v
    log_w = -jnp.exp(w_ref[0])
    hi = log_w.astype(BF16)
    lo = (log_w - hi.astype(F32)).astype(BF16)
    hilo = jnp.concatenate([hi, lo], axis=1)
    cum2 = [jnp.dot(tri, rows(hilo, c), preferred_element_type=F32) for c in chunks]
    cum = jnp.concatenate([c2[:, :W] + c2[:, W:] for c2 in cum2], axis=0)
    g_in = jnp.exp(cum)
    g_inv = jnp.exp(-cum)
    a_t = -kk * jnp.exp(cum - log_w)
    r_t = r * g_in
    b_t = b * g_inv
    k_t = k * g_inv
    g_last = [g_in[c * C + C - 1:(c + 1) * C, :] for c in chunks]
    bk = [jnp.concatenate([rows(b_t, c), rows(k_t, c)], axis=0) for c in chunks]
    bk_b = [t.astype(BF16) for t in bk]
    zv = [jnp.concatenate([zeros_cw, rows(v, c)], axis=0).astype(BF16) for c in chunks]
    ar = [[jnp.concatenate([jnp.where(mine[h], rows(a_t, c), 0.0), jnp.where(mine[h], rows(r_t, c), 0.0)],
                           axis=0).astype(BF16) for h in heads] for c in chunks]
    gram = [[lax.dot_general(ar[c][h], bk_b[c], NT_DIMS, preferred_element_type=F32) for h in heads]
            for c in chunks]
    top = [[jnp.where(strict2, gram[c][h][:C], 0.0) for h in heads] for c in chunks]
    g_top = [[top[c][h].astype(BF16) for h in heads] for c in chunks]
    g_bot = [[jnp.where(lower2, gram[c][h][C:], 0.0).astype(BF16) for h in heads] for c in chunks]
    n1 = [[top[c][h][:, :C] for h in heads] for c in chunks]
    inv = [[eye + n1[c][h] for h in heads] for c in chunks]
    pw = [[bdot(n1[c][h], n1[c][h]) for h in heads] for c in chunks]
    levels = C.bit_length() - 2
    for lvl in range(levels):
        if lvl < levels - 1:
            st = [[bdot(jnp.concatenate([pw[c][h], inv[c][h]], axis=0), pw[c][h]) for h in heads] for c in chunks]
            pw = [[st[c][h][:C] for h in heads] for c in chunks]
            inv = [[inv[c][h] + st[c][h][C:] for h in heads] for c in chunks]
        else:
            inv = [[inv[c][h] + bdot(inv[c][h], pw[c][h]) for h in heads] for c in chunks]
    t_inv = [[inv[c][h].astype(BF16) for h in heads] for c in chunks]
    xs = [[jnp.dot(g_top[c][h], zv[c], preferred_element_type=F32) for h in heads] for c in chunks]
    ax = [jnp.concatenate([rows(a_t, c), pick(*xs[c])], axis=1).astype(BF16) for c in chunks]
    pp = [[jnp.dot(t_inv[c][h], ax[c], preferred_element_type=F32) for h in heads] for c in chunks]
    rhs = []
    for c in chunks:
        p1 = pick(pp[c][0][:, :W], pp[c][1][:, :W])
        p2 = pick(pp[c][0][:, W:], pp[c][1][:, W:])
        rhs.append(jnp.concatenate([jnp.concatenate([p1, p2], axis=1),
                                    jnp.concatenate([zeros_cw, rows(v, c)], axis=1)], axis=0).astype(BF16))
    qq = [[jnp.dot(g_bot[c][h], rhs[c], preferred_element_type=F32) for h in heads] for c in chunks]
    mm = [lax.dot_general((bk[c] * g_last[c]).astype(BF16), rhs[c], TN_DIMS, preferred_element_type=F32)
          for c in chunks]
    q1 = [rows(r_t, c) + pick(qq[c][0][:, :W], qq[c][1][:, :W]) for c in chunks]
    q2 = [pick(qq[c][0][:, W:], qq[c][1][:, W:]) for c in chunks]
    m1 = [(jnp.where(same_head, mm[c][:, :W], 0.0)
           + jnp.where(diag_w, jnp.broadcast_to(g_last[c], (W, W)), 0.0)).astype(BF16) for c in chunks]
    m2 = [jnp.where(same_head, mm[c][:, W:], 0.0) for c in chunks]
    h_in = [state[...]]
    for c in chunks:
        h_in.append(jnp.dot(m1[c], h_in[c].astype(BF16), preferred_element_type=F32) + m2[c])
    state[...] = h_in[n_chunk]
    y = jnp.concatenate([bdot(q1[c], h_in[c]) + q2[c] for c in chunks], axis=0)
    mean = _split_dot(y, ones_bd) * (1.0 / HEAD_DIM)
    yc = y - mean
    var = _split_dot(yc * yc, ones_bd) * (1.0 / HEAD_DIM)
    yn = yc * lax.rsqrt(var + RWKV_GN_EPS) * lnw_ref[...] + lnb_ref[...]
    o_ref[0] = (yn + bonus).astype(BF16)


def _wkv_call(r, w_log, k, v, kk, a, ln_w, ln_b, r_k, tc=512):
    batch, seq, _ = r.shape
    n_pair = D_MODEL // LANES
    tok = pl.BlockSpec((1, tc, LANES), lambda b, h, t: (b, t, h))
    vec = pl.BlockSpec((1, LANES), lambda b, h, t: (0, h))
    return pl.pallas_call(
        _wkv_body,
        grid=(batch, n_pair, seq // tc),
        in_specs=[tok] * 6 + [vec] * 3,
        out_specs=tok,
        out_shape=jax.ShapeDtypeStruct(r.shape, BF16),
        scratch_shapes=[pltpu.VMEM((LANES, LANES), F32)],
        compiler_params=_params("arbitrary", "arbitrary", "arbitrary"),
        name="wkv7_chunked",
    )(r, w_log, k, v, kk, a, ln_w, ln_b, r_k)


def _rwkv_post_body(y_ref, g_ref, x_ref, mod_ref, wo_ref, o_ref):
    yg = y_ref[0] * g_ref[0]
    o_ref[0] = x_ref[0] + mod_ref[0, 2:3, :] * jnp.dot(yg, wo_ref[...], preferred_element_type=F32)


def _rwkv_post_call(y, g, x, mod, w_o, ts=512):
    batch, seq, _ = x.shape
    tok = pl.BlockSpec((1, ts, D_MODEL), lambda b, i: (b, i, 0))
    return pl.pallas_call(
        _rwkv_post_body,
        grid=(batch, seq // ts),
        in_specs=[tok, tok, tok, pl.BlockSpec((1, 3, D_MODEL), lambda b, i: (b, 0, 0)),
                  pl.BlockSpec((D_MODEL, D_MODEL), lambda b, i: (0, 0))],
        out_specs=tok,
        out_shape=jax.ShapeDtypeStruct(x.shape, F32),
        compiler_params=_params("arbitrary", "arbitrary"),
        name="rwkv_out_proj",
    )(y, g, x, mod, w_o)


def _moe_body(is_last, x_ref, mod_ref, rw_ref, rb_ref, w13_ref, w2_ref, fg_ref, o_ref, h_scr, gate_scr, act_scr, acc):
    grp = pl.program_id(2)
    tm = x_ref.shape[1]
    n_g, epg = N_EXPERT_GROUPS, EXPERTS_PER_GROUP

    @pl.when(grp == 0)
    def _():
        h = _norm_mod(x_ref[0], mod_ref)
        hi = h.astype(BF16)
        h_scr[...] = hi
        lo = (h - hi.astype(F32)).astype(BF16)
        parts = jnp.dot(jnp.concatenate([hi, lo], axis=0), rw_ref[...], preferred_element_type=F32)
        logits = (parts[:tm, :LANES] + parts[:tm, LANES:]) + (parts[tm:, :LANES] + parts[tm:, LANES:])
        logits = logits.T[:N_EXPERTS]
        s = jax.nn.sigmoid(logits)
        sel = s + rb_ref[...]
        s_rows = [s[i:i + 1, :] for i in range(N_EXPERTS)]
        sel_rows = [sel[i:i + 1, :] for i in range(N_EXPERTS)]
        scores = []
        for g in range(n_g):
            members = sel_rows[g * epg:(g + 1) * epg]
            best = None
            for i in range(epg):
                for j in range(i + 1, epg):
                    pair = members[i] + members[j]
                    best = pair if best is None else jnp.maximum(best, pair)
            scores.append(best)
        g_idx = jnp.zeros_like(scores[0], dtype=jnp.int32)
        top = scores[0]
        for g in range(1, n_g):
            better = scores[g] > top
            g_idx = jnp.where(better, g, g_idx)
            top = jnp.where(better, scores[g], top)

        def pick(rows_, j):
            out = rows_[j]
            for g in range(1, n_g):
                out = jnp.where(g_idx == g, rows_[g * epg + j], out)
            return out

        in_sel = [pick(sel_rows, j) for j in range(epg)]
        in_s = [pick(s_rows, j) for j in range(epg)]
        chosen = []
        for j in range(epg):
            rank = jnp.zeros_like(g_idx)
            for i in range(epg):
                if i == j:
                    continue
                ahead = (in_sel[i] >= in_sel[j]) if i < j else (in_sel[i] > in_sel[j])
                rank = rank + jnp.where(ahead, 1, 0)
            chosen.append(rank < 2)
        den = sum(jnp.where(chosen[j], in_s[j], 0.0) for j in range(epg))
        gate_rows = []
        for ex in range(N_EXPERTS):
            g, j = divmod(ex, epg)
            on = jnp.logical_and(chosen[j], g_idx == g)
            gate_rows.append(jnp.where(on, in_s[j] / den, 0.0))
        pad = jnp.zeros((LANES - N_EXPERTS, tm), F32)
        gate_t = jnp.concatenate(gate_rows + [pad], axis=0).T
        g_hi = gate_t.astype(BF16)
        g_lo = (gate_t - g_hi.astype(F32)).astype(BF16)
        gate_scr[...] = jnp.concatenate([g_hi, g_lo], axis=1)
        acc[...] = jnp.zeros_like(acc)

    h = h_scr[...]
    gates = gate_scr[...]
    expert_row = lax.broadcasted_iota(jnp.int32, (2 * LANES, LANES), 0) % LANES
    for j in range(epg):
        pick_rows = jnp.where(expert_row == grp * epg + j, 1.0, 0.0).astype(BF16)
        gate_b = jnp.dot(gates, pick_rows, preferred_element_type=F32)
        hid = jnp.dot(h, w13_ref[0, j], preferred_element_type=F32)
        h1 = hid[:, :D_EXPERT]
        act = h1 * jax.nn.sigmoid(h1) * hid[:, D_EXPERT:] * jnp.concatenate([gate_b, gate_b], axis=1)
        act_scr[:, j * D_EXPERT:(j + 1) * D_EXPERT] = act.astype(BF16)
    w2 = w2_ref[0].reshape(epg * D_EXPERT, D_MODEL)
    acc[...] += jnp.dot(act_scr[...], w2, preferred_element_type=F32)

    @pl.when(grp == n_g - 1)
    def _():
        out = x_ref[0] + mod_ref[0, 2:3, :] * acc[...]
        if is_last:
            out = out * lax.rsqrt(jnp.mean(out * out, axis=-1, keepdims=True) + RMS_EPS) * fg_ref[...]
        o_ref[0] = out


def _moe_call(x, mod, router_w2, router_b, w13, w2, final_g, layer, is_last, tm=1024):
    batch, seq, _ = x.shape
    epg = EXPERTS_PER_GROUP
    tok = pl.BlockSpec((1, tm, D_MODEL), lambda b, i, g: (b, i, 0))
    return pl.pallas_call(
        functools.partial(_moe_body, is_last),
        grid=(batch, seq // tm, N_EXPERT_GROUPS),
        in_specs=[
            tok,
            pl.BlockSpec((1, 3, D_MODEL), lambda b, i, g: (b, 0, 0)),
            pl.BlockSpec((D_MODEL, 2 * LANES), lambda b, i, g: (0, 0)),
            pl.BlockSpec((N_EXPERTS, 1), lambda b, i, g: (0, 0)),
            pl.BlockSpec((1, epg, D_MODEL, 2 * D_EXPERT), lambda b, i, g: (layer, g, 0, 0)),
            pl.BlockSpec((1, epg, D_EXPERT, D_MODEL), lambda b, i, g: (layer, g, 0, 0)),
            pl.BlockSpec((1, D_MODEL), lambda b, i, g: (0, 0)),
        ],
        out_specs=tok,
        out_shape=jax.ShapeDtypeStruct(x.shape, F32),
        scratch_shapes=[
            pltpu.VMEM((tm, D_MODEL), BF16),
            pltpu.VMEM((tm, 2 * LANES), BF16),
            pltpu.VMEM((tm, epg * D_EXPERT), BF16),
            pltpu.VMEM((tm, D_MODEL), F32),
        ],
        compiler_params=_params("arbitrary", "arbitrary", "arbitrary"),
        name="moe",
    )(x, mod, router_w2, router_b, w13, w2, final_g)


def _pad_cols(w):
    n = w.shape[-1]
    return jnp.pad(w, ((0, 0), (0, -n % LANES)))


def _pad_rows(w):
    n = w.shape[0]
    return jnp.pad(w, ((0, -n % LANES), (0, 0)))


def kernel(x, c, ada_w, ada_b, mix_w_in, mix_w_out, sgu_norm_g, sgu_w, sgu_b, rwkv_mu, rwkv_w_rkv, rwkv_w_o, rwkv_w0, rwkv_w1, rwkv_w2, rwkv_a0, rwkv_a1, rwkv_a2, rwkv_v0, rwkv_v1, rwkv_v2, rwkv_g1, rwkv_g2, rwkv_k_k, rwkv_k_a, rwkv_r_k, rwkv_ln_w, rwkv_ln_b, router_w, router_b, moe_w1, moe_w3, moe_w2, final_norm_g):
    depth = ada_w.shape[0]
    row = lambda t: t.reshape(1, -1)
    mods = _ada_call(c, ada_w, ada_b)
    w13 = jnp.concatenate([moe_w1, moe_w3], axis=-1).astype(BF16)
    w2 = moe_w2.astype(BF16)
    rw_hi = router_w.astype(BF16)
    rw_lo = (router_w - rw_hi.astype(F32)).astype(BF16)
    router_w2 = jnp.concatenate([_pad_cols(rw_hi), _pad_cols(rw_lo)], axis=1)
    router_bc = router_b.reshape(N_EXPERTS, 1)
    final_g = row(final_norm_g)
    v_first = None
    for layer in range(depth):
        i = layer // 2
        mod = mods[2 * layer]
        if layer % 2 == 0:
            proj = _proj_call(x, mod, mix_w_in[i].astype(BF16))
            o_a = _attn_call(proj)
            bias_tile = jnp.repeat(sgu_b[i].T, HEAD_DIM, axis=1)
            x = _sgu_out_call(proj, o_a, x, mod, row(sgu_norm_g[i]), sgu_w[i], bias_tile,
                              mix_w_out[i].astype(BF16))
        else:
            weights = [
                rwkv_mu[i],
                rwkv_w_rkv[i, 0].astype(BF16), rwkv_w_rkv[i, 1].astype(BF16), rwkv_w_rkv[i, 2].astype(BF16),
                row(rwkv_w0[i]), _pad_cols(rwkv_w1[i]).astype(BF16), _pad_rows(rwkv_w2[i]).astype(BF16),
                row(rwkv_a0[i]), _pad_cols(rwkv_a1[i]).astype(BF16), _pad_rows(rwkv_a2[i]).astype(BF16),
                _pad_cols(rwkv_g1[i]).astype(BF16), _pad_rows(rwkv_g2[i]).astype(BF16),
                row(rwkv_k_k[i]), row(rwkv_k_a[i]),
            ]
            vres = None
            if i > 0:
                vres = [row(rwkv_v0[i - 1]), _pad_cols(rwkv_v1[i - 1]).astype(BF16),
                        _pad_rows(rwkv_v2[i - 1]).astype(BF16)]
            r, w_log, k, v, kk, a, g = _rwkv_pre_call(x, mod, weights, v_first, vres)
            if i == 0:
                v_first = v
            y = _wkv_call(r, w_log, k, v, kk, a, row(rwkv_ln_w[i]), row(rwkv_ln_b[i]), row(rwkv_r_k[i]))
            x = _rwkv_post_call(y, g, x, mod, rwkv_w_o[i].astype(BF16))
        x = _moe_call(x, mods[2 * layer + 1], router_w2, router_bc, w13, w2, final_g, layer, layer == depth - 1)
    return x
```

```python
import functools

import jax
import jax.numpy as jnp
from jax import lax
from jax.experimental import pallas as pl
from jax.experimental.pallas import tpu as pltpu

F32 = jnp.float32
BF16 = jnp.bfloat16
HIGHEST = lax.Precision.HIGHEST

D_MODEL = 1024
HEAD_DIM = 64
WIDTH_A = 512
WIDTH_B = 512
MIX_IN_WIDTH = 2560
ATTN_SPAN = 128
ATTN_DILATIONS = (1, 4, 16)
SGU_CHUNK = 128
N_GROUPS_B = 8
RWKV_HEADS = 16
RWKV_GN_EPS = 64e-5
WKV_CHUNK = 64
N_EXPERTS = 16
N_EXPERT_GROUPS = 4
EXPERTS_PER_GROUP = 4
D_EXPERT = 256
RMS_EPS = 1e-6
LANES = 128
VMEM_LIMIT = 56 * 1024 * 1024

NT_DIMS = (((1,), (1,)), ((), ()))
TN_DIMS = (((0,), (0,)), ((), ()))


def _params(*sem):
    return pltpu.CompilerParams(dimension_semantics=sem, vmem_limit_bytes=VMEM_LIMIT)


def _norm_mod(x, mod_ref):
    ms = jnp.mean(x * x, axis=-1, keepdims=True)
    return (x * lax.rsqrt(ms + RMS_EPS)) * (1.0 + mod_ref[0, 1:2, :]) + mod_ref[0, 0:1, :]


def _gelu(x):
    return 0.5 * x * (1.0 + lax.erf(x * 0.7071067811865476))


def _ada_body(c_ref, w_ref, b_ref, o_ref):
    c = c_ref[...]
    c_act = c * jax.nn.sigmoid(c)
    o_ref[0] = jnp.dot(c_act, w_ref[0], preferred_element_type=F32, precision=HIGHEST) + b_ref[0]


def _ada_call(c, ada_w, ada_b):
    n_pair = ada_w.shape[0] * ada_w.shape[1]
    batch = c.shape[0]
    w = ada_w.reshape(n_pair, D_MODEL, 3 * D_MODEL)
    b = ada_b.reshape(n_pair, 1, 3 * D_MODEL)
    out = pl.pallas_call(
        _ada_body,
        grid=(n_pair, 3),
        in_specs=[
            pl.BlockSpec((batch, D_MODEL), lambda p, j: (0, 0)),
            pl.BlockSpec((1, D_MODEL, D_MODEL), lambda p, j: (p, 0, j)),
            pl.BlockSpec((1, 1, D_MODEL), lambda p, j: (p, 0, j)),
        ],
        out_specs=pl.BlockSpec((1, batch, D_MODEL), lambda p, j: (p, 0, j)),
        out_shape=jax.ShapeDtypeStruct((n_pair, batch, 3 * D_MODEL), F32),
        compiler_params=_params("arbitrary", "arbitrary"),
        name="ada_mod",
    )(c, w, b)
    return out.reshape(n_pair, batch, 3, D_MODEL)


def _proj_body(x_ref, mod_ref, w_ref, o_ref):
    h = _norm_mod(x_ref[0], mod_ref)
    o_ref[0] = jnp.dot(h.astype(BF16), w_ref[...], preferred_element_type=F32)


def _proj_call(x, mod, w_in, ts=512):
    batch, seq, _ = x.shape
    return pl.pallas_call(
        _proj_body,
        grid=(batch, seq // ts),
        in_specs=[
            pl.BlockSpec((1, ts, D_MODEL), lambda b, i: (b, i, 0)),
            pl.BlockSpec((1, 3, D_MODEL), lambda b, i: (b, 0, 0)),
            pl.BlockSpec((D_MODEL, MIX_IN_WIDTH), lambda b, i: (0, 0)),
        ],
        out_specs=pl.BlockSpec((1, ts, MIX_IN_WIDTH), lambda b, i: (b, i, 0)),
        out_shape=jax.ShapeDtypeStruct((batch, seq, MIX_IN_WIDTH), F32),
        compiler_params=_params("arbitrary", "arbitrary"),
        name="mix_in_proj",
    )(x, mod, w_in)


def _attn_body(q_ref, k_ref, v_ref, o_ref, ob, lb):
    span = ATTN_SPAN
    W = LANES
    block_group = 4

    def iota(shape, dim):
        return lax.broadcasted_iota(jnp.int32, shape, dim)

    head0 = iota((span, W), 1) < HEAD_DIM
    mine = (head0, jnp.logical_not(head0))
    row2, col2 = iota((span, 2 * span), 0), iota((span, 2 * span), 1)
    mask_prev_cur = jnp.where(col2 < span, col2 - row2, row2 - (col2 - span)) >= 0
    mask_cur = iota((span, span), 1) <= iota((span, span), 0)

    def rows(start, n_rows, dil):
        if dil == 1:
            return pl.ds(pl.multiple_of(start, span), n_rows)
        return pl.ds(start, n_rows, stride=dil)

    def group(p, dil, q_starts, with_prev):
        blocks = range(len(q_starts))
        heads = range(2)
        qb, kcat, vaug, masks = [], [], [], []
        for j in blocks:
            n_keys = 2 * span if with_prev[j] else span
            k_start = q_starts[j] - span * dil if with_prev[j] else q_starts[j]
            qb.append(q_ref[0, rows(q_starts[j], span, dil), :] * (HEAD_DIM ** -0.5))
            kcat.append(k_ref[0, rows(k_start, n_keys, dil), :].astype(BF16))
            vb = v_ref[0, rows(k_start, n_keys, dil), :].astype(BF16)
            vaug.append(jnp.concatenate([vb, jnp.ones((n_keys, W), BF16)], axis=1))
            masks.append(mask_prev_cur if with_prev[j] else mask_cur)
        qh = [[jnp.where(mine[h], qb[j], 0.0).astype(BF16) for h in heads] for j in blocks]
        s = [[lax.dot_general(qh[j][h], kcat[j], NT_DIMS, preferred_element_type=F32) for h in heads] for j in blocks]
        s = [[jnp.where(masks[j], s[j][h], -jnp.inf) for h in heads] for j in blocks]
        m = [[jnp.max(s[j][h], axis=-1, keepdims=True) for h in heads] for j in blocks]
        pr = [[jnp.exp(s[j][h] - m[j][h]).astype(BF16) for h in heads] for j in blocks]
        ad = [[jnp.dot(pr[j][h], vaug[j], preferred_element_type=F32) for h in heads] for j in blocks]
        for j in blocks:
            den = [ad[j][h][:, W:] for h in heads]
            o = jnp.where(head0, ad[j][0][:, :W] / den[0], ad[j][1][:, :W] / den[1])
            lse = jnp.where(head0, m[j][0] + jnp.log(den[0]), m[j][1] + jnp.log(den[1]))
            qi = rows(q_starts[j], span, dil)
            ob[p, qi, :] = o
            lb[p, qi, :] = lse

    seq = q_ref.shape[1]
    for p, dil in enumerate(ATTN_DILATIONS):
        n_blk = seq // dil // span
        if n_blk == 1:
            def step(g, carry, p=p, dil=dil):
                group(p, dil, [g * block_group + j for j in range(block_group)], [False] * block_group)
                return carry
            lax.fori_loop(0, dil // block_group, step, 0)
        else:
            per_seq = n_blk // block_group

            def first(r, carry, p=p, dil=dil):
                group(p, dil, [r + j * span * dil for j in range(block_group)],
                      [False] + [True] * (block_group - 1))
                return carry

            def later(idx, carry, p=p, dil=dil, per_seq=per_seq):
                r = idx // (per_seq - 1)
                g = idx % (per_seq - 1) + 1
                group(p, dil, [r + (g * block_group + j) * span * dil for j in range(block_group)],
                      [True] * block_group)
                return carry

            lax.fori_loop(0, dil, first, 0)
            if per_seq > 1:
                lax.fori_loop(0, dil * (per_seq - 1), later, 0)

    l0, l1, l2 = lb[0], lb[1], lb[2]
    m = jnp.maximum(jnp.maximum(l0, l1), l2)
    w0, w1, w2 = jnp.exp(l0 - m), jnp.exp(l1 - m), jnp.exp(l2 - m)
    o_ref[0] = (w0 * ob[0] + w1 * ob[1] + w2 * ob[2]) / (w0 + w1 + w2)


def _attn_call(proj):
    batch, seq, _ = proj.shape
    n_pair = WIDTH_A // LANES
    blk = lambda off: pl.BlockSpec((1, seq, LANES), lambda b, h: (b, 0, off + h))
    return pl.pallas_call(
        _attn_body,
        grid=(batch, n_pair),
        in_specs=[blk(0), blk(n_pair), blk(2 * n_pair)],
        out_specs=pl.BlockSpec((1, seq, LANES), lambda b, h: (b, 0, h)),
        out_shape=jax.ShapeDtypeStruct((batch, seq, WIDTH_A), F32),
        scratch_shapes=[
            pltpu.VMEM((3, seq, LANES), F32),
            pltpu.VMEM((3, seq, LANES), F32),
        ],
        compiler_params=_params("arbitrary", "arbitrary"),
        name="dilated_attn",
    )(proj, proj, proj)


def _sgu_out_body(u_ref, z_ref, oa_ref, x_ref, mod_ref, ng_ref, sw_ref, sb_ref, wo_ref, o_ref, mix):
    ts = u_ref.shape[1]
    u = _gelu(u_ref[0])
    z = _gelu(z_ref[0])
    z = z * lax.rsqrt(jnp.mean(z * z, axis=-1, keepdims=True) + RMS_EPS) * ng_ref[...]
    zb = z.astype(BF16)
    row = lax.broadcasted_iota(jnp.int32, (SGU_CHUNK, SGU_CHUNK), 0)
    col = lax.broadcasted_iota(jnp.int32, (SGU_CHUNK, SGU_CHUNK), 1)
    causal = col <= row
    for g in range(N_GROUPS_B):
        w_g = jnp.where(causal, sw_ref[g], 0.0).astype(BF16)
        for cc in range(ts // SGU_CHUNK):
            rs = slice(cc * SGU_CHUNK, (cc + 1) * SGU_CHUNK)
            ls = slice(g * HEAD_DIM, (g + 1) * HEAD_DIM)
            mix[rs, ls] = jnp.dot(w_g, zb[rs, ls], preferred_element_type=F32)
    bias = jnp.concatenate([sb_ref[...]] * (ts // SGU_CHUNK), axis=0)
    o_b = u * (mix[...] + bias)
    y = jnp.dot(oa_ref[0].astype(BF16), wo_ref[0:WIDTH_A, :], preferred_element_type=F32)
    y = y + jnp.dot(o_b.astype(BF16), wo_ref[WIDTH_A:, :], preferred_element_type=F32)
    o_ref[0] = x_ref[0] + mod_ref[0, 2:3, :] * y


def _sgu_out_call(proj, o_a, x, mod, norm_g, sgu_w, sgu_bias_tile, w_out, ts=512):
    batch, seq, _ = x.shape
    u_blk = 3 * WIDTH_A // WIDTH_B
    full = lambda shape: pl.BlockSpec(shape, lambda b, i: (0,) * len(shape))
    return pl.pallas_call(
        _sgu_out_body,
        grid=(batch, seq // ts),
        in_specs=[
            pl.BlockSpec((1, ts, WIDTH_B), lambda b, i: (b, i, u_blk)),
            pl.BlockSpec((1, ts, WIDTH_B), lambda b, i: (b, i, u_blk + 1)),
            pl.BlockSpec((1, ts, WIDTH_A), lambda b, i: (b, i, 0)),
            pl.BlockSpec((1, ts, D_MODEL), lambda b, i: (b, i, 0)),
            pl.BlockSpec((1, 3, D_MODEL), lambda b, i: (b, 0, 0)),
            full((1, WIDTH_B)),
            full((N_GROUPS_B, SGU_CHUNK, SGU_CHUNK)),
            full((SGU_CHUNK, WIDTH_B)),
            full((WIDTH_A + WIDTH_B, D_MODEL)),
        ],
        out_specs=pl.BlockSpec((1, ts, D_MODEL), lambda b, i: (b, i, 0)),
        out_shape=jax.ShapeDtypeStruct(x.shape, F32),
        scratch_shapes=[pltpu.VMEM((ts, WIDTH_B), F32)],
        compiler_params=_params("arbitrary", "arbitrary"),
        name="sgu_out_proj",
    )(proj, proj, o_a, x, mod, norm_g, sgu_w, sgu_bias_tile, w_out)


def _rwkv_pre_body(has_vres, *refs):
    if has_vres:
        (x_ref, mod_ref, mu_ref, wr_ref, wk_ref, wv_ref, w0_ref, w1_ref, w2_ref, a0_ref, a1_ref, a2_ref,
         g1_ref, g2_ref, kk_ref, ka_ref, vf_ref, v0_ref, v1_ref, v2_ref,
         r_out, w_out, k_out, v_out, kk_out, a_out, g_out, carry) = refs
    else:
        (x_ref, mod_ref, mu_ref, wr_ref, wk_ref, wv_ref, w0_ref, w1_ref, w2_ref, a0_ref, a1_ref, a2_ref,
         g1_ref, g2_ref, kk_ref, ka_ref,
         r_out, w_out, k_out, v_out, kk_out, a_out, g_out, carry) = refs

    @pl.when(pl.program_id(1) == 0)
    def _():
        carry[...] = jnp.zeros_like(carry)

    h = _norm_mod(x_ref[0], mod_ref)
    ts = h.shape[0]
    first = lax.broadcasted_iota(jnp.int32, h.shape, 0) == 0
    h_prev = jnp.where(first, carry[0:1, :], pltpu.roll(h, 1, 0))
    carry[0:1, :] = h[ts - 1:ts, :]
    xx = h_prev - h

    def mixed(i):
        return (h + xx * mu_ref[i:i + 1, :]).astype(BF16)

    def mm(a, w_ref):
        return jnp.dot(a, w_ref[...], preferred_element_type=F32)

    xr, xw, xk, xv, xa, xg = [mixed(i) for i in range(6)]
    r = mm(xr, wr_ref)
    k = mm(xk, wk_ref)
    v = mm(xv, wv_ref)
    z = w0_ref[...] + mm(jnp.tanh(mm(xw, w1_ref)).astype(BF16), w2_ref)
    w_log = -(jnp.maximum(-z, 0.0) + jnp.log(1.0 + jnp.exp(-jnp.abs(z)))) - 0.5
    if has_vres:
        mix_v = jax.nn.sigmoid(v0_ref[...] + mm(mm(xv, v1_ref).astype(BF16), v2_ref))
        v = v + (vf_ref[0].astype(F32) - v) * mix_v
    a = jax.nn.sigmoid(a0_ref[...] + mm(mm(xa, a1_ref).astype(BF16), a2_ref))
    g = mm(jax.nn.sigmoid(mm(xg, g1_ref)).astype(BF16), g2_ref)
    r_out[0] = r.astype(BF16)
    w_out[0] = w_log
    kk_out[0] = (k * kk_ref[...]).astype(BF16)
    k_out[0] = (k * (1.0 + (a - 1.0) * ka_ref[...])).astype(BF16)
    v_out[0] = v.astype(BF16)
    a_out[0] = a.astype(BF16)
    g_out[0] = g.astype(BF16)


def _rwkv_pre_call(x, mod, weights, v_first, vres, ts=512):
    batch, seq, _ = x.shape
    tok = pl.BlockSpec((1, ts, D_MODEL), lambda b, i: (b, i, 0))
    full = lambda a: pl.BlockSpec(a.shape, lambda b, i: (0,) * a.ndim, pipeline_mode=pl.Buffered(1))
    ins = [x, mod] + list(weights)
    specs = [tok, pl.BlockSpec((1, 3, D_MODEL), lambda b, i: (b, 0, 0))] + [full(a) for a in weights]
    if vres is not None:
        ins += [v_first] + list(vres)
        specs += [tok] + [full(a) for a in vres]
    out_dtypes = [BF16, F32, BF16, BF16, BF16, BF16, BF16]
    return pl.pallas_call(
        functools.partial(_rwkv_pre_body, vres is not None),
        grid=(batch, seq // ts),
        in_specs=specs,
        out_specs=[tok] * 7,
        out_shape=[jax.ShapeDtypeStruct(x.shape, dt) for dt in out_dtypes],
        scratch_shapes=[pltpu.VMEM((8, D_MODEL), F32)],
        compiler_params=_params("arbitrary", "arbitrary"),
        name="rwkv_pre",
    )(*ins)


def _split_dot(x, ones):
    hi = x.astype(BF16)
    lo = (x - hi.astype(F32)).astype(BF16)
    rows = x.shape[0]
    both = jnp.dot(jnp.concatenate([hi, lo], axis=0), ones, preferred_element_type=F32)
    return both[:rows] + both[rows:]


def _wkv_body(r_ref, w_ref, k_ref, v_ref, kk_ref, a_ref, lnw_ref, lnb_ref, rk_ref, o_ref, state):
    C = WKV_CHUNK
    W = LANES
    n_chunk = r_ref.shape[1] // C
    n_pair = r_ref.shape[2] // W

    @pl.when(pl.program_id(2) == 0)
    def _():
        state[...] = jnp.zeros_like(state)

    def iota(shape, dim):
        return lax.broadcasted_iota(jnp.int32, shape, dim)

    def stack_heads(x):
        own = iota(x.shape, 1) % W < HEAD_DIM
        return jnp.concatenate([jnp.where(own, x, 0.0), jnp.where(own, 0.0, x)], axis=0).astype(BF16)

    def block_diag(x):
        left = iota(x.shape, 1) < C
        return jnp.concatenate([jnp.where(left, x, 0.0), jnp.where(left, 0.0, x)], axis=0).astype(BF16)

    row4, col4 = iota((C, 4 * C), 0), iota((C, 4 * C), 1) % C
    strict4 = col4 < row4
    lower4 = col4 <= row4
    tri = jnp.where(iota((C, C), 1) <= iota((C, C), 0), 1.0, 0.0).astype(BF16)
    eye2 = jnp.where(iota((C, 2 * C), 1) % C == iota((C, 2 * C), 0), 1.0, 0.0).astype(F32)
    rw, cw = iota((W, W), 0), iota((W, W), 1)
    same_head = (rw < HEAD_DIM) == (cw < HEAD_DIM)
    ones_bd = jnp.where(same_head, 1.0, 0.0).astype(BF16)
    diag_w = cw == rw
    zeros_cw = jnp.zeros((C, W), F32)

    def bdot(a, b):
        return jnp.dot(a.astype(BF16), b.astype(BF16), preferred_element_type=F32)

    chunks = range(n_chunk)
    pairs = range(n_pair)
    units = [(c, p) for c in chunks for p in pairs]
    rows = lambda t, c: t[c * C:(c + 1) * C]
    blk = lambda t, u: t[u[0] * C:(u[0] + 1) * C, u[1] * W:(u[1] + 1) * W]
    lanes = lambda t, p: t[:, p * W:(p + 1) * W]
    per_chunk = lambda vals: jnp.concatenate([jnp.broadcast_to(t, (C, n_pair * W)) for t in vals], axis=0)
    head_sums = lambda t: jnp.concatenate([_split_dot(lanes(t, p), ones_bd) for p in pairs], axis=1)

    r = r_ref[0].astype(F32)
    k = k_ref[0].astype(F32)
    v = v_ref[0].astype(F32)
    kk = kk_ref[0].astype(F32)
    kk = kk / jnp.maximum(jnp.sqrt(head_sums(kk * kk)), 1e-12)
    b = kk * a_ref[0].astype(F32)
    bonus = head_sums(r * k * rk_ref[...]) * v
    log_w = -jnp.exp(w_ref[0])
    hi = log_w.astype(BF16)
    lo = (log_w - hi.astype(F32)).astype(BF16)
    hilo = jnp.concatenate([hi, lo], axis=1)
    cum2 = [jnp.dot(tri, rows(hilo, c), preferred_element_type=F32) for c in chunks]
    cum = jnp.concatenate([c2[:, :n_pair * W] + c2[:, n_pair * W:] for c2 in cum2], axis=0)
    c_mid = [cum[c * C + C // 2 - 1:c * C + C // 2, :] for c in chunks]
    c_end = [cum[c * C + C - 1:(c + 1) * C, :] for c in chunks]
    g_last = [jnp.exp(t) for t in c_end]
    from_mid = jnp.exp(per_chunk(c_mid) - cum)
    to_mid = per_chunk([jnp.exp(-t) for t in c_mid])
    a_true = -kk * jnp.exp(cum - log_w)
    r_true = r * jnp.exp(cum)
    a_mid = a_true * to_mid
    r_mid = r_true * to_mid
    b_mid = b * from_mid
    k_mid = k * from_mid
    ar_mid = [jnp.concatenate([blk(a_mid, u), blk(r_mid, u)], axis=0).astype(BF16) for u in units]
    gram = [lax.dot_general(ar_mid[i], jnp.concatenate([stack_heads(blk(b_mid, u)), stack_heads(blk(k_mid, u))],
                                                        axis=0), NT_DIMS, preferred_element_type=F32)
            for i, u in enumerate(units)]
    top = [jnp.where(strict4, g[:C], 0.0) for g in gram]
    bot = [jnp.where(lower4, g[C:], 0.0).astype(BF16) for g in gram]
    n1 = [t[:, :2 * C] for t in top]
    inv = [eye2 + t for t in n1]
    pw = [bdot(t, block_diag(t)) for t in n1]
    levels = C.bit_length() - 2
    for lvl in range(levels):
        if lvl < levels - 1:
            st = [bdot(jnp.concatenate([pw[i], inv[i]], axis=0), block_diag(pw[i])) for i in range(len(units))]
            pw = [t[:C] for t in st]
            inv = [inv[i] + st[i][C:] for i in range(len(units))]
        else:
            inv = [inv[i] + bdot(inv[i], block_diag(pw[i])) for i in range(len(units))]
    x = [jnp.dot(top[i][:, 2 * C:].astype(BF16), stack_heads(blk(v, u)), preferred_element_type=F32)
         for i, u in enumerate(units)]
    pp = [jnp.dot(inv[i].astype(BF16), stack_heads(jnp.concatenate([blk(a_true, u), x[i]], axis=1)),
                  preferred_element_type=F32) for i, u in enumerate(units)]
    zv = [jnp.concatenate([zeros_cw, blk(v, u)], axis=1) for u in units]
    qq = [jnp.dot(bot[i], jnp.concatenate([stack_heads(pp[i]), stack_heads(zv[i])], axis=0),
                  preferred_element_type=F32) for i in range(len(units))]
    end_from_mid = [jnp.exp(c_end[c] - c_mid[c]) for c in chunks]
    bk_hat = [jnp.concatenate([blk(b_mid, u), blk(k_mid, u)], axis=0) * lanes(end_from_mid[u[0]], u[1])
              for u in units]
    mm = [lax.dot_general(bk_hat[i].astype(BF16), jnp.concatenate([pp[i], zv[i]], axis=0).astype(BF16), TN_DIMS,
                          preferred_element_type=F32) for i in range(len(units))]
    q1 = [blk(r_true, u) + qq[i][:, :W] for i, u in enumerate(units)]
    q2 = [t[:, W:] for t in qq]
    m1 = [(jnp.where(same_head, mm[i][:, :W], 0.0)
           + jnp.where(diag_w, jnp.broadcast_to(lanes(g_last[u[0]], u[1]), (W, W)), 0.0)).astype(BF16)
          for i, u in enumerate(units)]
    m2 = [jnp.where(same_head, t[:, W:], 0.0) for t in mm]
    h_cur = [state[p] for p in pairs]
    h_in = []
    for i, (c, p) in enumerate(units):
        h_in.append(h_cur[p])
        h_cur[p] = jnp.dot(m1[i], h_cur[p].astype(BF16), preferred_element_type=F32) + m2[i]
    for p in pairs:
        state[p] = h_cur[p]
    y_blk = [bdot(q1[i], h_in[i]) + q2[i] for i in range(len(units))]
    y = jnp.concatenate([jnp.concatenate([y_blk[c * n_pair + p] for p in pairs], axis=1) for c in chunks], axis=0)
    mean = head_sums(y) * (1.0 / HEAD_DIM)
    yc = y - mean
    var = head_sums(yc * yc) * (1.0 / HEAD_DIM)
    yn = yc * lax.rsqrt(var + RWKV_GN_EPS) * lnw_ref[...] + lnb_ref[...]
    o_ref[0] = (yn + bonus).astype(BF16)


def _wkv_call(r, w_log, k, v, kk, a, ln_w, ln_b, r_k, tc=512, pairs_per_step=4):
    batch, seq, _ = r.shape
    width = pairs_per_step * LANES
    tok = pl.BlockSpec((1, tc, width), lambda b, h, t: (b, t, h))
    vec = pl.BlockSpec((1, width), lambda b, h, t: (0, h))
    return pl.pallas_call(
        _wkv_body,
        grid=(batch, D_MODEL // width, seq // tc),
        in_specs=[tok] * 6 + [vec] * 3,
        out_specs=tok,
        out_shape=jax.ShapeDtypeStruct(r.shape, BF16),
        scratch_shapes=[pltpu.VMEM((pairs_per_step, LANES, LANES), F32)],
        compiler_params=_params("arbitrary", "arbitrary", "arbitrary"),
        name="wkv7_chunked",
    )(r, w_log, k, v, kk, a, ln_w, ln_b, r_k)


def _rwkv_post_body(y_ref, g_ref, x_ref, mod_ref, wo_ref, o_ref):
    yg = y_ref[0] * g_ref[0]
    o_ref[0] = x_ref[0] + mod_ref[0, 2:3, :] * jnp.dot(yg, wo_ref[...], preferred_element_type=F32)


def _rwkv_post_call(y, g, x, mod, w_o, ts=512):
    batch, seq, _ = x.shape
    tok = pl.BlockSpec((1, ts, D_MODEL), lambda b, i: (b, i, 0))
    return pl.pallas_call(
        _rwkv_post_body,
        grid=(batch, seq // ts),
        in_specs=[tok, tok, tok, pl.BlockSpec((1, 3, D_MODEL), lambda b, i: (b, 0, 0)),
                  pl.BlockSpec((D_MODEL, D_MODEL), lambda b, i: (0, 0))],
        out_specs=tok,
        out_shape=jax.ShapeDtypeStruct(x.shape, F32),
        compiler_params=_params("arbitrary", "arbitrary"),
        name="rwkv_out_proj",
    )(y, g, x, mod, w_o)


def _moe_body(is_last, x_ref, mod_ref, rw_ref, rb_ref, w13_ref, w2_ref, fg_ref, o_ref, h_scr, gate_scr, act_scr, acc):
    grp = pl.program_id(2)
    tm = x_ref.shape[1]
    n_g, epg = N_EXPERT_GROUPS, EXPERTS_PER_GROUP

    @pl.when(grp == 0)
    def _():
        h = _norm_mod(x_ref[0], mod_ref)
        hi = h.astype(BF16)
        h_scr[...] = hi
        lo = (h - hi.astype(F32)).astype(BF16)
        parts = jnp.dot(jnp.concatenate([hi, lo], axis=0), rw_ref[...], preferred_element_type=F32)
        logits = (parts[:tm, :LANES] + parts[:tm, LANES:]) + (parts[tm:, :LANES] + parts[tm:, LANES:])
        logits = logits.T[:N_EXPERTS]
        s = jax.nn.sigmoid(logits)
        sel = s + rb_ref[...]
        s_rows = [s[i:i + 1, :] for i in range(N_EXPERTS)]
        sel_rows = [sel[i:i + 1, :] for i in range(N_EXPERTS)]
        scores = []
        for g in range(n_g):
            members = sel_rows[g * epg:(g + 1) * epg]
            best = None
            for i in range(epg):
                for j in range(i + 1, epg):
                    pair = members[i] + members[j]
                    best = pair if best is None else jnp.maximum(best, pair)
            scores.append(best)
        g_idx = jnp.zeros_like(scores[0], dtype=jnp.int32)
        top = scores[0]
        for g in range(1, n_g):
            better = scores[g] > top
            g_idx = jnp.where(better, g, g_idx)
            top = jnp.where(better, scores[g], top)

        def pick(rows_, j):
            out = rows_[j]
            for g in range(1, n_g):
                out = jnp.where(g_idx == g, rows_[g * epg + j], out)
            return out

        in_sel = [pick(sel_rows, j) for j in range(epg)]
        in_s = [pick(s_rows, j) for j in range(epg)]
        chosen = []
        for j in range(epg):
            rank = jnp.zeros_like(g_idx)
            for i in range(epg):
                if i == j:
                    continue
                ahead = (in_sel[i] >= in_sel[j]) if i < j else (in_sel[i] > in_sel[j])
                rank = rank + jnp.where(ahead, 1, 0)
            chosen.append(rank < 2)
        den = sum(jnp.where(chosen[j], in_s[j], 0.0) for j in range(epg))
        gate_rows = []
        for ex in range(N_EXPERTS):
            g, j = divmod(ex, epg)
            on = jnp.logical_and(chosen[j], g_idx == g)
            gate_rows.append(jnp.where(on, in_s[j] / den, 0.0))
        pad = jnp.zeros((LANES - N_EXPERTS, tm), F32)
        gate_t = jnp.concatenate(gate_rows + [pad], axis=0).T
        g_hi = gate_t.astype(BF16)
        g_lo = (gate_t - g_hi.astype(F32)).astype(BF16)
        gate_scr[...] = jnp.concatenate([g_hi, g_lo], axis=1)
        acc[...] = jnp.zeros_like(acc)

    h = h_scr[...]
    gates = gate_scr[...]
    expert_row = lax.broadcasted_iota(jnp.int32, (2 * LANES, LANES), 0) % LANES
    for j in range(epg):
        pick_rows = jnp.where(expert_row == grp * epg + j, 1.0, 0.0).astype(BF16)
        gate_b = jnp.dot(gates, pick_rows, preferred_element_type=F32)
        hid = jnp.dot(h, w13_ref[0, j], preferred_element_type=F32)
        h1 = hid[:, :D_EXPERT]
        act = h1 * jax.nn.sigmoid(h1) * hid[:, D_EXPERT:] * jnp.concatenate([gate_b, gate_b], axis=1)
        act_scr[:, j * D_EXPERT:(j + 1) * D_EXPERT] = act.astype(BF16)
    w2 = w2_ref[0].reshape(epg * D_EXPERT, D_MODEL)
    acc[...] += jnp.dot(act_scr[...], w2, preferred_element_type=F32)

    @pl.when(grp == n_g - 1)
    def _():
        out = x_ref[0] + mod_ref[0, 2:3, :] * acc[...]
        if is_last:
            out = out * lax.rsqrt(jnp.mean(out * out, axis=-1, keepdims=True) + RMS_EPS) * fg_ref[...]
        o_ref[0] = out


def _moe_call(x, mod, router_w2, router_b, w13, w2, final_g, layer, is_last, tm=1024):
    batch, seq, _ = x.shape
    epg = EXPERTS_PER_GROUP
    tok = pl.BlockSpec((1, tm, D_MODEL), lambda b, i, g: (b, i, 0))
    return pl.pallas_call(
        functools.partial(_moe_body, is_last),
        grid=(batch, seq // tm, N_EXPERT_GROUPS),
        in_specs=[
            tok,
            pl.BlockSpec((1, 3, D_MODEL), lambda b, i, g: (b, 0, 0)),
            pl.BlockSpec((D_MODEL, 2 * LANES), lambda b, i, g: (0, 0)),
            pl.BlockSpec((N_EXPERTS, 1), lambda b, i, g: (0, 0)),
            pl.BlockSpec((1, epg, D_MODEL, 2 * D_EXPERT), lambda b, i, g: (layer, g, 0, 0)),
            pl.BlockSpec((1, epg, D_EXPERT, D_MODEL), lambda b, i, g: (layer, g, 0, 0)),
            pl.BlockSpec((1, D_MODEL), lambda b, i, g: (0, 0)),
        ],
        out_specs=tok,
        out_shape=jax.ShapeDtypeStruct(x.shape, F32),
        scratch_shapes=[
            pltpu.VMEM((tm, D_MODEL), BF16),
            pltpu.VMEM((tm, 2 * LANES), BF16),
            pltpu.VMEM((tm, epg * D_EXPERT), BF16),
            pltpu.VMEM((tm, D_MODEL), F32),
        ],
        compiler_params=_params("arbitrary", "arbitrary", "arbitrary"),
        name="moe",
    )(x, mod, router_w2, router_b, w13, w2, final_g)


def _pad_cols(w):
    n = w.shape[-1]
    return jnp.pad(w, ((0, 0), (0, -n % LANES)))


def _pad_rows(w):
    n = w.shape[0]
    return jnp.pad(w, ((0, -n % LANES), (0, 0)))


def kernel(x, c, ada_w, ada_b, mix_w_in, mix_w_out, sgu_norm_g, sgu_w, sgu_b, rwkv_mu, rwkv_w_rkv, rwkv_w_o, rwkv_w0, rwkv_w1, rwkv_w2, rwkv_a0, rwkv_a1, rwkv_a2, rwkv_v0, rwkv_v1, rwkv_v2, rwkv_g1, rwkv_g2, rwkv_k_k, rwkv_k_a, rwkv_r_k, rwkv_ln_w, rwkv_ln_b, router_w, router_b, moe_w1, moe_w3, moe_w2, final_norm_g):
    depth = ada_w.shape[0]
    row = lambda t: t.reshape(1, -1)
    mods = _ada_call(c, ada_w, ada_b)
    w13 = jnp.concatenate([moe_w1, moe_w3], axis=-1).astype(BF16)
    w2 = moe_w2.astype(BF16)
    rw_hi = router_w.astype(BF16)
    rw_lo = (router_w - rw_hi.astype(F32)).astype(BF16)
    router_w2 = jnp.concatenate([_pad_cols(rw_hi), _pad_cols(rw_lo)], axis=1)
    router_bc = router_b.reshape(N_EXPERTS, 1)
    final_g = row(final_norm_g)
    v_first = None
    for layer in range(depth):
        i = layer // 2
        mod = mods[2 * layer]
        if layer % 2 == 0:
            proj = _proj_call(x, mod, mix_w_in[i].astype(BF16))
            o_a = _attn_call(proj)
            bias_tile = jnp.repeat(sgu_b[i].T, HEAD_DIM, axis=1)
            x = _sgu_out_call(proj, o_a, x, mod, row(sgu_norm_g[i]), sgu_w[i], bias_tile,
                              mix_w_out[i].astype(BF16))
        else:
            weights = [
                rwkv_mu[i],
                rwkv_w_rkv[i, 0].astype(BF16), rwkv_w_rkv[i, 1].astype(BF16), rwkv_w_rkv[i, 2].astype(BF16),
                row(rwkv_w0[i]), _pad_cols(rwkv_w1[i]).astype(BF16), _pad_rows(rwkv_w2[i]).astype(BF16),
                row(rwkv_a0[i]), _pad_cols(rwkv_a1[i]).astype(BF16), _pad_rows(rwkv_a2[i]).astype(BF16),
                _pad_cols(rwkv_g1[i]).astype(BF16), _pad_rows(rwkv_g2[i]).astype(BF16),
                row(rwkv_k_k[i]), row(rwkv_k_a[i]),
            ]
            vres = None
            if i > 0:
                vres = [row(rwkv_v0[i - 1]), _pad_cols(rwkv_v1[i - 1]).astype(BF16),
                        _pad_rows(rwkv_v2[i - 1]).astype(BF16)]
            r, w_log, k, v, kk, a, g = _rwkv_pre_call(x, mod, weights, v_first, vres)
            if i == 0:
                v_first = v
            y = _wkv_call(r, w_log, k, v, kk, a, row(rwkv_ln_w[i]), row(rwkv_ln_b[i]), row(rwkv_r_k[i]))
            x = _rwkv_post_call(y, g, x, mod, rwkv_w_o[i].astype(BF16))
        x = _moe_call(x, mods[2 * layer + 1], router_w2, router_bc, w13, w2, final_g, layer, layer == depth - 1)
    return x
```

```python
import functools

import jax
import jax.numpy as jnp
from jax import lax
from jax.experimental import pallas as pl
from jax.experimental.pallas import tpu as pltpu

F32 = jnp.float32
BF16 = jnp.bfloat16
HIGHEST = lax.Precision.HIGHEST

D_MODEL = 1024
HEAD_DIM = 64
WIDTH_A = 512
WIDTH_B = 512
MIX_IN_WIDTH = 2560
ATTN_SPAN = 128
ATTN_DILATIONS = (1, 4, 16)
SGU_CHUNK = 128
N_GROUPS_B = 8
RWKV_HEADS = 16
RWKV_GN_EPS = 64e-5
WKV_CHUNK = 64
N_EXPERTS = 16
N_EXPERT_GROUPS = 4
EXPERTS_PER_GROUP = 4
D_EXPERT = 256
RMS_EPS = 1e-6
LANES = 128
VMEM_LIMIT = 56 * 1024 * 1024

NT_DIMS = (((1,), (1,)), ((), ()))
TN_DIMS = (((0,), (0,)), ((), ()))


def _params(*sem):
    return pltpu.CompilerParams(dimension_semantics=sem, vmem_limit_bytes=VMEM_LIMIT)


def _norm_mod(x, mod_ref):
    ms = jnp.mean(x * x, axis=-1, keepdims=True)
    return (x * lax.rsqrt(ms + RMS_EPS)) * (1.0 + mod_ref[0, 1:2, :]) + mod_ref[0, 0:1, :]


def _gelu(x):
    return 0.5 * x * (1.0 + lax.erf(x * 0.7071067811865476))


def _ada_body(c_ref, w_ref, b_ref, o_ref):
    c = c_ref[...]
    c_act = c * jax.nn.sigmoid(c)
    o_ref[0] = jnp.dot(c_act, w_ref[0], preferred_element_type=F32, precision=HIGHEST) + b_ref[0]


def _ada_call(c, ada_w, ada_b):
    n_pair = ada_w.shape[0] * ada_w.shape[1]
    batch = c.shape[0]
    w = ada_w.reshape(n_pair, D_MODEL, 3 * D_MODEL)
    b = ada_b.reshape(n_pair, 1, 3 * D_MODEL)
    out = pl.pallas_call(
        _ada_body,
        grid=(n_pair, 3),
        in_specs=[
            pl.BlockSpec((batch, D_MODEL), lambda p, j: (0, 0)),
            pl.BlockSpec((1, D_MODEL, D_MODEL), lambda p, j: (p, 0, j)),
            pl.BlockSpec((1, 1, D_MODEL), lambda p, j: (p, 0, j)),
        ],
        out_specs=pl.BlockSpec((1, batch, D_MODEL), lambda p, j: (p, 0, j)),
        out_shape=jax.ShapeDtypeStruct((n_pair, batch, 3 * D_MODEL), F32),
        compiler_params=_params("arbitrary", "arbitrary"),
        name="ada_mod",
    )(c, w, b)
    return out.reshape(n_pair, batch, 3, D_MODEL)


def _proj_body(x_ref, mod_ref, w_ref, o_ref):
    h = _norm_mod(x_ref[0], mod_ref)
    o_ref[0] = jnp.dot(h.astype(BF16), w_ref[...], preferred_element_type=F32)


def _proj_call(x, mod, w_in, ts=512):
    batch, seq, _ = x.shape
    return pl.pallas_call(
        _proj_body,
        grid=(batch, seq // ts),
        in_specs=[
            pl.BlockSpec((1, ts, D_MODEL), lambda b, i: (b, i, 0)),
            pl.BlockSpec((1, 3, D_MODEL), lambda b, i: (b, 0, 0)),
            pl.BlockSpec((D_MODEL, MIX_IN_WIDTH), lambda b, i: (0, 0)),
        ],
        out_specs=pl.BlockSpec((1, ts, MIX_IN_WIDTH), lambda b, i: (b, i, 0)),
        out_shape=jax.ShapeDtypeStruct((batch, seq, MIX_IN_WIDTH), F32),
        compiler_params=_params("arbitrary", "arbitrary"),
        name="mix_in_proj",
    )(x, mod, w_in)


def _attn_body(q_ref, k_ref, v_ref, o_ref, ob, lb):
    span = ATTN_SPAN
    W = LANES
    block_group = 4

    def iota(shape, dim):
        return lax.broadcasted_iota(jnp.int32, shape, dim)

    head0 = iota((span, W), 1) < HEAD_DIM
    mine = (head0, jnp.logical_not(head0))
    row2, col2 = iota((span, 2 * span), 0), iota((span, 2 * span), 1)
    mask_prev_cur = jnp.where(col2 < span, col2 - row2, row2 - (col2 - span)) >= 0
    mask_cur = iota((span, span), 1) <= iota((span, span), 0)

    def rows(start, n_rows, dil):
        if dil == 1:
            return pl.ds(pl.multiple_of(start, span), n_rows)
        return pl.ds(start, n_rows, stride=dil)

    def group(p, dil, q_starts, with_prev):
        blocks = range(len(q_starts))
        heads = range(2)
        qb, kcat, vaug, masks = [], [], [], []
        for j in blocks:
            n_keys = 2 * span if with_prev[j] else span
            k_start = q_starts[j] - span * dil if with_prev[j] else q_starts[j]
            qb.append(q_ref[0, rows(q_starts[j], span, dil), :] * (HEAD_DIM ** -0.5))
            kcat.append(k_ref[0, rows(k_start, n_keys, dil), :].astype(BF16))
            vb = v_ref[0, rows(k_start, n_keys, dil), :].astype(BF16)
            vaug.append(jnp.concatenate([vb, jnp.ones((n_keys, W), BF16)], axis=1))
            masks.append(mask_prev_cur if with_prev[j] else mask_cur)
        qh = [[jnp.where(mine[h], qb[j], 0.0).astype(BF16) for h in heads] for j in blocks]
        s = [[lax.dot_general(qh[j][h], kcat[j], NT_DIMS, preferred_element_type=F32) for h in heads] for j in blocks]
        s = [[jnp.where(masks[j], s[j][h], -jnp.inf) for h in heads] for j in blocks]
        m = [[jnp.max(s[j][h], axis=-1, keepdims=True) for h in heads] for j in blocks]
        pr = [[jnp.exp(s[j][h] - m[j][h]).astype(BF16) for h in heads] for j in blocks]
        ad = [[jnp.dot(pr[j][h], vaug[j], preferred_element_type=F32) for h in heads] for j in blocks]
        for j in blocks:
            den = [ad[j][h][:, W:] for h in heads]
            o = jnp.where(head0, ad[j][0][:, :W] / den[0], ad[j][1][:, :W] / den[1])
            lse = jnp.where(head0, m[j][0] + jnp.log(den[0]), m[j][1] + jnp.log(den[1]))
            qi = rows(q_starts[j], span, dil)
            ob[p, qi, :] = o
            lb[p, qi, :] = lse

    seq = q_ref.shape[1]
    for p, dil in enumerate(ATTN_DILATIONS):
        n_blk = seq // dil // span
        if n_blk == 1:
            def step(g, carry, p=p, dil=dil):
                group(p, dil, [g * block_group + j for j in range(block_group)], [False] * block_group)
                return carry
            lax.fori_loop(0, dil // block_group, step, 0)
        else:
            per_seq = n_blk // block_group

            def first(r, carry, p=p, dil=dil):
                group(p, dil, [r + j * span * dil for j in range(block_group)],
                      [False] + [True] * (block_group - 1))
                return carry

            def later(idx, carry, p=p, dil=dil, per_seq=per_seq):
                r = idx // (per_seq - 1)
                g = idx % (per_seq - 1) + 1
                group(p, dil, [r + (g * block_group + j) * span * dil for j in range(block_group)],
                      [True] * block_group)
                return carry

            lax.fori_loop(0, dil, first, 0)
            if per_seq > 1:
                lax.fori_loop(0, dil * (per_seq - 1), later, 0)

    l0, l1, l2 = lb[0], lb[1], lb[2]
    m = jnp.maximum(jnp.maximum(l0, l1), l2)
    w0, w1, w2 = jnp.exp(l0 - m), jnp.exp(l1 - m), jnp.exp(l2 - m)
    o_ref[0] = (w0 * ob[0] + w1 * ob[1] + w2 * ob[2]) / (w0 + w1 + w2)


def _attn_call(proj):
    batch, seq, _ = proj.shape
    n_pair = WIDTH_A // LANES
    blk = lambda off: pl.BlockSpec((1, seq, LANES), lambda b, h: (b, 0, off + h))
    return pl.pallas_call(
        _attn_body,
        grid=(batch, n_pair),
        in_specs=[blk(0), blk(n_pair), blk(2 * n_pair)],
        out_specs=pl.BlockSpec((1, seq, LANES), lambda b, h: (b, 0, h)),
        out_shape=jax.ShapeDtypeStruct((batch, seq, WIDTH_A), F32),
        scratch_shapes=[
            pltpu.VMEM((3, seq, LANES), F32),
            pltpu.VMEM((3, seq, LANES), F32),
        ],
        compiler_params=_params("arbitrary", "arbitrary"),
        name="dilated_attn",
    )(proj, proj, proj)


def _sgu_out_body(u_ref, z_ref, oa_ref, x_ref, mod_ref, ng_ref, sw_ref, sb_ref, wo_ref, o_ref, mix):
    ts = u_ref.shape[1]
    u = _gelu(u_ref[0])
    z = _gelu(z_ref[0])
    z = z * lax.rsqrt(jnp.mean(z * z, axis=-1, keepdims=True) + RMS_EPS) * ng_ref[...]
    zb = z.astype(BF16)
    row = lax.broadcasted_iota(jnp.int32, (SGU_CHUNK, SGU_CHUNK), 0)
    col = lax.broadcasted_iota(jnp.int32, (SGU_CHUNK, SGU_CHUNK), 1)
    causal = col <= row
    for g in range(N_GROUPS_B):
        w_g = jnp.where(causal, sw_ref[g], 0.0).astype(BF16)
        for cc in range(ts // SGU_CHUNK):
            rs = slice(cc * SGU_CHUNK, (cc + 1) * SGU_CHUNK)
            ls = slice(g * HEAD_DIM, (g + 1) * HEAD_DIM)
            mix[rs, ls] = jnp.dot(w_g, zb[rs, ls], preferred_element_type=F32)
    bias = jnp.concatenate([sb_ref[...]] * (ts // SGU_CHUNK), axis=0)
    o_b = u * (mix[...] + bias)
    y = jnp.dot(oa_ref[0].astype(BF16), wo_ref[0:WIDTH_A, :], preferred_element_type=F32)
    y = y + jnp.dot(o_b.astype(BF16), wo_ref[WIDTH_A:, :], preferred_element_type=F32)
    o_ref[0] = x_ref[0] + mod_ref[0, 2:3, :] * y


def _sgu_out_call(proj, o_a, x, mod, norm_g, sgu_w, sgu_bias_tile, w_out, ts=512):
    batch, seq, _ = x.shape
    u_blk = 3 * WIDTH_A // WIDTH_B
    full = lambda shape: pl.BlockSpec(shape, lambda b, i: (0,) * len(shape))
    return pl.pallas_call(
        _sgu_out_body,
        grid=(batch, seq // ts),
        in_specs=[
            pl.BlockSpec((1, ts, WIDTH_B), lambda b, i: (b, i, u_blk)),
            pl.BlockSpec((1, ts, WIDTH_B), lambda b, i: (b, i, u_blk + 1)),
            pl.BlockSpec((1, ts, WIDTH_A), lambda b, i: (b, i, 0)),
            pl.BlockSpec((1, ts, D_MODEL), lambda b, i: (b, i, 0)),
            pl.BlockSpec((1, 3, D_MODEL), lambda b, i: (b, 0, 0)),
            full((1, WIDTH_B)),
            full((N_GROUPS_B, SGU_CHUNK, SGU_CHUNK)),
            full((SGU_CHUNK, WIDTH_B)),
            full((WIDTH_A + WIDTH_B, D_MODEL)),
        ],
        out_specs=pl.BlockSpec((1, ts, D_MODEL), lambda b, i: (b, i, 0)),
        out_shape=jax.ShapeDtypeStruct(x.shape, F32),
        scratch_shapes=[pltpu.VMEM((ts, WIDTH_B), F32)],
        compiler_params=_params("arbitrary", "arbitrary"),
        name="sgu_out_proj",
    )(proj, proj, o_a, x, mod, norm_g, sgu_w, sgu_bias_tile, w_out)


def _rwkv_pre_body(has_vres, *refs):
    if has_vres:
        (x_ref, mod_ref, mu_ref, wr_ref, wk_ref, wv_ref, w0_ref, w1_ref, w2_ref, a0_ref, a1_ref, a2_ref,
         g1_ref, g2_ref, kk_ref, ka_ref, vf_ref, v0_ref, v1_ref, v2_ref,
         r_out, w_out, k_out, v_out, kk_out, a_out, g_out, carry) = refs
    else:
        (x_ref, mod_ref, mu_ref, wr_ref, wk_ref, wv_ref, w0_ref, w1_ref, w2_ref, a0_ref, a1_ref, a2_ref,
         g1_ref, g2_ref, kk_ref, ka_ref,
         r_out, w_out, k_out, v_out, kk_out, a_out, g_out, carry) = refs

    @pl.when(pl.program_id(1) == 0)
    def _():
        carry[...] = jnp.zeros_like(carry)

    h = _norm_mod(x_ref[0], mod_ref)
    ts = h.shape[0]
    first = lax.broadcasted_iota(jnp.int32, h.shape, 0) == 0
    h_prev = jnp.where(first, carry[0:1, :], pltpu.roll(h, 1, 0))
    carry[0:1, :] = h[ts - 1:ts, :]
    xx = h_prev - h

    def mixed(i):
        return (h + xx * mu_ref[i:i + 1, :]).astype(BF16)

    def mm(a, w_ref):
        return jnp.dot(a, w_ref[...], preferred_element_type=F32)

    xr, xw, xk, xv, xa, xg = [mixed(i) for i in range(6)]
    r = mm(xr, wr_ref)
    k = mm(xk, wk_ref)
    v = mm(xv, wv_ref)
    z = w0_ref[...] + mm(jnp.tanh(mm(xw, w1_ref)).astype(BF16), w2_ref)
    w_log = -(jnp.maximum(-z, 0.0) + jnp.log(1.0 + jnp.exp(-jnp.abs(z)))) - 0.5
    if has_vres:
        mix_v = jax.nn.sigmoid(v0_ref[...] + mm(mm(xv, v1_ref).astype(BF16), v2_ref))
        v = v + (vf_ref[0].astype(F32) - v) * mix_v
    a = jax.nn.sigmoid(a0_ref[...] + mm(mm(xa, a1_ref).astype(BF16), a2_ref))
    g = mm(jax.nn.sigmoid(mm(xg, g1_ref)).astype(BF16), g2_ref)
    r_out[0] = r.astype(BF16)
    w_out[0] = w_log
    kk_out[0] = (k * kk_ref[...]).astype(BF16)
    k_out[0] = (k * (1.0 + (a - 1.0) * ka_ref[...])).astype(BF16)
    v_out[0] = v.astype(BF16)
    a_out[0] = a.astype(BF16)
    g_out[0] = g.astype(BF16)


def _rwkv_pre_call(x, mod, weights, v_first, vres, ts=512):
    batch, seq, _ = x.shape
    tok = pl.BlockSpec((1, ts, D_MODEL), lambda b, i: (b, i, 0))
    full = lambda a: pl.BlockSpec(a.shape, lambda b, i: (0,) * a.ndim, pipeline_mode=pl.Buffered(1))
    ins = [x, mod] + list(weights)
    specs = [tok, pl.BlockSpec((1, 3, D_MODEL), lambda b, i: (b, 0, 0))] + [full(a) for a in weights]
    if vres is not None:
        ins += [v_first] + list(vres)
        specs += [tok] + [full(a) for a in vres]
    out_dtypes = [BF16, F32, BF16, BF16, BF16, BF16, BF16]
    return pl.pallas_call(
        functools.partial(_rwkv_pre_body, vres is not None),
        grid=(batch, seq // ts),
        in_specs=specs,
        out_specs=[tok] * 7,
        out_shape=[jax.ShapeDtypeStruct(x.shape, dt) for dt in out_dtypes],
        scratch_shapes=[pltpu.VMEM((8, D_MODEL), F32)],
        compiler_params=_params("arbitrary", "arbitrary"),
        name="rwkv_pre",
    )(*ins)


def _split_dot(x, ones):
    hi = x.astype(BF16)
    lo = (x - hi.astype(F32)).astype(BF16)
    rows = x.shape[0]
    both = jnp.dot(jnp.concatenate([hi, lo], axis=0), ones, preferred_element_type=F32)
    return both[:rows] + both[rows:]


def _wkv_body(r_ref, w_ref, k_ref, v_ref, kk_ref, a_ref, lnw_ref, lnb_ref, rk_ref, o_ref, state):
    C = WKV_CHUNK
    W = LANES
    n_chunk = r_ref.shape[1] // C
    n_pair = r_ref.shape[2] // W

    @pl.when(pl.program_id(2) == 0)
    def _():
        state[...] = jnp.zeros_like(state)

    def iota(shape, dim):
        return lax.broadcasted_iota(jnp.int32, shape, dim)

    def stack_heads(x):
        own = iota(x.shape, 1) % W < HEAD_DIM
        return jnp.concatenate([jnp.where(own, x, 0.0), jnp.where(own, 0.0, x)], axis=0).astype(BF16)

    def block_diag(x):
        left = iota(x.shape, 1) < C
        return jnp.concatenate([jnp.where(left, x, 0.0), jnp.where(left, 0.0, x)], axis=0).astype(BF16)

    row4, col4 = iota((C, 4 * C), 0), iota((C, 4 * C), 1) % C
    strict4 = col4 < row4
    lower4 = col4 <= row4
    tri = jnp.where(iota((C, C), 1) <= iota((C, C), 0), 1.0, 0.0).astype(BF16)
    eye2 = jnp.where(iota((C, 2 * C), 1) % C == iota((C, 2 * C), 0), 1.0, 0.0).astype(F32)
    rw, cw = iota((W, W), 0), iota((W, W), 1)
    same_head = (rw < HEAD_DIM) == (cw < HEAD_DIM)
    ones_bd = jnp.where(same_head, 1.0, 0.0).astype(BF16)
    diag_w = cw == rw
    zeros_cw = jnp.zeros((C, W), F32)

    def bdot(a, b):
        return jnp.dot(a.astype(BF16), b.astype(BF16), preferred_element_type=F32)

    chunks = range(n_chunk)
    pairs = range(n_pair)
    units = [(c, p) for c in chunks for p in pairs]
    rows = lambda t, c: t[c * C:(c + 1) * C]
    blk = lambda t, u: t[u[0] * C:(u[0] + 1) * C, u[1] * W:(u[1] + 1) * W]
    lanes = lambda t, p: t[:, p * W:(p + 1) * W]
    per_chunk = lambda vals: jnp.concatenate([jnp.broadcast_to(t, (C, n_pair * W)) for t in vals], axis=0)
    head_sums = lambda t: jnp.concatenate([_split_dot(lanes(t, p), ones_bd) for p in pairs], axis=1)

    r = r_ref[0].astype(F32)
    k = k_ref[0].astype(F32)
    v = v_ref[0].astype(F32)
    kk = kk_ref[0].astype(F32)
    kk = kk / jnp.maximum(jnp.sqrt(head_sums(kk * kk)), 1e-12)
    b = kk * a_ref[0].astype(F32)
    bonus = head_sums(r * k * rk_ref[...]) * v
    log_w = -jnp.exp(w_ref[0])
    hi = log_w.astype(BF16)
    lo = (log_w - hi.astype(F32)).astype(BF16)
    hilo = jnp.concatenate([hi, lo], axis=1)
    cum2 = [jnp.dot(tri, rows(hilo, c), preferred_element_type=F32) for c in chunks]
    cum = jnp.concatenate([c2[:, :n_pair * W] + c2[:, n_pair * W:] for c2 in cum2], axis=0)
    c_mid = [cum[c * C + C // 2 - 1:c * C + C // 2, :] for c in chunks]
    c_end = [cum[c * C + C - 1:(c + 1) * C, :] for c in chunks]
    g_last = [jnp.exp(t) for t in c_end]
    from_mid = jnp.exp(per_chunk(c_mid) - cum)
    to_mid = per_chunk([jnp.exp(-t) for t in c_mid])
    a_true = -kk * jnp.exp(cum - log_w)
    r_true = r * jnp.exp(cum)
    a_mid = a_true * to_mid
    r_mid = r_true * to_mid
    b_mid = b * from_mid
    k_mid = k * from_mid
    ar_mid = [jnp.concatenate([blk(a_mid, u), blk(r_mid, u)], axis=0).astype(BF16) for u in units]
    gram = [lax.dot_general(ar_mid[i], jnp.concatenate([stack_heads(blk(b_mid, u)), stack_heads(blk(k_mid, u))],
                                                        axis=0), NT_DIMS, preferred_element_type=F32)
            for i, u in enumerate(units)]
    top = [jnp.where(strict4, g[:C], 0.0) for g in gram]
    bot = [jnp.where(lower4, g[C:], 0.0).astype(BF16) for g in gram]
    n1 = [t[:, :2 * C] for t in top]
    inv = [eye2 + t for t in n1]
    pw = [bdot(t, block_diag(t)) for t in n1]
    levels = C.bit_length() - 2
    for lvl in range(levels):
        if lvl < levels - 1:
            st = [bdot(jnp.concatenate([pw[i], inv[i]], axis=0), block_diag(pw[i])) for i in range(len(units))]
            pw = [t[:C] for t in st]
            inv = [inv[i] + st[i][C:] for i in range(len(units))]
        else:
            inv = [inv[i] + bdot(inv[i], block_diag(pw[i])) for i in range(len(units))]
    x = [jnp.dot(top[i][:, 2 * C:].astype(BF16), stack_heads(blk(v, u)), preferred_element_type=F32)
         for i, u in enumerate(units)]
    pp = [jnp.dot(inv[i].astype(BF16), stack_heads(jnp.concatenate([blk(a_true, u), x[i]], axis=1)),
                  preferred_element_type=F32) for i, u in enumerate(units)]
    zv = [jnp.concatenate([zeros_cw, blk(v, u)], axis=1) for u in units]
    qq = [jnp.dot(bot[i], jnp.concatenate([stack_heads(pp[i]), stack_heads(zv[i])], axis=0),
                  preferred_element_type=F32) for i in range(len(units))]
    end_from_mid = [jnp.exp(c_end[c] - c_mid[c]) for c in chunks]
    bk_hat = [jnp.concatenate([blk(b_mid, u), blk(k_mid, u)], axis=0) * lanes(end_from_mid[u[0]], u[1])
              for u in units]
    mm = [lax.dot_general(bk_hat[i].astype(BF16), jnp.concatenate([pp[i], zv[i]], axis=0).astype(BF16), TN_DIMS,
                          preferred_element_type=F32) for i in range(len(units))]
    q1 = [blk(r_true, u) + qq[i][:, :W] for i, u in enumerate(units)]
    q2 = [t[:, W:] for t in qq]
    m1 = [(jnp.where(same_head, mm[i][:, :W], 0.0)
           + jnp.where(diag_w, jnp.broadcast_to(lanes(g_last[u[0]], u[1]), (W, W)), 0.0)).astype(BF16)
          for i, u in enumerate(units)]
    m2 = [jnp.where(same_head, t[:, W:], 0.0) for t in mm]
    h_cur = [state[p] for p in pairs]
    h_in = []
    for i, (c, p) in enumerate(units):
        h_in.append(h_cur[p])
        h_cur[p] = jnp.dot(m1[i], h_cur[p].astype(BF16), preferred_element_type=F32) + m2[i]
    for p in pairs:
        state[p] = h_cur[p]
    y_blk = [bdot(q1[i], h_in[i]) + q2[i] for i in range(len(units))]
    y = jnp.concatenate([jnp.concatenate([y_blk[c * n_pair + p] for p in pairs], axis=1) for c in chunks], axis=0)
    mean = head_sums(y) * (1.0 / HEAD_DIM)
    yc = y - mean
    var = head_sums(yc * yc) * (1.0 / HEAD_DIM)
    yn = yc * lax.rsqrt(var + RWKV_GN_EPS) * lnw_ref[...] + lnb_ref[...]
    o_ref[0] = (yn + bonus).astype(BF16)


def _wkv_call(r, w_log, k, v, kk, a, ln_w, ln_b, r_k, tc=512, pairs_per_step=4):
    batch, seq, _ = r.shape
    width = pairs_per_step * LANES
    tok = pl.BlockSpec((1, tc, width), lambda b, h, t: (b, t, h))
    vec = pl.BlockSpec((1, width), lambda b, h, t: (0, h))
    return pl.pallas_call(
        _wkv_body,
        grid=(batch, D_MODEL // width, seq // tc),
        in_specs=[tok] * 6 + [vec] * 3,
        out_specs=tok,
        out_shape=jax.ShapeDtypeStruct(r.shape, BF16),
        scratch_shapes=[pltpu.VMEM((pairs_per_step, LANES, LANES), F32)],
        compiler_params=_params("arbitrary", "arbitrary", "arbitrary"),
        name="wkv7_chunked",
    )(r, w_log, k, v, kk, a, ln_w, ln_b, r_k)


def _rwkv_post_body(y_ref, g_ref, x_ref, mod_ref, wo_ref, o_ref):
    yg = y_ref[0] * g_ref[0]
    o_ref[0] = x_ref[0] + mod_ref[0, 2:3, :] * jnp.dot(yg, wo_ref[...], preferred_element_type=F32)


def _rwkv_post_call(y, g, x, mod, w_o, ts=512):
    batch, seq, _ = x.shape
    tok = pl.BlockSpec((1, ts, D_MODEL), lambda b, i: (b, i, 0))
    return pl.pallas_call(
        _rwkv_post_body,
        grid=(batch, seq // ts),
        in_specs=[tok, tok, tok, pl.BlockSpec((1, 3, D_MODEL), lambda b, i: (b, 0, 0)),
                  pl.BlockSpec((D_MODEL, D_MODEL), lambda b, i: (0, 0))],
        out_specs=tok,
        out_shape=jax.ShapeDtypeStruct(x.shape, F32),
        compiler_params=_params("arbitrary", "arbitrary"),
        name="rwkv_out_proj",
    )(y, g, x, mod, w_o)


MOE_TILE = 512
MOE_BLOCK = 64
MOE_SLOTS = MOE_TILE + N_EXPERT_GROUPS * MOE_BLOCK
MOE_BLOCKS_PER_TILE = MOE_SLOTS // MOE_BLOCK
MOE_STEP_BLOCKS = 8


def _route(logits_t, rb):
    n_g, epg = N_EXPERT_GROUPS, EXPERTS_PER_GROUP
    s = jax.nn.sigmoid(logits_t)
    sel = s + rb
    s_rows = [s[i:i + 1, :] for i in range(N_EXPERTS)]
    sel_rows = [sel[i:i + 1, :] for i in range(N_EXPERTS)]
    scores = []
    for g in range(n_g):
        members = sel_rows[g * epg:(g + 1) * epg]
        best = None
        for i in range(epg):
            for j in range(i + 1, epg):
                pair = members[i] + members[j]
                best = pair if best is None else jnp.maximum(best, pair)
        scores.append(best)
    g_idx = jnp.zeros_like(scores[0], dtype=jnp.int32)
    top = scores[0]
    for g in range(1, n_g):
        better = scores[g] > top
        g_idx = jnp.where(better, g, g_idx)
        top = jnp.where(better, scores[g], top)

    def pick(rows_, j):
        out = rows_[j]
        for g in range(1, n_g):
            out = jnp.where(g_idx == g, rows_[g * epg + j], out)
        return out

    in_sel = [pick(sel_rows, j) for j in range(epg)]
    in_s = [pick(s_rows, j) for j in range(epg)]
    chosen = []
    for j in range(epg):
        rank = jnp.zeros_like(g_idx)
        for i in range(epg):
            if i == j:
                continue
            ahead = (in_sel[i] >= in_sel[j]) if i < j else (in_sel[i] > in_sel[j])
            rank = rank + jnp.where(ahead, 1, 0)
        chosen.append(rank < 2)
    den = sum(jnp.where(chosen[j], in_s[j], 0.0) for j in range(epg))
    gate_rows = []
    for ex in range(N_EXPERTS):
        g, j = divmod(ex, epg)
        on = jnp.logical_and(chosen[j], g_idx == g)
        gate_rows.append(jnp.where(on, in_s[j] / den, 0.0))
    return g_idx, gate_rows


def _slot_one_hot(pos_row):
    slot = lax.broadcasted_iota(jnp.int32, (MOE_SLOTS, pos_row.shape[1]), 0)
    return jnp.where(slot == pos_row, 1.0, 0.0).astype(BF16)


def _moe_sort_body(x_ref, mod_ref, rw_ref, rb_ref, hs_ref, gs_ref, pos_ref, cnt_ref):
    tm = x_ref.shape[1]
    n_g = N_EXPERT_GROUPS
    h = _norm_mod(x_ref[0], mod_ref)
    hi = h.astype(BF16)
    lo = (h - hi.astype(F32)).astype(BF16)
    parts = jnp.dot(jnp.concatenate([hi, lo], axis=0), rw_ref[...], preferred_element_type=F32)
    logits = (parts[:tm, :LANES] + parts[:tm, LANES:]) + (parts[tm:, :LANES] + parts[tm:, LANES:])
    g_idx, gate_rows = _route(logits.T[:N_EXPERTS], rb_ref[...])
    member = [jnp.where(g_idx == g, 1.0, 0.0) for g in range(n_g)]
    member8 = jnp.concatenate(member + [jnp.zeros((8 - n_g, tm), F32)], axis=0).astype(BF16)
    earlier = jnp.where(lax.broadcasted_iota(jnp.int32, (tm, tm), 0) < lax.broadcasted_iota(jnp.int32, (tm, tm), 1),
                        1.0, 0.0).astype(BF16)
    rank = jnp.dot(member8, earlier, preferred_element_type=F32)
    pos = jnp.zeros((1, tm), F32)
    offset = jnp.zeros((1, 1), F32)
    lane = lax.broadcasted_iota(jnp.int32, (1, LANES), 1)
    counts = jnp.zeros((1, LANES), F32)
    for g in range(n_g):
        n_members = jnp.sum(member[g], axis=1, keepdims=True)
        n_blocks = jnp.floor((n_members + (MOE_BLOCK - 1)) * (1.0 / MOE_BLOCK))
        pos = pos + member[g] * (offset + rank[g:g + 1, :])
        counts = counts + jnp.where(lane == g, n_blocks, 0.0)
        offset = offset + n_blocks * MOE_BLOCK
    pos_i = pos.astype(jnp.int32)
    one_hot = _slot_one_hot(pos_i)
    hs_ref[...] = jnp.dot(one_hot, hi, preferred_element_type=F32).astype(BF16)
    pad = jnp.zeros((LANES - N_EXPERTS, tm), F32)
    gate_t = jnp.concatenate(gate_rows + [pad], axis=0).T
    g_hi = gate_t.astype(BF16)
    g_lo = (gate_t - g_hi.astype(F32)).astype(BF16)
    gs_ref[...] = jnp.dot(one_hot, jnp.concatenate([g_hi, g_lo], axis=1), preferred_element_type=F32).astype(BF16)
    pos_ref[0] = pos_i
    cnt_ref[0] = counts.astype(jnp.int32)


def _moe_sort_call(x, mod, router_w2, router_b):
    batch, seq, _ = x.shape
    per_seq = seq // MOE_TILE
    n_tiles = batch * per_seq
    return pl.pallas_call(
        _moe_sort_body,
        grid=(batch, per_seq),
        in_specs=[
            pl.BlockSpec((1, MOE_TILE, D_MODEL), lambda b, i: (b, i, 0)),
            pl.BlockSpec((1, 3, D_MODEL), lambda b, i: (b, 0, 0)),
            pl.BlockSpec((D_MODEL, 2 * LANES), lambda b, i: (0, 0)),
            pl.BlockSpec((N_EXPERTS, 1), lambda b, i: (0, 0)),
        ],
        out_specs=[
            pl.BlockSpec((MOE_SLOTS, D_MODEL), lambda b, i: (b * per_seq + i, 0)),
            pl.BlockSpec((MOE_SLOTS, 2 * LANES), lambda b, i: (b * per_seq + i, 0)),
            pl.BlockSpec((1, 1, MOE_TILE), lambda b, i: (b * per_seq + i, 0, 0)),
            pl.BlockSpec((1, 1, LANES), lambda b, i: (b * per_seq + i, 0, 0)),
        ],
        out_shape=[
            jax.ShapeDtypeStruct((n_tiles * MOE_SLOTS, D_MODEL), BF16),
            jax.ShapeDtypeStruct((n_tiles * MOE_SLOTS, 2 * LANES), BF16),
            jax.ShapeDtypeStruct((n_tiles, 1, MOE_TILE), jnp.int32),
            jax.ShapeDtypeStruct((n_tiles, 1, LANES), jnp.int32),
        ],
        compiler_params=_params("arbitrary", "arbitrary"),
        name="moe_sort",
    )(x, mod, router_w2, router_b)


def _moe_step_tables(counts):
    n_tiles, n_g = counts.shape
    bpt, r = MOE_BLOCKS_PER_TILE, MOE_STEP_BLOCKS
    n_blocks = n_tiles * bpt
    n_steps = n_blocks // r + n_g
    start = jnp.cumsum(counts, axis=1) - counts
    local = jnp.arange(bpt, dtype=jnp.int32)[None, :, None]
    inside = (local >= start[:, None, :]) & (local < (start + counts)[:, None, :])
    block_group = jnp.where(inside.any(-1), jnp.argmax(inside, axis=-1), n_g).reshape(-1).astype(jnp.int32)
    order = jnp.argsort(block_group, stable=True).astype(jnp.int32)
    per_group = jnp.sum(counts, axis=0)
    group_first = jnp.cumsum(per_group) - per_group
    steps_per_group = (per_group + r - 1) // r
    step_end = jnp.cumsum(steps_per_group)
    s = jnp.arange(n_steps, dtype=jnp.int32)
    step_group = jnp.minimum(jnp.sum(s[:, None] >= step_end[None, :], axis=1), n_g - 1).astype(jnp.int32)
    step_local = s - (step_end - steps_per_group)[step_group]
    left = per_group[step_group] - step_local * r
    step_valid = jnp.clip(left, 0, r).astype(jnp.int32)
    j = jnp.arange(r, dtype=jnp.int32)[None, :]
    idx = group_first[step_group][:, None] + step_local[:, None] * r + j
    real = j < step_valid[:, None]
    step_blocks = jnp.where(real, order[jnp.clip(idx, 0, n_blocks - 1)], 0).astype(jnp.int32)
    flat_pos = (s[:, None] * r + j).astype(jnp.int32)
    block_pos = jnp.zeros((n_blocks,), jnp.int32).at[jnp.where(real, step_blocks, n_blocks).reshape(-1)].set(
        flat_pos.reshape(-1), mode="drop")
    return step_group, step_valid, step_blocks.reshape(-1), block_pos


def _moe_expert_body(layer, sg_ref, sv_ref, sb_ref, *refs):
    r = MOE_STEP_BLOCKS
    h_refs, g_refs = refs[:r], refs[r:2 * r]
    w13_ref, w2_ref, o_ref, act_scr = refs[2 * r:]
    step = pl.program_id(0)
    epg = EXPERTS_PER_GROUP
    rows_total = r * MOE_BLOCK

    @pl.when(sv_ref[step] > 0)
    def _():
        h = jnp.concatenate([ref[...] for ref in h_refs], axis=0)
        gates = jnp.concatenate([ref[...] for ref in g_refs], axis=0)
        real = lax.broadcasted_iota(jnp.int32, (rows_total, LANES), 0) < sv_ref[step] * MOE_BLOCK
        expert_row = lax.broadcasted_iota(jnp.int32, (2 * LANES, LANES), 0) % LANES
        for j in range(epg):
            pick_rows = jnp.where(expert_row == sg_ref[step] * epg + j, 1.0, 0.0).astype(BF16)
            gate_b = jnp.where(real, jnp.dot(gates, pick_rows, preferred_element_type=F32), 0.0)
            hid = jnp.dot(h, w13_ref[0, j], preferred_element_type=F32)
            h1 = hid[:, :D_EXPERT]
            act = h1 * jax.nn.sigmoid(h1) * hid[:, D_EXPERT:] * jnp.concatenate([gate_b, gate_b], axis=1)
            act_scr[:, j * D_EXPERT:(j + 1) * D_EXPERT] = act.astype(BF16)
        w2 = w2_ref[0].reshape(epg * D_EXPERT, D_MODEL)
        o_ref[...] = jnp.dot(act_scr[...], w2, preferred_element_type=F32).astype(BF16)


def _moe_expert_call(hs, gs, w13, w2, step_group, step_valid, step_blocks, layer):
    r = MOE_STEP_BLOCKS
    n_steps = step_group.shape[0]
    epg = EXPERTS_PER_GROUP
    rows_total = r * MOE_BLOCK

    def blk(width, j):
        return pl.BlockSpec((MOE_BLOCK, width), lambda s, sg, sv, sb, j=j: (sb[s * r + j], 0))

    grid_spec = pltpu.PrefetchScalarGridSpec(
        num_scalar_prefetch=3,
        grid=(n_steps,),
        in_specs=[blk(D_MODEL, j) for j in range(r)] + [blk(2 * LANES, j) for j in range(r)] + [
            pl.BlockSpec((1, epg, D_MODEL, 2 * D_EXPERT), lambda s, sg, sv, sb: (layer, sg[s], 0, 0)),
            pl.BlockSpec((1, epg, D_EXPERT, D_MODEL), lambda s, sg, sv, sb: (layer, sg[s], 0, 0)),
        ],
        out_specs=pl.BlockSpec((rows_total, D_MODEL), lambda s, sg, sv, sb: (s, 0)),
        scratch_shapes=[pltpu.VMEM((rows_total, epg * D_EXPERT), BF16)],
    )
    return pl.pallas_call(
        functools.partial(_moe_expert_body, layer),
        grid_spec=grid_spec,
        out_shape=jax.ShapeDtypeStruct((n_steps * rows_total, D_MODEL), BF16),
        compiler_params=_params("arbitrary"),
        name="moe_experts",
    )(step_group, step_valid, step_blocks, *([hs] * r), *([gs] * r), w13, w2)


def _moe_unsort_body(is_last, bp_ref, *refs):
    bpt = MOE_BLOCKS_PER_TILE
    y_refs = refs[:bpt]
    pos_ref, x_ref, mod_ref, fg_ref, o_ref = refs[bpt:]
    one_hot = _slot_one_hot(pos_ref[0])
    used = jnp.sum(one_hot.astype(F32), axis=1, keepdims=True) > 0.0
    ys = jnp.concatenate([ref[...] for ref in y_refs], axis=0)
    ys = jnp.where(used, ys, jnp.zeros_like(ys))
    y = lax.dot_general(one_hot, ys, TN_DIMS, preferred_element_type=F32)
    out = x_ref[0] + mod_ref[0, 2:3, :] * y
    if is_last:
        out = out * lax.rsqrt(jnp.mean(out * out, axis=-1, keepdims=True) + RMS_EPS) * fg_ref[...]
    o_ref[0] = out


def _moe_unsort_call(ys, pos, x, mod, final_g, block_pos, is_last):
    batch, seq, _ = x.shape
    per_seq = seq // MOE_TILE
    bpt = MOE_BLOCKS_PER_TILE

    def blk(j):
        return pl.BlockSpec((MOE_BLOCK, D_MODEL), lambda b, i, bp, j=j: (bp[(b * per_seq + i) * bpt + j], 0))

    grid_spec = pltpu.PrefetchScalarGridSpec(
        num_scalar_prefetch=1,
        grid=(batch, per_seq),
        in_specs=[blk(j) for j in range(bpt)] + [
            pl.BlockSpec((1, 1, MOE_TILE), lambda b, i, bp: (b * per_seq + i, 0, 0)),
            pl.BlockSpec((1, MOE_TILE, D_MODEL), lambda b, i, bp: (b, i, 0)),
            pl.BlockSpec((1, 3, D_MODEL), lambda b, i, bp: (b, 0, 0)),
            pl.BlockSpec((1, D_MODEL), lambda b, i, bp: (0, 0)),
        ],
        out_specs=pl.BlockSpec((1, MOE_TILE, D_MODEL), lambda b, i, bp: (b, i, 0)),
    )
    return pl.pallas_call(
        functools.partial(_moe_unsort_body, is_last),
        grid_spec=grid_spec,
        out_shape=jax.ShapeDtypeStruct(x.shape, F32),
        compiler_params=_params("arbitrary", "arbitrary"),
        name="moe_unsort",
    )(block_pos, *([ys] * bpt), pos, x, mod, final_g)


def _moe_layer(x, mod, router_w2, router_b, w13, w2, final_g, layer, is_last):
    hs, gs, pos, cnt = _moe_sort_call(x, mod, router_w2, router_b)
    step_group, step_valid, step_blocks, block_pos = _moe_step_tables(cnt[:, 0, :N_EXPERT_GROUPS])
    ys = _moe_expert_call(hs, gs, w13, w2, step_group, step_valid, step_blocks, layer)
    return _moe_unsort_call(ys, pos, x, mod, final_g, block_pos, is_last)


def _pad_cols(w):
    n = w.shape[-1]
    return jnp.pad(w, ((0, 0), (0, -n % LANES)))


def _pad_rows(w):
    n = w.shape[0]
    return jnp.pad(w, ((0, -n % LANES), (0, 0)))


def kernel(x, c, ada_w, ada_b, mix_w_in, mix_w_out, sgu_norm_g, sgu_w, sgu_b, rwkv_mu, rwkv_w_rkv, rwkv_w_o, rwkv_w0, rwkv_w1, rwkv_w2, rwkv_a0, rwkv_a1, rwkv_a2, rwkv_v0, rwkv_v1, rwkv_v2, rwkv_g1, rwkv_g2, rwkv_k_k, rwkv_k_a, rwkv_r_k, rwkv_ln_w, rwkv_ln_b, router_w, router_b, moe_w1, moe_w3, moe_w2, final_norm_g):
    depth = ada_w.shape[0]
    row = lambda t: t.reshape(1, -1)
    mods = _ada_call(c, ada_w, ada_b)
    w13 = jnp.concatenate([moe_w1, moe_w3], axis=-1).astype(BF16)
    w2 = moe_w2.astype(BF16)
    rw_hi = router_w.astype(BF16)
    rw_lo = (router_w - rw_hi.astype(F32)).astype(BF16)
    router_w2 = jnp.concatenate([_pad_cols(rw_hi), _pad_cols(rw_lo)], axis=1)
    router_bc = router_b.reshape(N_EXPERTS, 1)
    final_g = row(final_norm_g)
    v_first = None
    for layer in range(depth):
        i = layer // 2
        mod = mods[2 * layer]
        if layer % 2 == 0:
            proj = _proj_call(x, mod, mix_w_in[i].astype(BF16))
            o_a = _attn_call(proj)
            bias_tile = jnp.repeat(sgu_b[i].T, HEAD_DIM, axis=1)
            x = _sgu_out_call(proj, o_a, x, mod, row(sgu_norm_g[i]), sgu_w[i], bias_tile,
                              mix_w_out[i].astype(BF16))
        else:
            weights = [
                rwkv_mu[i],
                rwkv_w_rkv[i, 0].astype(BF16), rwkv_w_rkv[i, 1].astype(BF16), rwkv_w_rkv[i, 2].astype(BF16),
                row(rwkv_w0[i]), _pad_cols(rwkv_w1[i]).astype(BF16), _pad_rows(rwkv_w2[i]).astype(BF16),
                row(rwkv_a0[i]), _pad_cols(rwkv_a1[i]).astype(BF16), _pad_rows(rwkv_a2[i]).astype(BF16),
                _pad_cols(rwkv_g1[i]).astype(BF16), _pad_rows(rwkv_g2[i]).astype(BF16),
                row(rwkv_k_k[i]), row(rwkv_k_a[i]),
            ]
            vres = None
            if i > 0:
                vres = [row(rwkv_v0[i - 1]), _pad_cols(rwkv_v1[i - 1]).astype(BF16),
                        _pad_rows(rwkv_v2[i - 1]).astype(BF16)]
            r, w_log, k, v, kk, a, g = _rwkv_pre_call(x, mod, weights, v_first, vres)
            if i == 0:
                v_first = v
            y = _wkv_call(r, w_log, k, v, kk, a, row(rwkv_ln_w[i]), row(rwkv_ln_b[i]), row(rwkv_r_k[i]))
            x = _rwkv_post_call(y, g, x, mod, rwkv_w_o[i].astype(BF16))
        x = _moe_layer(x, mods[2 * layer + 1], router_w2, router_bc, w13, w2, final_g, layer, layer == depth - 1)
    return x
```

```python
import functools

import jax
import jax.numpy as jnp
from jax import lax
from jax.experimental import pallas as pl
from jax.experimental.pallas import tpu as pltpu

F32 = jnp.float32
BF16 = jnp.bfloat16

D_MODEL = 1024
HEAD_DIM = 64
WIDTH_A = 512
WIDTH_B = 512
MIX_IN_WIDTH = 2560
ATTN_SPAN = 128
ATTN_DILATIONS = (1, 4, 16)
SGU_CHUNK = 128
N_GROUPS_B = 8
RWKV_HEADS = 16
RWKV_GN_EPS = 64e-5
WKV_CHUNK = 64
N_EXPERTS = 16
N_EXPERT_GROUPS = 4
EXPERTS_PER_GROUP = 4
D_EXPERT = 256
RMS_EPS = 1e-6
LANES = 128
VMEM_LIMIT = 56 * 1024 * 1024

NT_DIMS = (((1,), (1,)), ((), ()))
TN_DIMS = (((0,), (0,)), ((), ()))


def _params(*sem):
    return pltpu.CompilerParams(dimension_semantics=sem, vmem_limit_bytes=VMEM_LIMIT)


def _norm_mod(x, mod_ref):
    ms = jnp.mean(x * x, axis=-1, keepdims=True)
    return (x * lax.rsqrt(ms + RMS_EPS)) * (1.0 + mod_ref[0, 1:2, :]) + mod_ref[0, 0:1, :]


def _gelu(x):
    return 0.5 * x * (1.0 + lax.erf(x * 0.7071067811865476))


def _ada_body(c_ref, w_ref, b_ref, o_ref):
    c = c_ref[...]
    c_act = c * jax.nn.sigmoid(c)
    n = c.shape[0]
    c_hi = c_act.astype(BF16)
    c_lo = (c_act - c_hi.astype(F32)).astype(BF16)
    c2 = jnp.concatenate([c_hi, c_lo], axis=0)
    w = w_ref[0]
    w_hi = w.astype(BF16)
    w_lo = (w - w_hi.astype(F32)).astype(BF16)
    by_hi = jnp.dot(c2, w_hi, preferred_element_type=F32)
    by_lo = jnp.dot(c2, w_lo, preferred_element_type=F32)
    o_ref[0] = (by_hi[:n] + by_hi[n:]) + (by_lo[:n] + by_lo[n:]) + b_ref[0]


def _ada_call(c, ada_w, ada_b):
    n_pair = ada_w.shape[0] * ada_w.shape[1]
    batch = c.shape[0]
    w = ada_w.reshape(n_pair, D_MODEL, 3 * D_MODEL)
    b = ada_b.reshape(n_pair, 1, 3 * D_MODEL)
    out = pl.pallas_call(
        _ada_body,
        grid=(n_pair, 3),
        in_specs=[
            pl.BlockSpec((batch, D_MODEL), lambda p, j: (0, 0)),
            pl.BlockSpec((1, D_MODEL, D_MODEL), lambda p, j: (p, 0, j)),
            pl.BlockSpec((1, 1, D_MODEL), lambda p, j: (p, 0, j)),
        ],
        out_specs=pl.BlockSpec((1, batch, D_MODEL), lambda p, j: (p, 0, j)),
        out_shape=jax.ShapeDtypeStruct((n_pair, batch, 3 * D_MODEL), F32),
        compiler_params=_params("arbitrary", "arbitrary"),
        name="ada_mod",
    )(c, w, b)
    return out.reshape(n_pair, batch, 3, D_MODEL)


def _proj_body(x_ref, mod_ref, w_ref, o_ref):
    h = _norm_mod(x_ref[0], mod_ref)
    o_ref[0] = jnp.dot(h.astype(BF16), w_ref[...], preferred_element_type=F32)


def _proj_call(x, mod, w_in, ts=512):
    batch, seq, _ = x.shape
    return pl.pallas_call(
        _proj_body,
        grid=(batch, seq // ts),
        in_specs=[
            pl.BlockSpec((1, ts, D_MODEL), lambda b, i: (b, i, 0)),
            pl.BlockSpec((1, 3, D_MODEL), lambda b, i: (b, 0, 0)),
            pl.BlockSpec((D_MODEL, MIX_IN_WIDTH), lambda b, i: (0, 0)),
        ],
        out_specs=pl.BlockSpec((1, ts, MIX_IN_WIDTH), lambda b, i: (b, i, 0)),
        out_shape=jax.ShapeDtypeStruct((batch, seq, MIX_IN_WIDTH), F32),
        compiler_params=_params("arbitrary", "arbitrary"),
        name="mix_in_proj",
    )(x, mod, w_in)


def _attn_body(q_ref, k_ref, v_ref, o_ref, ob, lb):
    span = ATTN_SPAN
    W = LANES
    block_group = 4

    def iota(shape, dim):
        return lax.broadcasted_iota(jnp.int32, shape, dim)

    head0 = iota((span, W), 1) < HEAD_DIM
    mine = (head0, jnp.logical_not(head0))
    row2, col2 = iota((span, 2 * span), 0), iota((span, 2 * span), 1)
    mask_prev_cur = jnp.where(col2 < span, col2 - row2, row2 - (col2 - span)) >= 0
    mask_cur = iota((span, span), 1) <= iota((span, span), 0)

    def rows(start, n_rows, dil):
        if dil == 1:
            return pl.ds(pl.multiple_of(start, span), n_rows)
        return pl.ds(start, n_rows, stride=dil)

    def group(p, dil, q_starts, with_prev):
        blocks = range(len(q_starts))
        heads = range(2)
        qb, kcat, vaug, masks = [], [], [], []
        for j in blocks:
            n_keys = 2 * span if with_prev[j] else span
            k_start = q_starts[j] - span * dil if with_prev[j] else q_starts[j]
            qb.append(q_ref[0, rows(q_starts[j], span, dil), :] * (HEAD_DIM ** -0.5))
            kcat.append(k_ref[0, rows(k_start, n_keys, dil), :].astype(BF16))
            vb = v_ref[0, rows(k_start, n_keys, dil), :].astype(BF16)
            vaug.append(jnp.concatenate([vb, jnp.ones((n_keys, W), BF16)], axis=1))
            masks.append(mask_prev_cur if with_prev[j] else mask_cur)
        qh = [[jnp.where(mine[h], qb[j], 0.0).astype(BF16) for h in heads] for j in blocks]
        s = [[lax.dot_general(qh[j][h], kcat[j], NT_DIMS, preferred_element_type=F32) for h in heads] for j in blocks]
        s = [[jnp.where(masks[j], s[j][h], -jnp.inf) for h in heads] for j in blocks]
        m = [[jnp.max(s[j][h], axis=-1, keepdims=True) for h in heads] for j in blocks]
        pr = [[jnp.exp(s[j][h] - m[j][h]).astype(BF16) for h in heads] for j in blocks]
        ad = [[jnp.dot(pr[j][h], vaug[j], preferred_element_type=F32) for h in heads] for j in blocks]
        for j in blocks:
            den = [ad[j][h][:, W:] for h in heads]
            o = jnp.where(head0, ad[j][0][:, :W] / den[0], ad[j][1][:, :W] / den[1])
            lse = jnp.where(head0, m[j][0] + jnp.log(den[0]), m[j][1] + jnp.log(den[1]))
            qi = rows(q_starts[j], span, dil)
            ob[p, qi, :] = o
            lb[p, qi, :] = lse

    seq = q_ref.shape[1]
    for p, dil in enumerate(ATTN_DILATIONS):
        n_blk = seq // dil // span
        if n_blk == 1:
            def step(g, carry, p=p, dil=dil):
                group(p, dil, [g * block_group + j for j in range(block_group)], [False] * block_group)
                return carry
            lax.fori_loop(0, dil // block_group, step, 0)
        else:
            per_seq = n_blk // block_group

            def first(r, carry, p=p, dil=dil):
                group(p, dil, [r + j * span * dil for j in range(block_group)],
                      [False] + [True] * (block_group - 1))
                return carry

            def later(idx, carry, p=p, dil=dil, per_seq=per_seq):
                r = idx // (per_seq - 1)
                g = idx % (per_seq - 1) + 1
                group(p, dil, [r + (g * block_group + j) * span * dil for j in range(block_group)],
                      [True] * block_group)
                return carry

            lax.fori_loop(0, dil, first, 0)
            if per_seq > 1:
                lax.fori_loop(0, dil * (per_seq - 1), later, 0)

    l0, l1, l2 = lb[0], lb[1], lb[2]
    m = jnp.maximum(jnp.maximum(l0, l1), l2)
    w0, w1, w2 = jnp.exp(l0 - m), jnp.exp(l1 - m), jnp.exp(l2 - m)
    o_ref[0] = (w0 * ob[0] + w1 * ob[1] + w2 * ob[2]) / (w0 + w1 + w2)


def _attn_call(proj):
    batch, seq, _ = proj.shape
    n_pair = WIDTH_A // LANES
    blk = lambda off: pl.BlockSpec((1, seq, LANES), lambda b, h: (b, 0, off + h))
    return pl.pallas_call(
        _attn_body,
        grid=(batch, n_pair),
        in_specs=[blk(0), blk(n_pair), blk(2 * n_pair)],
        out_specs=pl.BlockSpec((1, seq, LANES), lambda b, h: (b, 0, h)),
        out_shape=jax.ShapeDtypeStruct((batch, seq, WIDTH_A), F32),
        scratch_shapes=[
            pltpu.VMEM((3, seq, LANES), F32),
            pltpu.VMEM((3, seq, LANES), F32),
        ],
        compiler_params=_params("arbitrary", "arbitrary"),
        name="dilated_attn",
    )(proj, proj, proj)


def _sgu_out_body(u_ref, z_ref, oa_ref, x_ref, mod_ref, ng_ref, sw_ref, sb_ref, wo_ref, o_ref, mix):
    ts = u_ref.shape[1]
    u = _gelu(u_ref[0])
    z = _gelu(z_ref[0])
    z = z * lax.rsqrt(jnp.mean(z * z, axis=-1, keepdims=True) + RMS_EPS) * ng_ref[...]
    zb = z.astype(BF16)
    row = lax.broadcasted_iota(jnp.int32, (SGU_CHUNK, SGU_CHUNK), 0)
    col = lax.broadcasted_iota(jnp.int32, (SGU_CHUNK, SGU_CHUNK), 1)
    causal = col <= row
    for g in range(N_GROUPS_B):
        w_g = jnp.where(causal, sw_ref[g], 0.0).astype(BF16)
        for cc in range(ts // SGU_CHUNK):
            rs = slice(cc * SGU_CHUNK, (cc + 1) * SGU_CHUNK)
            ls = slice(g * HEAD_DIM, (g + 1) * HEAD_DIM)
            mix[rs, ls] = jnp.dot(w_g, zb[rs, ls], preferred_element_type=F32)
    bias = jnp.concatenate([sb_ref[...]] * (ts // SGU_CHUNK), axis=0)
    o_b = u * (mix[...] + bias)
    y = jnp.dot(oa_ref[0].astype(BF16), wo_ref[0:WIDTH_A, :], preferred_element_type=F32)
    y = y + jnp.dot(o_b.astype(BF16), wo_ref[WIDTH_A:, :], preferred_element_type=F32)
    o_ref[0] = x_ref[0] + mod_ref[0, 2:3, :] * y


def _sgu_out_call(proj, o_a, x, mod, norm_g, sgu_w, sgu_bias_tile, w_out, ts=512):
    batch, seq, _ = x.shape
    u_blk = 3 * WIDTH_A // WIDTH_B
    full = lambda shape: pl.BlockSpec(shape, lambda b, i: (0,) * len(shape))
    return pl.pallas_call(
        _sgu_out_body,
        grid=(batch, seq // ts),
        in_specs=[
            pl.BlockSpec((1, ts, WIDTH_B), lambda b, i: (b, i, u_blk)),
            pl.BlockSpec((1, ts, WIDTH_B), lambda b, i: (b, i, u_blk + 1)),
            pl.BlockSpec((1, ts, WIDTH_A), lambda b, i: (b, i, 0)),
            pl.BlockSpec((1, ts, D_MODEL), lambda b, i: (b, i, 0)),
            pl.BlockSpec((1, 3, D_MODEL), lambda b, i: (b, 0, 0)),
            full((1, WIDTH_B)),
            full((N_GROUPS_B, SGU_CHUNK, SGU_CHUNK)),
            full((SGU_CHUNK, WIDTH_B)),
            full((WIDTH_A + WIDTH_B, D_MODEL)),
        ],
        out_specs=pl.BlockSpec((1, ts, D_MODEL), lambda b, i: (b, i, 0)),
        out_shape=jax.ShapeDtypeStruct(x.shape, F32),
        scratch_shapes=[pltpu.VMEM((ts, WIDTH_B), F32)],
        compiler_params=_params("arbitrary", "arbitrary"),
        name="sgu_out_proj",
    )(proj, proj, o_a, x, mod, norm_g, sgu_w, sgu_bias_tile, w_out)


def _rwkv_pre_body(has_vres, *refs):
    if has_vres:
        (x_ref, mod_ref, mu_ref, wr_ref, wk_ref, wv_ref, w0_ref, w1_ref, w2_ref, a0_ref, a1_ref, a2_ref,
         g1_ref, g2_ref, kk_ref, ka_ref, vf_ref, v0_ref, v1_ref, v2_ref,
         r_out, w_out, k_out, v_out, kk_out, a_out, g_out, carry) = refs
    else:
        (x_ref, mod_ref, mu_ref, wr_ref, wk_ref, wv_ref, w0_ref, w1_ref, w2_ref, a0_ref, a1_ref, a2_ref,
         g1_ref, g2_ref, kk_ref, ka_ref,
         r_out, w_out, k_out, v_out, kk_out, a_out, g_out, carry) = refs

    @pl.when(pl.program_id(1) == 0)
    def _():
        carry[...] = jnp.zeros_like(carry)

    h = _norm_mod(x_ref[0], mod_ref)
    ts = h.shape[0]
    first = lax.broadcasted_iota(jnp.int32, h.shape, 0) == 0
    h_prev = jnp.where(first, carry[0:1, :], pltpu.roll(h, 1, 0))
    carry[0:1, :] = h[ts - 1:ts, :]
    xx = h_prev - h

    def mixed(i):
        return (h + xx * mu_ref[i:i + 1, :]).astype(BF16)

    def mm(a, w_ref):
        return jnp.dot(a, w_ref[...], preferred_element_type=F32)

    xr, xw, xk, xv, xa, xg = [mixed(i) for i in range(6)]
    r = mm(xr, wr_ref)
    k = mm(xk, wk_ref)
    v = mm(xv, wv_ref)
    z = w0_ref[...] + mm(jnp.tanh(mm(xw, w1_ref)).astype(BF16), w2_ref)
    w_log = -(jnp.maximum(-z, 0.0) + jnp.log(1.0 + jnp.exp(-jnp.abs(z)))) - 0.5
    if has_vres:
        mix_v = jax.nn.sigmoid(v0_ref[...] + mm(mm(xv, v1_ref).astype(BF16), v2_ref))
        v = v + (vf_ref[0].astype(F32) - v) * mix_v
    a = jax.nn.sigmoid(a0_ref[...] + mm(mm(xa, a1_ref).astype(BF16), a2_ref))
    g = mm(jax.nn.sigmoid(mm(xg, g1_ref)).astype(BF16), g2_ref)
    r_out[0] = r.astype(BF16)
    w_out[0] = w_log
    kk_out[0] = (k * kk_ref[...]).astype(BF16)
    k_out[0] = (k * (1.0 + (a - 1.0) * ka_ref[...])).astype(BF16)
    v_out[0] = v.astype(BF16)
    a_out[0] = a.astype(BF16)
    g_out[0] = g.astype(BF16)


def _rwkv_pre_call(x, mod, weights, v_first, vres, ts=512):
    batch, seq, _ = x.shape
    tok = pl.BlockSpec((1, ts, D_MODEL), lambda b, i: (b, i, 0))
    full = lambda a: pl.BlockSpec(a.shape, lambda b, i: (0,) * a.ndim, pipeline_mode=pl.Buffered(1))
    ins = [x, mod] + list(weights)
    specs = [tok, pl.BlockSpec((1, 3, D_MODEL), lambda b, i: (b, 0, 0))] + [full(a) for a in weights]
    if vres is not None:
        ins += [v_first] + list(vres)
        specs += [tok] + [full(a) for a in vres]
    out_dtypes = [BF16, F32, BF16, BF16, BF16, BF16, BF16]
    return pl.pallas_call(
        functools.partial(_rwkv_pre_body, vres is not None),
        grid=(batch, seq // ts),
        in_specs=specs,
        out_specs=[tok] * 7,
        out_shape=[jax.ShapeDtypeStruct(x.shape, dt) for dt in out_dtypes],
        scratch_shapes=[pltpu.VMEM((8, D_MODEL), F32)],
        compiler_params=_params("arbitrary", "arbitrary"),
        name="rwkv_pre",
    )(*ins)


def _split_dot(x, ones):
    hi = x.astype(BF16)
    lo = (x - hi.astype(F32)).astype(BF16)
    rows = x.shape[0]
    both = jnp.dot(jnp.concatenate([hi, lo], axis=0), ones, preferred_element_type=F32)
    return both[:rows] + both[rows:]


def _wkv_body(r_ref, w_ref, k_ref, v_ref, kk_ref, a_ref, lnw_ref, lnb_ref, rk_ref, o_ref, state):
    C = WKV_CHUNK
    W = LANES
    n_chunk = r_ref.shape[1] // C
    n_pair = r_ref.shape[2] // W

    @pl.when(pl.program_id(2) == 0)
    def _():
        state[...] = jnp.zeros_like(state)

    def iota(shape, dim):
        return lax.broadcasted_iota(jnp.int32, shape, dim)

    def stack_heads(x):
        own = iota(x.shape, 1) % W < HEAD_DIM
        return jnp.concatenate([jnp.where(own, x, 0.0), jnp.where(own, 0.0, x)], axis=0).astype(BF16)

    def block_diag(x):
        left = iota(x.shape, 1) < C
        return jnp.concatenate([jnp.where(left, x, 0.0), jnp.where(left, 0.0, x)], axis=0).astype(BF16)

    row4, col4 = iota((C, 4 * C), 0), iota((C, 4 * C), 1) % C
    strict4 = col4 < row4
    lower4 = col4 <= row4
    tri = jnp.where(iota((C, C), 1) <= iota((C, C), 0), 1.0, 0.0).astype(BF16)
    eye2 = jnp.where(iota((C, 2 * C), 1) % C == iota((C, 2 * C), 0), 1.0, 0.0).astype(F32)
    rw, cw = iota((W, W), 0), iota((W, W), 1)
    same_head = (rw < HEAD_DIM) == (cw < HEAD_DIM)
    ones_bd = jnp.where(same_head, 1.0, 0.0).astype(BF16)
    diag_w = cw == rw
    zeros_cw = jnp.zeros((C, W), F32)

    def bdot(a, b):
        return jnp.dot(a.astype(BF16), b.astype(BF16), preferred_element_type=F32)

    chunks = range(n_chunk)
    pairs = range(n_pair)
    units = [(c, p) for c in chunks for p in pairs]
    rows = lambda t, c: t[c * C:(c + 1) * C]
    blk = lambda t, u: t[u[0] * C:(u[0] + 1) * C, u[1] * W:(u[1] + 1) * W]
    lanes = lambda t, p: t[:, p * W:(p + 1) * W]
    per_chunk = lambda vals: jnp.concatenate([jnp.broadcast_to(t, (C, n_pair * W)) for t in vals], axis=0)
    head_sums = lambda t: jnp.concatenate([_split_dot(lanes(t, p), ones_bd) for p in pairs], axis=1)

    r = r_ref[0].astype(F32)
    k = k_ref[0].astype(F32)
    v = v_ref[0].astype(F32)
    kk = kk_ref[0].astype(F32)
    kk = kk / jnp.maximum(jnp.sqrt(head_sums(kk * kk)), 1e-12)
    b = kk * a_ref[0].astype(F32)
    bonus = head_sums(r * k * rk_ref[...]) * v
    log_w = -jnp.exp(w_ref[0])
    hi = log_w.astype(BF16)
    lo = (log_w - hi.astype(F32)).astype(BF16)
    hilo = jnp.concatenate([hi, lo], axis=1)
    cum2 = [jnp.dot(tri, rows(hilo, c), preferred_element_type=F32) for c in chunks]
    cum = jnp.concatenate([c2[:, :n_pair * W] + c2[:, n_pair * W:] for c2 in cum2], axis=0)
    c_mid = [cum[c * C + C // 2 - 1:c * C + C // 2, :] for c in chunks]
    c_end = [cum[c * C + C - 1:(c + 1) * C, :] for c in chunks]
    g_last = [jnp.exp(t) for t in c_end]
    from_mid = jnp.exp(per_chunk(c_mid) - cum)
    to_mid = per_chunk([jnp.exp(-t) for t in c_mid])
    a_true = -kk * jnp.exp(cum - log_w)
    r_true = r * jnp.exp(cum)
    a_mid = a_true * to_mid
    r_mid = r_true * to_mid
    b_mid = b * from_mid
    k_mid = k * from_mid
    ar_mid = [jnp.concatenate([blk(a_mid, u), blk(r_mid, u)], axis=0).astype(BF16) for u in units]
    gram = [lax.dot_general(ar_mid[i], jnp.concatenate([stack_heads(blk(b_mid, u)), stack_heads(blk(k_mid, u))],
                                                        axis=0), NT_DIMS, preferred_element_type=F32)
            for i, u in enumerate(units)]
    top = [jnp.where(strict4, g[:C], 0.0) for g in gram]
    bot = [jnp.where(lower4, g[C:], 0.0).astype(BF16) for g in gram]
    n1 = [t[:, :2 * C] for t in top]
    inv = [eye2 + t for t in n1]
    pw = [bdot(t, block_diag(t)) for t in n1]
    levels = C.bit_length() - 2
    for lvl in range(levels):
        if lvl < levels - 1:
            st = [bdot(jnp.concatenate([pw[i], inv[i]], axis=0), block_diag(pw[i])) for i in range(len(units))]
            pw = [t[:C] for t in st]
            inv = [inv[i] + st[i][C:] for i in range(len(units))]
        else:
            inv = [inv[i] + bdot(inv[i], block_diag(pw[i])) for i in range(len(units))]
    x = [jnp.dot(top[i][:, 2 * C:].astype(BF16), stack_heads(blk(v, u)), preferred_element_type=F32)
         for i, u in enumerate(units)]
    pp = [jnp.dot(inv[i].astype(BF16), stack_heads(jnp.concatenate([blk(a_true, u), x[i]], axis=1)),
                  preferred_element_type=F32) for i, u in enumerate(units)]
    zv = [jnp.concatenate([zeros_cw, blk(v, u)], axis=1) for u in units]
    qq = [jnp.dot(bot[i], jnp.concatenate([stack_heads(pp[i]), stack_heads(zv[i])], axis=0),
                  preferred_element_type=F32) for i in range(len(units))]
    end_from_mid = [jnp.exp(c_end[c] - c_mid[c]) for c in chunks]
    bk_hat = [jnp.concatenate([blk(b_mid, u), blk(k_mid, u)], axis=0) * lanes(end_from_mid[u[0]], u[1])
              for u in units]
    mm = [lax.dot_general(bk_hat[i].astype(BF16), jnp.concatenate([pp[i], zv[i]], axis=0).astype(BF16), TN_DIMS,
                          preferred_element_type=F32) for i in range(len(units))]
    q1 = [blk(r_true, u) + qq[i][:, :W] for i, u in enumerate(units)]
    q2 = [t[:, W:] for t in qq]
    m1 = [(jnp.where(same_head, mm[i][:, :W], 0.0)
           + jnp.where(diag_w, jnp.broadcast_to(lanes(g_last[u[0]], u[1]), (W, W)), 0.0)).astype(BF16)
          for i, u in enumerate(units)]
    m2 = [jnp.where(same_head, t[:, W:], 0.0) for t in mm]
    h_cur = [state[p] for p in pairs]
    h_in = []
    for i, (c, p) in enumerate(units):
        h_in.append(h_cur[p])
        h_cur[p] = jnp.dot(m1[i], h_cur[p].astype(BF16), preferred_element_type=F32) + m2[i]
    for p in pairs:
        state[p] = h_cur[p]
    y_blk = [bdot(q1[i], h_in[i]) + q2[i] for i in range(len(units))]
    y = jnp.concatenate([jnp.concatenate([y_blk[c * n_pair + p] for p in pairs], axis=1) for c in chunks], axis=0)
    mean = head_sums(y) * (1.0 / HEAD_DIM)
    yc = y - mean
    var = head_sums(yc * yc) * (1.0 / HEAD_DIM)
    yn = yc * lax.rsqrt(var + RWKV_GN_EPS) * lnw_ref[...] + lnb_ref[...]
    o_ref[0] = (yn + bonus).astype(BF16)


def _wkv_call(r, w_log, k, v, kk, a, ln_w, ln_b, r_k, tc=256, pairs_per_step=8):
    batch, seq, _ = r.shape
    width = pairs_per_step * LANES
    tok = pl.BlockSpec((1, tc, width), lambda b, h, t: (b, t, h))
    vec = pl.BlockSpec((1, width), lambda b, h, t: (0, h))
    return pl.pallas_call(
        _wkv_body,
        grid=(batch, D_MODEL // width, seq // tc),
        in_specs=[tok] * 6 + [vec] * 3,
        out_specs=tok,
        out_shape=jax.ShapeDtypeStruct(r.shape, BF16),
        scratch_shapes=[pltpu.VMEM((pairs_per_step, LANES, LANES), F32)],
        compiler_params=_params("arbitrary", "arbitrary", "arbitrary"),
        name="wkv7_chunked",
    )(r, w_log, k, v, kk, a, ln_w, ln_b, r_k)


def _rwkv_post_body(y_ref, g_ref, x_ref, mod_ref, wo_ref, o_ref):
    yg = y_ref[0] * g_ref[0]
    o_ref[0] = x_ref[0] + mod_ref[0, 2:3, :] * jnp.dot(yg, wo_ref[...], preferred_element_type=F32)


def _rwkv_post_call(y, g, x, mod, w_o, ts=512):
    batch, seq, _ = x.shape
    tok = pl.BlockSpec((1, ts, D_MODEL), lambda b, i: (b, i, 0))
    return pl.pallas_call(
        _rwkv_post_body,
        grid=(batch, seq // ts),
        in_specs=[tok, tok, tok, pl.BlockSpec((1, 3, D_MODEL), lambda b, i: (b, 0, 0)),
                  pl.BlockSpec((D_MODEL, D_MODEL), lambda b, i: (0, 0))],
        out_specs=tok,
        out_shape=jax.ShapeDtypeStruct(x.shape, F32),
        compiler_params=_params("arbitrary", "arbitrary"),
        name="rwkv_out_proj",
    )(y, g, x, mod, w_o)


MOE_TILE = 512
MOE_BLOCK = 64
MOE_SLOTS = MOE_TILE + N_EXPERT_GROUPS * MOE_BLOCK
MOE_BLOCKS_PER_TILE = MOE_SLOTS // MOE_BLOCK
MOE_STEP_BLOCKS = 8


def _route(logits_t, rb):
    n_g, epg = N_EXPERT_GROUPS, EXPERTS_PER_GROUP
    s = jax.nn.sigmoid(logits_t)
    sel = s + rb
    s_rows = [s[i:i + 1, :] for i in range(N_EXPERTS)]
    sel_rows = [sel[i:i + 1, :] for i in range(N_EXPERTS)]
    scores = []
    for g in range(n_g):
        members = sel_rows[g * epg:(g + 1) * epg]
        best = None
        for i in range(epg):
            for j in range(i + 1, epg):
                pair = members[i] + members[j]
                best = pair if best is None else jnp.maximum(best, pair)
        scores.append(best)
    g_idx = jnp.zeros_like(scores[0], dtype=jnp.int32)
    top = scores[0]
    for g in range(1, n_g):
        better = scores[g] > top
        g_idx = jnp.where(better, g, g_idx)
        top = jnp.where(better, scores[g], top)

    def pick(rows_, j):
        out = rows_[j]
        for g in range(1, n_g):
            out = jnp.where(g_idx == g, rows_[g * epg + j], out)
        return out

    in_sel = [pick(sel_rows, j) for j in range(epg)]
    in_s = [pick(s_rows, j) for j in range(epg)]
    chosen = []
    for j in range(epg):
        rank = jnp.zeros_like(g_idx)
        for i in range(epg):
            if i == j:
                continue
            ahead = (in_sel[i] >= in_sel[j]) if i < j else (in_sel[i] > in_sel[j])
            rank = rank + jnp.where(ahead, 1, 0)
        chosen.append(rank < 2)
    den = sum(jnp.where(chosen[j], in_s[j], 0.0) for j in range(epg))
    gate_rows = []
    for ex in range(N_EXPERTS):
        g, j = divmod(ex, epg)
        on = jnp.logical_and(chosen[j], g_idx == g)
        gate_rows.append(jnp.where(on, in_s[j] / den, 0.0))
    return g_idx, gate_rows


def _slot_one_hot(pos_row):
    slot = lax.broadcasted_iota(jnp.int32, (MOE_SLOTS, pos_row.shape[1]), 0)
    return jnp.where(slot == pos_row, 1.0, 0.0).astype(BF16)


def _moe_sort_body(x_ref, mod_ref, rw_ref, rb_ref, hs_ref, gs_ref, pos_ref, cnt_ref):
    tm = x_ref.shape[1]
    n_g = N_EXPERT_GROUPS
    h = _norm_mod(x_ref[0], mod_ref)
    hi = h.astype(BF16)
    lo = (h - hi.astype(F32)).astype(BF16)
    parts = jnp.dot(jnp.concatenate([hi, lo], axis=0), rw_ref[...], preferred_element_type=F32)
    logits = (parts[:tm, :LANES] + parts[:tm, LANES:]) + (parts[tm:, :LANES] + parts[tm:, LANES:])
    g_idx, gate_rows = _route(logits.T[:N_EXPERTS], rb_ref[...])
    member = [jnp.where(g_idx == g, 1.0, 0.0) for g in range(n_g)]
    member8 = jnp.concatenate(member + [jnp.zeros((8 - n_g, tm), F32)], axis=0).astype(BF16)
    earlier = jnp.where(lax.broadcasted_iota(jnp.int32, (tm, tm), 0) < lax.broadcasted_iota(jnp.int32, (tm, tm), 1),
                        1.0, 0.0).astype(BF16)
    rank = jnp.dot(member8, earlier, preferred_element_type=F32)
    pos = jnp.zeros((1, tm), F32)
    offset = jnp.zeros((1, 1), F32)
    lane = lax.broadcasted_iota(jnp.int32, (1, LANES), 1)
    counts = jnp.zeros((1, LANES), F32)
    for g in range(n_g):
        n_members = jnp.sum(member[g], axis=1, keepdims=True)
        n_blocks = jnp.floor((n_members + (MOE_BLOCK - 1)) * (1.0 / MOE_BLOCK))
        pos = pos + member[g] * (offset + rank[g:g + 1, :])
        counts = counts + jnp.where(lane == g, n_blocks, 0.0)
        offset = offset + n_blocks * MOE_BLOCK
    pos_i = pos.astype(jnp.int32)
    one_hot = _slot_one_hot(pos_i)
    hs_ref[...] = jnp.dot(one_hot, hi, preferred_element_type=F32).astype(BF16)
    pad = jnp.zeros((LANES - N_EXPERTS, tm), F32)
    gate_t = jnp.concatenate(gate_rows + [pad], axis=0).T
    g_hi = gate_t.astype(BF16)
    g_lo = (gate_t - g_hi.astype(F32)).astype(BF16)
    gs_ref[...] = jnp.dot(one_hot, jnp.concatenate([g_hi, g_lo], axis=1), preferred_element_type=F32).astype(BF16)
    pos_ref[0] = pos_i
    cnt_ref[0] = counts.astype(jnp.int32)


def _moe_sort_call(x, mod, router_w2, router_b):
    batch, seq, _ = x.shape
    per_seq = seq // MOE_TILE
    n_tiles = batch * per_seq
    return pl.pallas_call(
        _moe_sort_body,
        grid=(batch, per_seq),
        in_specs=[
            pl.BlockSpec((1, MOE_TILE, D_MODEL), lambda b, i: (b, i, 0)),
            pl.BlockSpec((1, 3, D_MODEL), lambda b, i: (b, 0, 0)),
            pl.BlockSpec((D_MODEL, 2 * LANES), lambda b, i: (0, 0)),
            pl.BlockSpec((N_EXPERTS, 1), lambda b, i: (0, 0)),
        ],
        out_specs=[
            pl.BlockSpec((MOE_SLOTS, D_MODEL), lambda b, i: (b * per_seq + i, 0)),
            pl.BlockSpec((MOE_SLOTS, 2 * LANES), lambda b, i: (b * per_seq + i, 0)),
            pl.BlockSpec((1, 1, MOE_TILE), lambda b, i: (b * per_seq + i, 0, 0)),
            pl.BlockSpec((1, 1, LANES), lambda b, i: (b * per_seq + i, 0, 0)),
        ],
        out_shape=[
            jax.ShapeDtypeStruct((n_tiles * MOE_SLOTS, D_MODEL), BF16),
            jax.ShapeDtypeStruct((n_tiles * MOE_SLOTS, 2 * LANES), BF16),
            jax.ShapeDtypeStruct((n_tiles, 1, MOE_TILE), jnp.int32),
            jax.ShapeDtypeStruct((n_tiles, 1, LANES), jnp.int32),
        ],
        compiler_params=_params("arbitrary", "arbitrary"),
        name="moe_sort",
    )(x, mod, router_w2, router_b)


def _moe_tables_body(cnt_ref, sg_ref, sv_ref, sb_ref, bp_ref):
    n_tiles = cnt_ref.shape[0]
    n_g, r, bpt = N_EXPERT_GROUPS, MOE_STEP_BLOCKS, MOE_BLOCKS_PER_TILE
    n_steps = sg_ref.shape[0]

    def fill(ref, n, value):
        def body(i, carry):
            ref[i] = value
            return carry
        lax.fori_loop(0, n, body, 0)

    fill(sb_ref, n_steps * r, 0)
    fill(bp_ref, n_tiles * bpt, 0)
    fill(sg_ref, n_steps, n_g - 1)
    fill(sv_ref, n_steps, 0)
    slot = jnp.int32(0)
    for g in range(n_g):
        def tile_body(i, k, g=g):
            first = i * bpt
            for g_before in range(g):
                first = first + cnt_ref[i, g_before]

            def block_body(j, k):
                sb_ref[k] = first + j
                bp_ref[first + j] = k
                return k + 1
            return lax.fori_loop(0, cnt_ref[i, g], block_body, k)

        slot_end = lax.fori_loop(0, n_tiles, tile_body, slot)
        step_first = slot // r
        step_stop = (slot_end + r - 1) // r

        def step_body(s, carry, g=g, slot_end=slot_end):
            sg_ref[s] = g
            sv_ref[s] = jnp.minimum(slot_end - s * r, r)
            return carry
        lax.fori_loop(step_first, step_stop, step_body, 0)
        slot = step_stop * r


def _moe_step_tables(counts):
    n_tiles = counts.shape[0]
    n_blocks = n_tiles * MOE_BLOCKS_PER_TILE
    n_steps = n_blocks // MOE_STEP_BLOCKS + N_EXPERT_GROUPS
    smem = pl.BlockSpec(memory_space=pltpu.SMEM)
    return pl.pallas_call(
        _moe_tables_body,
        in_specs=[smem],
        out_specs=[smem] * 4,
        out_shape=[jax.ShapeDtypeStruct((n_steps,), jnp.int32), jax.ShapeDtypeStruct((n_steps,), jnp.int32),
                   jax.ShapeDtypeStruct((n_steps * MOE_STEP_BLOCKS,), jnp.int32),
                   jax.ShapeDtypeStruct((n_blocks,), jnp.int32)],
        name="moe_tables",
    )(counts)


def _moe_expert_body(sg_ref, sv_ref, sb_ref, *refs):
    r = MOE_STEP_BLOCKS
    h_refs, g_refs = refs[:r], refs[r:2 * r]
    w1_ref, w3_ref, w2_ref, o_ref, w13_scr, w2_scr, act_scr = refs[2 * r:]
    step = pl.program_id(0)
    epg = EXPERTS_PER_GROUP
    rows_total = r * MOE_BLOCK
    group = sg_ref[step]
    new_group = jnp.logical_or(step == 0, group != sg_ref[jnp.maximum(step - 1, 0)])

    @pl.when(jnp.logical_and(sv_ref[step] > 0, new_group))
    def _():
        w13_scr[:, :, :D_EXPERT] = w1_ref[0].astype(BF16)
        w13_scr[:, :, D_EXPERT:] = w3_ref[0].astype(BF16)
        w2_scr[...] = w2_ref[0].reshape(epg * D_EXPERT, D_MODEL).astype(BF16)

    @pl.when(sv_ref[step] > 0)
    def _():
        h = jnp.concatenate([ref[...] for ref in h_refs], axis=0)
        gates = jnp.concatenate([ref[...] for ref in g_refs], axis=0)
        real = lax.broadcasted_iota(jnp.int32, (rows_total, LANES), 0) < sv_ref[step] * MOE_BLOCK
        expert_row = lax.broadcasted_iota(jnp.int32, (2 * LANES, LANES), 0) % LANES
        for j in range(epg):
            pick_rows = jnp.where(expert_row == group * epg + j, 1.0, 0.0).astype(BF16)
            gate_b = jnp.where(real, jnp.dot(gates, pick_rows, preferred_element_type=F32), 0.0)
            hid = jnp.dot(h, w13_scr[j], preferred_element_type=F32)
            h1 = hid[:, :D_EXPERT]
            act = h1 * jax.nn.sigmoid(h1) * hid[:, D_EXPERT:] * jnp.concatenate([gate_b, gate_b], axis=1)
            act_scr[:, j * D_EXPERT:(j + 1) * D_EXPERT] = act.astype(BF16)
        o_ref[...] = jnp.dot(act_scr[...], w2_scr[...], preferred_element_type=F32).astype(BF16)


def _moe_expert_call(hs, gs, w1, w3, w2, step_group, step_valid, step_blocks, layer):
    r = MOE_STEP_BLOCKS
    n_steps = step_group.shape[0]
    epg = EXPERTS_PER_GROUP
    rows_total = r * MOE_BLOCK

    def blk(width, j):
        return pl.BlockSpec((MOE_BLOCK, width), lambda s, sg, sv, sb, j=j: (sb[s * r + j], 0))

    def group_weights(shape):
        return pl.BlockSpec((1, epg) + shape, lambda s, sg, sv, sb: (layer, sg[s], 0, 0))

    grid_spec = pltpu.PrefetchScalarGridSpec(
        num_scalar_prefetch=3,
        grid=(n_steps,),
        in_specs=[blk(D_MODEL, j) for j in range(r)] + [blk(2 * LANES, j) for j in range(r)] + [
            group_weights((D_MODEL, D_EXPERT)), group_weights((D_MODEL, D_EXPERT)),
            group_weights((D_EXPERT, D_MODEL))],
        out_specs=pl.BlockSpec((rows_total, D_MODEL), lambda s, sg, sv, sb: (s, 0)),
        scratch_shapes=[
            pltpu.VMEM((epg, D_MODEL, 2 * D_EXPERT), BF16),
            pltpu.VMEM((epg * D_EXPERT, D_MODEL), BF16),
            pltpu.VMEM((rows_total, epg * D_EXPERT), BF16),
        ],
    )
    return pl.pallas_call(
        _moe_expert_body,
        grid_spec=grid_spec,
        out_shape=jax.ShapeDtypeStruct((n_steps * rows_total, D_MODEL), BF16),
        compiler_params=_params("arbitrary"),
        name="moe_experts",
    )(step_group, step_valid, step_blocks, *([hs] * r), *([gs] * r), w1, w3, w2)


def _moe_unsort_body(is_last, bp_ref, *refs):
    bpt = MOE_BLOCKS_PER_TILE
    y_refs = refs[:bpt]
    pos_ref, x_ref, mod_ref, fg_ref, o_ref = refs[bpt:]
    one_hot = _slot_one_hot(pos_ref[0])
    used = jnp.sum(one_hot.astype(F32), axis=1, keepdims=True) > 0.0
    ys = jnp.concatenate([ref[...] for ref in y_refs], axis=0)
    ys = jnp.where(used, ys, jnp.zeros_like(ys))
    y = lax.dot_general(one_hot, ys, TN_DIMS, preferred_element_type=F32)
    out = x_ref[0] + mod_ref[0, 2:3, :] * y
    if is_last:
        out = out * lax.rsqrt(jnp.mean(out * out, axis=-1, keepdims=True) + RMS_EPS) * fg_ref[...]
    o_ref[0] = out


def _moe_unsort_call(ys, pos, x, mod, final_g, block_pos, is_last):
    batch, seq, _ = x.shape
    per_seq = seq // MOE_TILE
    bpt = MOE_BLOCKS_PER_TILE

    def blk(j):
        return pl.BlockSpec((MOE_BLOCK, D_MODEL), lambda b, i, bp, j=j: (bp[(b * per_seq + i) * bpt + j], 0))

    grid_spec = pltpu.PrefetchScalarGridSpec(
        num_scalar_prefetch=1,
        grid=(batch, per_seq),
        in_specs=[blk(j) for j in range(bpt)] + [
            pl.BlockSpec((1, 1, MOE_TILE), lambda b, i, bp: (b * per_seq + i, 0, 0)),
            pl.BlockSpec((1, MOE_TILE, D_MODEL), lambda b, i, bp: (b, i, 0)),
            pl.BlockSpec((1, 3, D_MODEL), lambda b, i, bp: (b, 0, 0)),
            pl.BlockSpec((1, D_MODEL), lambda b, i, bp: (0, 0)),
        ],
        out_specs=pl.BlockSpec((1, MOE_TILE, D_MODEL), lambda b, i, bp: (b, i, 0)),
    )
    return pl.pallas_call(
        functools.partial(_moe_unsort_body, is_last),
        grid_spec=grid_spec,
        out_shape=jax.ShapeDtypeStruct(x.shape, F32),
        compiler_params=_params("arbitrary", "arbitrary"),
        name="moe_unsort",
    )(block_pos, *([ys] * bpt), pos, x, mod, final_g)


def _moe_layer(x, mod, router_w2, router_b, w1, w3, w2, final_g, layer, is_last):
    hs, gs, pos, cnt = _moe_sort_call(x, mod, router_w2, router_b)
    step_group, step_valid, step_blocks, block_pos = _moe_step_tables(cnt[:, 0, :])
    ys = _moe_expert_call(hs, gs, w1, w3, w2, step_group, step_valid, step_blocks, layer)
    return _moe_unsort_call(ys, pos, x, mod, final_g, block_pos, is_last)


def _pad_cols(w):
    n = w.shape[-1]
    return jnp.pad(w, ((0, 0), (0, -n % LANES)))


def _pad_rows(w):
    n = w.shape[0]
    return jnp.pad(w, ((0, -n % LANES), (0, 0)))


def kernel(x, c, ada_w, ada_b, mix_w_in, mix_w_out, sgu_norm_g, sgu_w, sgu_b, rwkv_mu, rwkv_w_rkv, rwkv_w_o, rwkv_w0, rwkv_w1, rwkv_w2, rwkv_a0, rwkv_a1, rwkv_a2, rwkv_v0, rwkv_v1, rwkv_v2, rwkv_g1, rwkv_g2, rwkv_k_k, rwkv_k_a, rwkv_r_k, rwkv_ln_w, rwkv_ln_b, router_w, router_b, moe_w1, moe_w3, moe_w2, final_norm_g):
    depth = ada_w.shape[0]
    row = lambda t: t.reshape(1, -1)
    mods = _ada_call(c, ada_w, ada_b)
    rw_hi = router_w.astype(BF16)
    rw_lo = (router_w - rw_hi.astype(F32)).astype(BF16)
    router_w2 = jnp.concatenate([_pad_cols(rw_hi), _pad_cols(rw_lo)], axis=1)
    router_bc = router_b.reshape(N_EXPERTS, 1)
    final_g = row(final_norm_g)
    v_first = None
    for layer in range(depth):
        i = layer // 2
        mod = mods[2 * layer]
        if layer % 2 == 0:
            proj = _proj_call(x, mod, mix_w_in[i].astype(BF16))
            o_a = _attn_call(proj)
            bias_tile = jnp.repeat(sgu_b[i].T, HEAD_DIM, axis=1)
            x = _sgu_out_call(proj, o_a, x, mod, row(sgu_norm_g[i]), sgu_w[i], bias_tile,
                              mix_w_out[i].astype(BF16))
        else:
            weights = [
                rwkv_mu[i],
                rwkv_w_rkv[i, 0].astype(BF16), rwkv_w_rkv[i, 1].astype(BF16), rwkv_w_rkv[i, 2].astype(BF16),
                row(rwkv_w0[i]), _pad_cols(rwkv_w1[i]).astype(BF16), _pad_rows(rwkv_w2[i]).astype(BF16),
                row(rwkv_a0[i]), _pad_cols(rwkv_a1[i]).astype(BF16), _pad_rows(rwkv_a2[i]).astype(BF16),
                _pad_cols(rwkv_g1[i]).astype(BF16), _pad_rows(rwkv_g2[i]).astype(BF16),
                row(rwkv_k_k[i]), row(rwkv_k_a[i]),
            ]
            vres = None
            if i > 0:
                vres = [row(rwkv_v0[i - 1]), _pad_cols(rwkv_v1[i - 1]).astype(BF16),
                        _pad_rows(rwkv_v2[i - 1]).astype(BF16)]
            r, w_log, k, v, kk, a, g = _rwkv_pre_call(x, mod, weights, v_first, vres)
            if i == 0:
                v_first = v
            y = _wkv_call(r, w_log, k, v, kk, a, row(rwkv_ln_w[i]), row(rwkv_ln_b[i]), row(rwkv_r_k[i]))
            x = _rwkv_post_call(y, g, x, mod, rwkv_w_o[i].astype(BF16))
        x = _moe_layer(x, mods[2 * layer + 1], router_w2, router_bc, moe_w1, moe_w3, moe_w2, final_g, layer,
                       layer == depth - 1)
    return x
```

```python
import functools

import jax
import jax.numpy as jnp
from jax import lax
from jax.experimental import pallas as pl
from jax.experimental.pallas import tpu as pltpu

F32 = jnp.float32
BF16 = jnp.bfloat16

D_MODEL = 1024
HEAD_DIM = 64
WIDTH_A = 512
WIDTH_B = 512
MIX_IN_WIDTH = 2560
ATTN_SPAN = 128
ATTN_DILATIONS = (1, 4, 16)
SGU_CHUNK = 128
N_GROUPS_B = 8
RWKV_HEADS = 16
RWKV_GN_EPS = 64e-5
WKV_CHUNK = 64
N_EXPERTS = 16
N_EXPERT_GROUPS = 4
EXPERTS_PER_GROUP = 4
D_EXPERT = 256
RMS_EPS = 1e-6
LANES = 128
VMEM_LIMIT = 56 * 1024 * 1024

NT_DIMS = (((1,), (1,)), ((), ()))
TN_DIMS = (((0,), (0,)), ((), ()))


def _params(*sem):
    return pltpu.CompilerParams(dimension_semantics=sem, vmem_limit_bytes=VMEM_LIMIT)


def _norm_mod(x, mod_ref):
    ms = jnp.mean(x * x, axis=-1, keepdims=True)
    return (x * lax.rsqrt(ms + RMS_EPS)) * (1.0 + mod_ref[0, 1:2, :]) + mod_ref[0, 0:1, :]


def _gelu(x):
    return 0.5 * x * (1.0 + lax.erf(x * 0.7071067811865476))


def _ada_body(c_ref, w_ref, b_ref, o_ref):
    c = c_ref[...]
    c_act = c * jax.nn.sigmoid(c)
    n = c.shape[0]
    c_hi = c_act.astype(BF16)
    c_lo = (c_act - c_hi.astype(F32)).astype(BF16)
    c2 = jnp.concatenate([c_hi, c_lo], axis=0)
    w = w_ref[0]
    w_hi = w.astype(BF16)
    w_lo = (w - w_hi.astype(F32)).astype(BF16)
    by_hi = jnp.dot(c2, w_hi, preferred_element_type=F32)
    by_lo = jnp.dot(c2, w_lo, preferred_element_type=F32)
    o_ref[0] = (by_hi[:n] + by_hi[n:]) + (by_lo[:n] + by_lo[n:]) + b_ref[0]


def _ada_call(c, ada_w, ada_b):
    n_pair = ada_w.shape[0] * ada_w.shape[1]
    batch = c.shape[0]
    w = ada_w.reshape(n_pair, D_MODEL, 3 * D_MODEL)
    b = ada_b.reshape(n_pair, 1, 3 * D_MODEL)
    out = pl.pallas_call(
        _ada_body,
        grid=(n_pair, 3),
        in_specs=[
            pl.BlockSpec((batch, D_MODEL), lambda p, j: (0, 0)),
            pl.BlockSpec((1, D_MODEL, D_MODEL), lambda p, j: (p, 0, j)),
            pl.BlockSpec((1, 1, D_MODEL), lambda p, j: (p, 0, j)),
        ],
        out_specs=pl.BlockSpec((1, batch, D_MODEL), lambda p, j: (p, 0, j)),
        out_shape=jax.ShapeDtypeStruct((n_pair, batch, 3 * D_MODEL), F32),
        compiler_params=_params("arbitrary", "arbitrary"),
        name="ada_mod",
    )(c, w, b)
    return out.reshape(n_pair, batch, 3, D_MODEL)


def _proj_body(x_ref, mod_ref, w_ref, o_ref):
    h = _norm_mod(x_ref[0], mod_ref)
    o_ref[0] = jnp.dot(h.astype(BF16), w_ref[...], preferred_element_type=F32)


def _proj_call(x, mod, w_in, ts=512):
    batch, seq, _ = x.shape
    return pl.pallas_call(
        _proj_body,
        grid=(batch, seq // ts),
        in_specs=[
            pl.BlockSpec((1, ts, D_MODEL), lambda b, i: (b, i, 0)),
            pl.BlockSpec((1, 3, D_MODEL), lambda b, i: (b, 0, 0)),
            pl.BlockSpec((D_MODEL, MIX_IN_WIDTH), lambda b, i: (0, 0)),
        ],
        out_specs=pl.BlockSpec((1, ts, MIX_IN_WIDTH), lambda b, i: (b, i, 0)),
        out_shape=jax.ShapeDtypeStruct((batch, seq, MIX_IN_WIDTH), F32),
        compiler_params=_params("arbitrary", "arbitrary"),
        name="mix_in_proj",
    )(x, mod, w_in)


def _attn_body(q_ref, k_ref, v_ref, o_ref, ob, lb):
    span = ATTN_SPAN
    W = LANES
    block_group = 8

    def iota(shape, dim):
        return lax.broadcasted_iota(jnp.int32, shape, dim)

    head0 = iota((span, W), 1) < HEAD_DIM
    mine = (head0, jnp.logical_not(head0))
    row2, col2 = iota((span, 2 * span), 0), iota((span, 2 * span), 1)
    mask_prev_cur = jnp.where(col2 < span, col2 - row2, row2 - (col2 - span)) >= 0
    mask_cur = iota((span, span), 1) <= iota((span, span), 0)

    def rows(start, n_rows, dil):
        return pl.ds(start, n_rows) if dil == 1 else pl.ds(start, n_rows, stride=dil)

    def group(p, dil, q_starts, with_prev):
        blocks = range(len(q_starts))
        heads = range(2)
        qb, kcat, vaug, masks = [], [], [], []
        for j in blocks:
            n_keys = 2 * span if with_prev[j] else span
            k_start = q_starts[j] - span * dil if with_prev[j] else q_starts[j]
            qb.append(q_ref[0, rows(q_starts[j], span, dil), :] * (HEAD_DIM ** -0.5))
            kcat.append(k_ref[0, rows(k_start, n_keys, dil), :].astype(BF16))
            vb = v_ref[0, rows(k_start, n_keys, dil), :].astype(BF16)
            vaug.append(jnp.concatenate([vb, jnp.ones((n_keys, W), BF16)], axis=1))
            masks.append(mask_prev_cur if with_prev[j] else mask_cur)
        qh = [[jnp.where(mine[h], qb[j], 0.0).astype(BF16) for h in heads] for j in blocks]
        s = [[lax.dot_general(qh[j][h], kcat[j], NT_DIMS, preferred_element_type=F32) for h in heads] for j in blocks]
        s = [[jnp.where(masks[j], s[j][h], -jnp.inf) for h in heads] for j in blocks]
        m = [[jnp.max(s[j][h], axis=-1, keepdims=True) for h in heads] for j in blocks]
        pr = [[jnp.exp(s[j][h] - m[j][h]).astype(BF16) for h in heads] for j in blocks]
        ad = [[jnp.dot(pr[j][h], vaug[j], preferred_element_type=F32) for h in heads] for j in blocks]
        for j in blocks:
            den = [ad[j][h][:, W:] for h in heads]
            o = jnp.where(head0, ad[j][0][:, :W] / den[0], ad[j][1][:, :W] / den[1])
            lse = jnp.where(head0, m[j][0] + jnp.log(den[0]), m[j][1] + jnp.log(den[1]))
            qi = rows(q_starts[j], span, dil)
            ob[p, qi, :] = o
            lb[p, qi, :] = lse

    seq = q_ref.shape[1]
    for p, dil in enumerate(ATTN_DILATIONS):
        n_blk = seq // dil // span
        blocks = [(r + n * span * dil, n > 0) for r in range(dil) for n in range(n_blk)]
        for g0 in range(0, len(blocks), block_group):
            members = blocks[g0:g0 + block_group]
            group(p, dil, [b[0] for b in members], [b[1] for b in members])

    l0, l1, l2 = lb[0], lb[1], lb[2]
    m = jnp.maximum(jnp.maximum(l0, l1), l2)
    w0, w1, w2 = jnp.exp(l0 - m), jnp.exp(l1 - m), jnp.exp(l2 - m)
    o_ref[0] = (w0 * ob[0] + w1 * ob[1] + w2 * ob[2]) / (w0 + w1 + w2)


def _attn_call(proj):
    batch, seq, _ = proj.shape
    n_pair = WIDTH_A // LANES
    blk = lambda off: pl.BlockSpec((1, seq, LANES), lambda b, h: (b, 0, off + h))
    return pl.pallas_call(
        _attn_body,
        grid=(batch, n_pair),
        in_specs=[blk(0), blk(n_pair), blk(2 * n_pair)],
        out_specs=pl.BlockSpec((1, seq, LANES), lambda b, h: (b, 0, h)),
        out_shape=jax.ShapeDtypeStruct((batch, seq, WIDTH_A), F32),
        scratch_shapes=[
            pltpu.VMEM((3, seq, LANES), F32),
            pltpu.VMEM((3, seq, LANES), F32),
        ],
        compiler_params=_params("arbitrary", "arbitrary"),
        name="dilated_attn",
    )(proj, proj, proj)


def _sgu_out_body(u_ref, z_ref, oa_ref, x_ref, mod_ref, ng_ref, sw_ref, sb_ref, wo_ref, o_ref, mix):
    ts = u_ref.shape[1]
    u = _gelu(u_ref[0])
    z = _gelu(z_ref[0])
    z = z * lax.rsqrt(jnp.mean(z * z, axis=-1, keepdims=True) + RMS_EPS) * ng_ref[...]
    zb = z.astype(BF16)
    row = lax.broadcasted_iota(jnp.int32, (SGU_CHUNK, SGU_CHUNK), 0)
    col = lax.broadcasted_iota(jnp.int32, (SGU_CHUNK, SGU_CHUNK), 1)
    causal = col <= row
    for g in range(N_GROUPS_B):
        w_g = jnp.where(causal, sw_ref[g], 0.0).astype(BF16)
        for cc in range(ts // SGU_CHUNK):
            rs = slice(cc * SGU_CHUNK, (cc + 1) * SGU_CHUNK)
            ls = slice(g * HEAD_DIM, (g + 1) * HEAD_DIM)
            mix[rs, ls] = jnp.dot(w_g, zb[rs, ls], preferred_element_type=F32)
    bias = jnp.concatenate([sb_ref[...]] * (ts // SGU_CHUNK), axis=0)
    o_b = u * (mix[...] + bias)
    y = jnp.dot(oa_ref[0].astype(BF16), wo_ref[0:WIDTH_A, :], preferred_element_type=F32)
    y = y + jnp.dot(o_b.astype(BF16), wo_ref[WIDTH_A:, :], preferred_element_type=F32)
    o_ref[0] = x_ref[0] + mod_ref[0, 2:3, :] * y


def _sgu_out_call(proj, o_a, x, mod, norm_g, sgu_w, sgu_bias_tile, w_out, ts=1024):
    batch, seq, _ = x.shape
    u_blk = 3 * WIDTH_A // WIDTH_B
    full = lambda shape: pl.BlockSpec(shape, lambda b, i: (0,) * len(shape))
    return pl.pallas_call(
        _sgu_out_body,
        grid=(batch, seq // ts),
        in_specs=[
            pl.BlockSpec((1, ts, WIDTH_B), lambda b, i: (b, i, u_blk)),
            pl.BlockSpec((1, ts, WIDTH_B), lambda b, i: (b, i, u_blk + 1)),
            pl.BlockSpec((1, ts, WIDTH_A), lambda b, i: (b, i, 0)),
            pl.BlockSpec((1, ts, D_MODEL), lambda b, i: (b, i, 0)),
            pl.BlockSpec((1, 3, D_MODEL), lambda b, i: (b, 0, 0)),
            full((1, WIDTH_B)),
            full((N_GROUPS_B, SGU_CHUNK, SGU_CHUNK)),
            full((SGU_CHUNK, WIDTH_B)),
            full((WIDTH_A + WIDTH_B, D_MODEL)),
        ],
        out_specs=pl.BlockSpec((1, ts, D_MODEL), lambda b, i: (b, i, 0)),
        out_shape=jax.ShapeDtypeStruct(x.shape, F32),
        scratch_shapes=[pltpu.VMEM((ts, WIDTH_B), F32)],
        compiler_params=_params("arbitrary", "arbitrary"),
        name="sgu_out_proj",
    )(proj, proj, o_a, x, mod, norm_g, sgu_w, sgu_bias_tile, w_out)


def _rwkv_pre_body(has_vres, *refs):
    if has_vres:
        (x_ref, mod_ref, mu_ref, wr_ref, wk_ref, wv_ref, w0_ref, w1_ref, w2_ref, a0_ref, a1_ref, a2_ref,
         g1_ref, g2_ref, kk_ref, ka_ref, vf_ref, v0_ref, v1_ref, v2_ref,
         r_out, w_out, k_out, v_out, kk_out, a_out, g_out, carry) = refs
    else:
        (x_ref, mod_ref, mu_ref, wr_ref, wk_ref, wv_ref, w0_ref, w1_ref, w2_ref, a0_ref, a1_ref, a2_ref,
         g1_ref, g2_ref, kk_ref, ka_ref,
         r_out, w_out, k_out, v_out, kk_out, a_out, g_out, carry) = refs

    @pl.when(pl.program_id(1) == 0)
    def _():
        carry[...] = jnp.zeros_like(carry)

    h = _norm_mod(x_ref[0], mod_ref)
    ts = h.shape[0]
    first = lax.broadcasted_iota(jnp.int32, h.shape, 0) == 0
    h_prev = jnp.where(first, carry[0:1, :], pltpu.roll(h, 1, 0))
    carry[0:1, :] = h[ts - 1:ts, :]
    xx = h_prev - h

    def mixed(i):
        return (h + xx * mu_ref[i:i + 1, :]).astype(BF16)

    def mm(a, w_ref):
        return jnp.dot(a, w_ref[...], preferred_element_type=F32)

    xr, xw, xk, xv, xa, xg = [mixed(i) for i in range(6)]
    r = mm(xr, wr_ref)
    k = mm(xk, wk_ref)
    v = mm(xv, wv_ref)
    z = w0_ref[...] + mm(jnp.tanh(mm(xw, w1_ref)).astype(BF16), w2_ref)
    w_log = -(jnp.maximum(-z, 0.0) + jnp.log(1.0 + jnp.exp(-jnp.abs(z)))) - 0.5
    if has_vres:
        mix_v = jax.nn.sigmoid(v0_ref[...] + mm(mm(xv, v1_ref).astype(BF16), v2_ref))
        v = v + (vf_ref[0].astype(F32) - v) * mix_v
    a = jax.nn.sigmoid(a0_ref[...] + mm(mm(xa, a1_ref).astype(BF16), a2_ref))
    g = mm(jax.nn.sigmoid(mm(xg, g1_ref)).astype(BF16), g2_ref)
    r_out[0] = r.astype(BF16)
    w_out[0] = w_log
    kk_out[0] = (k * kk_ref[...]).astype(BF16)
    k_out[0] = (k * (1.0 + (a - 1.0) * ka_ref[...])).astype(BF16)
    v_out[0] = v.astype(BF16)
    a_out[0] = a.astype(BF16)
    g_out[0] = g.astype(BF16)


def _rwkv_pre_call(x, mod, weights, v_first, vres, ts=512):
    batch, seq, _ = x.shape
    tok = pl.BlockSpec((1, ts, D_MODEL), lambda b, i: (b, i, 0))
    full = lambda a: pl.BlockSpec(a.shape, lambda b, i: (0,) * a.ndim, pipeline_mode=pl.Buffered(1))
    ins = [x, mod] + list(weights)
    specs = [tok, pl.BlockSpec((1, 3, D_MODEL), lambda b, i: (b, 0, 0))] + [full(a) for a in weights]
    if vres is not None:
        ins += [v_first] + list(vres)
        specs += [tok] + [full(a) for a in vres]
    out_dtypes = [BF16, F32, BF16, BF16, BF16, BF16, BF16]
    return pl.pallas_call(
        functools.partial(_rwkv_pre_body, vres is not None),
        grid=(batch, seq // ts),
        in_specs=specs,
        out_specs=[tok] * 7,
        out_shape=[jax.ShapeDtypeStruct(x.shape, dt) for dt in out_dtypes],
        scratch_shapes=[pltpu.VMEM((8, D_MODEL), F32)],
        compiler_params=_params("arbitrary", "arbitrary"),
        name="rwkv_pre",
    )(*ins)


def _wkv_body(r_ref, w_ref, k_ref, v_ref, kk_ref, a_ref, lnw_ref, lnb_ref, rk_ref, o_ref, state):
    C = WKV_CHUNK
    W = LANES
    n_chunk = r_ref.shape[1] // C
    n_pair = r_ref.shape[2] // W

    @pl.when(pl.program_id(2) == 0)
    def _():
        state[...] = jnp.zeros_like(state)

    def iota(shape, dim):
        return lax.broadcasted_iota(jnp.int32, shape, dim)

    def stack_heads(x):
        own = iota(x.shape, 1) % W < HEAD_DIM
        return jnp.concatenate([jnp.where(own, x, 0.0), jnp.where(own, 0.0, x)], axis=0).astype(BF16)

    def block_diag(x):
        left = iota(x.shape, 1) < C
        return jnp.concatenate([jnp.where(left, x, 0.0), jnp.where(left, 0.0, x)], axis=0).astype(BF16)

    row4, col4 = iota((C, 4 * C), 0), iota((C, 4 * C), 1) % C
    strict4 = col4 < row4
    lower4 = col4 <= row4
    tri = jnp.where(iota((C, C), 1) <= iota((C, C), 0), 1.0, 0.0).astype(BF16)
    eye2 = jnp.where(iota((C, 2 * C), 1) % C == iota((C, 2 * C), 0), 1.0, 0.0).astype(F32)
    rw, cw = iota((W, W), 0), iota((W, W), 1)
    same_head = (rw < HEAD_DIM) == (cw < HEAD_DIM)
    ones_bd = jnp.where(same_head, 1.0, 0.0).astype(BF16)
    diag_w = cw == rw
    zeros_cw = jnp.zeros((C, W), F32)

    def bdot(a, b):
        return jnp.dot(a.astype(BF16), b.astype(BF16), preferred_element_type=F32)

    chunks = range(n_chunk)
    pairs = range(n_pair)
    units = [(c, p) for c in chunks for p in pairs]
    rows = lambda t, c: t[c * C:(c + 1) * C]
    blk = lambda t, u: t[u[0] * C:(u[0] + 1) * C, u[1] * W:(u[1] + 1) * W]
    lanes = lambda t, p: t[:, p * W:(p + 1) * W]
    per_chunk = lambda vals: jnp.concatenate([jnp.broadcast_to(t, (C, n_pair * W)) for t in vals], axis=0)
    head_sums = lambda t: jnp.concatenate([bdot(lanes(t, p), ones_bd) for p in pairs], axis=1)

    r = r_ref[0].astype(F32)
    k = k_ref[0].astype(F32)
    v = v_ref[0].astype(F32)
    kk = kk_ref[0].astype(F32)
    kk = kk / jnp.maximum(jnp.sqrt(head_sums(kk * kk)), 1e-12)
    b = kk * a_ref[0].astype(F32)
    bonus = head_sums(r * k * rk_ref[...]) * v
    log_w = -jnp.exp(w_ref[0])
    hi = log_w.astype(BF16)
    lo = (log_w - hi.astype(F32)).astype(BF16)
    hilo = jnp.concatenate([hi, lo], axis=1)
    cum2 = [jnp.dot(tri, rows(hilo, c), preferred_element_type=F32) for c in chunks]
    cum = jnp.concatenate([c2[:, :n_pair * W] + c2[:, n_pair * W:] for c2 in cum2], axis=0)
    c_mid = [cum[c * C + C // 2 - 1:c * C + C // 2, :] for c in chunks]
    c_end = [cum[c * C + C - 1:(c + 1) * C, :] for c in chunks]
    g_last = [jnp.exp(t) for t in c_end]
    from_mid = jnp.exp(per_chunk(c_mid) - cum)
    to_mid = per_chunk([jnp.exp(-t) for t in c_mid])
    a_true = -kk * jnp.exp(cum - log_w)
    r_true = r * jnp.exp(cum)
    a_mid = a_true * to_mid
    r_mid = r_true * to_mid
    b_mid = b * from_mid
    k_mid = k * from_mid
    ar_mid = [jnp.concatenate([blk(a_mid, u), blk(r_mid, u)], axis=0).astype(BF16) for u in units]
    gram = [lax.dot_general(ar_mid[i], jnp.concatenate([stack_heads(blk(b_mid, u)), stack_heads(blk(k_mid, u))],
                                                        axis=0), NT_DIMS, preferred_element_type=F32)
            for i, u in enumerate(units)]
    top = [jnp.where(strict4, g[:C], 0.0) for g in gram]
    bot = [jnp.where(lower4, g[C:], 0.0).astype(BF16) for g in gram]
    n1 = [t[:, :2 * C] for t in top]
    inv = [eye2 + t for t in n1]
    pw = [bdot(t, block_diag(t)) for t in n1]
    levels = C.bit_length() - 2
    for lvl in range(levels):
        if lvl < levels - 1:
            st = [bdot(jnp.concatenate([pw[i], inv[i]], axis=0), block_diag(pw[i])) for i in range(len(units))]
            pw = [t[:C] for t in st]
            inv = [inv[i] + st[i][C:] for i in range(len(units))]
        else:
            inv = [inv[i] + bdot(inv[i], block_diag(pw[i])) for i in range(len(units))]
    x = [jnp.dot(top[i][:, 2 * C:].astype(BF16), stack_heads(blk(v, u)), preferred_element_type=F32)
         for i, u in enumerate(units)]
    pp = [jnp.dot(inv[i].astype(BF16), stack_heads(jnp.concatenate([blk(a_true, u), x[i]], axis=1)),
                  preferred_element_type=F32) for i, u in enumerate(units)]
    zv = [jnp.concatenate([zeros_cw, blk(v, u)], axis=1) for u in units]
    qq = [jnp.dot(bot[i], jnp.concatenate([stack_heads(pp[i]), stack_heads(zv[i])], axis=0),
                  preferred_element_type=F32) for i in range(len(units))]
    end_from_mid = [jnp.exp(c_end[c] - c_mid[c]) for c in chunks]
    bk_hat = [jnp.concatenate([blk(b_mid, u), blk(k_mid, u)], axis=0) * lanes(end_from_mid[u[0]], u[1])
              for u in units]
    mm = [lax.dot_general(bk_hat[i].astype(BF16), jnp.concatenate([pp[i], zv[i]], axis=0).astype(BF16), TN_DIMS,
                          preferred_element_type=F32) for i in range(len(units))]
    q1 = [blk(r_true, u) + qq[i][:, :W] for i, u in enumerate(units)]
    q2 = [t[:, W:] for t in qq]
    m1 = [(jnp.where(same_head, mm[i][:, :W], 0.0)
           + jnp.where(diag_w, jnp.broadcast_to(lanes(g_last[u[0]], u[1]), (W, W)), 0.0)).astype(BF16)
          for i, u in enumerate(units)]
    m2 = [jnp.where(same_head, t[:, W:], 0.0) for t in mm]
    h_cur = [state[p] for p in pairs]
    h_in = []
    for i, (c, p) in enumerate(units):
        h_in.append(h_cur[p])
        h_cur[p] = jnp.dot(m1[i], h_cur[p].astype(BF16), preferred_element_type=F32) + m2[i]
    for p in pairs:
        state[p] = h_cur[p]
    y_blk = [bdot(q1[i], h_in[i]) + q2[i] for i in range(len(units))]
    y = jnp.concatenate([jnp.concatenate([y_blk[c * n_pair + p] for p in pairs], axis=1) for c in chunks], axis=0)
    mean = head_sums(y) * (1.0 / HEAD_DIM)
    yc = y - mean
    var = head_sums(yc * yc) * (1.0 / HEAD_DIM)
    yn = yc * lax.rsqrt(var + RWKV_GN_EPS) * lnw_ref[...] + lnb_ref[...]
    o_ref[0] = (yn + bonus).astype(BF16)


def _wkv_call(r, w_log, k, v, kk, a, ln_w, ln_b, r_k, tc=256, pairs_per_step=8):
    batch, seq, _ = r.shape
    width = pairs_per_step * LANES
    tok = pl.BlockSpec((1, tc, width), lambda b, h, t: (b, t, h))
    vec = pl.BlockSpec((1, width), lambda b, h, t: (0, h))
    return pl.pallas_call(
        _wkv_body,
        grid=(batch, D_MODEL // width, seq // tc),
        in_specs=[tok] * 6 + [vec] * 3,
        out_specs=tok,
        out_shape=jax.ShapeDtypeStruct(r.shape, BF16),
        scratch_shapes=[pltpu.VMEM((pairs_per_step, LANES, LANES), F32)],
        compiler_params=_params("arbitrary", "arbitrary", "arbitrary"),
        name="wkv7_chunked",
    )(r, w_log, k, v, kk, a, ln_w, ln_b, r_k)


def _rwkv_post_body(y_ref, g_ref, x_ref, mod_ref, wo_ref, o_ref):
    yg = y_ref[0] * g_ref[0]
    o_ref[0] = x_ref[0] + mod_ref[0, 2:3, :] * jnp.dot(yg, wo_ref[...], preferred_element_type=F32)


def _rwkv_post_call(y, g, x, mod, w_o, ts=1024):
    batch, seq, _ = x.shape
    tok = pl.BlockSpec((1, ts, D_MODEL), lambda b, i: (b, i, 0))
    return pl.pallas_call(
        _rwkv_post_body,
        grid=(batch, seq // ts),
        in_specs=[tok, tok, tok, pl.BlockSpec((1, 3, D_MODEL), lambda b, i: (b, 0, 0)),
                  pl.BlockSpec((D_MODEL, D_MODEL), lambda b, i: (0, 0))],
        out_specs=tok,
        out_shape=jax.ShapeDtypeStruct(x.shape, F32),
        compiler_params=_params("arbitrary", "arbitrary"),
        name="rwkv_out_proj",
    )(y, g, x, mod, w_o)


MOE_TILE = 512
MOE_BLOCK = 64
MOE_SLOTS = MOE_TILE + N_EXPERT_GROUPS * MOE_BLOCK
MOE_BLOCKS_PER_TILE = MOE_SLOTS // MOE_BLOCK
MOE_STEP_BLOCKS = 8


def _route(logits_t, rb):
    n_g, epg = N_EXPERT_GROUPS, EXPERTS_PER_GROUP
    s = jax.nn.sigmoid(logits_t)
    sel = s + rb
    s_rows = [s[i:i + 1, :] for i in range(N_EXPERTS)]
    sel_rows = [sel[i:i + 1, :] for i in range(N_EXPERTS)]
    scores = []
    for g in range(n_g):
        members = sel_rows[g * epg:(g + 1) * epg]
        best = None
        for i in range(epg):
            for j in range(i + 1, epg):
                pair = members[i] + members[j]
                best = pair if best is None else jnp.maximum(best, pair)
        scores.append(best)
    g_idx = jnp.zeros_like(scores[0], dtype=jnp.int32)
    top = scores[0]
    for g in range(1, n_g):
        better = scores[g] > top
        g_idx = jnp.where(better, g, g_idx)
        top = jnp.where(better, scores[g], top)

    def pick(rows_, j):
        out = rows_[j]
        for g in range(1, n_g):
            out = jnp.where(g_idx == g, rows_[g * epg + j], out)
        return out

    in_sel = [pick(sel_rows, j) for j in range(epg)]
    in_s = [pick(s_rows, j) for j in range(epg)]
    chosen = []
    for j in range(epg):
        rank = jnp.zeros_like(g_idx)
        for i in range(epg):
            if i == j:
                continue
            ahead = (in_sel[i] >= in_sel[j]) if i < j else (in_sel[i] > in_sel[j])
            rank = rank + jnp.where(ahead, 1, 0)
        chosen.append(rank < 2)
    den = sum(jnp.where(chosen[j], in_s[j], 0.0) for j in range(epg))
    gate_rows = []
    for ex in range(N_EXPERTS):
        g, j = divmod(ex, epg)
        on = jnp.logical_and(chosen[j], g_idx == g)
        gate_rows.append(jnp.where(on, in_s[j] / den, 0.0))
    return g_idx, gate_rows


def _slot_one_hot(pos_row):
    slot = lax.broadcasted_iota(jnp.int32, (MOE_SLOTS, pos_row.shape[1]), 0)
    return jnp.where(slot == pos_row, 1.0, 0.0).astype(BF16)


def _moe_sort_body(x_ref, mod_ref, rw_ref, rb_ref, hs_ref, gs_ref, pos_ref, cnt_ref):
    tm = x_ref.shape[1]
    n_g = N_EXPERT_GROUPS
    h = _norm_mod(x_ref[0], mod_ref)
    hi = h.astype(BF16)
    lo = (h - hi.astype(F32)).astype(BF16)
    parts = jnp.dot(jnp.concatenate([hi, lo], axis=0), rw_ref[...], preferred_element_type=F32)
    logits = (parts[:tm, :LANES] + parts[:tm, LANES:]) + (parts[tm:, :LANES] + parts[tm:, LANES:])
    g_idx, gate_rows = _route(logits.T[:N_EXPERTS], rb_ref[...])
    member = [jnp.where(g_idx == g, 1.0, 0.0) for g in range(n_g)]
    member8 = jnp.concatenate(member + [jnp.zeros((8 - n_g, tm), F32)], axis=0).astype(BF16)
    earlier = jnp.where(lax.broadcasted_iota(jnp.int32, (tm, tm), 0) < lax.broadcasted_iota(jnp.int32, (tm, tm), 1),
                        1.0, 0.0).astype(BF16)
    rank = jnp.dot(member8, earlier, preferred_element_type=F32)
    pos = jnp.zeros((1, tm), F32)
    offset = jnp.zeros((1, 1), F32)
    lane = lax.broadcasted_iota(jnp.int32, (1, LANES), 1)
    counts = jnp.zeros((1, LANES), F32)
    for g in range(n_g):
        n_members = jnp.sum(member[g], axis=1, keepdims=True)
        n_blocks = jnp.floor((n_members + (MOE_BLOCK - 1)) * (1.0 / MOE_BLOCK))
        pos = pos + member[g] * (offset + rank[g:g + 1, :])
        counts = counts + jnp.where(lane == g, n_blocks, 0.0)
        offset = offset + n_blocks * MOE_BLOCK
    pos_i = pos.astype(jnp.int32)
    one_hot = _slot_one_hot(pos_i)
    hs_ref[...] = jnp.dot(one_hot, hi, preferred_element_type=F32).astype(BF16)
    pad = jnp.zeros((LANES - N_EXPERTS, tm), F32)
    gate_t = jnp.concatenate(gate_rows + [pad], axis=0).T
    g_hi = gate_t.astype(BF16)
    g_lo = (gate_t - g_hi.astype(F32)).astype(BF16)
    gs_ref[...] = jnp.dot(one_hot, jnp.concatenate([g_hi, g_lo], axis=1), preferred_element_type=F32).astype(BF16)
    pos_ref[0] = pos_i
    cnt_ref[0] = counts.astype(jnp.int32)


def _moe_sort_call(x, mod, router_w2, router_b):
    batch, seq, _ = x.shape
    per_seq = seq // MOE_TILE
    n_tiles = batch * per_seq
    return pl.pallas_call(
        _moe_sort_body,
        grid=(batch, per_seq),
        in_specs=[
            pl.BlockSpec((1, MOE_TILE, D_MODEL), lambda b, i: (b, i, 0)),
            pl.BlockSpec((1, 3, D_MODEL), lambda b, i: (b, 0, 0)),
            pl.BlockSpec((D_MODEL, 2 * LANES), lambda b, i: (0, 0)),
            pl.BlockSpec((N_EXPERTS, 1), lambda b, i: (0, 0)),
        ],
        out_specs=[
            pl.BlockSpec((MOE_SLOTS, D_MODEL), lambda b, i: (b * per_seq + i, 0)),
            pl.BlockSpec((MOE_SLOTS, 2 * LANES), lambda b, i: (b * per_seq + i, 0)),
            pl.BlockSpec((1, 1, MOE_TILE), lambda b, i: (b * per_seq + i, 0, 0)),
            pl.BlockSpec((1, 1, LANES), lambda b, i: (b * per_seq + i, 0, 0)),
        ],
        out_shape=[
            jax.ShapeDtypeStruct((n_tiles * MOE_SLOTS, D_MODEL), BF16),
            jax.ShapeDtypeStruct((n_tiles * MOE_SLOTS, 2 * LANES), BF16),
            jax.ShapeDtypeStruct((n_tiles, 1, MOE_TILE), jnp.int32),
            jax.ShapeDtypeStruct((n_tiles, 1, LANES), jnp.int32),
        ],
        compiler_params=_params("arbitrary", "arbitrary"),
        name="moe_sort",
    )(x, mod, router_w2, router_b)


def _moe_tables_body(cnt_ref, sg_ref, sv_ref, sb_ref, bp_ref):
    n_tiles = cnt_ref.shape[0]
    n_g, r, bpt = N_EXPERT_GROUPS, MOE_STEP_BLOCKS, MOE_BLOCKS_PER_TILE
    n_steps = sg_ref.shape[0]

    def fill(ref, n, value):
        def body(i, carry):
            ref[i] = value
            return carry
        lax.fori_loop(0, n, body, 0)

    fill(sb_ref, n_steps * r, 0)
    fill(bp_ref, n_tiles * bpt, 0)
    fill(sg_ref, n_steps, n_g - 1)
    fill(sv_ref, n_steps, 0)
    slot = jnp.int32(0)
    for g in range(n_g):
        def tile_body(i, k, g=g):
            first = i * bpt
            for g_before in range(g):
                first = first + cnt_ref[i, g_before]

            def block_body(j, k):
                sb_ref[k] = first + j
                bp_ref[first + j] = k
                return k + 1
            return lax.fori_loop(0, cnt_ref[i, g], block_body, k)

        slot_end = lax.fori_loop(0, n_tiles, tile_body, slot)
        step_first = slot // r
        step_stop = (slot_end + r - 1) // r

        def step_body(s, carry, g=g, slot_end=slot_end):
            sg_ref[s] = g
            sv_ref[s] = jnp.minimum(slot_end - s * r, r)
            return carry
        lax.fori_loop(step_first, step_stop, step_body, 0)
        slot = step_stop * r


def _moe_step_tables(counts):
    n_tiles = counts.shape[0]
    n_blocks = n_tiles * MOE_BLOCKS_PER_TILE
    n_steps = n_blocks // MOE_STEP_BLOCKS + N_EXPERT_GROUPS
    smem = pl.BlockSpec(memory_space=pltpu.SMEM)
    return pl.pallas_call(
        _moe_tables_body,
        in_specs=[smem],
        out_specs=[smem] * 4,
        out_shape=[jax.ShapeDtypeStruct((n_steps,), jnp.int32), jax.ShapeDtypeStruct((n_steps,), jnp.int32),
                   jax.ShapeDtypeStruct((n_steps * MOE_STEP_BLOCKS,), jnp.int32),
                   jax.ShapeDtypeStruct((n_blocks,), jnp.int32)],
        name="moe_tables",
    )(counts)


def _moe_expert_body(sg_ref, sv_ref, sb_ref, *refs):
    r = MOE_STEP_BLOCKS
    h_refs, g_refs = refs[:r], refs[r:2 * r]
    w1_ref, w3_ref, w2_ref, o_ref, w13_scr, w2_scr, act_scr = refs[2 * r:]
    step = pl.program_id(0)
    epg = EXPERTS_PER_GROUP
    rows_total = r * MOE_BLOCK
    group = sg_ref[step]
    new_group = jnp.logical_or(step == 0, group != sg_ref[jnp.maximum(step - 1, 0)])

    @pl.when(jnp.logical_and(sv_ref[step] > 0, new_group))
    def _():
        w13_scr[:, :, :D_EXPERT] = w1_ref[0].astype(BF16)
        w13_scr[:, :, D_EXPERT:] = w3_ref[0].astype(BF16)
        w2_scr[...] = w2_ref[0].reshape(epg * D_EXPERT, D_MODEL).astype(BF16)

    @pl.when(sv_ref[step] > 0)
    def _():
        h = jnp.concatenate([ref[...] for ref in h_refs], axis=0)
        gates = jnp.concatenate([ref[...] for ref in g_refs], axis=0)
        real = lax.broadcasted_iota(jnp.int32, (rows_total, LANES), 0) < sv_ref[step] * MOE_BLOCK
        expert_row = lax.broadcasted_iota(jnp.int32, (2 * LANES, LANES), 0) % LANES
        for j in range(epg):
            pick_rows = jnp.where(expert_row == group * epg + j, 1.0, 0.0).astype(BF16)
            gate_b = jnp.where(real, jnp.dot(gates, pick_rows, preferred_element_type=F32), 0.0)
            hid = jnp.dot(h, w13_scr[j], preferred_element_type=F32)
            h1 = hid[:, :D_EXPERT]
            act = h1 * jax.nn.sigmoid(h1) * hid[:, D_EXPERT:] * jnp.concatenate([gate_b, gate_b], axis=1)
            act_scr[:, j * D_EXPERT:(j + 1) * D_EXPERT] = act.astype(BF16)
        o_ref[...] = jnp.dot(act_scr[...], w2_scr[...], preferred_element_type=F32).astype(BF16)


def _moe_expert_call(hs, gs, w1, w3, w2, step_group, step_valid, step_blocks, layer):
    r = MOE_STEP_BLOCKS
    n_steps = step_group.shape[0]
    epg = EXPERTS_PER_GROUP
    rows_total = r * MOE_BLOCK

    def blk(width, j):
        return pl.BlockSpec((MOE_BLOCK, width), lambda s, sg, sv, sb, j=j: (sb[s * r + j], 0))

    def group_weights(shape):
        return pl.BlockSpec((1, epg) + shape, lambda s, sg, sv, sb: (layer, sg[s], 0, 0))

    grid_spec = pltpu.PrefetchScalarGridSpec(
        num_scalar_prefetch=3,
        grid=(n_steps,),
        in_specs=[blk(D_MODEL, j) for j in range(r)] + [blk(2 * LANES, j) for j in range(r)] + [
            group_weights((D_MODEL, D_EXPERT)), group_weights((D_MODEL, D_EXPERT)),
            group_weights((D_EXPERT, D_MODEL))],
        out_specs=pl.BlockSpec((rows_total, D_MODEL), lambda s, sg, sv, sb: (s, 0)),
        scratch_shapes=[
            pltpu.VMEM((epg, D_MODEL, 2 * D_EXPERT), BF16),
            pltpu.VMEM((epg * D_EXPERT, D_MODEL), BF16),
            pltpu.VMEM((rows_total, epg * D_EXPERT), BF16),
        ],
    )
    return pl.pallas_call(
        _moe_expert_body,
        grid_spec=grid_spec,
        out_shape=jax.ShapeDtypeStruct((n_steps * rows_total, D_MODEL), BF16),
        compiler_params=_params("arbitrary"),
        name="moe_experts",
    )(step_group, step_valid, step_blocks, *([hs] * r), *([gs] * r), w1, w3, w2)


def _moe_unsort_body(is_last, bp_ref, *refs):
    bpt = MOE_BLOCKS_PER_TILE
    y_refs = refs[:bpt]
    pos_ref, x_ref, mod_ref, fg_ref, o_ref = refs[bpt:]
    one_hot = _slot_one_hot(pos_ref[0])
    used = jnp.sum(one_hot.astype(F32), axis=1, keepdims=True) > 0.0
    ys = jnp.concatenate([ref[...] for ref in y_refs], axis=0)
    ys = jnp.where(used, ys, jnp.zeros_like(ys))
    y = lax.dot_general(one_hot, ys, TN_DIMS, preferred_element_type=F32)
    out = x_ref[0] + mod_ref[0, 2:3, :] * y
    if is_last:
        out = out * lax.rsqrt(jnp.mean(out * out, axis=-1, keepdims=True) + RMS_EPS) * fg_ref[...]
    o_ref[0] = out


def _moe_unsort_call(ys, pos, x, mod, final_g, block_pos, is_last):
    batch, seq, _ = x.shape
    per_seq = seq // MOE_TILE
    bpt = MOE_BLOCKS_PER_TILE

    def blk(j):
        return pl.BlockSpec((MOE_BLOCK, D_MODEL), lambda b, i, bp, j=j: (bp[(b * per_seq + i) * bpt + j], 0))

    grid_spec = pltpu.PrefetchScalarGridSpec(
        num_scalar_prefetch=1,
        grid=(batch, per_seq),
        in_specs=[blk(j) for j in range(bpt)] + [
            pl.BlockSpec((1, 1, MOE_TILE), lambda b, i, bp: (b * per_seq + i, 0, 0)),
            pl.BlockSpec((1, MOE_TILE, D_MODEL), lambda b, i, bp: (b, i, 0)),
            pl.BlockSpec((1, 3, D_MODEL), lambda b, i, bp: (b, 0, 0)),
            pl.BlockSpec((1, D_MODEL), lambda b, i, bp: (0, 0)),
        ],
        out_specs=pl.BlockSpec((1, MOE_TILE, D_MODEL), lambda b, i, bp: (b, i, 0)),
    )
    return pl.pallas_call(
        functools.partial(_moe_unsort_body, is_last),
        grid_spec=grid_spec,
        out_shape=jax.ShapeDtypeStruct(x.shape, F32),
        compiler_params=_params("arbitrary", "arbitrary"),
        name="moe_unsort",
    )(block_pos, *([ys] * bpt), pos, x, mod, final_g)


def _moe_layer(x, mod, router_w2, router_b, w1, w3, w2, final_g, layer, is_last):
    hs, gs, pos, cnt = _moe_sort_call(x, mod, router_w2, router_b)
    step_group, step_valid, step_blocks, block_pos = _moe_step_tables(cnt[:, 0, :])
    ys = _moe_expert_call(hs, gs, w1, w3, w2, step_group, step_valid, step_blocks, layer)
    return _moe_unsort_call(ys, pos, x, mod, final_g, block_pos, is_last)


def _pad_cols(w):
    n = w.shape[-1]
    return jnp.pad(w, ((0, 0), (0, -n % LANES)))


def _pad_rows(w):
    n = w.shape[0]
    return jnp.pad(w, ((0, -n % LANES), (0, 0)))


def kernel(x, c, ada_w, ada_b, mix_w_in, mix_w_out, sgu_norm_g, sgu_w, sgu_b, rwkv_mu, rwkv_w_rkv, rwkv_w_o, rwkv_w0, rwkv_w1, rwkv_w2, rwkv_a0, rwkv_a1, rwkv_a2, rwkv_v0, rwkv_v1, rwkv_v2, rwkv_g1, rwkv_g2, rwkv_k_k, rwkv_k_a, rwkv_r_k, rwkv_ln_w, rwkv_ln_b, router_w, router_b, moe_w1, moe_w3, moe_w2, final_norm_g):
    depth = ada_w.shape[0]
    row = lambda t: t.reshape(1, -1)
    mods = _ada_call(c, ada_w, ada_b)
    rw_hi = router_w.astype(BF16)
    rw_lo = (router_w - rw_hi.astype(F32)).astype(BF16)
    router_w2 = jnp.concatenate([_pad_cols(rw_hi), _pad_cols(rw_lo)], axis=1)
    router_bc = router_b.reshape(N_EXPERTS, 1)
    final_g = row(final_norm_g)
    v_first = None
    for layer in range(depth):
        i = layer // 2
        mod = mods[2 * layer]
        if layer % 2 == 0:
            proj = _proj_call(x, mod, mix_w_in[i].astype(BF16))
            o_a = _attn_call(proj)
            bias_tile = jnp.repeat(sgu_b[i].T, HEAD_DIM, axis=1)
            x = _sgu_out_call(proj, o_a, x, mod, row(sgu_norm_g[i]), sgu_w[i], bias_tile,
                              mix_w_out[i].astype(BF16))
        else:
            weights = [
                rwkv_mu[i],
                rwkv_w_rkv[i, 0].astype(BF16), rwkv_w_rkv[i, 1].astype(BF16), rwkv_w_rkv[i, 2].astype(BF16),
                row(rwkv_w0[i]), _pad_cols(rwkv_w1[i]).astype(BF16), _pad_rows(rwkv_w2[i]).astype(BF16),
                row(rwkv_a0[i]), _pad_cols(rwkv_a1[i]).astype(BF16), _pad_rows(rwkv_a2[i]).astype(BF16),
                _pad_cols(rwkv_g1[i]).astype(BF16), _pad_rows(rwkv_g2[i]).astype(BF16),
                row(rwkv_k_k[i]), row(rwkv_k_a[i]),
            ]
            vres = None
            if i > 0:
                vres = [row(rwkv_v0[i - 1]), _pad_cols(rwkv_v1[i - 1]).astype(BF16),
                        _pad_rows(rwkv_v2[i - 1]).astype(BF16)]
            r, w_log, k, v, kk, a, g = _rwkv_pre_call(x, mod, weights, v_first, vres)
            if i == 0:
                v_first = v
            y = _wkv_call(r, w_log, k, v, kk, a, row(rwkv_ln_w[i]), row(rwkv_ln_b[i]), row(rwkv_r_k[i]))
            x = _rwkv_post_call(y, g, x, mod, rwkv_w_o[i].astype(BF16))
        x = _moe_layer(x, mods[2 * layer + 1], router_w2, router_bc, moe_w1, moe_w3, moe_w2, final_g, layer,
                       layer == depth - 1)
    return x
```

```python
import functools

import jax
import jax.numpy as jnp
from jax import lax
from jax.experimental import pallas as pl
from jax.experimental.pallas import tpu as pltpu

F32 = jnp.float32
BF16 = jnp.bfloat16

D_MODEL = 1024
HEAD_DIM = 64
WIDTH_A = 512
WIDTH_B = 512
MIX_IN_WIDTH = 2560
ATTN_SPAN = 128
ATTN_DILATIONS = (1, 4, 16)
SGU_CHUNK = 128
N_GROUPS_B = 8
RWKV_HEADS = 16
RWKV_GN_EPS = 64e-5
WKV_CHUNK = 64
N_EXPERTS = 16
N_EXPERT_GROUPS = 4
EXPERTS_PER_GROUP = 4
D_EXPERT = 256
RMS_EPS = 1e-6
LANES = 128
VMEM_LIMIT = 56 * 1024 * 1024

NT_DIMS = (((1,), (1,)), ((), ()))
TN_DIMS = (((0,), (0,)), ((), ()))


def _params(*sem):
    return pltpu.CompilerParams(dimension_semantics=sem, vmem_limit_bytes=VMEM_LIMIT)


def _norm_mod(x, mod_ref):
    ms = jnp.mean(x * x, axis=-1, keepdims=True)
    return (x * lax.rsqrt(ms + RMS_EPS)) * (1.0 + mod_ref[0, 1:2, :]) + mod_ref[0, 0:1, :]


def _gelu(x):
    return 0.5 * x * (1.0 + lax.erf(x * 0.7071067811865476))


def _ada_body(c_ref, w_ref, b_ref, o_ref):
    c = c_ref[...]
    c_act = c * jax.nn.sigmoid(c)
    n = c.shape[0]
    c_hi = c_act.astype(BF16)
    c_lo = (c_act - c_hi.astype(F32)).astype(BF16)
    c2 = jnp.concatenate([c_hi, c_lo], axis=0)
    w = w_ref[0]
    w_hi = w.astype(BF16)
    w_lo = (w - w_hi.astype(F32)).astype(BF16)
    by_hi = jnp.dot(c2, w_hi, preferred_element_type=F32)
    by_lo = jnp.dot(c2, w_lo, preferred_element_type=F32)
    o_ref[0] = (by_hi[:n] + by_hi[n:]) + (by_lo[:n] + by_lo[n:]) + b_ref[0]


def _ada_call(c, ada_w, ada_b):
    n_pair = ada_w.shape[0] * ada_w.shape[1]
    batch = c.shape[0]
    w = ada_w.reshape(n_pair, D_MODEL, 3 * D_MODEL)
    b = ada_b.reshape(n_pair, 1, 3 * D_MODEL)
    out = pl.pallas_call(
        _ada_body,
        grid=(n_pair, 3),
        in_specs=[
            pl.BlockSpec((batch, D_MODEL), lambda p, j: (0, 0)),
            pl.BlockSpec((1, D_MODEL, D_MODEL), lambda p, j: (p, 0, j)),
            pl.BlockSpec((1, 1, D_MODEL), lambda p, j: (p, 0, j)),
        ],
        out_specs=pl.BlockSpec((1, batch, D_MODEL), lambda p, j: (p, 0, j)),
        out_shape=jax.ShapeDtypeStruct((n_pair, batch, 3 * D_MODEL), F32),
        compiler_params=_params("arbitrary", "arbitrary"),
        name="ada_mod",
    )(c, w, b)
    return out.reshape(n_pair, batch, 3, D_MODEL)


def _proj_body(x_ref, mod_ref, w_ref, o_ref):
    h = _norm_mod(x_ref[0], mod_ref)
    o_ref[0] = jnp.dot(h.astype(BF16), w_ref[...], preferred_element_type=F32)


def _proj_call(x, mod, w_in, ts=512):
    batch, seq, _ = x.shape
    return pl.pallas_call(
        _proj_body,
        grid=(batch, seq // ts),
        in_specs=[
            pl.BlockSpec((1, ts, D_MODEL), lambda b, i: (b, i, 0)),
            pl.BlockSpec((1, 3, D_MODEL), lambda b, i: (b, 0, 0)),
            pl.BlockSpec((D_MODEL, MIX_IN_WIDTH), lambda b, i: (0, 0)),
        ],
        out_specs=pl.BlockSpec((1, ts, MIX_IN_WIDTH), lambda b, i: (b, i, 0)),
        out_shape=jax.ShapeDtypeStruct((batch, seq, MIX_IN_WIDTH), F32),
        compiler_params=_params("arbitrary", "arbitrary"),
        name="mix_in_proj",
    )(x, mod, w_in)


def _attn_body(q_ref, k_ref, v_ref, o_ref, ob, lb):
    span = ATTN_SPAN
    W = LANES
    block_group = 8

    def iota(shape, dim):
        return lax.broadcasted_iota(jnp.int32, shape, dim)

    head0 = iota((span, W), 1) < HEAD_DIM
    mine = (head0, jnp.logical_not(head0))
    row2, col2 = iota((span, 2 * span), 0), iota((span, 2 * span), 1)
    mask_prev_cur = jnp.where(col2 < span, col2 - row2, row2 - (col2 - span)) >= 0
    mask_cur = iota((span, span), 1) <= iota((span, span), 0)

    def rows(start, n_rows, dil):
        return pl.ds(start, n_rows) if dil == 1 else pl.ds(start, n_rows, stride=dil)

    def group(p, dil, q_starts, with_prev):
        blocks = range(len(q_starts))
        heads = range(2)
        qb, kcat, vaug, masks = [], [], [], []
        for j in blocks:
            n_keys = 2 * span if with_prev[j] else span
            k_start = q_starts[j] - span * dil if with_prev[j] else q_starts[j]
            qb.append(q_ref[0, rows(q_starts[j], span, dil), :] * (HEAD_DIM ** -0.5))
            kcat.append(k_ref[0, rows(k_start, n_keys, dil), :].astype(BF16))
            vb = v_ref[0, rows(k_start, n_keys, dil), :].astype(BF16)
            vaug.append(jnp.concatenate([vb, jnp.ones((n_keys, W), BF16)], axis=1))
            masks.append(mask_prev_cur if with_prev[j] else mask_cur)
        qh = [[jnp.where(mine[h], qb[j], 0.0).astype(BF16) for h in heads] for j in blocks]
        s = [[lax.dot_general(qh[j][h], kcat[j], NT_DIMS, preferred_element_type=F32) for h in heads] for j in blocks]
        s = [[jnp.where(masks[j], s[j][h], -jnp.inf) for h in heads] for j in blocks]
        m = [[jnp.max(s[j][h], axis=-1, keepdims=True) for h in heads] for j in blocks]
        pr = [[jnp.exp(s[j][h] - m[j][h]).astype(BF16) for h in heads] for j in blocks]
        ad = [[jnp.dot(pr[j][h], vaug[j], preferred_element_type=F32) for h in heads] for j in blocks]
        for j in blocks:
            den = [ad[j][h][:, W:] for h in heads]
            o = jnp.where(head0, ad[j][0][:, :W] / den[0], ad[j][1][:, :W] / den[1])
            lse = jnp.where(head0, m[j][0] + jnp.log(den[0]), m[j][1] + jnp.log(den[1]))
            qi = rows(q_starts[j], span, dil)
            ob[p, qi, :] = o
            lb[p, qi, :] = lse

    seq = q_ref.shape[1]
    for p, dil in enumerate(ATTN_DILATIONS):
        n_blk = seq // dil // span
        blocks = [(r + n * span * dil, n > 0) for r in range(dil) for n in range(n_blk)]
        for g0 in range(0, len(blocks), block_group):
            members = blocks[g0:g0 + block_group]
            group(p, dil, [b[0] for b in members], [b[1] for b in members])

    l0, l1, l2 = lb[0], lb[1], lb[2]
    m = jnp.maximum(jnp.maximum(l0, l1), l2)
    w0, w1, w2 = jnp.exp(l0 - m), jnp.exp(l1 - m), jnp.exp(l2 - m)
    o_ref[0] = (w0 * ob[0] + w1 * ob[1] + w2 * ob[2]) / (w0 + w1 + w2)


def _attn_call(proj):
    batch, seq, _ = proj.shape
    n_pair = WIDTH_A // LANES
    blk = lambda off: pl.BlockSpec((1, seq, LANES), lambda b, h: (b, 0, off + h))
    return pl.pallas_call(
        _attn_body,
        grid=(batch, n_pair),
        in_specs=[blk(0), blk(n_pair), blk(2 * n_pair)],
        out_specs=pl.BlockSpec((1, seq, LANES), lambda b, h: (b, 0, h)),
        out_shape=jax.ShapeDtypeStruct((batch, seq, WIDTH_A), F32),
        scratch_shapes=[
            pltpu.VMEM((3, seq, LANES), F32),
            pltpu.VMEM((3, seq, LANES), F32),
        ],
        compiler_params=_params("arbitrary", "arbitrary"),
        name="dilated_attn",
    )(proj, proj, proj)


def _sgu_out_body(u_ref, z_ref, oa_ref, x_ref, mod_ref, ng_ref, sw_ref, sb_ref, wo_ref, o_ref, mix):
    ts = u_ref.shape[1]
    u = _gelu(u_ref[0])
    z = _gelu(z_ref[0])
    z = z * lax.rsqrt(jnp.mean(z * z, axis=-1, keepdims=True) + RMS_EPS) * ng_ref[...]
    zb = z.astype(BF16)
    row = lax.broadcasted_iota(jnp.int32, (SGU_CHUNK, SGU_CHUNK), 0)
    col = lax.broadcasted_iota(jnp.int32, (SGU_CHUNK, SGU_CHUNK), 1)
    causal = col <= row
    for g in range(N_GROUPS_B):
        w_g = jnp.where(causal, sw_ref[g], 0.0).astype(BF16)
        for cc in range(ts // SGU_CHUNK):
            rs = slice(cc * SGU_CHUNK, (cc + 1) * SGU_CHUNK)
            ls = slice(g * HEAD_DIM, (g + 1) * HEAD_DIM)
            mix[rs, ls] = jnp.dot(w_g, zb[rs, ls], preferred_element_type=F32)
    bias = jnp.concatenate([sb_ref[...]] * (ts // SGU_CHUNK), axis=0)
    o_b = u * (mix[...] + bias)
    y = jnp.dot(oa_ref[0].astype(BF16), wo_ref[0:WIDTH_A, :], preferred_element_type=F32)
    y = y + jnp.dot(o_b.astype(BF16), wo_ref[WIDTH_A:, :], preferred_element_type=F32)
    o_ref[0] = x_ref[0] + mod_ref[0, 2:3, :] * y


def _sgu_out_call(proj, o_a, x, mod, norm_g, sgu_w, sgu_bias_tile, w_out, ts=1024):
    batch, seq, _ = x.shape
    u_blk = 3 * WIDTH_A // WIDTH_B
    full = lambda shape: pl.BlockSpec(shape, lambda b, i: (0,) * len(shape))
    return pl.pallas_call(
        _sgu_out_body,
        grid=(batch, seq // ts),
        in_specs=[
            pl.BlockSpec((1, ts, WIDTH_B), lambda b, i: (b, i, u_blk)),
            pl.BlockSpec((1, ts, WIDTH_B), lambda b, i: (b, i, u_blk + 1)),
            pl.BlockSpec((1, ts, WIDTH_A), lambda b, i: (b, i, 0)),
            pl.BlockSpec((1, ts, D_MODEL), lambda b, i: (b, i, 0)),
            pl.BlockSpec((1, 3, D_MODEL), lambda b, i: (b, 0, 0)),
            full((1, WIDTH_B)),
            full((N_GROUPS_B, SGU_CHUNK, SGU_CHUNK)),
            full((SGU_CHUNK, WIDTH_B)),
            full((WIDTH_A + WIDTH_B, D_MODEL)),
        ],
        out_specs=pl.BlockSpec((1, ts, D_MODEL), lambda b, i: (b, i, 0)),
        out_shape=jax.ShapeDtypeStruct(x.shape, F32),
        scratch_shapes=[pltpu.VMEM((ts, WIDTH_B), F32)],
        compiler_params=_params("arbitrary", "arbitrary"),
        name="sgu_out_proj",
    )(proj, proj, o_a, x, mod, norm_g, sgu_w, sgu_bias_tile, w_out)


def _rwkv_pre_body(has_vres, *refs):
    if has_vres:
        (x_ref, mod_ref, mu_ref, wr_ref, wk_ref, wv_ref, w0_ref, w1_ref, w2_ref, a0_ref, a1_ref, a2_ref,
         g1_ref, g2_ref, kk_ref, ka_ref, vf_ref, v0_ref, v1_ref, v2_ref,
         r_out, w_out, k_out, v_out, kk_out, a_out, g_out, carry) = refs
    else:
        (x_ref, mod_ref, mu_ref, wr_ref, wk_ref, wv_ref, w0_ref, w1_ref, w2_ref, a0_ref, a1_ref, a2_ref,
         g1_ref, g2_ref, kk_ref, ka_ref,
         r_out, w_out, k_out, v_out, kk_out, a_out, g_out, carry) = refs

    @pl.when(pl.program_id(1) == 0)
    def _():
        carry[...] = jnp.zeros_like(carry)

    h = _norm_mod(x_ref[0], mod_ref)
    ts = h.shape[0]
    first = lax.broadcasted_iota(jnp.int32, h.shape, 0) == 0
    h_prev = jnp.where(first, carry[0:1, :], pltpu.roll(h, 1, 0))
    carry[0:1, :] = h[ts - 1:ts, :]
    xx = h_prev - h

    def mixed(i):
        return (h + xx * mu_ref[i:i + 1, :]).astype(BF16)

    def mm(a, w_ref):
        return jnp.dot(a, w_ref[...], preferred_element_type=F32)

    xr, xw, xk, xv, xa, xg = [mixed(i) for i in range(6)]
    r = mm(xr, wr_ref)
    k = mm(xk, wk_ref)
    v = mm(xv, wv_ref)
    z = w0_ref[...] + mm(jnp.tanh(mm(xw, w1_ref)).astype(BF16), w2_ref)
    w_log = -(jnp.maximum(-z, 0.0) + jnp.log(1.0 + jnp.exp(-jnp.abs(z)))) - 0.5
    if has_vres:
        mix_v = jax.nn.sigmoid(v0_ref[...] + mm(mm(xv, v1_ref).astype(BF16), v2_ref))
        v = v + (vf_ref[0].astype(F32) - v) * mix_v
    a = jax.nn.sigmoid(a0_ref[...] + mm(mm(xa, a1_ref).astype(BF16), a2_ref))
    g = mm(jax.nn.sigmoid(mm(xg, g1_ref)).astype(BF16), g2_ref)
    r_out[0] = r.astype(BF16)
    w_out[0] = w_log
    kk_out[0] = (k * kk_ref[...]).astype(BF16)
    k_out[0] = (k * (1.0 + (a - 1.0) * ka_ref[...])).astype(BF16)
    v_out[0] = v.astype(BF16)
    a_out[0] = a.astype(BF16)
    g_out[0] = g.astype(BF16)


def _rwkv_pre_call(x, mod, weights, v_first, vres, ts=512):
    batch, seq, _ = x.shape
    tok = pl.BlockSpec((1, ts, D_MODEL), lambda b, i: (b, i, 0))
    full = lambda a: pl.BlockSpec(a.shape, lambda b, i: (0,) * a.ndim, pipeline_mode=pl.Buffered(1))
    ins = [x, mod] + list(weights)
    specs = [tok, pl.BlockSpec((1, 3, D_MODEL), lambda b, i: (b, 0, 0))] + [full(a) for a in weights]
    if vres is not None:
        ins += [v_first] + list(vres)
        specs += [tok] + [full(a) for a in vres]
    out_dtypes = [BF16, F32, BF16, BF16, BF16, BF16, BF16]
    return pl.pallas_call(
        functools.partial(_rwkv_pre_body, vres is not None),
        grid=(batch, seq // ts),
        in_specs=specs,
        out_specs=[tok] * 7,
        out_shape=[jax.ShapeDtypeStruct(x.shape, dt) for dt in out_dtypes],
        scratch_shapes=[pltpu.VMEM((8, D_MODEL), F32)],
        compiler_params=_params("arbitrary", "arbitrary"),
        name="rwkv_pre",
    )(*ins)


def _wkv_body(r_ref, w_ref, k_ref, v_ref, kk_ref, a_ref, lnw_ref, lnb_ref, rk_ref, o_ref, state):
    C = WKV_CHUNK
    W = LANES
    n_chunk = r_ref.shape[1] // C
    n_pair = r_ref.shape[2] // W

    @pl.when(pl.program_id(2) == 0)
    def _():
        state[...] = jnp.zeros_like(state)

    def iota(shape, dim):
        return lax.broadcasted_iota(jnp.int32, shape, dim)

    def stack_heads(x):
        own = iota(x.shape, 1) % W < HEAD_DIM
        return jnp.concatenate([jnp.where(own, x, 0.0), jnp.where(own, 0.0, x)], axis=0).astype(BF16)

    def block_diag(x):
        left = iota(x.shape, 1) < C
        return jnp.concatenate([jnp.where(left, x, 0.0), jnp.where(left, 0.0, x)], axis=0).astype(BF16)

    row4, col4 = iota((C, 4 * C), 0), iota((C, 4 * C), 1) % C
    strict4 = col4 < row4
    lower4 = col4 <= row4
    tri = jnp.where(iota((C, C), 1) <= iota((C, C), 0), 1.0, 0.0).astype(BF16)
    eye2 = jnp.where(iota((C, 2 * C), 1) % C == iota((C, 2 * C), 0), 1.0, 0.0).astype(F32)
    rw, cw = iota((W, W), 0), iota((W, W), 1)
    same_head = (rw < HEAD_DIM) == (cw < HEAD_DIM)
    ones_bd = jnp.where(same_head, 1.0, 0.0).astype(BF16)
    diag_w = cw == rw
    zeros_cw = jnp.zeros((C, W), F32)

    def bdot(a, b):
        return jnp.dot(a.astype(BF16), b.astype(BF16), preferred_element_type=F32)

    chunks = range(n_chunk)
    pairs = range(n_pair)
    units = [(c, p) for c in chunks for p in pairs]
    rows = lambda t, c: t[c * C:(c + 1) * C]
    blk = lambda t, u: t[u[0] * C:(u[0] + 1) * C, u[1] * W:(u[1] + 1) * W]
    lanes = lambda t, p: t[:, p * W:(p + 1) * W]
    per_chunk = lambda vals: jnp.concatenate([jnp.broadcast_to(t, (C, n_pair * W)) for t in vals], axis=0)
    head_sums = lambda t: jnp.concatenate([bdot(lanes(t, p), ones_bd) for p in pairs], axis=1)

    r = r_ref[0].astype(F32)
    k = k_ref[0].astype(F32)
    v = v_ref[0].astype(F32)
    kk = kk_ref[0].astype(F32)
    kk = kk / jnp.maximum(jnp.sqrt(head_sums(kk * kk)), 1e-12)
    b = kk * a_ref[0].astype(F32)
    bonus = head_sums(r * k * rk_ref[...]) * v
    log_w = -jnp.exp(w_ref[0])
    hi = log_w.astype(BF16)
    lo = (log_w - hi.astype(F32)).astype(BF16)
    hilo = jnp.concatenate([hi, lo], axis=1)
    cum2 = [jnp.dot(tri, rows(hilo, c), preferred_element_type=F32) for c in chunks]
    cum = jnp.concatenate([c2[:, :n_pair * W] + c2[:, n_pair * W:] for c2 in cum2], axis=0)
    c_mid = [cum[c * C + C // 2 - 1:c * C + C // 2, :] for c in chunks]
    c_end = [cum[c * C + C - 1:(c + 1) * C, :] for c in chunks]
    g_last = [jnp.exp(t) for t in c_end]
    from_mid = jnp.exp(per_chunk(c_mid) - cum)
    to_mid = per_chunk([jnp.exp(-t) for t in c_mid])
    a_true = -kk * jnp.exp(cum - log_w)
    r_true = r * jnp.exp(cum)
    a_mid = a_true * to_mid
    r_mid = r_true * to_mid
    b_mid = b * from_mid
    k_mid = k * from_mid
    ar_mid = [jnp.concatenate([blk(a_mid, u), blk(r_mid, u)], axis=0).astype(BF16) for u in units]
    gram = [lax.dot_general(ar_mid[i], jnp.concatenate([stack_heads(blk(b_mid, u)), stack_heads(blk(k_mid, u))],
                                                        axis=0), NT_DIMS, preferred_element_type=F32)
            for i, u in enumerate(units)]
    top = [jnp.where(strict4, g[:C], 0.0) for g in gram]
    bot = [jnp.where(lower4, g[C:], 0.0).astype(BF16) for g in gram]
    n1 = [t[:, :2 * C] for t in top]
    inv = [eye2 + t for t in n1]
    pw = [bdot(t, block_diag(t)) for t in n1]
    levels = C.bit_length() - 2
    for lvl in range(levels):
        if lvl < levels - 1:
            st = [bdot(jnp.concatenate([pw[i], inv[i]], axis=0), block_diag(pw[i])) for i in range(len(units))]
            pw = [t[:C] for t in st]
            inv = [inv[i] + st[i][C:] for i in range(len(units))]
        else:
            inv = [inv[i] + bdot(inv[i], block_diag(pw[i])) for i in range(len(units))]
    x = [jnp.dot(top[i][:, 2 * C:].astype(BF16), stack_heads(blk(v, u)), preferred_element_type=F32)
         for i, u in enumerate(units)]
    pp = [jnp.dot(inv[i].astype(BF16), stack_heads(jnp.concatenate([blk(a_true, u), x[i]], axis=1)),
                  preferred_element_type=F32) for i, u in enumerate(units)]
    zv = [jnp.concatenate([zeros_cw, blk(v, u)], axis=1) for u in units]
    qq = [jnp.dot(bot[i], jnp.concatenate([stack_heads(pp[i]), stack_heads(zv[i])], axis=0),
                  preferred_element_type=F32) for i in range(len(units))]
    end_from_mid = [jnp.exp(c_end[c] - c_mid[c]) for c in chunks]
    bk_hat = [jnp.concatenate([blk(b_mid, u), blk(k_mid, u)], axis=0) * lanes(end_from_mid[u[0]], u[1])
              for u in units]
    mm = [lax.dot_general(bk_hat[i].astype(BF16), jnp.concatenate([pp[i], zv[i]], axis=0).astype(BF16), TN_DIMS,
                          preferred_element_type=F32) for i in range(len(units))]
    q1 = [blk(r_true, u) + qq[i][:, :W] for i, u in enumerate(units)]
    q2 = [t[:, W:] for t in qq]
    m1 = [(jnp.where(same_head, mm[i][:, :W], 0.0)
           + jnp.where(diag_w, jnp.broadcast_to(lanes(g_last[u[0]], u[1]), (W, W)), 0.0)).astype(BF16)
          for i, u in enumerate(units)]
    m2 = [jnp.where(same_head, t[:, W:], 0.0) for t in mm]
    h_cur = [state[p] for p in pairs]
    h_in = []
    for i, (c, p) in enumerate(units):
        h_in.append(h_cur[p])
        h_cur[p] = jnp.dot(m1[i], h_cur[p].astype(BF16), preferred_element_type=F32) + m2[i]
    for p in pairs:
        state[p] = h_cur[p]
    y_blk = [bdot(q1[i], h_in[i]) + q2[i] for i in range(len(units))]
    y = jnp.concatenate([jnp.concatenate([y_blk[c * n_pair + p] for p in pairs], axis=1) for c in chunks], axis=0)
    mean = head_sums(y) * (1.0 / HEAD_DIM)
    yc = y - mean
    var = head_sums(yc * yc) * (1.0 / HEAD_DIM)
    yn = yc * lax.rsqrt(var + RWKV_GN_EPS) * lnw_ref[...] + lnb_ref[...]
    o_ref[0] = (yn + bonus).astype(BF16)


def _wkv_call(r, w_log, k, v, kk, a, ln_w, ln_b, r_k, tc=256, pairs_per_step=8):
    batch, seq, _ = r.shape
    width = pairs_per_step * LANES
    tok = pl.BlockSpec((1, tc, width), lambda b, h, t: (b, t, h))
    vec = pl.BlockSpec((1, width), lambda b, h, t: (0, h))
    return pl.pallas_call(
        _wkv_body,
        grid=(batch, D_MODEL // width, seq // tc),
        in_specs=[tok] * 6 + [vec] * 3,
        out_specs=tok,
        out_shape=jax.ShapeDtypeStruct(r.shape, BF16),
        scratch_shapes=[pltpu.VMEM((pairs_per_step, LANES, LANES), F32)],
        compiler_params=_params("arbitrary", "arbitrary", "arbitrary"),
        name="wkv7_chunked",
    )(r, w_log, k, v, kk, a, ln_w, ln_b, r_k)


def _rwkv_post_body(y_ref, g_ref, x_ref, mod_ref, wo_ref, o_ref):
    yg = y_ref[0] * g_ref[0]
    o_ref[0] = x_ref[0] + mod_ref[0, 2:3, :] * jnp.dot(yg, wo_ref[...], preferred_element_type=F32)


def _rwkv_post_call(y, g, x, mod, w_o, ts=1024):
    batch, seq, _ = x.shape
    tok = pl.BlockSpec((1, ts, D_MODEL), lambda b, i: (b, i, 0))
    return pl.pallas_call(
        _rwkv_post_body,
        grid=(batch, seq // ts),
        in_specs=[tok, tok, tok, pl.BlockSpec((1, 3, D_MODEL), lambda b, i: (b, 0, 0)),
                  pl.BlockSpec((D_MODEL, D_MODEL), lambda b, i: (0, 0))],
        out_specs=tok,
        out_shape=jax.ShapeDtypeStruct(x.shape, F32),
        compiler_params=_params("arbitrary", "arbitrary"),
        name="rwkv_out_proj",
    )(y, g, x, mod, w_o)


MOE_TILE = 512
MOE_BLOCK = 64
MOE_SLOTS = MOE_TILE + N_EXPERT_GROUPS * MOE_BLOCK
MOE_BLOCKS_PER_TILE = MOE_SLOTS // MOE_BLOCK
MOE_STEP_BLOCKS = 8


def _route(logits_t, rb):
    n_g, epg = N_EXPERT_GROUPS, EXPERTS_PER_GROUP
    s = jax.nn.sigmoid(logits_t)
    sel = s + rb
    s_rows = [s[i:i + 1, :] for i in range(N_EXPERTS)]
    sel_rows = [sel[i:i + 1, :] for i in range(N_EXPERTS)]
    scores = []
    for g in range(n_g):
        members = sel_rows[g * epg:(g + 1) * epg]
        best = None
        for i in range(epg):
            for j in range(i + 1, epg):
                pair = members[i] + members[j]
                best = pair if best is None else jnp.maximum(best, pair)
        scores.append(best)
    g_idx = jnp.zeros_like(scores[0], dtype=jnp.int32)
    top = scores[0]
    for g in range(1, n_g):
        better = scores[g] > top
        g_idx = jnp.where(better, g, g_idx)
        top = jnp.where(better, scores[g], top)

    def pick(rows_, j):
        out = rows_[j]
        for g in range(1, n_g):
            out = jnp.where(g_idx == g, rows_[g * epg + j], out)
        return out

    in_sel = [pick(sel_rows, j) for j in range(epg)]
    in_s = [pick(s_rows, j) for j in range(epg)]
    chosen = []
    for j in range(epg):
        rank = jnp.zeros_like(g_idx)
        for i in range(epg):
            if i == j:
                continue
            ahead = (in_sel[i] >= in_sel[j]) if i < j else (in_sel[i] > in_sel[j])
            rank = rank + jnp.where(ahead, 1, 0)
        chosen.append(rank < 2)
    den = sum(jnp.where(chosen[j], in_s[j], 0.0) for j in range(epg))
    gate_rows = []
    for ex in range(N_EXPERTS):
        g, j = divmod(ex, epg)
        on = jnp.logical_and(chosen[j], g_idx == g)
        gate_rows.append(jnp.where(on, in_s[j] / den, 0.0))
    return g_idx, gate_rows


def _slot_one_hot(pos_row):
    slot = lax.broadcasted_iota(jnp.int32, (MOE_SLOTS, pos_row.shape[1]), 0)
    return jnp.where(slot == pos_row, 1.0, 0.0).astype(BF16)


def _moe_sort_body(x_ref, mod_ref, rw_ref, rb_ref, hs_ref, gs_ref, pos_ref, cnt_ref):
    tm = MOE_TILE
    n_g = N_EXPERT_GROUPS
    tiles = range(x_ref.shape[1] // tm)
    h = [_norm_mod(x_ref[0, i * tm:(i + 1) * tm, :], mod_ref) for i in tiles]
    hi = [t.astype(BF16) for t in h]
    lo = [(h[i] - hi[i].astype(F32)).astype(BF16) for i in tiles]
    parts = [jnp.dot(jnp.concatenate([hi[i], lo[i]], axis=0), rw_ref[...], preferred_element_type=F32)
             for i in tiles]
    logits = [(p[:tm, :LANES] + p[:tm, LANES:]) + (p[tm:, :LANES] + p[tm:, LANES:]) for p in parts]
    routed = [_route(t.T[:N_EXPERTS], rb_ref[...]) for t in logits]
    member = [[jnp.where(routed[i][0] == g, 1.0, 0.0) for g in range(n_g)] for i in tiles]
    earlier = jnp.where(lax.broadcasted_iota(jnp.int32, (tm, tm), 0) < lax.broadcasted_iota(jnp.int32, (tm, tm), 1),
                        1.0, 0.0).astype(BF16)
    rank = [jnp.dot(jnp.concatenate(member[i] + [jnp.zeros((8 - n_g, tm), F32)], axis=0).astype(BF16), earlier,
                    preferred_element_type=F32) for i in tiles]
    lane = lax.broadcasted_iota(jnp.int32, (1, LANES), 1)
    one_hot = []
    for i in tiles:
        pos = jnp.zeros((1, tm), F32)
        offset = jnp.zeros((1, 1), F32)
        counts = jnp.zeros((1, LANES), F32)
        for g in range(n_g):
            n_members = jnp.sum(member[i][g], axis=1, keepdims=True)
            n_blocks = jnp.floor((n_members + (MOE_BLOCK - 1)) * (1.0 / MOE_BLOCK))
            pos = pos + member[i][g] * (offset + rank[i][g:g + 1, :])
            counts = counts + jnp.where(lane == g, n_blocks, 0.0)
            offset = offset + n_blocks * MOE_BLOCK
        pos_i = pos.astype(jnp.int32)
        pos_ref[i] = pos_i
        cnt_ref[i] = counts.astype(jnp.int32)
        one_hot.append(_slot_one_hot(pos_i))
    pad = jnp.zeros((LANES - N_EXPERTS, tm), F32)
    gate_t = [jnp.concatenate(routed[i][1] + [pad], axis=0).T for i in tiles]
    g_hi = [t.astype(BF16) for t in gate_t]
    g_lo = [(gate_t[i] - g_hi[i].astype(F32)).astype(BF16) for i in tiles]
    hs = [jnp.dot(one_hot[i], hi[i], preferred_element_type=F32) for i in tiles]
    gs = [jnp.dot(one_hot[i], jnp.concatenate([g_hi[i], g_lo[i]], axis=1), preferred_element_type=F32)
          for i in tiles]
    for i in tiles:
        hs_ref[i * MOE_SLOTS:(i + 1) * MOE_SLOTS, :] = hs[i].astype(BF16)
        gs_ref[i * MOE_SLOTS:(i + 1) * MOE_SLOTS, :] = gs[i].astype(BF16)


def _moe_sort_call(x, mod, router_w2, router_b, tiles_per_step=4):
    batch, seq, _ = x.shape
    rows = tiles_per_step * MOE_TILE
    per_seq = seq // rows
    n_tiles = batch * seq // MOE_TILE
    step = lambda b, i: b * per_seq + i
    return pl.pallas_call(
        _moe_sort_body,
        grid=(batch, per_seq),
        in_specs=[
            pl.BlockSpec((1, rows, D_MODEL), lambda b, i: (b, i, 0)),
            pl.BlockSpec((1, 3, D_MODEL), lambda b, i: (b, 0, 0)),
            pl.BlockSpec((D_MODEL, 2 * LANES), lambda b, i: (0, 0)),
            pl.BlockSpec((N_EXPERTS, 1), lambda b, i: (0, 0)),
        ],
        out_specs=[
            pl.BlockSpec((tiles_per_step * MOE_SLOTS, D_MODEL), lambda b, i: (step(b, i), 0)),
            pl.BlockSpec((tiles_per_step * MOE_SLOTS, 2 * LANES), lambda b, i: (step(b, i), 0)),
            pl.BlockSpec((tiles_per_step, 1, MOE_TILE), lambda b, i: (step(b, i), 0, 0)),
            pl.BlockSpec((tiles_per_step, 1, LANES), lambda b, i: (step(b, i), 0, 0)),
        ],
        out_shape=[
            jax.ShapeDtypeStruct((n_tiles * MOE_SLOTS, D_MODEL), BF16),
            jax.ShapeDtypeStruct((n_tiles * MOE_SLOTS, 2 * LANES), BF16),
            jax.ShapeDtypeStruct((n_tiles, 1, MOE_TILE), jnp.int32),
            jax.ShapeDtypeStruct((n_tiles, 1, LANES), jnp.int32),
        ],
        compiler_params=_params("arbitrary", "arbitrary"),
        name="moe_sort",
    )(x, mod, router_w2, router_b)


def _moe_tables_body(cnt_ref, sg_ref, sv_ref, sb_ref, bp_ref):
    n_tiles = cnt_ref.shape[0]
    n_g, r, bpt = N_EXPERT_GROUPS, MOE_STEP_BLOCKS, MOE_BLOCKS_PER_TILE
    n_steps = sg_ref.shape[0]

    def fill(ref, n, value):
        def body(i, carry):
            ref[i] = value
            return carry
        lax.fori_loop(0, n, body, 0)

    fill(sb_ref, n_steps * r, 0)
    fill(bp_ref, n_tiles * bpt, 0)
    fill(sg_ref, n_steps, n_g - 1)
    fill(sv_ref, n_steps, 0)
    slot = jnp.int32(0)
    for g in range(n_g):
        def tile_body(i, k, g=g):
            first = i * bpt
            for g_before in range(g):
                first = first + cnt_ref[i, g_before]

            def block_body(j, k):
                sb_ref[k] = first + j
                bp_ref[first + j] = k
                return k + 1
            return lax.fori_loop(0, cnt_ref[i, g], block_body, k)

        slot_end = lax.fori_loop(0, n_tiles, tile_body, slot)
        step_first = slot // r
        step_stop = (slot_end + r - 1) // r

        def step_body(s, carry, g=g, slot_end=slot_end):
            sg_ref[s] = g
            sv_ref[s] = jnp.minimum(slot_end - s * r, r)
            return carry
        lax.fori_loop(step_first, step_stop, step_body, 0)
        slot = step_stop * r


def _moe_step_tables(counts):
    n_tiles = counts.shape[0]
    n_blocks = n_tiles * MOE_BLOCKS_PER_TILE
    n_steps = n_blocks // MOE_STEP_BLOCKS + N_EXPERT_GROUPS
    smem = pl.BlockSpec(memory_space=pltpu.SMEM)
    return pl.pallas_call(
        _moe_tables_body,
        in_specs=[smem],
        out_specs=[smem] * 4,
        out_shape=[jax.ShapeDtypeStruct((n_steps,), jnp.int32), jax.ShapeDtypeStruct((n_steps,), jnp.int32),
                   jax.ShapeDtypeStruct((n_steps * MOE_STEP_BLOCKS,), jnp.int32),
                   jax.ShapeDtypeStruct((n_blocks,), jnp.int32)],
        name="moe_tables",
    )(counts)


def _moe_expert_body(sg_ref, sv_ref, sb_ref, *refs):
    r = MOE_STEP_BLOCKS
    h_refs, g_refs = refs[:r], refs[r:2 * r]
    w1_ref, w3_ref, w2_ref, o_ref, w13_scr, w2_scr, act_scr = refs[2 * r:]
    step = pl.program_id(0)
    epg = EXPERTS_PER_GROUP
    rows_total = r * MOE_BLOCK
    group = sg_ref[step]
    new_group = jnp.logical_or(step == 0, group != sg_ref[jnp.maximum(step - 1, 0)])

    @pl.when(sv_ref[step] == 0)
    def _():
        o_ref[...] = jnp.zeros_like(o_ref)

    @pl.when(jnp.logical_and(sv_ref[step] > 0, new_group))
    def _():
        w13_scr[:, :, :D_EXPERT] = w1_ref[0].astype(BF16)
        w13_scr[:, :, D_EXPERT:] = w3_ref[0].astype(BF16)
        w2_scr[...] = w2_ref[0].reshape(epg * D_EXPERT, D_MODEL).astype(BF16)

    @pl.when(sv_ref[step] > 0)
    def _():
        h = jnp.concatenate([ref[...] for ref in h_refs], axis=0)
        gates = jnp.concatenate([ref[...] for ref in g_refs], axis=0)
        real = lax.broadcasted_iota(jnp.int32, (rows_total, LANES), 0) < sv_ref[step] * MOE_BLOCK
        expert_row = lax.broadcasted_iota(jnp.int32, (2 * LANES, LANES), 0) % LANES
        for j in range(epg):
            pick_rows = jnp.where(expert_row == group * epg + j, 1.0, 0.0).astype(BF16)
            gate_b = jnp.where(real, jnp.dot(gates, pick_rows, preferred_element_type=F32), 0.0)
            hid = jnp.dot(h, w13_scr[j], preferred_element_type=F32)
            h1 = hid[:, :D_EXPERT]
            act = h1 * jax.nn.sigmoid(h1) * hid[:, D_EXPERT:] * jnp.concatenate([gate_b, gate_b], axis=1)
            act_scr[:, j * D_EXPERT:(j + 1) * D_EXPERT] = act.astype(BF16)
        o_ref[...] = jnp.dot(act_scr[...], w2_scr[...], preferred_element_type=F32).astype(BF16)


def _moe_expert_call(hs, gs, w1, w3, w2, step_group, step_valid, step_blocks, layer):
    r = MOE_STEP_BLOCKS
    n_steps = step_group.shape[0]
    epg = EXPERTS_PER_GROUP
    rows_total = r * MOE_BLOCK

    def blk(width, j):
        return pl.BlockSpec((MOE_BLOCK, width), lambda s, sg, sv, sb, j=j: (sb[s * r + j], 0))

    def group_weights(shape):
        return pl.BlockSpec((1, epg) + shape, lambda s, sg, sv, sb: (layer, sg[s], 0, 0))

    grid_spec = pltpu.PrefetchScalarGridSpec(
        num_scalar_prefetch=3,
        grid=(n_steps,),
        in_specs=[blk(D_MODEL, j) for j in range(r)] + [blk(2 * LANES, j) for j in range(r)] + [
            group_weights((D_MODEL, D_EXPERT)), group_weights((D_MODEL, D_EXPERT)),
            group_weights((D_EXPERT, D_MODEL))],
        out_specs=pl.BlockSpec((rows_total, D_MODEL), lambda s, sg, sv, sb: (s, 0)),
        scratch_shapes=[
            pltpu.VMEM((epg, D_MODEL, 2 * D_EXPERT), BF16),
            pltpu.VMEM((epg * D_EXPERT, D_MODEL), BF16),
            pltpu.VMEM((rows_total, epg * D_EXPERT), BF16),
        ],
    )
    return pl.pallas_call(
        _moe_expert_body,
        grid_spec=grid_spec,
        out_shape=jax.ShapeDtypeStruct((n_steps * rows_total, D_MODEL), BF16),
        compiler_params=_params("arbitrary"),
        name="moe_experts",
    )(step_group, step_valid, step_blocks, *([hs] * r), *([gs] * r), w1, w3, w2)


def _moe_unsort_body(is_last, bp_ref, *refs):
    bpt = MOE_BLOCKS_PER_TILE
    pos_ref, x_ref, mod_ref, fg_ref, o_ref = refs[-5:]
    y_refs = refs[:-5]
    tm = MOE_TILE
    tiles = range(len(y_refs) // bpt)
    slot = lax.broadcasted_iota(jnp.int32, (tm, MOE_SLOTS), 1)
    one_hot = []
    for i in tiles:
        pos_col = jnp.broadcast_to(pos_ref[i].astype(F32), (LANES, tm)).T.astype(jnp.int32)
        pos_col = jnp.concatenate([pos_col] * (MOE_SLOTS // LANES), axis=1)
        one_hot.append(jnp.where(slot == pos_col, 1.0, 0.0).astype(BF16))
    ys = [jnp.concatenate([ref[...] for ref in y_refs[i * bpt:(i + 1) * bpt]], axis=0) for i in tiles]
    y = [jnp.dot(one_hot[i], ys[i], preferred_element_type=F32) for i in tiles]
    for i in tiles:
        out = x_ref[0, i * tm:(i + 1) * tm, :] + mod_ref[0, 2:3, :] * y[i]
        if is_last:
            out = out * lax.rsqrt(jnp.mean(out * out, axis=-1, keepdims=True) + RMS_EPS) * fg_ref[...]
        o_ref[0, i * tm:(i + 1) * tm, :] = out


def _moe_unsort_call(ys, pos, x, mod, final_g, block_pos, is_last, tiles_per_step=4):
    batch, seq, _ = x.shape
    rows = tiles_per_step * MOE_TILE
    per_seq = seq // rows
    bpt = MOE_BLOCKS_PER_TILE
    n_in = tiles_per_step * bpt

    def blk(j):
        return pl.BlockSpec((MOE_BLOCK, D_MODEL), lambda b, i, bp, j=j: (bp[(b * per_seq + i) * n_in + j], 0))

    grid_spec = pltpu.PrefetchScalarGridSpec(
        num_scalar_prefetch=1,
        grid=(batch, per_seq),
        in_specs=[blk(j) for j in range(n_in)] + [
            pl.BlockSpec((tiles_per_step, 1, MOE_TILE), lambda b, i, bp: (b * per_seq + i, 0, 0)),
            pl.BlockSpec((1, rows, D_MODEL), lambda b, i, bp: (b, i, 0)),
            pl.BlockSpec((1, 3, D_MODEL), lambda b, i, bp: (b, 0, 0)),
            pl.BlockSpec((1, D_MODEL), lambda b, i, bp: (0, 0)),
        ],
        out_specs=pl.BlockSpec((1, rows, D_MODEL), lambda b, i, bp: (b, i, 0)),
    )
    return pl.pallas_call(
        functools.partial(_moe_unsort_body, is_last),
        grid_spec=grid_spec,
        out_shape=jax.ShapeDtypeStruct(x.shape, F32),
        compiler_params=_params("arbitrary", "arbitrary"),
        name="moe_unsort",
    )(block_pos, *([ys] * n_in), pos, x, mod, final_g)


def _moe_layer(x, mod, router_w2, router_b, w1, w3, w2, final_g, layer, is_last):
    hs, gs, pos, cnt = _moe_sort_call(x, mod, router_w2, router_b)
    step_group, step_valid, step_blocks, block_pos = _moe_step_tables(cnt[:, 0, :])
    ys = _moe_expert_call(hs, gs, w1, w3, w2, step_group, step_valid, step_blocks, layer)
    return _moe_unsort_call(ys, pos, x, mod, final_g, block_pos, is_last)


def _pad_cols(w):
    n = w.shape[-1]
    return jnp.pad(w, ((0, 0), (0, -n % LANES)))


def _pad_rows(w):
    n = w.shape[0]
    return jnp.pad(w, ((0, -n % LANES), (0, 0)))


def kernel(x, c, ada_w, ada_b, mix_w_in, mix_w_out, sgu_norm_g, sgu_w, sgu_b, rwkv_mu, rwkv_w_rkv, rwkv_w_o, rwkv_w0, rwkv_w1, rwkv_w2, rwkv_a0, rwkv_a1, rwkv_a2, rwkv_v0, rwkv_v1, rwkv_v2, rwkv_g1, rwkv_g2, rwkv_k_k, rwkv_k_a, rwkv_r_k, rwkv_ln_w, rwkv_ln_b, router_w, router_b, moe_w1, moe_w3, moe_w2, final_norm_g):
    depth = ada_w.shape[0]
    row = lambda t: t.reshape(1, -1)
    mods = _ada_call(c, ada_w, ada_b)
    rw_hi = router_w.astype(BF16)
    rw_lo = (router_w - rw_hi.astype(F32)).astype(BF16)
    router_w2 = jnp.concatenate([_pad_cols(rw_hi), _pad_cols(rw_lo)], axis=1)
    router_bc = router_b.reshape(N_EXPERTS, 1)
    final_g = row(final_norm_g)
    v_first = None
    for layer in range(depth):
        i = layer // 2
        mod = mods[2 * layer]
        if layer % 2 == 0:
            proj = _proj_call(x, mod, mix_w_in[i].astype(BF16))
            o_a = _attn_call(proj)
            bias_tile = jnp.repeat(sgu_b[i].T, HEAD_DIM, axis=1)
            x = _sgu_out_call(proj, o_a, x, mod, row(sgu_norm_g[i]), sgu_w[i], bias_tile,
                              mix_w_out[i].astype(BF16))
        else:
            weights = [
                rwkv_mu[i],
                rwkv_w_rkv[i, 0].astype(BF16), rwkv_w_rkv[i, 1].astype(BF16), rwkv_w_rkv[i, 2].astype(BF16),
                row(rwkv_w0[i]), _pad_cols(rwkv_w1[i]).astype(BF16), _pad_rows(rwkv_w2[i]).astype(BF16),
                row(rwkv_a0[i]), _pad_cols(rwkv_a1[i]).astype(BF16), _pad_rows(rwkv_a2[i]).astype(BF16),
                _pad_cols(rwkv_g1[i]).astype(BF16), _pad_rows(rwkv_g2[i]).astype(BF16),
                row(rwkv_k_k[i]), row(rwkv_k_a[i]),
            ]
            vres = None
            if i > 0:
                vres = [row(rwkv_v0[i - 1]), _pad_cols(rwkv_v1[i - 1]).astype(BF16),
                        _pad_rows(rwkv_v2[i - 1]).astype(BF16)]
            r, w_log, k, v, kk, a, g = _rwkv_pre_call(x, mod, weights, v_first, vres)
            if i == 0:
                v_first = v
            y = _wkv_call(r, w_log, k, v, kk, a, row(rwkv_ln_w[i]), row(rwkv_ln_b[i]), row(rwkv_r_k[i]))
            x = _rwkv_post_call(y, g, x, mod, rwkv_w_o[i].astype(BF16))
        x = _moe_layer(x, mods[2 * layer + 1], router_w2, router_bc, moe_w1, moe_w3, moe_w2, final_g, layer,
                       layer == depth - 1)
    return x
```

```python
import functools

import jax
import jax.numpy as jnp
from jax import lax
from jax.experimental import pallas as pl
from jax.experimental.pallas import tpu as pltpu

F32 = jnp.float32
BF16 = jnp.bfloat16

D_MODEL = 1024
HEAD_DIM = 64
WIDTH_A = 512
WIDTH_B = 512
MIX_IN_WIDTH = 2560
ATTN_SPAN = 128
ATTN_DILATIONS = (1, 4, 16)
SGU_CHUNK = 128
N_GROUPS_B = 8
RWKV_HEADS = 16
RWKV_GN_EPS = 64e-5
WKV_CHUNK = 64
N_EXPERTS = 16
N_EXPERT_GROUPS = 4
EXPERTS_PER_GROUP = 4
D_EXPERT = 256
RMS_EPS = 1e-6
LANES = 128
VMEM_LIMIT = 56 * 1024 * 1024

NT_DIMS = (((1,), (1,)), ((), ()))
TN_DIMS = (((0,), (0,)), ((), ()))


def _params(*sem):
    return pltpu.CompilerParams(dimension_semantics=sem, vmem_limit_bytes=VMEM_LIMIT)


def _norm_mod(x, mod_ref):
    ms = jnp.mean(x * x, axis=-1, keepdims=True)
    return (x * lax.rsqrt(ms + RMS_EPS)) * (1.0 + mod_ref[0, 1:2, :]) + mod_ref[0, 0:1, :]


def _gelu(x):
    return 0.5 * x * (1.0 + lax.erf(x * 0.7071067811865476))


def _ada_body(c_ref, w_ref, b_ref, o_ref):
    c = c_ref[...]
    c_act = c * jax.nn.sigmoid(c)
    n = c.shape[0]
    c_hi = c_act.astype(BF16)
    c_lo = (c_act - c_hi.astype(F32)).astype(BF16)
    c2 = jnp.concatenate([c_hi, c_lo], axis=0)
    w = w_ref[0]
    w_hi = w.astype(BF16)
    w_lo = (w - w_hi.astype(F32)).astype(BF16)
    by_hi = jnp.dot(c2, w_hi, preferred_element_type=F32)
    by_lo = jnp.dot(c2, w_lo, preferred_element_type=F32)
    o_ref[0] = (by_hi[:n] + by_hi[n:]) + (by_lo[:n] + by_lo[n:]) + b_ref[0]


def _ada_call(c, ada_w, ada_b):
    n_pair = ada_w.shape[0] * ada_w.shape[1]
    batch = c.shape[0]
    w = ada_w.reshape(n_pair, D_MODEL, 3 * D_MODEL)
    b = ada_b.reshape(n_pair, 1, 3 * D_MODEL)
    out = pl.pallas_call(
        _ada_body,
        grid=(n_pair, 3),
        in_specs=[
            pl.BlockSpec((batch, D_MODEL), lambda p, j: (0, 0)),
            pl.BlockSpec((1, D_MODEL, D_MODEL), lambda p, j: (p, 0, j)),
            pl.BlockSpec((1, 1, D_MODEL), lambda p, j: (p, 0, j)),
        ],
        out_specs=pl.BlockSpec((1, batch, D_MODEL), lambda p, j: (p, 0, j)),
        out_shape=jax.ShapeDtypeStruct((n_pair, batch, 3 * D_MODEL), F32),
        compiler_params=_params("arbitrary", "arbitrary"),
        name="ada_mod",
    )(c, w, b)
    return out.reshape(n_pair, batch, 3, D_MODEL)


def _proj_body(x_ref, mod_ref, w_ref, o_ref):
    h = _norm_mod(x_ref[0], mod_ref)
    o_ref[0] = jnp.dot(h.astype(BF16), w_ref[...], preferred_element_type=F32)


def _proj_call(x, mod, w_in, ts=512):
    batch, seq, _ = x.shape
    return pl.pallas_call(
        _proj_body,
        grid=(batch, seq // ts),
        in_specs=[
            pl.BlockSpec((1, ts, D_MODEL), lambda b, i: (b, i, 0)),
            pl.BlockSpec((1, 3, D_MODEL), lambda b, i: (b, 0, 0)),
            pl.BlockSpec((D_MODEL, MIX_IN_WIDTH), lambda b, i: (0, 0)),
        ],
        out_specs=pl.BlockSpec((1, ts, MIX_IN_WIDTH), lambda b, i: (b, i, 0)),
        out_shape=jax.ShapeDtypeStruct((batch, seq, MIX_IN_WIDTH), F32),
        compiler_params=_params("arbitrary", "arbitrary"),
        name="mix_in_proj",
    )(x, mod, w_in)


def _attn_body(q_ref, k_ref, v_ref, o_ref, ob, lb):
    span = ATTN_SPAN
    W = LANES
    block_group = 8

    def iota(shape, dim):
        return lax.broadcasted_iota(jnp.int32, shape, dim)

    head0 = iota((span, W), 1) < HEAD_DIM
    mine = (head0, jnp.logical_not(head0))
    row2, col2 = iota((span, 2 * span), 0), iota((span, 2 * span), 1)
    mask_prev_cur = jnp.where(col2 < span, col2 - row2, row2 - (col2 - span)) >= 0
    mask_cur = iota((span, span), 1) <= iota((span, span), 0)

    def rows(start, n_rows, dil):
        return pl.ds(start, n_rows) if dil == 1 else pl.ds(start, n_rows, stride=dil)

    def group(p, dil, q_starts, with_prev):
        blocks = range(len(q_starts))
        heads = range(2)
        qb, kcat, vaug, masks = [], [], [], []
        for j in blocks:
            n_keys = 2 * span if with_prev[j] else span
            k_start = q_starts[j] - span * dil if with_prev[j] else q_starts[j]
            qb.append(q_ref[0, rows(q_starts[j], span, dil), :] * (HEAD_DIM ** -0.5))
            kcat.append(k_ref[0, rows(k_start, n_keys, dil), :].astype(BF16))
            vb = v_ref[0, rows(k_start, n_keys, dil), :].astype(BF16)
            vaug.append(jnp.concatenate([vb, jnp.ones((n_keys, W), BF16)], axis=1))
            masks.append(mask_prev_cur if with_prev[j] else mask_cur)
        qh = [[jnp.where(mine[h], qb[j], 0.0).astype(BF16) for h in heads] for j in blocks]
        s = [[lax.dot_general(qh[j][h], kcat[j], NT_DIMS, preferred_element_type=F32) for h in heads] for j in blocks]
        s = [[jnp.where(masks[j], s[j][h], -jnp.inf) for h in heads] for j in blocks]
        m = [[jnp.max(s[j][h], axis=-1, keepdims=True) for h in heads] for j in blocks]
        pr = [[jnp.exp(s[j][h] - m[j][h]).astype(BF16) for h in heads] for j in blocks]
        ad = [[jnp.dot(pr[j][h], vaug[j], preferred_element_type=F32) for h in heads] for j in blocks]
        for j in blocks:
            den = jnp.where(head0, ad[j][0][:, W:], ad[j][1][:, W:])
            o = jnp.where(head0, ad[j][0][:, :W], ad[j][1][:, :W]) / den
            lse = jnp.where(head0, m[j][0], m[j][1]) + jnp.log(den)
            qi = rows(q_starts[j], span, dil)
            ob[p, qi, :] = o
            lb[p, qi, :] = lse

    seq = q_ref.shape[1]
    for p, dil in enumerate(ATTN_DILATIONS):
        n_blk = seq // dil // span
        blocks = [(r + n * span * dil, n > 0) for r in range(dil) for n in range(n_blk)]
        for g0 in range(0, len(blocks), block_group):
            members = blocks[g0:g0 + block_group]
            group(p, dil, [b[0] for b in members], [b[1] for b in members])

    l0, l1, l2 = lb[0], lb[1], lb[2]
    m = jnp.maximum(jnp.maximum(l0, l1), l2)
    w0, w1, w2 = jnp.exp(l0 - m), jnp.exp(l1 - m), jnp.exp(l2 - m)
    o_ref[0] = (w0 * ob[0] + w1 * ob[1] + w2 * ob[2]) / (w0 + w1 + w2)


def _attn_call(proj):
    batch, seq, _ = proj.shape
    n_pair = WIDTH_A // LANES
    blk = lambda off: pl.BlockSpec((1, seq, LANES), lambda b, h: (b, 0, off + h))
    return pl.pallas_call(
        _attn_body,
        grid=(batch, n_pair),
        in_specs=[blk(0), blk(n_pair), blk(2 * n_pair)],
        out_specs=pl.BlockSpec((1, seq, LANES), lambda b, h: (b, 0, h)),
        out_shape=jax.ShapeDtypeStruct((batch, seq, WIDTH_A), F32),
        scratch_shapes=[
            pltpu.VMEM((3, seq, LANES), F32),
            pltpu.VMEM((3, seq, LANES), F32),
        ],
        compiler_params=_params("arbitrary", "arbitrary"),
        name="dilated_attn",
    )(proj, proj, proj)


def _sgu_out_body(u_ref, z_ref, oa_ref, x_ref, mod_ref, ng_ref, sw_ref, sb_ref, wo_ref, o_ref, mix):
    ts = u_ref.shape[1]
    u = _gelu(u_ref[0])
    z = _gelu(z_ref[0])
    z = z * lax.rsqrt(jnp.mean(z * z, axis=-1, keepdims=True) + RMS_EPS) * ng_ref[...]
    zb = z.astype(BF16)
    row = lax.broadcasted_iota(jnp.int32, (SGU_CHUNK, SGU_CHUNK), 0)
    col = lax.broadcasted_iota(jnp.int32, (SGU_CHUNK, SGU_CHUNK), 1)
    causal = col <= row
    for g in range(N_GROUPS_B):
        w_g = jnp.where(causal, sw_ref[g], 0.0).astype(BF16)
        for cc in range(ts // SGU_CHUNK):
            rs = slice(cc * SGU_CHUNK, (cc + 1) * SGU_CHUNK)
            ls = slice(g * HEAD_DIM, (g + 1) * HEAD_DIM)
            mix[rs, ls] = jnp.dot(w_g, zb[rs, ls], preferred_element_type=F32)
    bias = jnp.concatenate([sb_ref[...]] * (ts // SGU_CHUNK), axis=0)
    o_b = u * (mix[...] + bias)
    y = jnp.dot(oa_ref[0].astype(BF16), wo_ref[0:WIDTH_A, :], preferred_element_type=F32)
    y = y + jnp.dot(o_b.astype(BF16), wo_ref[WIDTH_A:, :], preferred_element_type=F32)
    o_ref[0] = x_ref[0] + mod_ref[0, 2:3, :] * y


def _sgu_out_call(proj, o_a, x, mod, norm_g, sgu_w, sgu_bias_tile, w_out, ts=1024):
    batch, seq, _ = x.shape
    u_blk = 3 * WIDTH_A // WIDTH_B
    full = lambda shape: pl.BlockSpec(shape, lambda b, i: (0,) * len(shape))
    return pl.pallas_call(
        _sgu_out_body,
        grid=(batch, seq // ts),
        in_specs=[
            pl.BlockSpec((1, ts, WIDTH_B), lambda b, i: (b, i, u_blk)),
            pl.BlockSpec((1, ts, WIDTH_B), lambda b, i: (b, i, u_blk + 1)),
            pl.BlockSpec((1, ts, WIDTH_A), lambda b, i: (b, i, 0)),
            pl.BlockSpec((1, ts, D_MODEL), lambda b, i: (b, i, 0)),
            pl.BlockSpec((1, 3, D_MODEL), lambda b, i: (b, 0, 0)),
            full((1, WIDTH_B)),
            full((N_GROUPS_B, SGU_CHUNK, SGU_CHUNK)),
            full((SGU_CHUNK, WIDTH_B)),
            full((WIDTH_A + WIDTH_B, D_MODEL)),
        ],
        out_specs=pl.BlockSpec((1, ts, D_MODEL), lambda b, i: (b, i, 0)),
        out_shape=jax.ShapeDtypeStruct(x.shape, F32),
        scratch_shapes=[pltpu.VMEM((ts, WIDTH_B), F32)],
        compiler_params=_params("arbitrary", "arbitrary"),
        name="sgu_out_proj",
    )(proj, proj, o_a, x, mod, norm_g, sgu_w, sgu_bias_tile, w_out)


def _rwkv_pre_body(has_vres, *refs):
    if has_vres:
        (x_ref, mod_ref, mu_ref, wr_ref, wk_ref, wv_ref, w0_ref, w1_ref, w2_ref, a0_ref, a1_ref, a2_ref,
         g1_ref, g2_ref, kk_ref, ka_ref, vf_ref, v0_ref, v1_ref, v2_ref,
         r_out, w_out, k_out, v_out, kk_out, a_out, g_out, carry) = refs
    else:
        (x_ref, mod_ref, mu_ref, wr_ref, wk_ref, wv_ref, w0_ref, w1_ref, w2_ref, a0_ref, a1_ref, a2_ref,
         g1_ref, g2_ref, kk_ref, ka_ref,
         r_out, w_out, k_out, v_out, kk_out, a_out, g_out, carry) = refs

    @pl.when(pl.program_id(1) == 0)
    def _():
        carry[...] = jnp.zeros_like(carry)

    h = _norm_mod(x_ref[0], mod_ref)
    ts = h.shape[0]
    first = lax.broadcasted_iota(jnp.int32, h.shape, 0) == 0
    h_prev = jnp.where(first, carry[0:1, :], pltpu.roll(h, 1, 0))
    carry[0:1, :] = h[ts - 1:ts, :]
    xx = h_prev - h

    def mixed(i):
        return (h + xx * mu_ref[i:i + 1, :]).astype(BF16)

    def mm(a, w_ref):
        return jnp.dot(a, w_ref[...], preferred_element_type=F32)

    xr, xw, xk, xv, xa, xg = [mixed(i) for i in range(6)]
    r = mm(xr, wr_ref)
    k = mm(xk, wk_ref)
    v = mm(xv, wv_ref)
    z = w0_ref[...] + mm(jnp.tanh(mm(xw, w1_ref)).astype(BF16), w2_ref)
    w_log = -(jnp.maximum(-z, 0.0) + jnp.log(1.0 + jnp.exp(-jnp.abs(z)))) - 0.5
    if has_vres:
        mix_v = jax.nn.sigmoid(v0_ref[...] + mm(mm(xv, v1_ref).astype(BF16), v2_ref))
        v = v + (vf_ref[0].astype(F32) - v) * mix_v
    a = jax.nn.sigmoid(a0_ref[...] + mm(mm(xa, a1_ref).astype(BF16), a2_ref))
    g = mm(jax.nn.sigmoid(mm(xg, g1_ref)).astype(BF16), g2_ref)
    r_out[0] = r.astype(BF16)
    w_out[0] = w_log
    kk_out[0] = (k * kk_ref[...]).astype(BF16)
    k_out[0] = (k * (1.0 + (a - 1.0) * ka_ref[...])).astype(BF16)
    v_out[0] = v.astype(BF16)
    a_out[0] = a.astype(BF16)
    g_out[0] = g.astype(BF16)


def _rwkv_pre_call(x, mod, weights, v_first, vres, ts=512):
    batch, seq, _ = x.shape
    tok = pl.BlockSpec((1, ts, D_MODEL), lambda b, i: (b, i, 0))
    full = lambda a: pl.BlockSpec(a.shape, lambda b, i: (0,) * a.ndim, pipeline_mode=pl.Buffered(1))
    ins = [x, mod] + list(weights)
    specs = [tok, pl.BlockSpec((1, 3, D_MODEL), lambda b, i: (b, 0, 0))] + [full(a) for a in weights]
    if vres is not None:
        ins += [v_first] + list(vres)
        specs += [tok] + [full(a) for a in vres]
    out_dtypes = [BF16, F32, BF16, BF16, BF16, BF16, BF16]
    return pl.pallas_call(
        functools.partial(_rwkv_pre_body, vres is not None),
        grid=(batch, seq // ts),
        in_specs=specs,
        out_specs=[tok] * 7,
        out_shape=[jax.ShapeDtypeStruct(x.shape, dt) for dt in out_dtypes],
        scratch_shapes=[pltpu.VMEM((8, D_MODEL), F32)],
        compiler_params=_params("arbitrary", "arbitrary"),
        name="rwkv_pre",
    )(*ins)


def _wkv_body(r_ref, w_ref, k_ref, v_ref, kk_ref, a_ref, lnw_ref, lnb_ref, rk_ref, o_ref, state):
    C = WKV_CHUNK
    W = LANES
    n_chunk = r_ref.shape[1] // C
    n_pair = r_ref.shape[2] // W

    @pl.when(pl.program_id(2) == 0)
    def _():
        state[...] = jnp.zeros_like(state)

    def iota(shape, dim):
        return lax.broadcasted_iota(jnp.int32, shape, dim)

    def stack_heads(x):
        own = iota(x.shape, 1) % W < HEAD_DIM
        return jnp.concatenate([jnp.where(own, x, 0.0), jnp.where(own, 0.0, x)], axis=0).astype(BF16)

    def block_diag(x):
        left = iota(x.shape, 1) < C
        return jnp.concatenate([jnp.where(left, x, 0.0), jnp.where(left, 0.0, x)], axis=0).astype(BF16)

    row4, col4 = iota((C, 4 * C), 0), iota((C, 4 * C), 1) % C
    strict4 = col4 < row4
    lower4 = col4 <= row4
    tri = jnp.where(iota((C, C), 1) <= iota((C, C), 0), 1.0, 0.0).astype(BF16)
    eye2 = jnp.where(iota((C, 2 * C), 1) % C == iota((C, 2 * C), 0), 1.0, 0.0).astype(F32)
    rw, cw = iota((W, W), 0), iota((W, W), 1)
    same_head = (rw < HEAD_DIM) == (cw < HEAD_DIM)
    ones_bd = jnp.where(same_head, 1.0, 0.0).astype(BF16)
    diag_w = cw == rw
    zeros_cw = jnp.zeros((C, W), F32)

    def bdot(a, b):
        return jnp.dot(a.astype(BF16), b.astype(BF16), preferred_element_type=F32)

    chunks = range(n_chunk)
    pairs = range(n_pair)
    units = [(c, p) for c in chunks for p in pairs]
    rows = lambda t, c: t[c * C:(c + 1) * C]
    blk = lambda t, u: t[u[0] * C:(u[0] + 1) * C, u[1] * W:(u[1] + 1) * W]
    lanes = lambda t, p: t[:, p * W:(p + 1) * W]
    per_chunk = lambda vals: jnp.concatenate([jnp.broadcast_to(t, (C, n_pair * W)) for t in vals], axis=0)
    head_sums = lambda t: jnp.concatenate([bdot(lanes(t, p), ones_bd) for p in pairs], axis=1)

    r = r_ref[0].astype(F32)
    k = k_ref[0].astype(F32)
    v = v_ref[0].astype(F32)
    kk = kk_ref[0].astype(F32)
    kk = kk / jnp.maximum(jnp.sqrt(head_sums(kk * kk)), 1e-12)
    b = kk * a_ref[0].astype(F32)
    bonus = head_sums(r * k * rk_ref[...]) * v
    log_w = -jnp.exp(w_ref[0])
    hi = log_w.astype(BF16)
    lo = (log_w - hi.astype(F32)).astype(BF16)
    hilo = jnp.concatenate([hi, lo], axis=1)
    cum2 = [jnp.dot(tri, rows(hilo, c), preferred_element_type=F32) for c in chunks]
    cum = jnp.concatenate([c2[:, :n_pair * W] + c2[:, n_pair * W:] for c2 in cum2], axis=0)
    c_mid = [cum[c * C + C // 2 - 1:c * C + C // 2, :] for c in chunks]
    c_end = [cum[c * C + C - 1:(c + 1) * C, :] for c in chunks]
    g_last = [jnp.exp(t) for t in c_end]
    from_mid = jnp.exp(per_chunk(c_mid) - cum)
    to_mid = per_chunk([jnp.exp(-t) for t in c_mid])
    a_true = -kk * jnp.exp(cum - log_w)
    r_true = r * jnp.exp(cum)
    a_mid = a_true * to_mid
    r_mid = r_true * to_mid
    b_mid = b * from_mid
    k_mid = k * from_mid
    ar_mid = [jnp.concatenate([blk(a_mid, u), blk(r_mid, u)], axis=0).astype(BF16) for u in units]
    gram = [lax.dot_general(ar_mid[i], jnp.concatenate([stack_heads(blk(b_mid, u)), stack_heads(blk(k_mid, u))],
                                                        axis=0), NT_DIMS, preferred_element_type=F32)
            for i, u in enumerate(units)]
    top = [jnp.where(strict4, g[:C], 0.0) for g in gram]
    bot = [jnp.where(lower4, g[C:], 0.0).astype(BF16) for g in gram]
    n1 = [t[:, :2 * C] for t in top]
    inv = [eye2 + t for t in n1]
    pw = [bdot(t, block_diag(t)) for t in n1]
    levels = C.bit_length() - 2
    for lvl in range(levels):
        if lvl < levels - 1:
            st = [bdot(jnp.concatenate([pw[i], inv[i]], axis=0), block_diag(pw[i])) for i in range(len(units))]
            pw = [t[:C] for t in st]
            inv = [inv[i] + st[i][C:] for i in range(len(units))]
        else:
            inv = [inv[i] + bdot(inv[i], block_diag(pw[i])) for i in range(len(units))]
    x = [jnp.dot(top[i][:, 2 * C:].astype(BF16), stack_heads(blk(v, u)), preferred_element_type=F32)
         for i, u in enumerate(units)]
    pp = [jnp.dot(inv[i].astype(BF16), stack_heads(jnp.concatenate([blk(a_true, u), x[i]], axis=1)),
                  preferred_element_type=F32) for i, u in enumerate(units)]
    zv = [jnp.concatenate([zeros_cw, blk(v, u)], axis=1) for u in units]
    qq = [jnp.dot(bot[i], jnp.concatenate([stack_heads(pp[i]), stack_heads(zv[i])], axis=0),
                  preferred_element_type=F32) for i in range(len(units))]
    end_from_mid = [jnp.exp(c_end[c] - c_mid[c]) for c in chunks]
    bk_hat = [jnp.concatenate([blk(b_mid, u), blk(k_mid, u)], axis=0) * lanes(end_from_mid[u[0]], u[1])
              for u in units]
    mm = [lax.dot_general(bk_hat[i].astype(BF16), jnp.concatenate([pp[i], zv[i]], axis=0).astype(BF16), TN_DIMS,
                          preferred_element_type=F32) for i in range(len(units))]
    q1 = [blk(r_true, u) + qq[i][:, :W] for i, u in enumerate(units)]
    q2 = [t[:, W:] for t in qq]
    m1 = [(jnp.where(same_head, mm[i][:, :W], 0.0)
           + jnp.where(diag_w, jnp.broadcast_to(lanes(g_last[u[0]], u[1]), (W, W)), 0.0)).astype(BF16)
          for i, u in enumerate(units)]
    m2 = [jnp.where(same_head, t[:, W:], 0.0) for t in mm]
    h_cur = [state[p] for p in pairs]
    h_in = []
    for i, (c, p) in enumerate(units):
        h_in.append(h_cur[p])
        h_cur[p] = jnp.dot(m1[i], h_cur[p].astype(BF16), preferred_element_type=F32) + m2[i]
    for p in pairs:
        state[p] = h_cur[p]
    y_blk = [bdot(q1[i], h_in[i]) + q2[i] for i in range(len(units))]
    y = jnp.concatenate([jnp.concatenate([y_blk[c * n_pair + p] for p in pairs], axis=1) for c in chunks], axis=0)
    mean = head_sums(y) * (1.0 / HEAD_DIM)
    yc = y - mean
    var = head_sums(yc * yc) * (1.0 / HEAD_DIM)
    yn = yc * lax.rsqrt(var + RWKV_GN_EPS) * lnw_ref[...] + lnb_ref[...]
    o_ref[0] = (yn + bonus).astype(BF16)


def _wkv_call(r, w_log, k, v, kk, a, ln_w, ln_b, r_k, tc=256, pairs_per_step=8):
    batch, seq, _ = r.shape
    width = pairs_per_step * LANES
    tok = pl.BlockSpec((1, tc, width), lambda b, h, t: (b, t, h))
    vec = pl.BlockSpec((1, width), lambda b, h, t: (0, h))
    return pl.pallas_call(
        _wkv_body,
        grid=(batch, D_MODEL // width, seq // tc),
        in_specs=[tok] * 6 + [vec] * 3,
        out_specs=tok,
        out_shape=jax.ShapeDtypeStruct(r.shape, BF16),
        scratch_shapes=[pltpu.VMEM((pairs_per_step, LANES, LANES), F32)],
        compiler_params=_params("arbitrary", "arbitrary", "arbitrary"),
        name="wkv7_chunked",
    )(r, w_log, k, v, kk, a, ln_w, ln_b, r_k)


def _rwkv_post_body(y_ref, g_ref, x_ref, mod_ref, wo_ref, o_ref):
    yg = y_ref[0] * g_ref[0]
    o_ref[0] = x_ref[0] + mod_ref[0, 2:3, :] * jnp.dot(yg, wo_ref[...], preferred_element_type=F32)


def _rwkv_post_call(y, g, x, mod, w_o, ts=1024):
    batch, seq, _ = x.shape
    tok = pl.BlockSpec((1, ts, D_MODEL), lambda b, i: (b, i, 0))
    return pl.pallas_call(
        _rwkv_post_body,
        grid=(batch, seq // ts),
        in_specs=[tok, tok, tok, pl.BlockSpec((1, 3, D_MODEL), lambda b, i: (b, 0, 0)),
                  pl.BlockSpec((D_MODEL, D_MODEL), lambda b, i: (0, 0))],
        out_specs=tok,
        out_shape=jax.ShapeDtypeStruct(x.shape, F32),
        compiler_params=_params("arbitrary", "arbitrary"),
        name="rwkv_out_proj",
    )(y, g, x, mod, w_o)


MOE_TILE = 512
MOE_BLOCK = 64
MOE_SLOTS = MOE_TILE + N_EXPERT_GROUPS * MOE_BLOCK
MOE_BLOCKS_PER_TILE = MOE_SLOTS // MOE_BLOCK
MOE_STEP_BLOCKS = 16


def _route(logits_t, rb):
    n_g, epg = N_EXPERT_GROUPS, EXPERTS_PER_GROUP
    s = jax.nn.sigmoid(logits_t)
    sel = s + rb
    s_rows = [s[i:i + 1, :] for i in range(N_EXPERTS)]
    sel_rows = [sel[i:i + 1, :] for i in range(N_EXPERTS)]
    scores = []
    for g in range(n_g):
        members = sel_rows[g * epg:(g + 1) * epg]
        best = None
        for i in range(epg):
            for j in range(i + 1, epg):
                pair = members[i] + members[j]
                best = pair if best is None else jnp.maximum(best, pair)
        scores.append(best)
    g_idx = jnp.zeros_like(scores[0], dtype=jnp.int32)
    top = scores[0]
    for g in range(1, n_g):
        better = scores[g] > top
        g_idx = jnp.where(better, g, g_idx)
        top = jnp.where(better, scores[g], top)

    def pick(rows_, j):
        out = rows_[j]
        for g in range(1, n_g):
            out = jnp.where(g_idx == g, rows_[g * epg + j], out)
        return out

    in_sel = [pick(sel_rows, j) for j in range(epg)]
    in_s = [pick(s_rows, j) for j in range(epg)]
    chosen = []
    for j in range(epg):
        rank = jnp.zeros_like(g_idx)
        for i in range(epg):
            if i == j:
                continue
            ahead = (in_sel[i] >= in_sel[j]) if i < j else (in_sel[i] > in_sel[j])
            rank = rank + jnp.where(ahead, 1, 0)
        chosen.append(rank < 2)
    den = sum(jnp.where(chosen[j], in_s[j], 0.0) for j in range(epg))
    gate_rows = []
    for ex in range(N_EXPERTS):
        g, j = divmod(ex, epg)
        on = jnp.logical_and(chosen[j], g_idx == g)
        gate_rows.append(jnp.where(on, in_s[j] / den, 0.0))
    return g_idx, gate_rows


def _slot_one_hot(pos_row):
    slot = lax.broadcasted_iota(jnp.int32, (MOE_SLOTS, pos_row.shape[1]), 0)
    return jnp.where(slot == pos_row, 1.0, 0.0).astype(BF16)


def _moe_sort_body(x_ref, mod_ref, rw_ref, rb_ref, hs_ref, gs_ref, pos_ref, cnt_ref):
    tm = MOE_TILE
    n_g = N_EXPERT_GROUPS
    tiles = range(x_ref.shape[1] // tm)
    h = [_norm_mod(x_ref[0, i * tm:(i + 1) * tm, :], mod_ref) for i in tiles]
    hi = [t.astype(BF16) for t in h]
    lo = [(h[i] - hi[i].astype(F32)).astype(BF16) for i in tiles]
    parts = [jnp.dot(jnp.concatenate([hi[i], lo[i]], axis=0), rw_ref[...], preferred_element_type=F32)
             for i in tiles]
    logits = [(p[:tm, :LANES] + p[:tm, LANES:]) + (p[tm:, :LANES] + p[tm:, LANES:]) for p in parts]
    routed = [_route(t.T[:N_EXPERTS], rb_ref[...]) for t in logits]
    member = [[jnp.where(routed[i][0] == g, 1.0, 0.0) for g in range(n_g)] for i in tiles]
    earlier = jnp.where(lax.broadcasted_iota(jnp.int32, (tm, tm), 0) < lax.broadcasted_iota(jnp.int32, (tm, tm), 1),
                        1.0, 0.0).astype(BF16)
    rank = [jnp.dot(jnp.concatenate(member[i] + [jnp.zeros((8 - n_g, tm), F32)], axis=0).astype(BF16), earlier,
                    preferred_element_type=F32) for i in tiles]
    lane = lax.broadcasted_iota(jnp.int32, (1, LANES), 1)
    one_hot = []
    for i in tiles:
        pos = jnp.zeros((1, tm), F32)
        offset = jnp.zeros((1, 1), F32)
        counts = jnp.zeros((1, LANES), F32)
        for g in range(n_g):
            n_members = jnp.sum(member[i][g], axis=1, keepdims=True)
            n_blocks = jnp.floor((n_members + (MOE_BLOCK - 1)) * (1.0 / MOE_BLOCK))
            pos = pos + member[i][g] * (offset + rank[i][g:g + 1, :])
            counts = counts + jnp.where(lane == g, n_blocks, 0.0)
            offset = offset + n_blocks * MOE_BLOCK
        pos_i = pos.astype(jnp.int32)
        pos_ref[i] = pos_i
        cnt_ref[i] = counts.astype(jnp.int32)
        one_hot.append(_slot_one_hot(pos_i))
    pad = jnp.zeros((LANES - N_EXPERTS, tm), F32)
    gate_t = [jnp.concatenate(routed[i][1] + [pad], axis=0).T for i in tiles]
    g_hi = [t.astype(BF16) for t in gate_t]
    g_lo = [(gate_t[i] - g_hi[i].astype(F32)).astype(BF16) for i in tiles]
    hs = [jnp.dot(one_hot[i], hi[i], preferred_element_type=F32) for i in tiles]
    gs = [jnp.dot(one_hot[i], jnp.concatenate([g_hi[i], g_lo[i]], axis=1), preferred_element_type=F32)
          for i in tiles]
    for i in tiles:
        hs_ref[i * MOE_SLOTS:(i + 1) * MOE_SLOTS, :] = hs[i].astype(BF16)
        gs_ref[i * MOE_SLOTS:(i + 1) * MOE_SLOTS, :] = gs[i].astype(BF16)


def _moe_sort_call(x, mod, router_w2, router_b, tiles_per_step=4):
    batch, seq, _ = x.shape
    rows = tiles_per_step * MOE_TILE
    per_seq = seq // rows
    n_tiles = batch * seq // MOE_TILE
    step = lambda b, i: b * per_seq + i
    return pl.pallas_call(
        _moe_sort_body,
        grid=(batch, per_seq),
        in_specs=[
            pl.BlockSpec((1, rows, D_MODEL), lambda b, i: (b, i, 0)),
            pl.BlockSpec((1, 3, D_MODEL), lambda b, i: (b, 0, 0)),
            pl.BlockSpec((D_MODEL, 2 * LANES), lambda b, i: (0, 0)),
            pl.BlockSpec((N_EXPERTS, 1), lambda b, i: (0, 0)),
        ],
        out_specs=[
            pl.BlockSpec((tiles_per_step * MOE_SLOTS, D_MODEL), lambda b, i: (step(b, i), 0)),
            pl.BlockSpec((tiles_per_step * MOE_SLOTS, 2 * LANES), lambda b, i: (step(b, i), 0)),
            pl.BlockSpec((tiles_per_step, 1, MOE_TILE), lambda b, i: (step(b, i), 0, 0)),
            pl.BlockSpec((tiles_per_step, 1, LANES), lambda b, i: (step(b, i), 0, 0)),
        ],
        out_shape=[
            jax.ShapeDtypeStruct((n_tiles * MOE_SLOTS, D_MODEL), BF16),
            jax.ShapeDtypeStruct((n_tiles * MOE_SLOTS, 2 * LANES), BF16),
            jax.ShapeDtypeStruct((n_tiles, 1, MOE_TILE), jnp.int32),
            jax.ShapeDtypeStruct((n_tiles, 1, LANES), jnp.int32),
        ],
        compiler_params=_params("arbitrary", "arbitrary"),
        name="moe_sort",
    )(x, mod, router_w2, router_b)


def _moe_tables_body(cnt_ref, sg_ref, sv_ref, sb_ref, bp_ref):
    n_tiles = cnt_ref.shape[0]
    n_g, r, bpt = N_EXPERT_GROUPS, MOE_STEP_BLOCKS, MOE_BLOCKS_PER_TILE
    n_steps = sg_ref.shape[0]

    def fill(ref, n, value):
        def body(i, carry):
            ref[i] = value
            return carry
        lax.fori_loop(0, n, body, 0)

    fill(sb_ref, n_steps * r, 0)
    fill(bp_ref, n_tiles * bpt, 0)
    fill(sg_ref, n_steps, n_g - 1)
    fill(sv_ref, n_steps, 0)
    slot = jnp.int32(0)
    for g in range(n_g):
        def tile_body(i, k, g=g):
            first = i * bpt
            for g_before in range(g):
                first = first + cnt_ref[i, g_before]

            def block_body(j, k):
                sb_ref[k] = first + j
                bp_ref[first + j] = k
                return k + 1
            return lax.fori_loop(0, cnt_ref[i, g], block_body, k)

        slot_end = lax.fori_loop(0, n_tiles, tile_body, slot)
        step_first = slot // r
        step_stop = (slot_end + r - 1) // r

        def step_body(s, carry, g=g, slot_end=slot_end):
            sg_ref[s] = g
            sv_ref[s] = jnp.minimum(slot_end - s * r, r)
            return carry
        lax.fori_loop(step_first, step_stop, step_body, 0)
        slot = step_stop * r


def _moe_step_tables(counts):
    n_tiles = counts.shape[0]
    n_blocks = n_tiles * MOE_BLOCKS_PER_TILE
    n_steps = n_blocks // MOE_STEP_BLOCKS + N_EXPERT_GROUPS
    smem = pl.BlockSpec(memory_space=pltpu.SMEM)
    return pl.pallas_call(
        _moe_tables_body,
        in_specs=[smem],
        out_specs=[smem] * 4,
        out_shape=[jax.ShapeDtypeStruct((n_steps,), jnp.int32), jax.ShapeDtypeStruct((n_steps,), jnp.int32),
                   jax.ShapeDtypeStruct((n_steps * MOE_STEP_BLOCKS,), jnp.int32),
                   jax.ShapeDtypeStruct((n_blocks,), jnp.int32)],
        name="moe_tables",
    )(counts)


def _moe_expert_body(sg_ref, sv_ref, sb_ref, *refs):
    r = MOE_STEP_BLOCKS
    h_refs, g_refs = refs[:r], refs[r:2 * r]
    w1_ref, w3_ref, w2_ref, o_ref, w13_scr, w2_scr, act_scr = refs[2 * r:]
    step = pl.program_id(0)
    epg = EXPERTS_PER_GROUP
    rows_total = r * MOE_BLOCK
    group = sg_ref[step]
    new_group = jnp.logical_or(step == 0, group != sg_ref[jnp.maximum(step - 1, 0)])

    @pl.when(sv_ref[step] == 0)
    def _():
        o_ref[...] = jnp.zeros_like(o_ref)

    @pl.when(jnp.logical_and(sv_ref[step] > 0, new_group))
    def _():
        w13_scr[:, :, :D_EXPERT] = w1_ref[0].astype(BF16)
        w13_scr[:, :, D_EXPERT:] = w3_ref[0].astype(BF16)
        w2_scr[...] = w2_ref[0].reshape(epg * D_EXPERT, D_MODEL).astype(BF16)

    @pl.when(sv_ref[step] > 0)
    def _():
        h = jnp.concatenate([ref[...] for ref in h_refs], axis=0)
        gates = jnp.concatenate([ref[...] for ref in g_refs], axis=0)
        real = lax.broadcasted_iota(jnp.int32, (rows_total, LANES), 0) < sv_ref[step] * MOE_BLOCK
        expert_row = lax.broadcasted_iota(jnp.int32, (2 * LANES, epg * LANES), 0) % LANES
        lane_block = lax.broadcasted_iota(jnp.int32, (2 * LANES, epg * LANES), 1) // LANES
        pick_rows = jnp.where(expert_row == group * epg + lane_block, 1.0, 0.0).astype(BF16)
        gate_all = jnp.dot(gates, pick_rows, preferred_element_type=F32)
        for j in range(epg):
            gate_b = jnp.where(real, gate_all[:, j * LANES:(j + 1) * LANES], 0.0)
            hid = jnp.dot(h, w13_scr[j], preferred_element_type=F32)
            h1 = hid[:, :D_EXPERT]
            act = h1 * jax.nn.sigmoid(h1) * hid[:, D_EXPERT:] * jnp.concatenate([gate_b, gate_b], axis=1)
            act_scr[:, j * D_EXPERT:(j + 1) * D_EXPERT] = act.astype(BF16)
        o_ref[...] = jnp.dot(act_scr[...], w2_scr[...], preferred_element_type=F32).astype(BF16)


def _moe_expert_call(hs, gs, w1, w3, w2, step_group, step_valid, step_blocks, layer):
    r = MOE_STEP_BLOCKS
    n_steps = step_group.shape[0]
    epg = EXPERTS_PER_GROUP
    rows_total = r * MOE_BLOCK

    def blk(width, j):
        return pl.BlockSpec((MOE_BLOCK, width), lambda s, sg, sv, sb, j=j: (sb[s * r + j], 0))

    def group_weights(shape):
        return pl.BlockSpec((1, epg) + shape, lambda s, sg, sv, sb: (layer, sg[s], 0, 0))

    grid_spec = pltpu.PrefetchScalarGridSpec(
        num_scalar_prefetch=3,
        grid=(n_steps,),
        in_specs=[blk(D_MODEL, j) for j in range(r)] + [blk(2 * LANES, j) for j in range(r)] + [
            group_weights((D_MODEL, D_EXPERT)), group_weights((D_MODEL, D_EXPERT)),
            group_weights((D_EXPERT, D_MODEL))],
        out_specs=pl.BlockSpec((rows_total, D_MODEL), lambda s, sg, sv, sb: (s, 0)),
        scratch_shapes=[
            pltpu.VMEM((epg, D_MODEL, 2 * D_EXPERT), BF16),
            pltpu.VMEM((epg * D_EXPERT, D_MODEL), BF16),
            pltpu.VMEM((rows_total, epg * D_EXPERT), BF16),
        ],
    )
    return pl.pallas_call(
        _moe_expert_body,
        grid_spec=grid_spec,
        out_shape=jax.ShapeDtypeStruct((n_steps * rows_total, D_MODEL), BF16),
        compiler_params=_params("arbitrary"),
        name="moe_experts",
    )(step_group, step_valid, step_blocks, *([hs] * r), *([gs] * r), w1, w3, w2)


def _moe_unsort_body(is_last, bp_ref, *refs):
    bpt = MOE_BLOCKS_PER_TILE
    pos_ref, x_ref, mod_ref, fg_ref, o_ref = refs[-5:]
    y_refs = refs[:-5]
    tm = MOE_TILE
    tiles = range(len(y_refs) // bpt)
    slot = lax.broadcasted_iota(jnp.int32, (tm, MOE_SLOTS), 1)
    one_hot = []
    for i in tiles:
        pos_col = jnp.broadcast_to(pos_ref[i].astype(F32), (LANES, tm)).T.astype(jnp.int32)
        pos_col = jnp.concatenate([pos_col] * (MOE_SLOTS // LANES), axis=1)
        one_hot.append(jnp.where(slot == pos_col, 1.0, 0.0).astype(BF16))
    ys = [jnp.concatenate([ref[...] for ref in y_refs[i * bpt:(i + 1) * bpt]], axis=0) for i in tiles]
    y = [jnp.dot(one_hot[i], ys[i], preferred_element_type=F32) for i in tiles]
    for i in tiles:
        out = x_ref[0, i * tm:(i + 1) * tm, :] + mod_ref[0, 2:3, :] * y[i]
        if is_last:
            out = out * lax.rsqrt(jnp.mean(out * out, axis=-1, keepdims=True) + RMS_EPS) * fg_ref[...]
        o_ref[0, i * tm:(i + 1) * tm, :] = out


def _moe_unsort_call(ys, pos, x, mod, final_g, block_pos, is_last, tiles_per_step=4):
    batch, seq, _ = x.shape
    rows = tiles_per_step * MOE_TILE
    per_seq = seq // rows
    bpt = MOE_BLOCKS_PER_TILE
    n_in = tiles_per_step * bpt

    def blk(j):
        return pl.BlockSpec((MOE_BLOCK, D_MODEL), lambda b, i, bp, j=j: (bp[(b * per_seq + i) * n_in + j], 0))

    grid_spec = pltpu.PrefetchScalarGridSpec(
        num_scalar_prefetch=1,
        grid=(batch, per_seq),
        in_specs=[blk(j) for j in range(n_in)] + [
            pl.BlockSpec((tiles_per_step, 1, MOE_TILE), lambda b, i, bp: (b * per_seq + i, 0, 0)),
            pl.BlockSpec((1, rows, D_MODEL), lambda b, i, bp: (b, i, 0)),
            pl.BlockSpec((1, 3, D_MODEL), lambda b, i, bp: (b, 0, 0)),
            pl.BlockSpec((1, D_MODEL), lambda b, i, bp: (0, 0)),
        ],
        out_specs=pl.BlockSpec((1, rows, D_MODEL), lambda b, i, bp: (b, i, 0)),
    )
    return pl.pallas_call(
        functools.partial(_moe_unsort_body, is_last),
        grid_spec=grid_spec,
        out_shape=jax.ShapeDtypeStruct(x.shape, F32),
        compiler_params=_params("arbitrary", "arbitrary"),
        name="moe_unsort",
    )(block_pos, *([ys] * n_in), pos, x, mod, final_g)


def _moe_layer(x, mod, router_w2, router_b, w1, w3, w2, final_g, layer, is_last):
    hs, gs, pos, cnt = _moe_sort_call(x, mod, router_w2, router_b)
    step_group, step_valid, step_blocks, block_pos = _moe_step_tables(cnt[:, 0, :])
    ys = _moe_expert_call(hs, gs, w1, w3, w2, step_group, step_valid, step_blocks, layer)
    return _moe_unsort_call(ys, pos, x, mod, final_g, block_pos, is_last)


def _pad_cols(w):
    n = w.shape[-1]
    return jnp.pad(w, ((0, 0), (0, -n % LANES)))


def _pad_rows(w):
    n = w.shape[0]
    return jnp.pad(w, ((0, -n % LANES), (0, 0)))


def kernel(x, c, ada_w, ada_b, mix_w_in, mix_w_out, sgu_norm_g, sgu_w, sgu_b, rwkv_mu, rwkv_w_rkv, rwkv_w_o, rwkv_w0, rwkv_w1, rwkv_w2, rwkv_a0, rwkv_a1, rwkv_a2, rwkv_v0, rwkv_v1, rwkv_v2, rwkv_g1, rwkv_g2, rwkv_k_k, rwkv_k_a, rwkv_r_k, rwkv_ln_w, rwkv_ln_b, router_w, router_b, moe_w1, moe_w3, moe_w2, final_norm_g):
    depth = ada_w.shape[0]
    row = lambda t: t.reshape(1, -1)
    mods = _ada_call(c, ada_w, ada_b)
    rw_hi = router_w.astype(BF16)
    rw_lo = (router_w - rw_hi.astype(F32)).astype(BF16)
    router_w2 = jnp.concatenate([_pad_cols(rw_hi), _pad_cols(rw_lo)], axis=1)
    router_bc = router_b.reshape(N_EXPERTS, 1)
    final_g = row(final_norm_g)
    v_first = None
    for layer in range(depth):
        i = layer // 2
        mod = mods[2 * layer]
        if layer % 2 == 0:
            proj = _proj_call(x, mod, mix_w_in[i].astype(BF16))
            o_a = _attn_call(proj)
            bias_tile = jnp.repeat(sgu_b[i].T, HEAD_DIM, axis=1)
            x = _sgu_out_call(proj, o_a, x, mod, row(sgu_norm_g[i]), sgu_w[i], bias_tile,
                              mix_w_out[i].astype(BF16))
        else:
            weights = [
                rwkv_mu[i],
                rwkv_w_rkv[i, 0].astype(BF16), rwkv_w_rkv[i, 1].astype(BF16), rwkv_w_rkv[i, 2].astype(BF16),
                row(rwkv_w0[i]), _pad_cols(rwkv_w1[i]).astype(BF16), _pad_rows(rwkv_w2[i]).astype(BF16),
                row(rwkv_a0[i]), _pad_cols(rwkv_a1[i]).astype(BF16), _pad_rows(rwkv_a2[i]).astype(BF16),
                _pad_cols(rwkv_g1[i]).astype(BF16), _pad_rows(rwkv_g2[i]).astype(BF16),
                row(rwkv_k_k[i]), row(rwkv_k_a[i]),
            ]
            vres = None
            if i > 0:
                vres = [row(rwkv_v0[i - 1]), _pad_cols(rwkv_v1[i - 1]).astype(BF16),
                        _pad_rows(rwkv_v2[i - 1]).astype(BF16)]
            r, w_log, k, v, kk, a, g = _rwkv_pre_call(x, mod, weights, v_first, vres)
            if i == 0:
                v_first = v
            y = _wkv_call(r, w_log, k, v, kk, a, row(rwkv_ln_w[i]), row(rwkv_ln_b[i]), row(rwkv_r_k[i]))
            x = _rwkv_post_call(y, g, x, mod, rwkv_w_o[i].astype(BF16))
        x = _moe_layer(x, mods[2 * layer + 1], router_w2, router_bc, moe_w1, moe_w3, moe_w2, final_g, layer,
                       layer == depth - 1)
    return x
```

```python
import functools

import jax
import jax.numpy as jnp
from jax import lax
from jax.experimental import pallas as pl
from jax.experimental.pallas import tpu as pltpu

F32 = jnp.float32
BF16 = jnp.bfloat16

D_MODEL = 1024
HEAD_DIM = 64
WIDTH_A = 512
WIDTH_B = 512
MIX_IN_WIDTH = 2560
ATTN_SPAN = 128
ATTN_DILATIONS = (1, 4, 16)
SGU_CHUNK = 128
N_GROUPS_B = 8
RWKV_HEADS = 16
RWKV_GN_EPS = 64e-5
WKV_CHUNK = 64
N_EXPERTS = 16
N_EXPERT_GROUPS = 4
EXPERTS_PER_GROUP = 4
D_EXPERT = 256
RMS_EPS = 1e-6
LANES = 128
VMEM_LIMIT = 56 * 1024 * 1024

NT_DIMS = (((1,), (1,)), ((), ()))
TN_DIMS = (((0,), (0,)), ((), ()))


def _params(*sem):
    return pltpu.CompilerParams(dimension_semantics=sem, vmem_limit_bytes=VMEM_LIMIT)


def _norm_mod(x, mod_ref):
    ms = jnp.mean(x * x, axis=-1, keepdims=True)
    return (x * lax.rsqrt(ms + RMS_EPS)) * (1.0 + mod_ref[0, 1:2, :]) + mod_ref[0, 0:1, :]


def _gelu(x):
    return 0.5 * x * (1.0 + lax.erf(x * 0.7071067811865476))


def _ada_body(c_ref, w_ref, b_ref, o_ref):
    c = c_ref[...]
    c_act = c * jax.nn.sigmoid(c)
    n = c.shape[0]
    c_hi = c_act.astype(BF16)
    c_lo = (c_act - c_hi.astype(F32)).astype(BF16)
    c2 = jnp.concatenate([c_hi, c_lo], axis=0)
    w = w_ref[0]
    w_hi = w.astype(BF16)
    w_lo = (w - w_hi.astype(F32)).astype(BF16)
    by_hi = jnp.dot(c2, w_hi, preferred_element_type=F32)
    by_lo = jnp.dot(c2, w_lo, preferred_element_type=F32)
    o_ref[0] = (by_hi[:n] + by_hi[n:]) + (by_lo[:n] + by_lo[n:]) + b_ref[0]


def _ada_call(c, ada_w, ada_b):
    n_pair = ada_w.shape[0] * ada_w.shape[1]
    batch = c.shape[0]
    w = ada_w.reshape(n_pair, D_MODEL, 3 * D_MODEL)
    b = ada_b.reshape(n_pair, 1, 3 * D_MODEL)
    out = pl.pallas_call(
        _ada_body,
        grid=(n_pair, 3),
        in_specs=[
            pl.BlockSpec((batch, D_MODEL), lambda p, j: (0, 0)),
            pl.BlockSpec((1, D_MODEL, D_MODEL), lambda p, j: (p, 0, j)),
            pl.BlockSpec((1, 1, D_MODEL), lambda p, j: (p, 0, j)),
        ],
        out_specs=pl.BlockSpec((1, batch, D_MODEL), lambda p, j: (p, 0, j)),
        out_shape=jax.ShapeDtypeStruct((n_pair, batch, 3 * D_MODEL), F32),
        compiler_params=_params("arbitrary", "arbitrary"),
        name="ada_mod",
    )(c, w, b)
    return out.reshape(n_pair, batch, 3, D_MODEL)


def _proj_body(x_ref, mod_ref, w_ref, o_ref):
    h = _norm_mod(x_ref[0], mod_ref)
    o_ref[0] = jnp.dot(h.astype(BF16), w_ref[...], preferred_element_type=F32)


def _proj_call(x, mod, w_in, ts=1024):
    batch, seq, _ = x.shape
    return pl.pallas_call(
        _proj_body,
        grid=(batch, seq // ts),
        in_specs=[
            pl.BlockSpec((1, ts, D_MODEL), lambda b, i: (b, i, 0)),
            pl.BlockSpec((1, 3, D_MODEL), lambda b, i: (b, 0, 0)),
            pl.BlockSpec((D_MODEL, MIX_IN_WIDTH), lambda b, i: (0, 0)),
        ],
        out_specs=pl.BlockSpec((1, ts, MIX_IN_WIDTH), lambda b, i: (b, i, 0)),
        out_shape=jax.ShapeDtypeStruct((batch, seq, MIX_IN_WIDTH), F32),
        compiler_params=_params("arbitrary", "arbitrary"),
        name="mix_in_proj",
    )(x, mod, w_in)


def _attn_body(q_ref, k_ref, v_ref, o_ref, ob, lb):
    span = ATTN_SPAN
    W = LANES
    block_group = 4

    def iota(shape, dim):
        return lax.broadcasted_iota(jnp.int32, shape, dim)

    head0 = iota((span, W), 1) < HEAD_DIM
    mine = (head0, jnp.logical_not(head0))
    row2, col2 = iota((span, 2 * span), 0), iota((span, 2 * span), 1)
    mask_prev_cur = jnp.where(col2 < span, col2 - row2, row2 - (col2 - span)) >= 0
    mask_cur = iota((span, span), 1) <= iota((span, span), 0)

    def rows(start, n_rows, dil):
        return pl.ds(start, n_rows) if dil == 1 else pl.ds(start, n_rows, stride=dil)

    def group(p, dil, q_starts, with_prev):
        blocks = range(len(q_starts))
        heads = range(2)
        qb, kcat, vaug, masks = [], [], [], []
        for j in blocks:
            n_keys = 2 * span if with_prev[j] else span
            k_start = q_starts[j] - span * dil if with_prev[j] else q_starts[j]
            qb.append(q_ref[0, rows(q_starts[j], span, dil), :] * (HEAD_DIM ** -0.5))
            kcat.append(k_ref[0, rows(k_start, n_keys, dil), :].astype(BF16))
            vb = v_ref[0, rows(k_start, n_keys, dil), :].astype(BF16)
            vaug.append(jnp.concatenate([vb, jnp.ones((n_keys, W), BF16)], axis=1))
            masks.append(mask_prev_cur if with_prev[j] else mask_cur)
        qh = [[jnp.where(mine[h], qb[j], 0.0).astype(BF16) for h in heads] for j in blocks]
        s = [[lax.dot_general(qh[j][h], kcat[j], NT_DIMS, preferred_element_type=F32) for h in heads] for j in blocks]
        s = [[jnp.where(masks[j], s[j][h], -jnp.inf) for h in heads] for j in blocks]
        m = [[jnp.max(s[j][h], axis=-1, keepdims=True) for h in heads] for j in blocks]
        pr = [[jnp.exp(s[j][h] - m[j][h]).astype(BF16) for h in heads] for j in blocks]
        ad = [[jnp.dot(pr[j][h], vaug[j], preferred_element_type=F32) for h in heads] for j in blocks]
        for j in blocks:
            den = jnp.where(head0, ad[j][0][:, W:], ad[j][1][:, W:])
            o = jnp.where(head0, ad[j][0][:, :W], ad[j][1][:, :W]) / den
            lse = jnp.where(head0, m[j][0], m[j][1]) + jnp.log(den)
            qi = rows(q_starts[j], span, dil)
            ob[p, qi, :] = o
            lb[p, qi, :] = lse

    seq = q_ref.shape[1]
    for p, dil in enumerate(ATTN_DILATIONS):
        n_blk = seq // dil // span
        blocks = [(r + n * span * dil, n > 0) for r in range(dil) for n in range(n_blk)]
        for g0 in range(0, len(blocks), block_group):
            members = blocks[g0:g0 + block_group]
            group(p, dil, [b[0] for b in members], [b[1] for b in members])

    l0, l1, l2 = lb[0], lb[1], lb[2]
    m = jnp.maximum(jnp.maximum(l0, l1), l2)
    w0, w1, w2 = jnp.exp(l0 - m), jnp.exp(l1 - m), jnp.exp(l2 - m)
    o_ref[0] = (w0 * ob[0] + w1 * ob[1] + w2 * ob[2]) / (w0 + w1 + w2)


def _attn_call(proj):
    batch, seq, _ = proj.shape
    n_pair = WIDTH_A // LANES
    blk = lambda off: pl.BlockSpec((1, seq, LANES), lambda b, h: (b, 0, off + h))
    return pl.pallas_call(
        _attn_body,
        grid=(batch, n_pair),
        in_specs=[blk(0), blk(n_pair), blk(2 * n_pair)],
        out_specs=pl.BlockSpec((1, seq, LANES), lambda b, h: (b, 0, h)),
        out_shape=jax.ShapeDtypeStruct((batch, seq, WIDTH_A), F32),
        scratch_shapes=[
            pltpu.VMEM((3, seq, LANES), F32),
            pltpu.VMEM((3, seq, LANES), F32),
        ],
        compiler_params=_params("arbitrary", "arbitrary"),
        name="dilated_attn",
    )(proj, proj, proj)


def _sgu_out_body(u_ref, z_ref, oa_ref, x_ref, mod_ref, ng_ref, sw_ref, sb_ref, wo_ref, o_ref, mix):
    ts = u_ref.shape[1]
    u = _gelu(u_ref[0])
    z = _gelu(z_ref[0])
    z = z * lax.rsqrt(jnp.mean(z * z, axis=-1, keepdims=True) + RMS_EPS) * ng_ref[...]
    zb = z.astype(BF16)
    row = lax.broadcasted_iota(jnp.int32, (SGU_CHUNK, SGU_CHUNK), 0)
    col = lax.broadcasted_iota(jnp.int32, (SGU_CHUNK, SGU_CHUNK), 1)
    causal = col <= row
    for g in range(N_GROUPS_B):
        w_g = jnp.where(causal, sw_ref[g], 0.0).astype(BF16)
        for cc in range(ts // SGU_CHUNK):
            rs = slice(cc * SGU_CHUNK, (cc + 1) * SGU_CHUNK)
            ls = slice(g * HEAD_DIM, (g + 1) * HEAD_DIM)
            mix[rs, ls] = jnp.dot(w_g, zb[rs, ls], preferred_element_type=F32)
    bias = jnp.concatenate([sb_ref[...]] * (ts // SGU_CHUNK), axis=0)
    o_b = u * (mix[...] + bias)
    y = jnp.dot(oa_ref[0].astype(BF16), wo_ref[0:WIDTH_A, :], preferred_element_type=F32)
    y = y + jnp.dot(o_b.astype(BF16), wo_ref[WIDTH_A:, :], preferred_element_type=F32)
    o_ref[0] = x_ref[0] + mod_ref[0, 2:3, :] * y


def _sgu_out_call(proj, o_a, x, mod, norm_g, sgu_w, sgu_bias_tile, w_out, ts=1024):
    batch, seq, _ = x.shape
    u_blk = 3 * WIDTH_A // WIDTH_B
    full = lambda shape: pl.BlockSpec(shape, lambda b, i: (0,) * len(shape))
    return pl.pallas_call(
        _sgu_out_body,
        grid=(batch, seq // ts),
        in_specs=[
            pl.BlockSpec((1, ts, WIDTH_B), lambda b, i: (b, i, u_blk)),
            pl.BlockSpec((1, ts, WIDTH_B), lambda b, i: (b, i, u_blk + 1)),
            pl.BlockSpec((1, ts, WIDTH_A), lambda b, i: (b, i, 0)),
            pl.BlockSpec((1, ts, D_MODEL), lambda b, i: (b, i, 0)),
            pl.BlockSpec((1, 3, D_MODEL), lambda b, i: (b, 0, 0)),
            full((1, WIDTH_B)),
            full((N_GROUPS_B, SGU_CHUNK, SGU_CHUNK)),
            full((SGU_CHUNK, WIDTH_B)),
            full((WIDTH_A + WIDTH_B, D_MODEL)),
        ],
        out_specs=pl.BlockSpec((1, ts, D_MODEL), lambda b, i: (b, i, 0)),
        out_shape=jax.ShapeDtypeStruct(x.shape, F32),
        scratch_shapes=[pltpu.VMEM((ts, WIDTH_B), F32)],
        compiler_params=_params("arbitrary", "arbitrary"),
        name="sgu_out_proj",
    )(proj, proj, o_a, x, mod, norm_g, sgu_w, sgu_bias_tile, w_out)


def _rwkv_pre_body(has_vres, *refs):
    if has_vres:
        (x_ref, mod_ref, mu_ref, wr_ref, wk_ref, wv_ref, w0_ref, w1_ref, w2_ref, a0_ref, a1_ref, a2_ref,
         g1_ref, g2_ref, kk_ref, ka_ref, vf_ref, v0_ref, v1_ref, v2_ref,
         r_out, w_out, k_out, v_out, kk_out, a_out, g_out, carry) = refs
    else:
        (x_ref, mod_ref, mu_ref, wr_ref, wk_ref, wv_ref, w0_ref, w1_ref, w2_ref, a0_ref, a1_ref, a2_ref,
         g1_ref, g2_ref, kk_ref, ka_ref,
         r_out, w_out, k_out, v_out, kk_out, a_out, g_out, carry) = refs

    @pl.when(pl.program_id(1) == 0)
    def _():
        carry[...] = jnp.zeros_like(carry)

    h = _norm_mod(x_ref[0], mod_ref)
    ts = h.shape[0]
    first = lax.broadcasted_iota(jnp.int32, h.shape, 0) == 0
    h_prev = jnp.where(first, carry[0:1, :], pltpu.roll(h, 1, 0))
    carry[0:1, :] = h[ts - 1:ts, :]
    xx = h_prev - h

    def mixed(i):
        return (h + xx * mu_ref[i:i + 1, :]).astype(BF16)

    def mm(a, w_ref):
        return jnp.dot(a, w_ref[...], preferred_element_type=F32)

    xr, xw, xk, xv, xa, xg = [mixed(i) for i in range(6)]
    w_low = mm(xw, w1_ref)
    a_low = mm(xa, a1_ref)
    g_low = mm(xg, g1_ref)
    v_low = mm(xv, v1_ref) if has_vres else None
    r = mm(xr, wr_ref)
    k = mm(xk, wk_ref)
    v = mm(xv, wv_ref)
    z = w0_ref[...] + mm(jnp.tanh(w_low).astype(BF16), w2_ref)
    w_log = -(jnp.maximum(-z, 0.0) + jnp.log(1.0 + jnp.exp(-jnp.abs(z)))) - 0.5
    if has_vres:
        mix_v = jax.nn.sigmoid(v0_ref[...] + mm(v_low.astype(BF16), v2_ref))
        v = v + (vf_ref[0].astype(F32) - v) * mix_v
    a = jax.nn.sigmoid(a0_ref[...] + mm(a_low.astype(BF16), a2_ref))
    g = mm(jax.nn.sigmoid(g_low).astype(BF16), g2_ref)
    r_out[0] = r.astype(BF16)
    w_out[0] = w_log
    kk_out[0] = (k * kk_ref[...]).astype(BF16)
    k_out[0] = (k * (1.0 + (a - 1.0) * ka_ref[...])).astype(BF16)
    v_out[0] = v.astype(BF16)
    a_out[0] = a.astype(BF16)
    g_out[0] = g.astype(BF16)


def _rwkv_pre_call(x, mod, weights, v_first, vres, ts=512):
    batch, seq, _ = x.shape
    tok = pl.BlockSpec((1, ts, D_MODEL), lambda b, i: (b, i, 0))
    full = lambda a: pl.BlockSpec(a.shape, lambda b, i: (0,) * a.ndim, pipeline_mode=pl.Buffered(1))
    ins = [x, mod] + list(weights)
    specs = [tok, pl.BlockSpec((1, 3, D_MODEL), lambda b, i: (b, 0, 0))] + [full(a) for a in weights]
    if vres is not None:
        ins += [v_first] + list(vres)
        specs += [tok] + [full(a) for a in vres]
    out_dtypes = [BF16, F32, BF16, BF16, BF16, BF16, BF16]
    return pl.pallas_call(
        functools.partial(_rwkv_pre_body, vres is not None),
        grid=(batch, seq // ts),
        in_specs=specs,
        out_specs=[tok] * 7,
        out_shape=[jax.ShapeDtypeStruct(x.shape, dt) for dt in out_dtypes],
        scratch_shapes=[pltpu.VMEM((8, D_MODEL), F32)],
        compiler_params=_params("arbitrary", "arbitrary"),
        name="rwkv_pre",
    )(*ins)


def _wkv_body(r_ref, w_ref, k_ref, v_ref, kk_ref, a_ref, lnw_ref, lnb_ref, rk_ref, o_ref, state):
    C = WKV_CHUNK
    W = LANES
    n_chunk = r_ref.shape[1] // C
    n_pair = r_ref.shape[2] // W

    @pl.when(pl.program_id(2) == 0)
    def _():
        state[...] = jnp.zeros_like(state)

    def iota(shape, dim):
        return lax.broadcasted_iota(jnp.int32, shape, dim)

    def stack_heads(x):
        own = iota(x.shape, 1) % W < HEAD_DIM
        return jnp.concatenate([jnp.where(own, x, 0.0), jnp.where(own, 0.0, x)], axis=0).astype(BF16)

    def block_diag(x):
        left = iota(x.shape, 1) < C
        return jnp.concatenate([jnp.where(left, x, 0.0), jnp.where(left, 0.0, x)], axis=0).astype(BF16)

    row4, col4 = iota((C, 4 * C), 0), iota((C, 4 * C), 1) % C
    strict4 = col4 < row4
    lower4 = col4 <= row4
    tri = jnp.where(iota((C, C), 1) <= iota((C, C), 0), 1.0, 0.0).astype(BF16)
    eye2 = jnp.where(iota((C, 2 * C), 1) % C == iota((C, 2 * C), 0), 1.0, 0.0).astype(F32)
    rw, cw = iota((W, W), 0), iota((W, W), 1)
    same_head = (rw < HEAD_DIM) == (cw < HEAD_DIM)
    ones_bd = jnp.where(same_head, 1.0, 0.0).astype(BF16)
    diag_w = cw == rw
    zeros_cw = jnp.zeros((C, W), F32)

    def bdot(a, b):
        return jnp.dot(a.astype(BF16), b.astype(BF16), preferred_element_type=F32)

    chunks = range(n_chunk)
    pairs = range(n_pair)
    units = [(c, p) for c in chunks for p in pairs]
    rows = lambda t, c: t[c * C:(c + 1) * C]
    blk = lambda t, u: t[u[0] * C:(u[0] + 1) * C, u[1] * W:(u[1] + 1) * W]
    lanes = lambda t, p: t[:, p * W:(p + 1) * W]
    per_chunk = lambda vals: jnp.concatenate([jnp.broadcast_to(t, (C, n_pair * W)) for t in vals], axis=0)
    head_sums = lambda t: jnp.concatenate([bdot(lanes(t, p), ones_bd) for p in pairs], axis=1)

    r = r_ref[0].astype(F32)
    k = k_ref[0].astype(F32)
    v = v_ref[0].astype(F32)
    kk = kk_ref[0].astype(F32)
    kk = kk / jnp.maximum(jnp.sqrt(head_sums(kk * kk)), 1e-12)
    b = kk * a_ref[0].astype(F32)
    bonus = head_sums(r * k * rk_ref[...]) * v
    log_w = -jnp.exp(w_ref[0])
    hi = log_w.astype(BF16)
    lo = (log_w - hi.astype(F32)).astype(BF16)
    hilo = jnp.concatenate([hi, lo], axis=1)
    cum2 = [jnp.dot(tri, rows(hilo, c), preferred_element_type=F32) for c in chunks]
    cum = jnp.concatenate([c2[:, :n_pair * W] + c2[:, n_pair * W:] for c2 in cum2], axis=0)
    c_mid = [cum[c * C + C // 2 - 1:c * C + C // 2, :] for c in chunks]
    c_end = [cum[c * C + C - 1:(c + 1) * C, :] for c in chunks]
    g_last = [jnp.exp(t) for t in c_end]
    from_mid = jnp.exp(per_chunk(c_mid) - cum)
    to_mid = per_chunk([jnp.exp(-t) for t in c_mid])
    a_true = -kk * jnp.exp(cum - log_w)
    r_true = r * jnp.exp(cum)
    a_mid = a_true * to_mid
    r_mid = r_true * to_mid
    b_mid = b * from_mid
    k_mid = k * from_mid
    ar_mid = [jnp.concatenate([blk(a_mid, u), blk(r_mid, u)], axis=0).astype(BF16) for u in units]
    gram = [lax.dot_general(ar_mid[i], jnp.concatenate([stack_heads(blk(b_mid, u)), stack_heads(blk(k_mid, u))],
                                                        axis=0), NT_DIMS, preferred_element_type=F32)
            for i, u in enumerate(units)]
    top = [jnp.where(strict4, g[:C], 0.0) for g in gram]
    bot = [jnp.where(lower4, g[C:], 0.0).astype(BF16) for g in gram]
    n1 = [t[:, :2 * C] for t in top]
    inv = [eye2 + t for t in n1]
    pw = [bdot(t, block_diag(t)) for t in n1]
    levels = C.bit_length() - 2
    for lvl in range(levels):
        if lvl < levels - 1:
            st = [bdot(jnp.concatenate([pw[i], inv[i]], axis=0), block_diag(pw[i])) for i in range(len(units))]
            pw = [t[:C] for t in st]
            inv = [inv[i] + st[i][C:] for i in range(len(units))]
        else:
            inv = [inv[i] + bdot(inv[i], block_diag(pw[i])) for i in range(len(units))]
    x = [jnp.dot(top[i][:, 2 * C:].astype(BF16), stack_heads(blk(v, u)), preferred_element_type=F32)
         for i, u in enumerate(units)]
    pp = [jnp.dot(inv[i].astype(BF16), stack_heads(jnp.concatenate([blk(a_true, u), x[i]], axis=1)),
                  preferred_element_type=F32) for i, u in enumerate(units)]
    zv = [jnp.concatenate([zeros_cw, blk(v, u)], axis=1) for u in units]
    qq = [jnp.dot(bot[i], jnp.concatenate([stack_heads(pp[i]), stack_heads(zv[i])], axis=0),
                  preferred_element_type=F32) for i in range(len(units))]
    end_from_mid = [jnp.exp(c_end[c] - c_mid[c]) for c in chunks]
    bk_hat = [jnp.concatenate([blk(b_mid, u), blk(k_mid, u)], axis=0) * lanes(end_from_mid[u[0]], u[1])
              for u in units]
    mm = [lax.dot_general(bk_hat[i].astype(BF16), jnp.concatenate([pp[i], zv[i]], axis=0).astype(BF16), TN_DIMS,
                          preferred_element_type=F32) for i in range(len(units))]
    q1 = [blk(r_true, u) + qq[i][:, :W] for i, u in enumerate(units)]
    q2 = [t[:, W:] for t in qq]
    m1 = [(jnp.where(same_head, mm[i][:, :W], 0.0)
           + jnp.where(diag_w, jnp.broadcast_to(lanes(g_last[u[0]], u[1]), (W, W)), 0.0)).astype(BF16)
          for i, u in enumerate(units)]
    m2 = [jnp.where(same_head, t[:, W:], 0.0) for t in mm]
    h_cur = [state[p] for p in pairs]
    h_in = []
    for i, (c, p) in enumerate(units):
        h_in.append(h_cur[p])
        h_cur[p] = jnp.dot(m1[i], h_cur[p].astype(BF16), preferred_element_type=F32) + m2[i]
    for p in pairs:
        state[p] = h_cur[p]
    y_blk = [bdot(q1[i], h_in[i]) + q2[i] for i in range(len(units))]
    y = jnp.concatenate([jnp.concatenate([y_blk[c * n_pair + p] for p in pairs], axis=1) for c in chunks], axis=0)
    mean = head_sums(y) * (1.0 / HEAD_DIM)
    yc = y - mean
    var = head_sums(yc * yc) * (1.0 / HEAD_DIM)
    yn = yc * lax.rsqrt(var + RWKV_GN_EPS) * lnw_ref[...] + lnb_ref[...]
    o_ref[0] = (yn + bonus).astype(BF16)


def _wkv_call(r, w_log, k, v, kk, a, ln_w, ln_b, r_k, tc=256, pairs_per_step=8):
    batch, seq, _ = r.shape
    width = pairs_per_step * LANES
    tok = pl.BlockSpec((1, tc, width), lambda b, h, t: (b, t, h))
    vec = pl.BlockSpec((1, width), lambda b, h, t: (0, h))
    return pl.pallas_call(
        _wkv_body,
        grid=(batch, D_MODEL // width, seq // tc),
        in_specs=[tok] * 6 + [vec] * 3,
        out_specs=tok,
        out_shape=jax.ShapeDtypeStruct(r.shape, BF16),
        scratch_shapes=[pltpu.VMEM((pairs_per_step, LANES, LANES), F32)],
        compiler_params=_params("arbitrary", "arbitrary", "arbitrary"),
        name="wkv7_chunked",
    )(r, w_log, k, v, kk, a, ln_w, ln_b, r_k)


def _rwkv_post_body(y_ref, g_ref, x_ref, mod_ref, wo_ref, o_ref):
    yg = y_ref[0] * g_ref[0]
    o_ref[0] = x_ref[0] + mod_ref[0, 2:3, :] * jnp.dot(yg, wo_ref[...], preferred_element_type=F32)


def _rwkv_post_call(y, g, x, mod, w_o, ts=1024):
    batch, seq, _ = x.shape
    tok = pl.BlockSpec((1, ts, D_MODEL), lambda b, i: (b, i, 0))
    return pl.pallas_call(
        _rwkv_post_body,
        grid=(batch, seq // ts),
        in_specs=[tok, tok, tok, pl.BlockSpec((1, 3, D_MODEL), lambda b, i: (b, 0, 0)),
                  pl.BlockSpec((D_MODEL, D_MODEL), lambda b, i: (0, 0))],
        out_specs=tok,
        out_shape=jax.ShapeDtypeStruct(x.shape, F32),
        compiler_params=_params("arbitrary", "arbitrary"),
        name="rwkv_out_proj",
    )(y, g, x, mod, w_o)


MOE_TILE = 512
MOE_BLOCK = 64
MOE_SLOTS = MOE_TILE + N_EXPERT_GROUPS * MOE_BLOCK
MOE_BLOCKS_PER_TILE = MOE_SLOTS // MOE_BLOCK
MOE_STEP_BLOCKS = 16


def _route(logits_t, rb):
    n_g, epg = N_EXPERT_GROUPS, EXPERTS_PER_GROUP
    s = jax.nn.sigmoid(logits_t)
    sel = s + rb
    s_rows = [s[i:i + 1, :] for i in range(N_EXPERTS)]
    sel_rows = [sel[i:i + 1, :] for i in range(N_EXPERTS)]
    scores = []
    for g in range(n_g):
        members = sel_rows[g * epg:(g + 1) * epg]
        best = None
        for i in range(epg):
            for j in range(i + 1, epg):
                pair = members[i] + members[j]
                best = pair if best is None else jnp.maximum(best, pair)
        scores.append(best)
    g_idx = jnp.zeros_like(scores[0], dtype=jnp.int32)
    top = scores[0]
    for g in range(1, n_g):
        better = scores[g] > top
        g_idx = jnp.where(better, g, g_idx)
        top = jnp.where(better, scores[g], top)

    def pick(rows_, j):
        out = rows_[j]
        for g in range(1, n_g):
            out = jnp.where(g_idx == g, rows_[g * epg + j], out)
        return out

    in_sel = [pick(sel_rows, j) for j in range(epg)]
    in_s = [pick(s_rows, j) for j in range(epg)]
    chosen = []
    for j in range(epg):
        rank = jnp.zeros_like(g_idx)
        for i in range(epg):
            if i == j:
                continue
            ahead = (in_sel[i] >= in_sel[j]) if i < j else (in_sel[i] > in_sel[j])
            rank = rank + jnp.where(ahead, 1, 0)
        chosen.append(rank < 2)
    den = sum(jnp.where(chosen[j], in_s[j], 0.0) for j in range(epg))
    gate_rows = []
    for ex in range(N_EXPERTS):
        g, j = divmod(ex, epg)
        on = jnp.logical_and(chosen[j], g_idx == g)
        gate_rows.append(jnp.where(on, in_s[j] / den, 0.0))
    return g_idx, gate_rows


def _slot_one_hot(pos_row):
    slot = lax.broadcasted_iota(jnp.int32, (MOE_SLOTS, pos_row.shape[1]), 0)
    return jnp.where(slot == pos_row, 1.0, 0.0).astype(BF16)


def _moe_sort_body(x_ref, mod_ref, rw_ref, rb_ref, hs_ref, gs_ref, pos_ref, cnt_ref):
    tm = MOE_TILE
    n_g = N_EXPERT_GROUPS
    tiles = range(x_ref.shape[1] // tm)
    h = [_norm_mod(x_ref[0, i * tm:(i + 1) * tm, :], mod_ref) for i in tiles]
    hi = [t.astype(BF16) for t in h]
    lo = [(h[i] - hi[i].astype(F32)).astype(BF16) for i in tiles]
    parts = [jnp.dot(jnp.concatenate([hi[i], lo[i]], axis=0), rw_ref[...], preferred_element_type=F32)
             for i in tiles]
    logits = [(p[:tm, :LANES] + p[:tm, LANES:]) + (p[tm:, :LANES] + p[tm:, LANES:]) for p in parts]
    routed = [_route(t.T[:N_EXPERTS], rb_ref[...]) for t in logits]
    member = [[jnp.where(routed[i][0] == g, 1.0, 0.0) for g in range(n_g)] for i in tiles]
    earlier = jnp.where(lax.broadcasted_iota(jnp.int32, (tm, tm), 0) < lax.broadcasted_iota(jnp.int32, (tm, tm), 1),
                        1.0, 0.0).astype(BF16)
    rank = [jnp.dot(jnp.concatenate(member[i] + [jnp.zeros((8 - n_g, tm), F32)], axis=0).astype(BF16), earlier,
                    preferred_element_type=F32) for i in tiles]
    lane = lax.broadcasted_iota(jnp.int32, (1, LANES), 1)
    one_hot = []
    for i in tiles:
        pos = jnp.zeros((1, tm), F32)
        offset = jnp.zeros((1, 1), F32)
        counts = jnp.zeros((1, LANES), F32)
        for g in range(n_g):
            n_members = jnp.sum(member[i][g], axis=1, keepdims=True)
            n_blocks = jnp.floor((n_members + (MOE_BLOCK - 1)) * (1.0 / MOE_BLOCK))
            pos = pos + member[i][g] * (offset + rank[i][g:g + 1, :])
            counts = counts + jnp.where(lane == g, n_blocks, 0.0)
            offset = offset + n_blocks * MOE_BLOCK
        pos_i = pos.astype(jnp.int32)
        pos_ref[i] = pos_i
        cnt_ref[i] = counts.astype(jnp.int32)
        one_hot.append(_slot_one_hot(pos_i))
    pad = jnp.zeros((LANES - N_EXPERTS, tm), F32)
    gate_t = [jnp.concatenate(routed[i][1] + [pad], axis=0).T for i in tiles]
    g_hi = [t.astype(BF16) for t in gate_t]
    g_lo = [(gate_t[i] - g_hi[i].astype(F32)).astype(BF16) for i in tiles]
    hs = [jnp.dot(one_hot[i], hi[i], preferred_element_type=F32) for i in tiles]
    gs = [jnp.dot(one_hot[i], jnp.concatenate([g_hi[i], g_lo[i]], axis=1), preferred_element_type=F32)
          for i in tiles]
    for i in tiles:
        hs_ref[i * MOE_SLOTS:(i + 1) * MOE_SLOTS, :] = hs[i].astype(BF16)
        gs_ref[i * MOE_SLOTS:(i + 1) * MOE_SLOTS, :] = gs[i].astype(BF16)


def _moe_sort_call(x, mod, router_w2, router_b, tiles_per_step=4):
    batch, seq, _ = x.shape
    rows = tiles_per_step * MOE_TILE
    per_seq = seq // rows
    n_tiles = batch * seq // MOE_TILE
    step = lambda b, i: b * per_seq + i
    return pl.pallas_call(
        _moe_sort_body,
        grid=(batch, per_seq),
        in_specs=[
            pl.BlockSpec((1, rows, D_MODEL), lambda b, i: (b, i, 0)),
            pl.BlockSpec((1, 3, D_MODEL), lambda b, i: (b, 0, 0)),
            pl.BlockSpec((D_MODEL, 2 * LANES), lambda b, i: (0, 0)),
            pl.BlockSpec((N_EXPERTS, 1), lambda b, i: (0, 0)),
        ],
        out_specs=[
            pl.BlockSpec((tiles_per_step * MOE_SLOTS, D_MODEL), lambda b, i: (step(b, i), 0)),
            pl.BlockSpec((tiles_per_step * MOE_SLOTS, 2 * LANES), lambda b, i: (step(b, i), 0)),
            pl.BlockSpec((tiles_per_step, 1, MOE_TILE), lambda b, i: (step(b, i), 0, 0)),
            pl.BlockSpec((tiles_per_step, 1, LANES), lambda b, i: (step(b, i), 0, 0)),
        ],
        out_shape=[
            jax.ShapeDtypeStruct((n_tiles * MOE_SLOTS, D_MODEL), BF16),
            jax.ShapeDtypeStruct((n_tiles * MOE_SLOTS, 2 * LANES), BF16),
            jax.ShapeDtypeStruct((n_tiles, 1, MOE_TILE), jnp.int32),
            jax.ShapeDtypeStruct((n_tiles, 1, LANES), jnp.int32),
        ],
        compiler_params=_params("arbitrary", "arbitrary"),
        name="moe_sort",
    )(x, mod, router_w2, router_b)


def _moe_tables_body(cnt_ref, sg_ref, sv_ref, sb_ref, bp_ref):
    n_tiles = cnt_ref.shape[0]
    n_g, r, bpt = N_EXPERT_GROUPS, MOE_STEP_BLOCKS, MOE_BLOCKS_PER_TILE
    n_steps = sg_ref.shape[0]

    def fill(ref, n, value):
        def body(i, carry):
            ref[i] = value
            return carry
        lax.fori_loop(0, n, body, 0)

    fill(sb_ref, n_steps * r, 0)
    fill(bp_ref, n_tiles * bpt, 0)
    fill(sg_ref, n_steps, n_g - 1)
    fill(sv_ref, n_steps, 0)
    slot = jnp.int32(0)
    for g in range(n_g):
        def tile_body(i, k, g=g):
            first = i * bpt
            for g_before in range(g):
                first = first + cnt_ref[i, g_before]

            def block_body(j, k):
                sb_ref[k] = first + j
                bp_ref[first + j] = k
                return k + 1
            return lax.fori_loop(0, cnt_ref[i, g], block_body, k)

        slot_end = lax.fori_loop(0, n_tiles, tile_body, slot)
        step_first = slot // r
        step_stop = (slot_end + r - 1) // r

        def step_body(s, carry, g=g, slot_end=slot_end):
            sg_ref[s] = g
            sv_ref[s] = jnp.minimum(slot_end - s * r, r)
            return carry
        lax.fori_loop(step_first, step_stop, step_body, 0)
        slot = step_stop * r


def _moe_step_tables(counts):
    n_tiles = counts.shape[0]
    n_blocks = n_tiles * MOE_BLOCKS_PER_TILE
    n_steps = n_blocks // MOE_STEP_BLOCKS + N_EXPERT_GROUPS
    smem = pl.BlockSpec(memory_space=pltpu.SMEM)
    return pl.pallas_call(
        _moe_tables_body,
        in_specs=[smem],
        out_specs=[smem] * 4,
        out_shape=[jax.ShapeDtypeStruct((n_steps,), jnp.int32), jax.ShapeDtypeStruct((n_steps,), jnp.int32),
                   jax.ShapeDtypeStruct((n_steps * MOE_STEP_BLOCKS,), jnp.int32),
                   jax.ShapeDtypeStruct((n_blocks,), jnp.int32)],
        name="moe_tables",
    )(counts)


def _moe_expert_body(sg_ref, sv_ref, sb_ref, *refs):
    r = MOE_STEP_BLOCKS
    h_refs, g_refs = refs[:r], refs[r:2 * r]
    w1_ref, w3_ref, w2_ref, o_ref, w13_scr, w2_scr, act_scr = refs[2 * r:]
    step = pl.program_id(0)
    epg = EXPERTS_PER_GROUP
    rows_total = r * MOE_BLOCK
    group = sg_ref[step]
    new_group = jnp.logical_or(step == 0, group != sg_ref[jnp.maximum(step - 1, 0)])

    @pl.when(sv_ref[step] == 0)
    def _():
        o_ref[...] = jnp.zeros_like(o_ref)

    @pl.when(jnp.logical_and(sv_ref[step] > 0, new_group))
    def _():
        w13_scr[:, :, :D_EXPERT] = w1_ref[0].astype(BF16)
        w13_scr[:, :, D_EXPERT:] = w3_ref[0].astype(BF16)
        w2_scr[...] = w2_ref[0].reshape(epg * D_EXPERT, D_MODEL).astype(BF16)

    @pl.when(sv_ref[step] > 0)
    def _():
        h = jnp.concatenate([ref[...] for ref in h_refs], axis=0)
        gates = jnp.concatenate([ref[...] for ref in g_refs], axis=0)
        real = lax.broadcasted_iota(jnp.int32, (rows_total, LANES), 0) < sv_ref[step] * MOE_BLOCK
        expert_row = lax.broadcasted_iota(jnp.int32, (2 * LANES, epg * LANES), 0) % LANES
        lane_block = lax.broadcasted_iota(jnp.int32, (2 * LANES, epg * LANES), 1) // LANES
        pick_rows = jnp.where(expert_row == group * epg + lane_block, 1.0, 0.0).astype(BF16)
        gate_all = jnp.dot(gates, pick_rows, preferred_element_type=F32)
        for j in range(epg):
            gate_b = jnp.where(real, gate_all[:, j * LANES:(j + 1) * LANES], 0.0)
            hid = jnp.dot(h, w13_scr[j], preferred_element_type=F32)
            h1 = hid[:, :D_EXPERT]
            act = h1 * jax.nn.sigmoid(h1) * hid[:, D_EXPERT:] * jnp.concatenate([gate_b, gate_b], axis=1)
            act_scr[:, j * D_EXPERT:(j + 1) * D_EXPERT] = act.astype(BF16)
        o_ref[...] = jnp.dot(act_scr[...], w2_scr[...], preferred_element_type=F32).astype(BF16)


def _moe_expert_call(hs, gs, w1, w3, w2, step_group, step_valid, step_blocks, layer):
    r = MOE_STEP_BLOCKS
    n_steps = step_group.shape[0]
    epg = EXPERTS_PER_GROUP
    rows_total = r * MOE_BLOCK

    def blk(width, j):
        return pl.BlockSpec((MOE_BLOCK, width), lambda s, sg, sv, sb, j=j: (sb[s * r + j], 0))

    def group_weights(shape):
        return pl.BlockSpec((1, epg) + shape, lambda s, sg, sv, sb: (layer, sg[s], 0, 0))

    grid_spec = pltpu.PrefetchScalarGridSpec(
        num_scalar_prefetch=3,
        grid=(n_steps,),
        in_specs=[blk(D_MODEL, j) for j in range(r)] + [blk(2 * LANES, j) for j in range(r)] + [
            group_weights((D_MODEL, D_EXPERT)), group_weights((D_MODEL, D_EXPERT)),
            group_weights((D_EXPERT, D_MODEL))],
        out_specs=pl.BlockSpec((rows_total, D_MODEL), lambda s, sg, sv, sb: (s, 0)),
        scratch_shapes=[
            pltpu.VMEM((epg, D_MODEL, 2 * D_EXPERT), BF16),
            pltpu.VMEM((epg * D_EXPERT, D_MODEL), BF16),
            pltpu.VMEM((rows_total, epg * D_EXPERT), BF16),
        ],
    )
    return pl.pallas_call(
        _moe_expert_body,
        grid_spec=grid_spec,
        out_shape=jax.ShapeDtypeStruct((n_steps * rows_total, D_MODEL), BF16),
        compiler_params=_params("arbitrary"),
        name="moe_experts",
    )(step_group, step_valid, step_blocks, *([hs] * r), *([gs] * r), w1, w3, w2)


def _moe_unsort_body(is_last, bp_ref, *refs):
    bpt = MOE_BLOCKS_PER_TILE
    pos_ref, x_ref, mod_ref, fg_ref, o_ref = refs[-5:]
    y_refs = refs[:-5]
    tm = MOE_TILE
    tiles = range(len(y_refs) // bpt)
    slot = lax.broadcasted_iota(jnp.int32, (tm, MOE_SLOTS), 1)
    one_hot = []
    for i in tiles:
        pos_col = jnp.broadcast_to(pos_ref[i].astype(F32), (LANES, tm)).T.astype(jnp.int32)
        pos_col = jnp.concatenate([pos_col] * (MOE_SLOTS // LANES), axis=1)
        one_hot.append(jnp.where(slot == pos_col, 1.0, 0.0).astype(BF16))
    ys = [jnp.concatenate([ref[...] for ref in y_refs[i * bpt:(i + 1) * bpt]], axis=0) for i in tiles]
    y = [jnp.dot(one_hot[i], ys[i], preferred_element_type=F32) for i in tiles]
    for i in tiles:
        out = x_ref[0, i * tm:(i + 1) * tm, :] + mod_ref[0, 2:3, :] * y[i]
        if is_last:
            out = out * lax.rsqrt(jnp.mean(out * out, axis=-1, keepdims=True) + RMS_EPS) * fg_ref[...]
        o_ref[0, i * tm:(i + 1) * tm, :] = out


def _moe_unsort_call(ys, pos, x, mod, final_g, block_pos, is_last, tiles_per_step=4):
    batch, seq, _ = x.shape
    rows = tiles_per_step * MOE_TILE
    per_seq = seq // rows
    bpt = MOE_BLOCKS_PER_TILE
    n_in = tiles_per_step * bpt

    def blk(j):
        return pl.BlockSpec((MOE_BLOCK, D_MODEL), lambda b, i, bp, j=j: (bp[(b * per_seq + i) * n_in + j], 0))

    grid_spec = pltpu.PrefetchScalarGridSpec(
        num_scalar_prefetch=1,
        grid=(batch, per_seq),
        in_specs=[blk(j) for j in range(n_in)] + [
            pl.BlockSpec((tiles_per_step, 1, MOE_TILE), lambda b, i, bp: (b * per_seq + i, 0, 0)),
            pl.BlockSpec((1, rows, D_MODEL), lambda b, i, bp: (b, i, 0)),
            pl.BlockSpec((1, 3, D_MODEL), lambda b, i, bp: (b, 0, 0)),
            pl.BlockSpec((1, D_MODEL), lambda b, i, bp: (0, 0)),
        ],
        out_specs=pl.BlockSpec((1, rows, D_MODEL), lambda b, i, bp: (b, i, 0)),
    )
    return pl.pallas_call(
        functools.partial(_moe_unsort_body, is_last),
        grid_spec=grid_spec,
        out_shape=jax.ShapeDtypeStruct(x.shape, F32),
        compiler_params=_params("arbitrary", "arbitrary"),
        name="moe_unsort",
    )(block_pos, *([ys] * n_in), pos, x, mod, final_g)


def _moe_layer(x, mod, router_w2, router_b, w1, w3, w2, final_g, layer, is_last):
    hs, gs, pos, cnt = _moe_sort_call(x, mod, router_w2, router_b)
    step_group, step_valid, step_blocks, block_pos = _moe_step_tables(cnt[:, 0, :])
    ys = _moe_expert_call(hs, gs, w1, w3, w2, step_group, step_valid, step_blocks, layer)
    return _moe_unsort_call(ys, pos, x, mod, final_g, block_pos, is_last)


def _pad_cols(w):
    n = w.shape[-1]
    return jnp.pad(w, ((0, 0), (0, -n % LANES)))


def _pad_rows(w):
    n = w.shape[0]
    return jnp.pad(w, ((0, -n % LANES), (0, 0)))


def kernel(x, c, ada_w, ada_b, mix_w_in, mix_w_out, sgu_norm_g, sgu_w, sgu_b, rwkv_mu, rwkv_w_rkv, rwkv_w_o, rwkv_w0, rwkv_w1, rwkv_w2, rwkv_a0, rwkv_a1, rwkv_a2, rwkv_v0, rwkv_v1, rwkv_v2, rwkv_g1, rwkv_g2, rwkv_k_k, rwkv_k_a, rwkv_r_k, rwkv_ln_w, rwkv_ln_b, router_w, router_b, moe_w1, moe_w3, moe_w2, final_norm_g):
    depth = ada_w.shape[0]
    row = lambda t: t.reshape(1, -1)
    mods = _ada_call(c, ada_w, ada_b)
    rw_hi = router_w.astype(BF16)
    rw_lo = (router_w - rw_hi.astype(F32)).astype(BF16)
    router_w2 = jnp.concatenate([_pad_cols(rw_hi), _pad_cols(rw_lo)], axis=1)
    router_bc = router_b.reshape(N_EXPERTS, 1)
    final_g = row(final_norm_g)
    v_first = None
    for layer in range(depth):
        i = layer // 2
        mod = mods[2 * layer]
        if layer % 2 == 0:
            proj = _proj_call(x, mod, mix_w_in[i].astype(BF16))
            o_a = _attn_call(proj)
            bias_tile = jnp.repeat(sgu_b[i].T, HEAD_DIM, axis=1)
            x = _sgu_out_call(proj, o_a, x, mod, row(sgu_norm_g[i]), sgu_w[i], bias_tile,
                              mix_w_out[i].astype(BF16))
        else:
            weights = [
                rwkv_mu[i],
                rwkv_w_rkv[i, 0].astype(BF16), rwkv_w_rkv[i, 1].astype(BF16), rwkv_w_rkv[i, 2].astype(BF16),
                row(rwkv_w0[i]), _pad_cols(rwkv_w1[i]).astype(BF16), _pad_rows(rwkv_w2[i]).astype(BF16),
                row(rwkv_a0[i]), _pad_cols(rwkv_a1[i]).astype(BF16), _pad_rows(rwkv_a2[i]).astype(BF16),
                _pad_cols(rwkv_g1[i]).astype(BF16), _pad_rows(rwkv_g2[i]).astype(BF16),
                row(rwkv_k_k[i]), row(rwkv_k_a[i]),
            ]
            vres = None
            if i > 0:
                vres = [row(rwkv_v0[i - 1]), _pad_cols(rwkv_v1[i - 1]).astype(BF16),
                        _pad_rows(rwkv_v2[i - 1]).astype(BF16)]
            r, w_log, k, v, kk, a, g = _rwkv_pre_call(x, mod, weights, v_first, vres)
            if i == 0:
                v_first = v
            y = _wkv_call(r, w_log, k, v, kk, a, row(rwkv_ln_w[i]), row(rwkv_ln_b[i]), row(rwkv_r_k[i]))
            x = _rwkv_post_call(y, g, x, mod, rwkv_w_o[i].astype(BF16))
        x = _moe_layer(x, mods[2 * layer + 1], router_w2, router_bc, moe_w1, moe_w3, moe_w2, final_g, layer,
                       layer == depth - 1)
    return x
```

```python
import functools

import jax
import jax.numpy as jnp
from jax import lax
from jax.experimental import pallas as pl
from jax.experimental.pallas import tpu as pltpu

F32 = jnp.float32
BF16 = jnp.bfloat16

D_MODEL = 1024
HEAD_DIM = 64
WIDTH_A = 512
WIDTH_B = 512
MIX_IN_WIDTH = 2560
ATTN_SPAN = 128
ATTN_DILATIONS = (1, 4, 16)
SGU_CHUNK = 128
N_GROUPS_B = 8
RWKV_HEADS = 16
RWKV_GN_EPS = 64e-5
WKV_CHUNK = 64
N_EXPERTS = 16
N_EXPERT_GROUPS = 4
EXPERTS_PER_GROUP = 4
D_EXPERT = 256
RMS_EPS = 1e-6
LANES = 128
VMEM_LIMIT = 56 * 1024 * 1024

NT_DIMS = (((1,), (1,)), ((), ()))
TN_DIMS = (((0,), (0,)), ((), ()))


def _params(*sem):
    return pltpu.CompilerParams(dimension_semantics=sem, vmem_limit_bytes=VMEM_LIMIT)


def _norm_mod(x, mod_ref):
    ms = jnp.mean(x * x, axis=-1, keepdims=True)
    return (x * lax.rsqrt(ms + RMS_EPS)) * (1.0 + mod_ref[0, 1:2, :]) + mod_ref[0, 0:1, :]


def _gelu(x):
    return 0.5 * x * (1.0 + lax.erf(x * 0.7071067811865476))


def _ada_body(c_ref, w_ref, b_ref, o_ref):
    c = c_ref[...]
    c_act = c * jax.nn.sigmoid(c)
    n = c.shape[0]
    c_hi = c_act.astype(BF16)
    c_lo = (c_act - c_hi.astype(F32)).astype(BF16)
    c2 = jnp.concatenate([c_hi, c_lo], axis=0)
    w = w_ref[0]
    w_hi = w.astype(BF16)
    w_lo = (w - w_hi.astype(F32)).astype(BF16)
    by_hi = jnp.dot(c2, w_hi, preferred_element_type=F32)
    by_lo = jnp.dot(c2, w_lo, preferred_element_type=F32)
    o_ref[0] = (by_hi[:n] + by_hi[n:]) + (by_lo[:n] + by_lo[n:]) + b_ref[0]


def _ada_call(c, ada_w, ada_b):
    n_pair = ada_w.shape[0] * ada_w.shape[1]
    batch = c.shape[0]
    w = ada_w.reshape(n_pair, D_MODEL, 3 * D_MODEL)
    b = ada_b.reshape(n_pair, 1, 3 * D_MODEL)
    out = pl.pallas_call(
        _ada_body,
        grid=(n_pair, 3),
        in_specs=[
            pl.BlockSpec((batch, D_MODEL), lambda p, j: (0, 0)),
            pl.BlockSpec((1, D_MODEL, D_MODEL), lambda p, j: (p, 0, j)),
            pl.BlockSpec((1, 1, D_MODEL), lambda p, j: (p, 0, j)),
        ],
        out_specs=pl.BlockSpec((1, batch, D_MODEL), lambda p, j: (p, 0, j)),
        out_shape=jax.ShapeDtypeStruct((n_pair, batch, 3 * D_MODEL), F32),
        compiler_params=_params("arbitrary", "arbitrary"),
        name="ada_mod",
    )(c, w, b)
    return out.reshape(n_pair, batch, 3, D_MODEL)


def _proj_body(x_ref, mod_ref, w_ref, o_ref):
    h = _norm_mod(x_ref[0], mod_ref)
    o_ref[0] = jnp.dot(h.astype(BF16), w_ref[...], preferred_element_type=F32)


def _proj_call(x, mod, w_in, ts=1024):
    batch, seq, _ = x.shape
    return pl.pallas_call(
        _proj_body,
        grid=(batch, seq // ts),
        in_specs=[
            pl.BlockSpec((1, ts, D_MODEL), lambda b, i: (b, i, 0)),
            pl.BlockSpec((1, 3, D_MODEL), lambda b, i: (b, 0, 0)),
            pl.BlockSpec((D_MODEL, MIX_IN_WIDTH), lambda b, i: (0, 0)),
        ],
        out_specs=pl.BlockSpec((1, ts, MIX_IN_WIDTH), lambda b, i: (b, i, 0)),
        out_shape=jax.ShapeDtypeStruct((batch, seq, MIX_IN_WIDTH), F32),
        compiler_params=_params("arbitrary", "arbitrary"),
        name="mix_in_proj",
    )(x, mod, w_in)


def _attn_body(q_ref, k_ref, v_ref, o_ref, ob, lb):
    span = ATTN_SPAN
    W = LANES
    block_group = 8

    def iota(shape, dim):
        return lax.broadcasted_iota(jnp.int32, shape, dim)

    head0 = iota((span, W), 1) < HEAD_DIM
    mine = (head0, jnp.logical_not(head0))
    row2, col2 = iota((span, 2 * span), 0), iota((span, 2 * span), 1)
    mask_prev_cur = jnp.where(col2 < span, col2 - row2, row2 - (col2 - span)) >= 0
    mask_cur = iota((span, span), 1) <= iota((span, span), 0)

    def rows(start, n_rows, dil):
        return pl.ds(start, n_rows) if dil == 1 else pl.ds(start, n_rows, stride=dil)

    def group(p, dil, q_starts, with_prev):
        blocks = range(len(q_starts))
        heads = range(2)
        qb, kcat, vaug, masks = [], [], [], []
        for j in blocks:
            n_keys = 2 * span if with_prev[j] else span
            k_start = q_starts[j] - span * dil if with_prev[j] else q_starts[j]
            qb.append(q_ref[0, rows(q_starts[j], span, dil), :] * (HEAD_DIM ** -0.5))
            kcat.append(k_ref[0, rows(k_start, n_keys, dil), :].astype(BF16))
            vb = v_ref[0, rows(k_start, n_keys, dil), :].astype(BF16)
            vaug.append(jnp.concatenate([vb, jnp.ones((n_keys, W), BF16)], axis=1))
            masks.append(mask_prev_cur if with_prev[j] else mask_cur)
        qh = [[jnp.where(mine[h], qb[j], 0.0).astype(BF16) for h in heads] for j in blocks]
        s = [[lax.dot_general(qh[j][h], kcat[j], NT_DIMS, preferred_element_type=F32) for h in heads] for j in blocks]
        s = [[jnp.where(masks[j], s[j][h], -jnp.inf) for h in heads] for j in blocks]
        m = [[jnp.max(s[j][h], axis=-1, keepdims=True) for h in heads] for j in blocks]
        pr = [[jnp.exp(s[j][h] - m[j][h]).astype(BF16) for h in heads] for j in blocks]
        ad = [[jnp.dot(pr[j][h], vaug[j], preferred_element_type=F32) for h in heads] for j in blocks]
        for j in blocks:
            den = jnp.where(head0, ad[j][0][:, W:], ad[j][1][:, W:])
            o = jnp.where(head0, ad[j][0][:, :W], ad[j][1][:, :W]) / den
            lse = jnp.where(head0, m[j][0], m[j][1]) + jnp.log(den)
            qi = rows(q_starts[j], span, dil)
            ob[p, qi, :] = o
            lb[p, qi, :] = lse

    seq = q_ref.shape[1]
    for p, dil in enumerate(ATTN_DILATIONS):
        n_blk = seq // dil // span
        blocks = [(r + n * span * dil, n > 0) for r in range(dil) for n in range(n_blk)]
        for g0 in range(0, len(blocks), block_group):
            members = blocks[g0:g0 + block_group]
            group(p, dil, [b[0] for b in members], [b[1] for b in members])

    l0, l1, l2 = lb[0], lb[1], lb[2]
    m = jnp.maximum(jnp.maximum(l0, l1), l2)
    w0, w1, w2 = jnp.exp(l0 - m), jnp.exp(l1 - m), jnp.exp(l2 - m)
    o_ref[0] = (w0 * ob[0] + w1 * ob[1] + w2 * ob[2]) / (w0 + w1 + w2)


def _attn_call(proj):
    batch, seq, _ = proj.shape
    n_pair = WIDTH_A // LANES
    blk = lambda off: pl.BlockSpec((1, seq, LANES), lambda b, h: (b, 0, off + h))
    return pl.pallas_call(
        _attn_body,
        grid=(batch, n_pair),
        in_specs=[blk(0), blk(n_pair), blk(2 * n_pair)],
        out_specs=pl.BlockSpec((1, seq, LANES), lambda b, h: (b, 0, h)),
        out_shape=jax.ShapeDtypeStruct((batch, seq, WIDTH_A), F32),
        scratch_shapes=[
            pltpu.VMEM((3, seq, LANES), F32),
            pltpu.VMEM((3, seq, LANES), F32),
        ],
        compiler_params=_params("arbitrary", "arbitrary"),
        name="dilated_attn",
    )(proj, proj, proj)


def _sgu_out_body(u_ref, z_ref, oa_ref, x_ref, mod_ref, ng_ref, sw_ref, sb_ref, wo_ref, o_ref, mix):
    ts = u_ref.shape[1]
    u = _gelu(u_ref[0])
    z = _gelu(z_ref[0])
    z = z * lax.rsqrt(jnp.mean(z * z, axis=-1, keepdims=True) + RMS_EPS) * ng_ref[...]
    zb = z.astype(BF16)
    row = lax.broadcasted_iota(jnp.int32, (SGU_CHUNK, SGU_CHUNK), 0)
    col = lax.broadcasted_iota(jnp.int32, (SGU_CHUNK, SGU_CHUNK), 1)
    causal = col <= row
    for g in range(N_GROUPS_B):
        w_g = jnp.where(causal, sw_ref[g], 0.0).astype(BF16)
        for cc in range(ts // SGU_CHUNK):
            rs = slice(cc * SGU_CHUNK, (cc + 1) * SGU_CHUNK)
            ls = slice(g * HEAD_DIM, (g + 1) * HEAD_DIM)
            mix[rs, ls] = jnp.dot(w_g, zb[rs, ls], preferred_element_type=F32)
    bias = jnp.concatenate([sb_ref[...]] * (ts // SGU_CHUNK), axis=0)
    o_b = u * (mix[...] + bias)
    y = jnp.dot(oa_ref[0].astype(BF16), wo_ref[0:WIDTH_A, :], preferred_element_type=F32)
    y = y + jnp.dot(o_b.astype(BF16), wo_ref[WIDTH_A:, :], preferred_element_type=F32)
    o_ref[0] = x_ref[0] + mod_ref[0, 2:3, :] * y


def _sgu_out_call(proj, o_a, x, mod, norm_g, sgu_w, sgu_bias_tile, w_out, ts=1024):
    batch, seq, _ = x.shape
    u_blk = 3 * WIDTH_A // WIDTH_B
    full = lambda shape: pl.BlockSpec(shape, lambda b, i: (0,) * len(shape))
    return pl.pallas_call(
        _sgu_out_body,
        grid=(batch, seq // ts),
        in_specs=[
            pl.BlockSpec((1, ts, WIDTH_B), lambda b, i: (b, i, u_blk)),
            pl.BlockSpec((1, ts, WIDTH_B), lambda b, i: (b, i, u_blk + 1)),
            pl.BlockSpec((1, ts, WIDTH_A), lambda b, i: (b, i, 0)),
            pl.BlockSpec((1, ts, D_MODEL), lambda b, i: (b, i, 0)),
            pl.BlockSpec((1, 3, D_MODEL), lambda b, i: (b, 0, 0)),
            full((1, WIDTH_B)),
            full((N_GROUPS_B, SGU_CHUNK, SGU_CHUNK)),
            full((SGU_CHUNK, WIDTH_B)),
            full((WIDTH_A + WIDTH_B, D_MODEL)),
        ],
        out_specs=pl.BlockSpec((1, ts, D_MODEL), lambda b, i: (b, i, 0)),
        out_shape=jax.ShapeDtypeStruct(x.shape, F32),
        scratch_shapes=[pltpu.VMEM((ts, WIDTH_B), F32)],
        compiler_params=_params("arbitrary", "arbitrary"),
        name="sgu_out_proj",
    )(proj, proj, o_a, x, mod, norm_g, sgu_w, sgu_bias_tile, w_out)


def _rwkv_pre_body(has_vres, *refs):
    if has_vres:
        (x_ref, mod_ref, mu_ref, wr_ref, wk_ref, wv_ref, w0_ref, w1_ref, w2_ref, a0_ref, a1_ref, a2_ref,
         g1_ref, g2_ref, kk_ref, ka_ref, vf_ref, v0_ref, v1_ref, v2_ref,
         r_out, w_out, k_out, v_out, kk_out, a_out, g_out, carry) = refs
    else:
        (x_ref, mod_ref, mu_ref, wr_ref, wk_ref, wv_ref, w0_ref, w1_ref, w2_ref, a0_ref, a1_ref, a2_ref,
         g1_ref, g2_ref, kk_ref, ka_ref,
         r_out, w_out, k_out, v_out, kk_out, a_out, g_out, carry) = refs

    @pl.when(pl.program_id(1) == 0)
    def _():
        carry[...] = jnp.zeros_like(carry)

    h = _norm_mod(x_ref[0], mod_ref)
    ts = h.shape[0]
    first = lax.broadcasted_iota(jnp.int32, h.shape, 0) == 0
    h_prev = jnp.where(first, carry[0:1, :], pltpu.roll(h, 1, 0))
    carry[0:1, :] = h[ts - 1:ts, :]
    xx = h_prev - h

    def mixed(i):
        return (h + xx * mu_ref[i:i + 1, :]).astype(BF16)

    def mm(a, w_ref):
        return jnp.dot(a, w_ref[...], preferred_element_type=F32)

    xr, xw, xk, xv, xa, xg = [mixed(i) for i in range(6)]
    w_low = mm(xw, w1_ref)
    a_low = mm(xa, a1_ref)
    g_low = mm(xg, g1_ref)
    r = mm(xr, wr_ref)
    k = mm(xk, wk_ref)
    v = mm(xv, wv_ref)
    v_low = mm(xv, v1_ref) if has_vres else None
    z = w0_ref[...] + mm(jnp.tanh(w_low).astype(BF16), w2_ref)
    w_log = -(jnp.maximum(-z, 0.0) + jnp.log(1.0 + jnp.exp(-jnp.abs(z)))) - 0.5
    if has_vres:
        mix_v = jax.nn.sigmoid(v0_ref[...] + mm(v_low.astype(BF16), v2_ref))
        v = v + (vf_ref[0].astype(F32) - v) * mix_v
    a = jax.nn.sigmoid(a0_ref[...] + mm(a_low.astype(BF16), a2_ref))
    g = mm(jax.nn.sigmoid(g_low).astype(BF16), g2_ref)
    r_out[0] = r.astype(BF16)
    w_out[0] = w_log
    kk_out[0] = (k * kk_ref[...]).astype(BF16)
    k_out[0] = (k * (1.0 + (a - 1.0) * ka_ref[...])).astype(BF16)
    v_out[0] = v.astype(BF16)
    a_out[0] = a.astype(BF16)
    g_out[0] = g.astype(BF16)


def _rwkv_pre_call(x, mod, weights, v_first, vres, ts=512):
    batch, seq, _ = x.shape
    tok = pl.BlockSpec((1, ts, D_MODEL), lambda b, i: (b, i, 0))
    full = lambda a: pl.BlockSpec(a.shape, lambda b, i: (0,) * a.ndim, pipeline_mode=pl.Buffered(1))
    ins = [x, mod] + list(weights)
    specs = [tok, pl.BlockSpec((1, 3, D_MODEL), lambda b, i: (b, 0, 0))] + [full(a) for a in weights]
    if vres is not None:
        ins += [v_first] + list(vres)
        specs += [tok] + [full(a) for a in vres]
    out_dtypes = [BF16, F32, BF16, BF16, BF16, BF16, BF16]
    return pl.pallas_call(
        functools.partial(_rwkv_pre_body, vres is not None),
        grid=(batch, seq // ts),
        in_specs=specs,
        out_specs=[tok] * 7,
        out_shape=[jax.ShapeDtypeStruct(x.shape, dt) for dt in out_dtypes],
        scratch_shapes=[pltpu.VMEM((8, D_MODEL), F32)],
        compiler_params=_params("arbitrary", "arbitrary"),
        name="rwkv_pre",
    )(*ins)


def _wkv_body(r_ref, w_ref, k_ref, v_ref, kk_ref, a_ref, lnw_ref, lnb_ref, rk_ref, o_ref, state):
    C = WKV_CHUNK
    W = LANES
    n_chunk = r_ref.shape[1] // C
    n_pair = r_ref.shape[2] // W

    @pl.when(pl.program_id(2) == 0)
    def _():
        state[...] = jnp.zeros_like(state)

    def iota(shape, dim):
        return lax.broadcasted_iota(jnp.int32, shape, dim)

    def stack_heads(x):
        own = iota(x.shape, 1) % W < HEAD_DIM
        return jnp.concatenate([jnp.where(own, x, 0.0), jnp.where(own, 0.0, x)], axis=0).astype(BF16)

    def block_diag(x):
        left = iota(x.shape, 1) < C
        return jnp.concatenate([jnp.where(left, x, 0.0), jnp.where(left, 0.0, x)], axis=0).astype(BF16)

    row4, col4 = iota((C, 4 * C), 0), iota((C, 4 * C), 1) % C
    strict4 = col4 < row4
    lower4 = col4 <= row4
    tri = jnp.where(iota((C, C), 1) <= iota((C, C), 0), 1.0, 0.0).astype(BF16)
    eye2 = jnp.where(iota((C, 2 * C), 1) % C == iota((C, 2 * C), 0), 1.0, 0.0).astype(F32)
    rw, cw = iota((W, W), 0), iota((W, W), 1)
    same_head = (rw < HEAD_DIM) == (cw < HEAD_DIM)
    ones_bd = jnp.where(same_head, 1.0, 0.0).astype(BF16)
    diag_w = cw == rw
    zeros_cw = jnp.zeros((C, W), F32)

    def bdot(a, b):
        return jnp.dot(a.astype(BF16), b.astype(BF16), preferred_element_type=F32)

    chunks = range(n_chunk)
    pairs = range(n_pair)
    units = [(c, p) for c in chunks for p in pairs]
    rows = lambda t, c: t[c * C:(c + 1) * C]
    blk = lambda t, u: t[u[0] * C:(u[0] + 1) * C, u[1] * W:(u[1] + 1) * W]
    lanes = lambda t, p: t[:, p * W:(p + 1) * W]
    per_chunk = lambda vals: jnp.concatenate([jnp.broadcast_to(t, (C, n_pair * W)) for t in vals], axis=0)
    head_sums = lambda t: jnp.concatenate([bdot(lanes(t, p), ones_bd) for p in pairs], axis=1)

    r = r_ref[0].astype(F32)
    k = k_ref[0].astype(F32)
    v = v_ref[0].astype(F32)
    kk = kk_ref[0].astype(F32)
    kk = kk / jnp.maximum(jnp.sqrt(head_sums(kk * kk)), 1e-12)
    b = kk * a_ref[0].astype(F32)
    bonus = head_sums(r * k * rk_ref[...]) * v
    log_w = -jnp.exp(w_ref[0])
    hi = log_w.astype(BF16)
    lo = (log_w - hi.astype(F32)).astype(BF16)
    hilo = jnp.concatenate([hi, lo], axis=1)
    cum2 = [jnp.dot(tri, rows(hilo, c), preferred_element_type=F32) for c in chunks]
    cum = jnp.concatenate([c2[:, :n_pair * W] + c2[:, n_pair * W:] for c2 in cum2], axis=0)
    c_mid = [cum[c * C + C // 2 - 1:c * C + C // 2, :] for c in chunks]
    c_end = [cum[c * C + C - 1:(c + 1) * C, :] for c in chunks]
    g_last = [jnp.exp(t) for t in c_end]
    from_mid = jnp.exp(per_chunk(c_mid) - cum)
    to_mid = per_chunk([jnp.exp(-t) for t in c_mid])
    a_true = -kk * jnp.exp(cum - log_w)
    r_true = r * jnp.exp(cum)
    a_mid = a_true * to_mid
    r_mid = r_true * to_mid
    b_mid = b * from_mid
    k_mid = k * from_mid
    ar_mid = [jnp.concatenate([blk(a_mid, u), blk(r_mid, u)], axis=0).astype(BF16) for u in units]
    gram = [lax.dot_general(ar_mid[i], jnp.concatenate([stack_heads(blk(b_mid, u)), stack_heads(blk(k_mid, u))],
                                                        axis=0), NT_DIMS, preferred_element_type=F32)
            for i, u in enumerate(units)]
    top = [jnp.where(strict4, g[:C], 0.0) for g in gram]
    bot = [jnp.where(lower4, g[C:], 0.0).astype(BF16) for g in gram]
    n1 = [t[:, :2 * C] for t in top]
    inv = [eye2 + t for t in n1]
    pw = [bdot(t, block_diag(t)) for t in n1]
    levels = C.bit_length() - 2
    for lvl in range(levels):
        if lvl < levels - 1:
            st = [bdot(jnp.concatenate([pw[i], inv[i]], axis=0), block_diag(pw[i])) for i in range(len(units))]
            pw = [t[:C] for t in st]
            inv = [inv[i] + st[i][C:] for i in range(len(units))]
        else:
            inv = [inv[i] + bdot(inv[i], block_diag(pw[i])) for i in range(len(units))]
    x = [jnp.dot(top[i][:, 2 * C:].astype(BF16), stack_heads(blk(v, u)), preferred_element_type=F32)
         for i, u in enumerate(units)]
    pp = [jnp.dot(inv[i].astype(BF16), stack_heads(jnp.concatenate([blk(a_true, u), x[i]], axis=1)),
                  preferred_element_type=F32) for i, u in enumerate(units)]
    zv = [jnp.concatenate([zeros_cw, blk(v, u)], axis=1) for u in units]
    qq = [jnp.dot(bot[i], jnp.concatenate([stack_heads(pp[i]), stack_heads(zv[i])], axis=0),
                  preferred_element_type=F32) for i in range(len(units))]
    end_from_mid = [jnp.exp(c_end[c] - c_mid[c]) for c in chunks]
    bk_hat = [jnp.concatenate([blk(b_mid, u), blk(k_mid, u)], axis=0) * lanes(end_from_mid[u[0]], u[1])
              for u in units]
    mm = [lax.dot_general(bk_hat[i].astype(BF16), jnp.concatenate([pp[i], zv[i]], axis=0).astype(BF16), TN_DIMS,
                          preferred_element_type=F32) for i in range(len(units))]
    q1 = [blk(r_true, u) + qq[i][:, :W] for i, u in enumerate(units)]
    q2 = [t[:, W:] for t in qq]
    m1 = [(jnp.where(same_head, mm[i][:, :W], 0.0)
           + jnp.where(diag_w, jnp.broadcast_to(lanes(g_last[u[0]], u[1]), (W, W)), 0.0)).astype(BF16)
          for i, u in enumerate(units)]
    m2 = [jnp.where(same_head, t[:, W:], 0.0) for t in mm]
    h_cur = [state[p] for p in pairs]
    h_in = []
    for i, (c, p) in enumerate(units):
        h_in.append(h_cur[p])
        h_cur[p] = jnp.dot(m1[i], h_cur[p].astype(BF16), preferred_element_type=F32) + m2[i]
    for p in pairs:
        state[p] = h_cur[p]
    y_blk = [bdot(q1[i], h_in[i]) + q2[i] for i in range(len(units))]
    y = jnp.concatenate([jnp.concatenate([y_blk[c * n_pair + p] for p in pairs], axis=1) for c in chunks], axis=0)
    mean = head_sums(y) * (1.0 / HEAD_DIM)
    yc = y - mean
    var = head_sums(yc * yc) * (1.0 / HEAD_DIM)
    yn = yc * lax.rsqrt(var + RWKV_GN_EPS) * lnw_ref[...] + lnb_ref[...]
    o_ref[0] = (yn + bonus).astype(BF16)


def _wkv_call(r, w_log, k, v, kk, a, ln_w, ln_b, r_k, tc=256, pairs_per_step=8):
    batch, seq, _ = r.shape
    width = pairs_per_step * LANES
    tok = pl.BlockSpec((1, tc, width), lambda b, h, t: (b, t, h))
    vec = pl.BlockSpec((1, width), lambda b, h, t: (0, h))
    return pl.pallas_call(
        _wkv_body,
        grid=(batch, D_MODEL // width, seq // tc),
        in_specs=[tok] * 6 + [vec] * 3,
        out_specs=tok,
        out_shape=jax.ShapeDtypeStruct(r.shape, BF16),
        scratch_shapes=[pltpu.VMEM((pairs_per_step, LANES, LANES), F32)],
        compiler_params=_params("arbitrary", "arbitrary", "arbitrary"),
        name="wkv7_chunked",
    )(r, w_log, k, v, kk, a, ln_w, ln_b, r_k)


def _rwkv_post_body(y_ref, g_ref, x_ref, mod_ref, wo_ref, o_ref):
    yg = y_ref[0] * g_ref[0]
    o_ref[0] = x_ref[0] + mod_ref[0, 2:3, :] * jnp.dot(yg, wo_ref[...], preferred_element_type=F32)


def _rwkv_post_call(y, g, x, mod, w_o, ts=1024):
    batch, seq, _ = x.shape
    tok = pl.BlockSpec((1, ts, D_MODEL), lambda b, i: (b, i, 0))
    return pl.pallas_call(
        _rwkv_post_body,
        grid=(batch, seq // ts),
        in_specs=[tok, tok, tok, pl.BlockSpec((1, 3, D_MODEL), lambda b, i: (b, 0, 0)),
                  pl.BlockSpec((D_MODEL, D_MODEL), lambda b, i: (0, 0))],
        out_specs=tok,
        out_shape=jax.ShapeDtypeStruct(x.shape, F32),
        compiler_params=_params("arbitrary", "arbitrary"),
        name="rwkv_out_proj",
    )(y, g, x, mod, w_o)


MOE_TILE = 512
MOE_BLOCK = 64
MOE_SLOTS = MOE_TILE + N_EXPERT_GROUPS * MOE_BLOCK
MOE_BLOCKS_PER_TILE = MOE_SLOTS // MOE_BLOCK
MOE_STEP_BLOCKS = 16


def _route(logits_t, rb):
    n_g, epg = N_EXPERT_GROUPS, EXPERTS_PER_GROUP
    s = jax.nn.sigmoid(logits_t)
    sel = s + rb
    s_rows = [s[i:i + 1, :] for i in range(N_EXPERTS)]
    sel_rows = [sel[i:i + 1, :] for i in range(N_EXPERTS)]
    scores = []
    for g in range(n_g):
        members = sel_rows[g * epg:(g + 1) * epg]
        best = None
        for i in range(epg):
            for j in range(i + 1, epg):
                pair = members[i] + members[j]
                best = pair if best is None else jnp.maximum(best, pair)
        scores.append(best)
    g_idx = jnp.zeros_like(scores[0], dtype=jnp.int32)
    top = scores[0]
    for g in range(1, n_g):
        better = scores[g] > top
        g_idx = jnp.where(better, g, g_idx)
        top = jnp.where(better, scores[g], top)

    def pick(rows_, j):
        out = rows_[j]
        for g in range(1, n_g):
            out = jnp.where(g_idx == g, rows_[g * epg + j], out)
        return out

    in_sel = [pick(sel_rows, j) for j in range(epg)]
    in_s = [pick(s_rows, j) for j in range(epg)]
    chosen = []
    for j in range(epg):
        rank = jnp.zeros_like(g_idx)
        for i in range(epg):
            if i == j:
                continue
            ahead = (in_sel[i] >= in_sel[j]) if i < j else (in_sel[i] > in_sel[j])
            rank = rank + jnp.where(ahead, 1, 0)
        chosen.append(rank < 2)
    den = sum(jnp.where(chosen[j], in_s[j], 0.0) for j in range(epg))
    gate_rows = []
    for ex in range(N_EXPERTS):
        g, j = divmod(ex, epg)
        on = jnp.logical_and(chosen[j], g_idx == g)
        gate_rows.append(jnp.where(on, in_s[j] / den, 0.0))
    return g_idx, gate_rows


def _slot_one_hot(pos_row):
    slot = lax.broadcasted_iota(jnp.int32, (MOE_SLOTS, pos_row.shape[1]), 0)
    return jnp.where(slot == pos_row, 1.0, 0.0).astype(BF16)


def _moe_sort_body(x_ref, mod_ref, rw_ref, rb_ref, hs_ref, gs_ref, pos_ref, cnt_ref):
    tm = MOE_TILE
    n_g = N_EXPERT_GROUPS
    tiles = range(x_ref.shape[1] // tm)
    h = [_norm_mod(x_ref[0, i * tm:(i + 1) * tm, :], mod_ref) for i in tiles]
    hi = [t.astype(BF16) for t in h]
    lo = [(h[i] - hi[i].astype(F32)).astype(BF16) for i in tiles]
    parts = [jnp.dot(jnp.concatenate([hi[i], lo[i]], axis=0), rw_ref[...], preferred_element_type=F32)
             for i in tiles]
    logits = [(p[:tm, :LANES] + p[:tm, LANES:]) + (p[tm:, :LANES] + p[tm:, LANES:]) for p in parts]
    routed = [_route(t.T[:N_EXPERTS], rb_ref[...]) for t in logits]
    member = [[jnp.where(routed[i][0] == g, 1.0, 0.0) for g in range(n_g)] for i in tiles]
    earlier = jnp.where(lax.broadcasted_iota(jnp.int32, (tm, tm), 0) < lax.broadcasted_iota(jnp.int32, (tm, tm), 1),
                        1.0, 0.0).astype(BF16)
    rank = [jnp.dot(jnp.concatenate(member[i] + [jnp.zeros((8 - n_g, tm), F32)], axis=0).astype(BF16), earlier,
                    preferred_element_type=F32) for i in tiles]
    lane = lax.broadcasted_iota(jnp.int32, (1, LANES), 1)
    one_hot = []
    for i in tiles:
        pos = jnp.zeros((1, tm), F32)
        offset = jnp.zeros((1, 1), F32)
        counts = jnp.zeros((1, LANES), F32)
        for g in range(n_g):
            n_members = jnp.sum(member[i][g], axis=1, keepdims=True)
            n_blocks = jnp.floor((n_members + (MOE_BLOCK - 1)) * (1.0 / MOE_BLOCK))
            pos = pos + member[i][g] * (offset + rank[i][g:g + 1, :])
            counts = counts + jnp.where(lane == g, n_blocks, 0.0)
            offset = offset + n_blocks * MOE_BLOCK
        pos_i = pos.astype(jnp.int32)
        pos_ref[i] = pos_i
        cnt_ref[i] = counts.astype(jnp.int32)
        one_hot.append(_slot_one_hot(pos_i))
    pad = jnp.zeros((LANES - N_EXPERTS, tm), F32)
    gate_t = [jnp.concatenate(routed[i][1] + [pad], axis=0).T for i in tiles]
    g_hi = [t.astype(BF16) for t in gate_t]
    g_lo = [(gate_t[i] - g_hi[i].astype(F32)).astype(BF16) for i in tiles]
    hs = [jnp.dot(one_hot[i], hi[i], preferred_element_type=F32) for i in tiles]
    gs = [jnp.dot(one_hot[i], jnp.concatenate([g_hi[i], g_lo[i]], axis=1), preferred_element_type=F32)
          for i in tiles]
    for i in tiles:
        hs_ref[i * MOE_SLOTS:(i + 1) * MOE_SLOTS, :] = hs[i].astype(BF16)
        gs_ref[i * MOE_SLOTS:(i + 1) * MOE_SLOTS, :] = gs[i].astype(BF16)


def _moe_sort_call(x, mod, router_w2, router_b, tiles_per_step=4):
    batch, seq, _ = x.shape
    rows = tiles_per_step * MOE_TILE
    per_seq = seq // rows
    n_tiles = batch * seq // MOE_TILE
    step = lambda b, i: b * per_seq + i
    return pl.pallas_call(
        _moe_sort_body,
        grid=(batch, per_seq),
        in_specs=[
            pl.BlockSpec((1, rows, D_MODEL), lambda b, i: (b, i, 0)),
            pl.BlockSpec((1, 3, D_MODEL), lambda b, i: (b, 0, 0)),
            pl.BlockSpec((D_MODEL, 2 * LANES), lambda b, i: (0, 0)),
            pl.BlockSpec((N_EXPERTS, 1), lambda b, i: (0, 0)),
        ],
        out_specs=[
            pl.BlockSpec((tiles_per_step * MOE_SLOTS, D_MODEL), lambda b, i: (step(b, i), 0)),
            pl.BlockSpec((tiles_per_step * MOE_SLOTS, 2 * LANES), lambda b, i: (step(b, i), 0)),
            pl.BlockSpec((tiles_per_step, 1, MOE_TILE), lambda b, i: (step(b, i), 0, 0)),
            pl.BlockSpec((tiles_per_step, 1, LANES), lambda b, i: (step(b, i), 0, 0)),
        ],
        out_shape=[
            jax.ShapeDtypeStruct((n_tiles * MOE_SLOTS, D_MODEL), BF16),
            jax.ShapeDtypeStruct((n_tiles * MOE_SLOTS, 2 * LANES), BF16),
            jax.ShapeDtypeStruct((n_tiles, 1, MOE_TILE), jnp.int32),
            jax.ShapeDtypeStruct((n_tiles, 1, LANES), jnp.int32),
        ],
        compiler_params=_params("arbitrary", "arbitrary"),
        name="moe_sort",
    )(x, mod, router_w2, router_b)


def _moe_tables_body(cnt_ref, sg_ref, sv_ref, sb_ref, bp_ref):
    n_tiles = cnt_ref.shape[0]
    n_g, r, bpt = N_EXPERT_GROUPS, MOE_STEP_BLOCKS, MOE_BLOCKS_PER_TILE
    n_steps = sg_ref.shape[0]

    def fill(ref, n, value):
        def body(i, carry):
            ref[i] = value
            return carry
        lax.fori_loop(0, n, body, 0)

    fill(sb_ref, n_steps * r, 0)
    fill(bp_ref, n_tiles * bpt, 0)
    fill(sg_ref, n_steps, n_g - 1)
    fill(sv_ref, n_steps, 0)
    slot = jnp.int32(0)
    for g in range(n_g):
        def tile_body(i, k, g=g):
            first = i * bpt
            for g_before in range(g):
                first = first + cnt_ref[i, g_before]

            def block_body(j, k):
                sb_ref[k] = first + j
                bp_ref[first + j] = k
                return k + 1
            return lax.fori_loop(0, cnt_ref[i, g], block_body, k)

        slot_end = lax.fori_loop(0, n_tiles, tile_body, slot)
        step_first = slot // r
        step_stop = (slot_end + r - 1) // r

        def step_body(s, carry, g=g, slot_end=slot_end):
            sg_ref[s] = g
            sv_ref[s] = jnp.minimum(slot_end - s * r, r)
            return carry
        lax.fori_loop(step_first, step_stop, step_body, 0)
        slot = step_stop * r


def _moe_step_tables(counts):
    n_tiles = counts.shape[0]
    n_blocks = n_tiles * MOE_BLOCKS_PER_TILE
    n_steps = n_blocks // MOE_STEP_BLOCKS + N_EXPERT_GROUPS
    smem = pl.BlockSpec(memory_space=pltpu.SMEM)
    return pl.pallas_call(
        _moe_tables_body,
        in_specs=[smem],
        out_specs=[smem] * 4,
        out_shape=[jax.ShapeDtypeStruct((n_steps,), jnp.int32), jax.ShapeDtypeStruct((n_steps,), jnp.int32),
                   jax.ShapeDtypeStruct((n_steps * MOE_STEP_BLOCKS,), jnp.int32),
                   jax.ShapeDtypeStruct((n_blocks,), jnp.int32)],
        name="moe_tables",
    )(counts)


def _moe_expert_body(sg_ref, sv_ref, sb_ref, *refs):
    r = MOE_STEP_BLOCKS
    h_refs, g_refs = refs[:r], refs[r:2 * r]
    w1_ref, w3_ref, w2_ref, o_ref, w13_scr, w2_scr, act_scr = refs[2 * r:]
    step = pl.program_id(0)
    epg = EXPERTS_PER_GROUP
    rows_total = r * MOE_BLOCK
    group = sg_ref[step]
    new_group = jnp.logical_or(step == 0, group != sg_ref[jnp.maximum(step - 1, 0)])

    @pl.when(sv_ref[step] == 0)
    def _():
        o_ref[...] = jnp.zeros_like(o_ref)

    @pl.when(jnp.logical_and(sv_ref[step] > 0, new_group))
    def _():
        w13_scr[:, :, :D_EXPERT] = w1_ref[0].astype(BF16)
        w13_scr[:, :, D_EXPERT:] = w3_ref[0].astype(BF16)
        w2_scr[...] = w2_ref[0].reshape(epg * D_EXPERT, D_MODEL).astype(BF16)

    @pl.when(sv_ref[step] > 0)
    def _():
        h = jnp.concatenate([ref[...] for ref in h_refs], axis=0)
        gates = jnp.concatenate([ref[...] for ref in g_refs], axis=0)
        real = lax.broadcasted_iota(jnp.int32, (rows_total, LANES), 0) < sv_ref[step] * MOE_BLOCK
        expert_row = lax.broadcasted_iota(jnp.int32, (2 * LANES, epg * LANES), 0) % LANES
        lane_block = lax.broadcasted_iota(jnp.int32, (2 * LANES, epg * LANES), 1) // LANES
        pick_rows = jnp.where(expert_row == group * epg + lane_block, 1.0, 0.0).astype(BF16)
        gate_all = jnp.dot(gates, pick_rows, preferred_element_type=F32)
        for j in range(epg):
            gate_b = jnp.where(real, gate_all[:, j * LANES:(j + 1) * LANES], 0.0)
            hid = jnp.dot(h, w13_scr[j], preferred_element_type=F32)
            h1 = hid[:, :D_EXPERT]
            act = h1 * jax.nn.sigmoid(h1) * hid[:, D_EXPERT:] * jnp.concatenate([gate_b, gate_b], axis=1)
            act_scr[:, j * D_EXPERT:(j + 1) * D_EXPERT] = act.astype(BF16)
        o_ref[...] = jnp.dot(act_scr[...], w2_scr[...], preferred_element_type=F32).astype(BF16)


def _moe_expert_call(hs, gs, w1, w3, w2, step_group, step_valid, step_blocks, layer):
    r = MOE_STEP_BLOCKS
    n_steps = step_group.shape[0]
    epg = EXPERTS_PER_GROUP
    rows_total = r * MOE_BLOCK

    def blk(width, j):
        return pl.BlockSpec((MOE_BLOCK, width), lambda s, sg, sv, sb, j=j: (sb[s * r + j], 0))

    def group_weights(shape):
        return pl.BlockSpec((1, epg) + shape, lambda s, sg, sv, sb: (layer, sg[s], 0, 0))

    grid_spec = pltpu.PrefetchScalarGridSpec(
        num_scalar_prefetch=3,
        grid=(n_steps,),
        in_specs=[blk(D_MODEL, j) for j in range(r)] + [blk(2 * LANES, j) for j in range(r)] + [
            group_weights((D_MODEL, D_EXPERT)), group_weights((D_MODEL, D_EXPERT)),
            group_weights((D_EXPERT, D_MODEL))],
        out_specs=pl.BlockSpec((rows_total, D_MODEL), lambda s, sg, sv, sb: (s, 0)),
        scratch_shapes=[
            pltpu.VMEM((epg, D_MODEL, 2 * D_EXPERT), BF16),
            pltpu.VMEM((epg * D_EXPERT, D_MODEL), BF16),
            pltpu.VMEM((rows_total, epg * D_EXPERT), BF16),
        ],
    )
    return pl.pallas_call(
        _moe_expert_body,
        grid_spec=grid_spec,
        out_shape=jax.ShapeDtypeStruct((n_steps * rows_total, D_MODEL), BF16),
        compiler_params=_params("arbitrary"),
        name="moe_experts",
    )(step_group, step_valid, step_blocks, *([hs] * r), *([gs] * r), w1, w3, w2)


def _moe_unsort_body(is_last, bp_ref, *refs):
    bpt = MOE_BLOCKS_PER_TILE
    pos_ref, x_ref, mod_ref, fg_ref, o_ref = refs[-5:]
    y_refs = refs[:-5]
    tm = MOE_TILE
    tiles = range(len(y_refs) // bpt)
    slot = lax.broadcasted_iota(jnp.int32, (tm, MOE_SLOTS), 1)
    one_hot = []
    for i in tiles:
        pos_col = jnp.broadcast_to(pos_ref[i].astype(F32), (LANES, tm)).T.astype(jnp.int32)
        pos_col = jnp.concatenate([pos_col] * (MOE_SLOTS // LANES), axis=1)
        one_hot.append(jnp.where(slot == pos_col, 1.0, 0.0).astype(BF16))
    ys = [jnp.concatenate([ref[...] for ref in y_refs[i * bpt:(i + 1) * bpt]], axis=0) for i in tiles]
    y = [jnp.dot(one_hot[i], ys[i], preferred_element_type=F32) for i in tiles]
    for i in tiles:
        out = x_ref[0, i * tm:(i + 1) * tm, :] + mod_ref[0, 2:3, :] * y[i]
        if is_last:
            out = out * lax.rsqrt(jnp.mean(out * out, axis=-1, keepdims=True) + RMS_EPS) * fg_ref[...]
        o_ref[0, i * tm:(i + 1) * tm, :] = out


def _moe_unsort_call(ys, pos, x, mod, final_g, block_pos, is_last, tiles_per_step=4):
    batch, seq, _ = x.shape
    rows = tiles_per_step * MOE_TILE
    per_seq = seq // rows
    bpt = MOE_BLOCKS_PER_TILE
    n_in = tiles_per_step * bpt

    def blk(j):
        return pl.BlockSpec((MOE_BLOCK, D_MODEL), lambda b, i, bp, j=j: (bp[(b * per_seq + i) * n_in + j], 0))

    grid_spec = pltpu.PrefetchScalarGridSpec(
        num_scalar_prefetch=1,
        grid=(batch, per_seq),
        in_specs=[blk(j) for j in range(n_in)] + [
            pl.BlockSpec((tiles_per_step, 1, MOE_TILE), lambda b, i, bp: (b * per_seq + i, 0, 0)),
            pl.BlockSpec((1, rows, D_MODEL), lambda b, i, bp: (b, i, 0)),
            pl.BlockSpec((1, 3, D_MODEL), lambda b, i, bp: (b, 0, 0)),
            pl.BlockSpec((1, D_MODEL), lambda b, i, bp: (0, 0)),
        ],
        out_specs=pl.BlockSpec((1, rows, D_MODEL), lambda b, i, bp: (b, i, 0)),
    )
    return pl.pallas_call(
        functools.partial(_moe_unsort_body, is_last),
        grid_spec=grid_spec,
        out_shape=jax.ShapeDtypeStruct(x.shape, F32),
        compiler_params=_params("arbitrary", "arbitrary"),
        name="moe_unsort",
    )(block_pos, *([ys] * n_in), pos, x, mod, final_g)


def _moe_layer(x, mod, router_w2, router_b, w1, w3, w2, final_g, layer, is_last):
    hs, gs, pos, cnt = _moe_sort_call(x, mod, router_w2, router_b)
    step_group, step_valid, step_blocks, block_pos = _moe_step_tables(cnt[:, 0, :])
    ys = _moe_expert_call(hs, gs, w1, w3, w2, step_group, step_valid, step_blocks, layer)
    return _moe_unsort_call(ys, pos, x, mod, final_g, block_pos, is_last)


def _pad_cols(w):
    n = w.shape[-1]
    return jnp.pad(w, ((0, 0), (0, -n % LANES)))


def _pad_rows(w):
    n = w.shape[0]
    return jnp.pad(w, ((0, -n % LANES), (0, 0)))


def kernel(x, c, ada_w, ada_b, mix_w_in, mix_w_out, sgu_norm_g, sgu_w, sgu_b, rwkv_mu, rwkv_w_rkv, rwkv_w_o, rwkv_w0, rwkv_w1, rwkv_w2, rwkv_a0, rwkv_a1, rwkv_a2, rwkv_v0, rwkv_v1, rwkv_v2, rwkv_g1, rwkv_g2, rwkv_k_k, rwkv_k_a, rwkv_r_k, rwkv_ln_w, rwkv_ln_b, router_w, router_b, moe_w1, moe_w3, moe_w2, final_norm_g):
    depth = ada_w.shape[0]
    row = lambda t: t.reshape(1, -1)
    mods = _ada_call(c, ada_w, ada_b)
    rw_hi = router_w.astype(BF16)
    rw_lo = (router_w - rw_hi.astype(F32)).astype(BF16)
    router_w2 = jnp.concatenate([_pad_cols(rw_hi), _pad_cols(rw_lo)], axis=1)
    router_bc = router_b.reshape(N_EXPERTS, 1)
    final_g = row(final_norm_g)
    v_first = None
    for layer in range(depth):
        i = layer // 2
        mod = mods[2 * layer]
        if layer % 2 == 0:
            proj = _proj_call(x, mod, mix_w_in[i].astype(BF16))
            o_a = _attn_call(proj)
            bias_tile = jnp.repeat(sgu_b[i].T, HEAD_DIM, axis=1)
            x = _sgu_out_call(proj, o_a, x, mod, row(sgu_norm_g[i]), sgu_w[i], bias_tile,
                              mix_w_out[i].astype(BF16))
        else:
            weights = [
                rwkv_mu[i],
                rwkv_w_rkv[i, 0].astype(BF16), rwkv_w_rkv[i, 1].astype(BF16), rwkv_w_rkv[i, 2].astype(BF16),
                row(rwkv_w0[i]), _pad_cols(rwkv_w1[i]).astype(BF16), _pad_rows(rwkv_w2[i]).astype(BF16),
                row(rwkv_a0[i]), _pad_cols(rwkv_a1[i]).astype(BF16), _pad_rows(rwkv_a2[i]).astype(BF16),
                _pad_cols(rwkv_g1[i]).astype(BF16), _pad_rows(rwkv_g2[i]).astype(BF16),
                row(rwkv_k_k[i]), row(rwkv_k_a[i]),
            ]
            vres = None
            if i > 0:
                vres = [row(rwkv_v0[i - 1]), _pad_cols(rwkv_v1[i - 1]).astype(BF16),
                        _pad_rows(rwkv_v2[i - 1]).astype(BF16)]
            r, w_log, k, v, kk, a, g = _rwkv_pre_call(x, mod, weights, v_first, vres)
            if i == 0:
                v_first = v
            y = _wkv_call(r, w_log, k, v, kk, a, row(rwkv_ln_w[i]), row(rwkv_ln_b[i]), row(rwkv_r_k[i]))
            x = _rwkv_post_call(y, g, x, mod, rwkv_w_o[i].astype(BF16))
        x = _moe_layer(x, mods[2 * layer + 1], router_w2, router_bc, moe_w1, moe_w3, moe_w2, final_g, layer,
                       layer == depth - 1)
    return x
```

```python
import functools

import jax
import jax.numpy as jnp
from jax import lax
from jax.experimental import pallas as pl
from jax.experimental.pallas import tpu as pltpu

F32 = jnp.float32
BF16 = jnp.bfloat16

D_MODEL = 1024
HEAD_DIM = 64
WIDTH_A = 512
WIDTH_B = 512
MIX_IN_WIDTH = 2560
ATTN_SPAN = 128
ATTN_DILATIONS = (1, 4, 16)
SGU_CHUNK = 128
N_GROUPS_B = 8
RWKV_HEADS = 16
RWKV_GN_EPS = 64e-5
WKV_CHUNK = 64
N_EXPERTS = 16
N_EXPERT_GROUPS = 4
EXPERTS_PER_GROUP = 4
D_EXPERT = 256
RMS_EPS = 1e-6
LANES = 128
VMEM_LIMIT = 56 * 1024 * 1024

NT_DIMS = (((1,), (1,)), ((), ()))
TN_DIMS = (((0,), (0,)), ((), ()))


def _params(*sem):
    return pltpu.CompilerParams(dimension_semantics=sem, vmem_limit_bytes=VMEM_LIMIT)


def _norm_mod(x, mod_ref):
    ms = jnp.mean(x * x, axis=-1, keepdims=True)
    return (x * lax.rsqrt(ms + RMS_EPS)) * (1.0 + mod_ref[0, 1:2, :]) + mod_ref[0, 0:1, :]


def _gelu(x):
    return 0.5 * x * (1.0 + lax.erf(x * 0.7071067811865476))


def _ada_body(c_ref, w_ref, b_ref, o_ref):
    c = c_ref[...]
    c_act = c * jax.nn.sigmoid(c)
    n = c.shape[0]
    c_hi = c_act.astype(BF16)
    c_lo = (c_act - c_hi.astype(F32)).astype(BF16)
    c2 = jnp.concatenate([c_hi, c_lo], axis=0)
    w = w_ref[0]
    w_hi = w.astype(BF16)
    w_lo = (w - w_hi.astype(F32)).astype(BF16)
    by_hi = jnp.dot(c2, w_hi, preferred_element_type=F32)
    by_lo = jnp.dot(c2, w_lo, preferred_element_type=F32)
    o_ref[0] = (by_hi[:n] + by_hi[n:]) + (by_lo[:n] + by_lo[n:]) + b_ref[0]


def _ada_call(c, ada_w, ada_b):
    n_pair = ada_w.shape[0] * ada_w.shape[1]
    batch = c.shape[0]
    w = ada_w.reshape(n_pair, D_MODEL, 3 * D_MODEL)
    b = ada_b.reshape(n_pair, 1, 3 * D_MODEL)
    out = pl.pallas_call(
        _ada_body,
        grid=(n_pair, 3),
        in_specs=[
            pl.BlockSpec((batch, D_MODEL), lambda p, j: (0, 0)),
            pl.BlockSpec((1, D_MODEL, D_MODEL), lambda p, j: (p, 0, j)),
            pl.BlockSpec((1, 1, D_MODEL), lambda p, j: (p, 0, j)),
        ],
        out_specs=pl.BlockSpec((1, batch, D_MODEL), lambda p, j: (p, 0, j)),
        out_shape=jax.ShapeDtypeStruct((n_pair, batch, 3 * D_MODEL), F32),
        compiler_params=_params("arbitrary", "arbitrary"),
        name="ada_mod",
    )(c, w, b)
    return out.reshape(n_pair, batch, 3, D_MODEL)


def _proj_body(x_ref, mod_ref, w_ref, o_ref):
    h = _norm_mod(x_ref[0], mod_ref)
    o_ref[0] = jnp.dot(h.astype(BF16), w_ref[...], preferred_element_type=F32)


def _proj_call(x, mod, w_in, ts=1024):
    batch, seq, _ = x.shape
    return pl.pallas_call(
        _proj_body,
        grid=(batch, seq // ts),
        in_specs=[
            pl.BlockSpec((1, ts, D_MODEL), lambda b, i: (b, i, 0)),
            pl.BlockSpec((1, 3, D_MODEL), lambda b, i: (b, 0, 0)),
            pl.BlockSpec((D_MODEL, MIX_IN_WIDTH), lambda b, i: (0, 0)),
        ],
        out_specs=pl.BlockSpec((1, ts, MIX_IN_WIDTH), lambda b, i: (b, i, 0)),
        out_shape=jax.ShapeDtypeStruct((batch, seq, MIX_IN_WIDTH), F32),
        compiler_params=_params("arbitrary", "arbitrary"),
        name="mix_in_proj",
    )(x, mod, w_in)


def _attn_body(q_ref, k_ref, v_ref, o_ref, ob, lb):
    span = ATTN_SPAN
    W = LANES
    block_group = 8

    def iota(shape, dim):
        return lax.broadcasted_iota(jnp.int32, shape, dim)

    head0 = iota((span, W), 1) < HEAD_DIM
    mine = (head0, jnp.logical_not(head0))
    row2, col2 = iota((span, 2 * span), 0), iota((span, 2 * span), 1)
    mask_prev_cur = jnp.where(col2 < span, col2 - row2, row2 - (col2 - span)) >= 0
    mask_cur = iota((span, span), 1) <= iota((span, span), 0)

    def rows(start, n_rows, dil):
        return pl.ds(start, n_rows) if dil == 1 else pl.ds(start, n_rows, stride=dil)

    def group(p, dil, q_starts, with_prev):
        blocks = range(len(q_starts))
        heads = range(2)
        qb, kcat, vaug, masks = [], [], [], []
        for j in blocks:
            n_keys = 2 * span if with_prev[j] else span
            k_start = q_starts[j] - span * dil if with_prev[j] else q_starts[j]
            qb.append(q_ref[0, rows(q_starts[j], span, dil), :] * (HEAD_DIM ** -0.5))
            kcat.append(k_ref[0, rows(k_start, n_keys, dil), :].astype(BF16))
            vb = v_ref[0, rows(k_start, n_keys, dil), :].astype(BF16)
            vaug.append(jnp.concatenate([vb, jnp.ones((n_keys, W), BF16)], axis=1))
            masks.append(mask_prev_cur if with_prev[j] else mask_cur)
        qh = [[jnp.where(mine[h], qb[j], 0.0).astype(BF16) for h in heads] for j in blocks]
        s = [[lax.dot_general(qh[j][h], kcat[j], NT_DIMS, preferred_element_type=F32) for h in heads] for j in blocks]
        s = [[jnp.where(masks[j], s[j][h], -jnp.inf) for h in heads] for j in blocks]
        m = [[jnp.max(s[j][h], axis=-1, keepdims=True) for h in heads] for j in blocks]
        pr = [[jnp.exp(s[j][h] - m[j][h]).astype(BF16) for h in heads] for j in blocks]
        ad = [[jnp.dot(pr[j][h], vaug[j], preferred_element_type=F32) for h in heads] for j in blocks]
        for j in blocks:
            den = jnp.where(head0, ad[j][0][:, W:], ad[j][1][:, W:])
            o = jnp.where(head0, ad[j][0][:, :W], ad[j][1][:, :W]) / den
            lse = jnp.where(head0, m[j][0], m[j][1]) + jnp.log(den)
            qi = rows(q_starts[j], span, dil)
            ob[p, qi, :] = o
            lb[p, qi, :] = lse

    seq = q_ref.shape[1]
    for p, dil in enumerate(ATTN_DILATIONS):
        n_blk = seq // dil // span
        blocks = [(r + n * span * dil, n > 0) for r in range(dil) for n in range(n_blk)]
        for g0 in range(0, len(blocks), block_group):
            members = blocks[g0:g0 + block_group]
            group(p, dil, [b[0] for b in members], [b[1] for b in members])

    l0, l1, l2 = lb[0], lb[1], lb[2]
    m = jnp.maximum(jnp.maximum(l0, l1), l2)
    w0, w1, w2 = jnp.exp(l0 - m), jnp.exp(l1 - m), jnp.exp(l2 - m)
    o_ref[0] = (w0 * ob[0] + w1 * ob[1] + w2 * ob[2]) / (w0 + w1 + w2)


def _attn_call(proj):
    batch, seq, _ = proj.shape
    n_pair = WIDTH_A // LANES
    blk = lambda off: pl.BlockSpec((1, seq, LANES), lambda b, h: (b, 0, off + h))
    return pl.pallas_call(
        _attn_body,
        grid=(batch, n_pair),
        in_specs=[blk(0), blk(n_pair), blk(2 * n_pair)],
        out_specs=pl.BlockSpec((1, seq, LANES), lambda b, h: (b, 0, h)),
        out_shape=jax.ShapeDtypeStruct((batch, seq, WIDTH_A), F32),
        scratch_shapes=[
            pltpu.VMEM((3, seq, LANES), F32),
            pltpu.VMEM((3, seq, LANES), F32),
        ],
        compiler_params=_params("arbitrary", "arbitrary"),
        name="dilated_attn",
    )(proj, proj, proj)


def _sgu_out_body(u_ref, z_ref, oa_ref, x_ref, mod_ref, ng_ref, sw_ref, sb_ref, wo_ref, o_ref, mix):
    ts = u_ref.shape[1]
    u = _gelu(u_ref[0])
    z = _gelu(z_ref[0])
    z = z * lax.rsqrt(jnp.mean(z * z, axis=-1, keepdims=True) + RMS_EPS) * ng_ref[...]
    zb = z.astype(BF16)
    row = lax.broadcasted_iota(jnp.int32, (SGU_CHUNK, SGU_CHUNK), 0)
    col = lax.broadcasted_iota(jnp.int32, (SGU_CHUNK, SGU_CHUNK), 1)
    causal = col <= row
    for g in range(N_GROUPS_B):
        w_g = jnp.where(causal, sw_ref[g], 0.0).astype(BF16)
        for cc in range(ts // SGU_CHUNK):
            rs = slice(cc * SGU_CHUNK, (cc + 1) * SGU_CHUNK)
            ls = slice(g * HEAD_DIM, (g + 1) * HEAD_DIM)
            mix[rs, ls] = jnp.dot(w_g, zb[rs, ls], preferred_element_type=F32)
    bias = jnp.concatenate([sb_ref[...]] * (ts // SGU_CHUNK), axis=0)
    o_b = u * (mix[...] + bias)
    y = jnp.dot(oa_ref[0].astype(BF16), wo_ref[0:WIDTH_A, :], preferred_element_type=F32)
    y = y + jnp.dot(o_b.astype(BF16), wo_ref[WIDTH_A:, :], preferred_element_type=F32)
    o_ref[0] = x_ref[0] + mod_ref[0, 2:3, :] * y


def _sgu_out_call(proj, o_a, x, mod, norm_g, sgu_w, sgu_bias_tile, w_out, ts=1024):
    batch, seq, _ = x.shape
    u_blk = 3 * WIDTH_A // WIDTH_B
    full = lambda shape: pl.BlockSpec(shape, lambda b, i: (0,) * len(shape))
    return pl.pallas_call(
        _sgu_out_body,
        grid=(batch, seq // ts),
        in_specs=[
            pl.BlockSpec((1, ts, WIDTH_B), lambda b, i: (b, i, u_blk)),
            pl.BlockSpec((1, ts, WIDTH_B), lambda b, i: (b, i, u_blk + 1)),
            pl.BlockSpec((1, ts, WIDTH_A), lambda b, i: (b, i, 0)),
            pl.BlockSpec((1, ts, D_MODEL), lambda b, i: (b, i, 0)),
            pl.BlockSpec((1, 3, D_MODEL), lambda b, i: (b, 0, 0)),
            full((1, WIDTH_B)),
            full((N_GROUPS_B, SGU_CHUNK, SGU_CHUNK)),
            full((SGU_CHUNK, WIDTH_B)),
            full((WIDTH_A + WIDTH_B, D_MODEL)),
        ],
        out_specs=pl.BlockSpec((1, ts, D_MODEL), lambda b, i: (b, i, 0)),
        out_shape=jax.ShapeDtypeStruct(x.shape, F32),
        scratch_shapes=[pltpu.VMEM((ts, WIDTH_B), F32)],
        compiler_params=_params("arbitrary", "arbitrary"),
        name="sgu_out_proj",
    )(proj, proj, o_a, x, mod, norm_g, sgu_w, sgu_bias_tile, w_out)


def _rwkv_pre_body(has_vres, *refs):
    if has_vres:
        (x_ref, mod_ref, mu_ref, wr_ref, wk_ref, wv_ref, w0_ref, w1_ref, w2_ref, a0_ref, a1_ref, a2_ref,
         g1_ref, g2_ref, kk_ref, ka_ref, vf_ref, v0_ref, v1_ref, v2_ref,
         r_out, w_out, k_out, v_out, kk_out, a_out, g_out, carry) = refs
    else:
        (x_ref, mod_ref, mu_ref, wr_ref, wk_ref, wv_ref, w0_ref, w1_ref, w2_ref, a0_ref, a1_ref, a2_ref,
         g1_ref, g2_ref, kk_ref, ka_ref,
         r_out, w_out, k_out, v_out, kk_out, a_out, g_out, carry) = refs

    @pl.when(pl.program_id(1) == 0)
    def _():
        carry[...] = jnp.zeros_like(carry)

    h = _norm_mod(x_ref[0], mod_ref)
    ts = h.shape[0]
    first = lax.broadcasted_iota(jnp.int32, h.shape, 0) == 0
    h_prev = jnp.where(first, carry[0:1, :], pltpu.roll(h, 1, 0))
    carry[0:1, :] = h[ts - 1:ts, :]
    xx = h_prev - h

    def mixed(i):
        return (h + xx * mu_ref[i:i + 1, :]).astype(BF16)

    def mm(a, w_ref):
        return jnp.dot(a, w_ref[...], preferred_element_type=F32)

    xr, xw, xk, xv, xa, xg = [mixed(i) for i in range(6)]
    w_low = mm(xw, w1_ref)
    a_low = mm(xa, a1_ref)
    g_low = mm(xg, g1_ref)
    r = mm(xr, wr_ref)
    k = mm(xk, wk_ref)
    v = mm(xv, wv_ref)
    v_low = mm(xv, v1_ref) if has_vres else None
    z = w0_ref[...] + mm(jnp.tanh(w_low).astype(BF16), w2_ref)
    w_log = -(jnp.maximum(-z, 0.0) + jnp.log(1.0 + jnp.exp(-jnp.abs(z)))) - 0.5
    if has_vres:
        mix_v = jax.nn.sigmoid(v0_ref[...] + mm(v_low.astype(BF16), v2_ref))
        v = v + (vf_ref[0].astype(F32) - v) * mix_v
    a = jax.nn.sigmoid(a0_ref[...] + mm(a_low.astype(BF16), a2_ref))
    g = mm(jax.nn.sigmoid(g_low).astype(BF16), g2_ref)
    r_out[0] = r.astype(BF16)
    w_out[0] = w_log
    kk_out[0] = (k * kk_ref[...]).astype(BF16)
    k_out[0] = (k * (1.0 + (a - 1.0) * ka_ref[...])).astype(BF16)
    v_out[0] = v.astype(BF16)
    a_out[0] = a.astype(BF16)
    g_out[0] = g.astype(BF16)


def _rwkv_pre_call(x, mod, weights, v_first, vres, ts=512):
    batch, seq, _ = x.shape
    tok = pl.BlockSpec((1, ts, D_MODEL), lambda b, i: (b, i, 0))
    full = lambda a: pl.BlockSpec(a.shape, lambda b, i: (0,) * a.ndim, pipeline_mode=pl.Buffered(1))
    ins = [x, mod] + list(weights)
    specs = [tok, pl.BlockSpec((1, 3, D_MODEL), lambda b, i: (b, 0, 0))] + [full(a) for a in weights]
    if vres is not None:
        ins += [v_first] + list(vres)
        specs += [tok] + [full(a) for a in vres]
    out_dtypes = [BF16, F32, BF16, BF16, BF16, BF16, BF16]
    return pl.pallas_call(
        functools.partial(_rwkv_pre_body, vres is not None),
        grid=(batch, seq // ts),
        in_specs=specs,
        out_specs=[tok] * 7,
        out_shape=[jax.ShapeDtypeStruct(x.shape, dt) for dt in out_dtypes],
        scratch_shapes=[pltpu.VMEM((8, D_MODEL), F32)],
        compiler_params=_params("arbitrary", "arbitrary"),
        name="rwkv_pre",
    )(*ins)


def _wkv_body(r_ref, w_ref, k_ref, v_ref, kk_ref, a_ref, lnw_ref, lnb_ref, rk_ref, o_ref, state):
    C = WKV_CHUNK
    W = LANES
    n_chunk = r_ref.shape[1] // C
    n_pair = r_ref.shape[2] // W

    @pl.when(pl.program_id(2) == 0)
    def _():
        state[...] = jnp.zeros_like(state)

    def iota(shape, dim):
        return lax.broadcasted_iota(jnp.int32, shape, dim)

    def stack_heads(x):
        own = iota(x.shape, 1) % W < HEAD_DIM
        return jnp.concatenate([jnp.where(own, x, 0.0), jnp.where(own, 0.0, x)], axis=0).astype(BF16)

    def block_diag(x):
        left = iota(x.shape, 1) < C
        return jnp.concatenate([jnp.where(left, x, 0.0), jnp.where(left, 0.0, x)], axis=0).astype(BF16)

    row4, col4 = iota((C, 4 * C), 0), iota((C, 4 * C), 1) % C
    strict4 = col4 < row4
    lower4 = col4 <= row4
    tri = jnp.where(iota((C, C), 1) <= iota((C, C), 0), 1.0, 0.0).astype(BF16)
    eye2 = jnp.where(iota((C, 2 * C), 1) % C == iota((C, 2 * C), 0), 1.0, 0.0).astype(F32)
    rw, cw = iota((W, W), 0), iota((W, W), 1)
    same_head = (rw < HEAD_DIM) == (cw < HEAD_DIM)
    ones_bd = jnp.where(same_head, 1.0, 0.0).astype(BF16)
    diag_w = cw == rw
    zeros_cw = jnp.zeros((C, W), F32)

    def bdot(a, b):
        return jnp.dot(a.astype(BF16), b.astype(BF16), preferred_element_type=F32)

    chunks = range(n_chunk)
    pairs = range(n_pair)
    units = [(c, p) for c in chunks for p in pairs]
    rows = lambda t, c: t[c * C:(c + 1) * C]
    blk = lambda t, u: t[u[0] * C:(u[0] + 1) * C, u[1] * W:(u[1] + 1) * W]
    lanes = lambda t, p: t[:, p * W:(p + 1) * W]
    per_chunk = lambda vals: jnp.concatenate([jnp.broadcast_to(t, (C, n_pair * W)) for t in vals], axis=0)
    head_sums = lambda t: jnp.concatenate([bdot(lanes(t, p), ones_bd) for p in pairs], axis=1)

    r = r_ref[0].astype(F32)
    k = k_ref[0].astype(F32)
    v = v_ref[0].astype(F32)
    kk = kk_ref[0].astype(F32)
    kk = kk / jnp.maximum(jnp.sqrt(head_sums(kk * kk)), 1e-12)
    b = kk * a_ref[0].astype(F32)
    bonus = head_sums(r * k * rk_ref[...]) * v
    log_w = -jnp.exp(w_ref[0])
    hi = log_w.astype(BF16)
    lo = (log_w - hi.astype(F32)).astype(BF16)
    hilo = jnp.concatenate([hi, lo], axis=1)
    cum2 = [jnp.dot(tri, rows(hilo, c), preferred_element_type=F32) for c in chunks]
    cum = jnp.concatenate([c2[:, :n_pair * W] + c2[:, n_pair * W:] for c2 in cum2], axis=0)
    c_mid = [cum[c * C + C // 2 - 1:c * C + C // 2, :] for c in chunks]
    c_end = [cum[c * C + C - 1:(c + 1) * C, :] for c in chunks]
    g_last = [jnp.exp(t) for t in c_end]
    from_mid = jnp.exp(per_chunk(c_mid) - cum)
    to_mid = per_chunk([jnp.exp(-t) for t in c_mid])
    a_true = -kk * jnp.exp(cum - log_w)
    r_true = r * jnp.exp(cum)
    a_mid = a_true * to_mid
    r_mid = r_true * to_mid
    b_mid = b * from_mid
    k_mid = k * from_mid
    ar_mid = [jnp.concatenate([blk(a_mid, u), blk(r_mid, u)], axis=0).astype(BF16) for u in units]
    gram = [lax.dot_general(ar_mid[i], jnp.concatenate([stack_heads(blk(b_mid, u)), stack_heads(blk(k_mid, u))],
                                                        axis=0), NT_DIMS, preferred_element_type=F32)
            for i, u in enumerate(units)]
    top = [jnp.where(strict4, g[:C], 0.0) for g in gram]
    bot = [jnp.where(lower4, g[C:], 0.0).astype(BF16) for g in gram]
    n1 = [t[:, :2 * C] for t in top]
    inv = [eye2 + t for t in n1]
    pw = [bdot(t, block_diag(t)) for t in n1]
    levels = C.bit_length() - 2
    for lvl in range(levels):
        if lvl < levels - 1:
            st = [bdot(jnp.concatenate([pw[i], inv[i]], axis=0), block_diag(pw[i])) for i in range(len(units))]
            pw = [t[:C] for t in st]
            inv = [inv[i] + st[i][C:] for i in range(len(units))]
        else:
            inv = [inv[i] + bdot(inv[i], block_diag(pw[i])) for i in range(len(units))]
    x = [jnp.dot(top[i][:, 2 * C:].astype(BF16), stack_heads(blk(v, u)), preferred_element_type=F32)
         for i, u in enumerate(units)]
    pp = [jnp.dot(inv[i].astype(BF16), stack_heads(jnp.concatenate([blk(a_true, u), x[i]], axis=1)),
                  preferred_element_type=F32) for i, u in enumerate(units)]
    zv = [jnp.concatenate([zeros_cw, blk(v, u)], axis=1) for u in units]
    qq = [jnp.dot(bot[i], jnp.concatenate([stack_heads(pp[i]), stack_heads(zv[i])], axis=0),
                  preferred_element_type=F32) for i in range(len(units))]
    end_from_mid = [jnp.exp(c_end[c] - c_mid[c]) for c in chunks]
    bk_hat = [jnp.concatenate([blk(b_mid, u), blk(k_mid, u)], axis=0) * lanes(end_from_mid[u[0]], u[1])
              for u in units]
    mm = [lax.dot_general(bk_hat[i].astype(BF16), jnp.concatenate([pp[i], zv[i]], axis=0).astype(BF16), TN_DIMS,
                          preferred_element_type=F32) for i in range(len(units))]
    q1 = [blk(r_true, u) + qq[i][:, :W] for i, u in enumerate(units)]
    q2 = [t[:, W:] for t in qq]
    m1 = [(jnp.where(same_head, mm[i][:, :W], 0.0)
           + jnp.where(diag_w, jnp.broadcast_to(lanes(g_last[u[0]], u[1]), (W, W)), 0.0)).astype(BF16)
          for i, u in enumerate(units)]
    m2 = [jnp.where(same_head, t[:, W:], 0.0) for t in mm]
    h_cur = [state[p] for p in pairs]
    h_in = []
    for i, (c, p) in enumerate(units):
        h_in.append(h_cur[p])
        h_cur[p] = jnp.dot(m1[i], h_cur[p].astype(BF16), preferred_element_type=F32) + m2[i]
    for p in pairs:
        state[p] = h_cur[p]
    y_blk = [bdot(q1[i], h_in[i]) + q2[i] for i in range(len(units))]
    y = jnp.concatenate([jnp.concatenate([y_blk[c * n_pair + p] for p in pairs], axis=1) for c in chunks], axis=0)
    mean = head_sums(y) * (1.0 / HEAD_DIM)
    yc = y - mean
    var = head_sums(yc * yc) * (1.0 / HEAD_DIM)
    yn = yc * lax.rsqrt(var + RWKV_GN_EPS) * lnw_ref[...] + lnb_ref[...]
    o_ref[0] = (yn + bonus).astype(BF16)


def _wkv_call(r, w_log, k, v, kk, a, ln_w, ln_b, r_k, tc=256, pairs_per_step=8):
    batch, seq, _ = r.shape
    width = pairs_per_step * LANES
    tok = pl.BlockSpec((1, tc, width), lambda b, h, t: (b, t, h))
    vec = pl.BlockSpec((1, width), lambda b, h, t: (0, h))
    return pl.pallas_call(
        _wkv_body,
        grid=(batch, D_MODEL // width, seq // tc),
        in_specs=[tok] * 6 + [vec] * 3,
        out_specs=tok,
        out_shape=jax.ShapeDtypeStruct(r.shape, BF16),
        scratch_shapes=[pltpu.VMEM((pairs_per_step, LANES, LANES), F32)],
        compiler_params=_params("arbitrary", "arbitrary", "arbitrary"),
        name="wkv7_chunked",
    )(r, w_log, k, v, kk, a, ln_w, ln_b, r_k)


def _rwkv_post_body(y_ref, g_ref, x_ref, mod_ref, wo_ref, o_ref):
    yg = y_ref[0] * g_ref[0]
    o_ref[0] = x_ref[0] + mod_ref[0, 2:3, :] * jnp.dot(yg, wo_ref[...], preferred_element_type=F32)


def _rwkv_post_call(y, g, x, mod, w_o, ts=1024):
    batch, seq, _ = x.shape
    tok = pl.BlockSpec((1, ts, D_MODEL), lambda b, i: (b, i, 0))
    return pl.pallas_call(
        _rwkv_post_body,
        grid=(batch, seq // ts),
        in_specs=[tok, tok, tok, pl.BlockSpec((1, 3, D_MODEL), lambda b, i: (b, 0, 0)),
                  pl.BlockSpec((D_MODEL, D_MODEL), lambda b, i: (0, 0))],
        out_specs=tok,
        out_shape=jax.ShapeDtypeStruct(x.shape, F32),
        compiler_params=_params("arbitrary", "arbitrary"),
        name="rwkv_out_proj",
    )(y, g, x, mod, w_o)


MOE_TILE = 512
MOE_BLOCK = 64
MOE_SLOTS = MOE_TILE + N_EXPERT_GROUPS * MOE_BLOCK
MOE_BLOCKS_PER_TILE = MOE_SLOTS // MOE_BLOCK
MOE_STEP_BLOCKS = 16


def _route(logits_t, rb):
    n_g, epg = N_EXPERT_GROUPS, EXPERTS_PER_GROUP
    s = jax.nn.sigmoid(logits_t)
    sel = s + rb
    s_rows = [s[i:i + 1, :] for i in range(N_EXPERTS)]
    sel_rows = [sel[i:i + 1, :] for i in range(N_EXPERTS)]
    scores = []
    for g in range(n_g):
        members = sel_rows[g * epg:(g + 1) * epg]
        best = None
        for i in range(epg):
            for j in range(i + 1, epg):
                pair = members[i] + members[j]
                best = pair if best is None else jnp.maximum(best, pair)
        scores.append(best)
    g_idx = jnp.zeros_like(scores[0], dtype=jnp.int32)
    top = scores[0]
    for g in range(1, n_g):
        better = scores[g] > top
        g_idx = jnp.where(better, g, g_idx)
        top = jnp.where(better, scores[g], top)

    def pick(rows_, j):
        out = rows_[j]
        for g in range(1, n_g):
            out = jnp.where(g_idx == g, rows_[g * epg + j], out)
        return out

    in_sel = [pick(sel_rows, j) for j in range(epg)]
    in_s = [pick(s_rows, j) for j in range(epg)]
    chosen = []
    for j in range(epg):
        rank = jnp.zeros_like(g_idx)
        for i in range(epg):
            if i == j:
                continue
            ahead = (in_sel[i] >= in_sel[j]) if i < j else (in_sel[i] > in_sel[j])
            rank = rank + jnp.where(ahead, 1, 0)
        chosen.append(rank < 2)
    den = sum(jnp.where(chosen[j], in_s[j], 0.0) for j in range(epg))
    gate_rows = []
    for ex in range(N_EXPERTS):
        g, j = divmod(ex, epg)
        on = jnp.logical_and(chosen[j], g_idx == g)
        gate_rows.append(jnp.where(on, in_s[j] / den, 0.0))
    return g_idx, gate_rows


def _slot_one_hot(pos_row):
    slot = lax.broadcasted_iota(jnp.int32, (MOE_SLOTS, pos_row.shape[1]), 0)
    return jnp.where(slot == pos_row, 1.0, 0.0).astype(BF16)


def _moe_sort_body(x_ref, mod_ref, rw_ref, rb_ref, hs_ref, gs_ref, pos_ref, cnt_ref):
    tm = MOE_TILE
    n_g = N_EXPERT_GROUPS
    tiles = range(x_ref.shape[1] // tm)
    h = [_norm_mod(x_ref[0, i * tm:(i + 1) * tm, :], mod_ref) for i in tiles]
    hi = [t.astype(BF16) for t in h]
    lo = [(h[i] - hi[i].astype(F32)).astype(BF16) for i in tiles]
    parts = [jnp.dot(jnp.concatenate([hi[i], lo[i]], axis=0), rw_ref[...], preferred_element_type=F32)
             for i in tiles]
    logits = [(p[:tm, :LANES] + p[:tm, LANES:]) + (p[tm:, :LANES] + p[tm:, LANES:]) for p in parts]
    routed = [_route(t.T[:N_EXPERTS], rb_ref[...]) for t in logits]
    member = [[jnp.where(routed[i][0] == g, 1.0, 0.0) for g in range(n_g)] for i in tiles]
    earlier = jnp.where(lax.broadcasted_iota(jnp.int32, (tm, tm), 0) < lax.broadcasted_iota(jnp.int32, (tm, tm), 1),
                        1.0, 0.0).astype(BF16)
    rank = [jnp.dot(jnp.concatenate(member[i] + [jnp.zeros((8 - n_g, tm), F32)], axis=0).astype(BF16), earlier,
                    preferred_element_type=F32) for i in tiles]
    lane = lax.broadcasted_iota(jnp.int32, (1, LANES), 1)
    one_hot = []
    for i in tiles:
        pos = jnp.zeros((1, tm), F32)
        offset = jnp.zeros((1, 1), F32)
        counts = jnp.zeros((1, LANES), F32)
        for g in range(n_g):
            n_members = jnp.sum(member[i][g], axis=1, keepdims=True)
            n_blocks = jnp.floor((n_members + (MOE_BLOCK - 1)) * (1.0 / MOE_BLOCK))
            pos = pos + member[i][g] * (offset + rank[i][g:g + 1, :])
            counts = counts + jnp.where(lane == g, n_blocks, 0.0)
            offset = offset + n_blocks * MOE_BLOCK
        pos_i = pos.astype(jnp.int32)
        pos_ref[i] = pos_i
        cnt_ref[i] = counts.astype(jnp.int32)
        one_hot.append(_slot_one_hot(pos_i))
    pad = jnp.zeros((LANES - N_EXPERTS, tm), F32)
    gate_t = [jnp.concatenate(routed[i][1] + [pad], axis=0).T for i in tiles]
    g_hi = [t.astype(BF16) for t in gate_t]
    g_lo = [(gate_t[i] - g_hi[i].astype(F32)).astype(BF16) for i in tiles]
    hs = [jnp.dot(one_hot[i], hi[i], preferred_element_type=F32) for i in tiles]
    gs = [jnp.dot(one_hot[i], jnp.concatenate([g_hi[i], g_lo[i]], axis=1), preferred_element_type=F32)
          for i in tiles]
    for i in tiles:
        hs_ref[i * MOE_SLOTS:(i + 1) * MOE_SLOTS, :] = hs[i].astype(BF16)
        gs_ref[i * MOE_SLOTS:(i + 1) * MOE_SLOTS, :] = gs[i].astype(BF16)


def _moe_sort_call(x, mod, router_w2, router_b, tiles_per_step=4):
    batch, seq, _ = x.shape
    rows = tiles_per_step * MOE_TILE
    per_seq = seq // rows
    n_tiles = batch * seq // MOE_TILE
    step = lambda b, i: b * per_seq + i
    return pl.pallas_call(
        _moe_sort_body,
        grid=(batch, per_seq),
        in_specs=[
            pl.BlockSpec((1, rows, D_MODEL), lambda b, i: (b, i, 0)),
            pl.BlockSpec((1, 3, D_MODEL), lambda b, i: (b, 0, 0)),
            pl.BlockSpec((D_MODEL, 2 * LANES), lambda b, i: (0, 0)),
            pl.BlockSpec((N_EXPERTS, 1), lambda b, i: (0, 0)),
        ],
        out_specs=[
            pl.BlockSpec((tiles_per_step * MOE_SLOTS, D_MODEL), lambda b, i: (step(b, i), 0)),
            pl.BlockSpec((tiles_per_step * MOE_SLOTS, 2 * LANES), lambda b, i: (step(b, i), 0)),
            pl.BlockSpec((tiles_per_step, 1, MOE_TILE), lambda b, i: (step(b, i), 0, 0)),
            pl.BlockSpec((tiles_per_step, 1, LANES), lambda b, i: (step(b, i), 0, 0)),
        ],
        out_shape=[
            jax.ShapeDtypeStruct((n_tiles * MOE_SLOTS, D_MODEL), BF16),
            jax.ShapeDtypeStruct((n_tiles * MOE_SLOTS, 2 * LANES), BF16),
            jax.ShapeDtypeStruct((n_tiles, 1, MOE_TILE), jnp.int32),
            jax.ShapeDtypeStruct((n_tiles, 1, LANES), jnp.int32),
        ],
        compiler_params=_params("arbitrary", "arbitrary"),
        name="moe_sort",
    )(x, mod, router_w2, router_b)


def _moe_tables_body(cnt_ref, sg_ref, sv_ref, sb_ref, bp_ref):
    n_tiles = cnt_ref.shape[0]
    n_g, r, bpt = N_EXPERT_GROUPS, MOE_STEP_BLOCKS, MOE_BLOCKS_PER_TILE
    n_steps = sg_ref.shape[0]

    def fill(ref, n, value, unroll=4):
        def body(i, carry):
            for u in range(unroll):
                ref[i * unroll + u] = value
            return carry
        lax.fori_loop(0, n // unroll, body, 0)
        for i in range(n - n % unroll, n):
            ref[i] = value

    fill(sb_ref, n_steps * r, 0)
    fill(bp_ref, n_tiles * bpt, 0)
    fill(sg_ref, n_steps, n_g - 1)
    fill(sv_ref, n_steps, 0)
    slot = jnp.int32(0)
    for g in range(n_g):
        def tile_body(i, k, g=g):
            first = i * bpt
            for g_before in range(g):
                first = first + cnt_ref[i, g_before]

            def block_body(j, k):
                sb_ref[k] = first + j
                bp_ref[first + j] = k
                return k + 1
            return lax.fori_loop(0, cnt_ref[i, g], block_body, k)

        slot_end = lax.fori_loop(0, n_tiles, tile_body, slot)
        step_first = slot // r
        step_stop = (slot_end + r - 1) // r

        def step_body(s, carry, g=g, slot_end=slot_end):
            sg_ref[s] = g
            sv_ref[s] = jnp.minimum(slot_end - s * r, r)
            return carry
        lax.fori_loop(step_first, step_stop, step_body, 0)
        slot = step_stop * r


def _moe_step_tables(counts):
    n_tiles = counts.shape[0]
    n_blocks = n_tiles * MOE_BLOCKS_PER_TILE
    n_steps = n_blocks // MOE_STEP_BLOCKS + N_EXPERT_GROUPS
    smem = pl.BlockSpec(memory_space=pltpu.SMEM)
    return pl.pallas_call(
        _moe_tables_body,
        in_specs=[smem],
        out_specs=[smem] * 4,
        out_shape=[jax.ShapeDtypeStruct((n_steps,), jnp.int32), jax.ShapeDtypeStruct((n_steps,), jnp.int32),
                   jax.ShapeDtypeStruct((n_steps * MOE_STEP_BLOCKS,), jnp.int32),
                   jax.ShapeDtypeStruct((n_blocks,), jnp.int32)],
        name="moe_tables",
    )(counts)


def _moe_expert_body(sg_ref, sv_ref, sb_ref, *refs):
    r = MOE_STEP_BLOCKS
    h_refs, g_refs = refs[:r], refs[r:2 * r]
    w1_ref, w3_ref, w2_ref, o_ref, w13_scr, w2_scr, act_scr = refs[2 * r:]
    step = pl.program_id(0)
    epg = EXPERTS_PER_GROUP
    rows_total = r * MOE_BLOCK
    group = sg_ref[step]
    new_group = jnp.logical_or(step == 0, group != sg_ref[jnp.maximum(step - 1, 0)])

    @pl.when(sv_ref[step] == 0)
    def _():
        o_ref[...] = jnp.zeros_like(o_ref)

    @pl.when(jnp.logical_and(sv_ref[step] > 0, new_group))
    def _():
        w13_scr[:, :, :D_EXPERT] = w1_ref[0].astype(BF16)
        w13_scr[:, :, D_EXPERT:] = w3_ref[0].astype(BF16)
        w2_scr[...] = w2_ref[0].reshape(epg * D_EXPERT, D_MODEL).astype(BF16)

    @pl.when(sv_ref[step] > 0)
    def _():
        h = jnp.concatenate([ref[...] for ref in h_refs], axis=0)
        gates = jnp.concatenate([ref[...] for ref in g_refs], axis=0)
        real = lax.broadcasted_iota(jnp.int32, (rows_total, LANES), 0) < sv_ref[step] * MOE_BLOCK
        expert_row = lax.broadcasted_iota(jnp.int32, (2 * LANES, epg * LANES), 0) % LANES
        lane_block = lax.broadcasted_iota(jnp.int32, (2 * LANES, epg * LANES), 1) // LANES
        pick_rows = jnp.where(expert_row == group * epg + lane_block, 1.0, 0.0).astype(BF16)
        gate_all = jnp.dot(gates, pick_rows, preferred_element_type=F32)
        for j in range(epg):
            gate_b = jnp.where(real, gate_all[:, j * LANES:(j + 1) * LANES], 0.0)
            hid = jnp.dot(h, w13_scr[j], preferred_element_type=F32)
            h1 = hid[:, :D_EXPERT]
            act = h1 * jax.nn.sigmoid(h1) * hid[:, D_EXPERT:] * jnp.concatenate([gate_b, gate_b], axis=1)
            act_scr[:, j * D_EXPERT:(j + 1) * D_EXPERT] = act.astype(BF16)
        o_ref[...] = jnp.dot(act_scr[...], w2_scr[...], preferred_element_type=F32).astype(BF16)


def _moe_expert_call(hs, gs, w1, w3, w2, step_group, step_valid, step_blocks, layer):
    r = MOE_STEP_BLOCKS
    n_steps = step_group.shape[0]
    epg = EXPERTS_PER_GROUP
    rows_total = r * MOE_BLOCK

    def blk(width, j):
        return pl.BlockSpec((MOE_BLOCK, width), lambda s, sg, sv, sb, j=j: (sb[s * r + j], 0))

    def group_weights(shape):
        return pl.BlockSpec((1, epg) + shape, lambda s, sg, sv, sb: (layer, sg[s], 0, 0))

    grid_spec = pltpu.PrefetchScalarGridSpec(
        num_scalar_prefetch=3,
        grid=(n_steps,),
        in_specs=[blk(D_MODEL, j) for j in range(r)] + [blk(2 * LANES, j) for j in range(r)] + [
            group_weights((D_MODEL, D_EXPERT)), group_weights((D_MODEL, D_EXPERT)),
            group_weights((D_EXPERT, D_MODEL))],
        out_specs=pl.BlockSpec((rows_total, D_MODEL), lambda s, sg, sv, sb: (s, 0)),
        scratch_shapes=[
            pltpu.VMEM((epg, D_MODEL, 2 * D_EXPERT), BF16),
            pltpu.VMEM((epg * D_EXPERT, D_MODEL), BF16),
            pltpu.VMEM((rows_total, epg * D_EXPERT), BF16),
        ],
    )
    return pl.pallas_call(
        _moe_expert_body,
        grid_spec=grid_spec,
        out_shape=jax.ShapeDtypeStruct((n_steps * rows_total, D_MODEL), BF16),
        compiler_params=_params("arbitrary"),
        name="moe_experts",
    )(step_group, step_valid, step_blocks, *([hs] * r), *([gs] * r), w1, w3, w2)


def _moe_unsort_body(is_last, bp_ref, *refs):
    bpt = MOE_BLOCKS_PER_TILE
    pos_ref, x_ref, mod_ref, fg_ref, o_ref = refs[-5:]
    y_refs = refs[:-5]
    tm = MOE_TILE
    tiles = range(len(y_refs) // bpt)
    slot = lax.broadcasted_iota(jnp.int32, (tm, MOE_SLOTS), 1)
    one_hot = []
    for i in tiles:
        pos_col = jnp.broadcast_to(pos_ref[i].astype(F32), (LANES, tm)).T.astype(jnp.int32)
        pos_col = jnp.concatenate([pos_col] * (MOE_SLOTS // LANES), axis=1)
        one_hot.append(jnp.where(slot == pos_col, 1.0, 0.0).astype(BF16))
    ys = [jnp.concatenate([ref[...] for ref in y_refs[i * bpt:(i + 1) * bpt]], axis=0) for i in tiles]
    y = [jnp.dot(one_hot[i], ys[i], preferred_element_type=F32) for i in tiles]
    for i in tiles:
        out = x_ref[0, i * tm:(i + 1) * tm, :] + mod_ref[0, 2:3, :] * y[i]
        if is_last:
            out = out * lax.rsqrt(jnp.mean(out * out, axis=-1, keepdims=True) + RMS_EPS) * fg_ref[...]
        o_ref[0, i * tm:(i + 1) * tm, :] = out


def _moe_unsort_call(ys, pos, x, mod, final_g, block_pos, is_last, tiles_per_step=4):
    batch, seq, _ = x.shape
    rows = tiles_per_step * MOE_TILE
    per_seq = seq // rows
    bpt = MOE_BLOCKS_PER_TILE
    n_in = tiles_per_step * bpt

    def blk(j):
        return pl.BlockSpec((MOE_BLOCK, D_MODEL), lambda b, i, bp, j=j: (bp[(b * per_seq + i) * n_in + j], 0))

    grid_spec = pltpu.PrefetchScalarGridSpec(
        num_scalar_prefetch=1,
        grid=(batch, per_seq),
        in_specs=[blk(j) for j in range(n_in)] + [
            pl.BlockSpec((tiles_per_step, 1, MOE_TILE), lambda b, i, bp: (b * per_seq + i, 0, 0)),
            pl.BlockSpec((1, rows, D_MODEL), lambda b, i, bp: (b, i, 0)),
            pl.BlockSpec((1, 3, D_MODEL), lambda b, i, bp: (b, 0, 0)),
            pl.BlockSpec((1, D_MODEL), lambda b, i, bp: (0, 0)),
        ],
        out_specs=pl.BlockSpec((1, rows, D_MODEL), lambda b, i, bp: (b, i, 0)),
    )
    return pl.pallas_call(
        functools.partial(_moe_unsort_body, is_last),
        grid_spec=grid_spec,
        out_shape=jax.ShapeDtypeStruct(x.shape, F32),
        compiler_params=_params("arbitrary", "arbitrary"),
        name="moe_unsort",
    )(block_pos, *([ys] * n_in), pos, x, mod, final_g)


def _moe_layer(x, mod, router_w2, router_b, w1, w3, w2, final_g, layer, is_last):
    hs, gs, pos, cnt = _moe_sort_call(x, mod, router_w2, router_b)
    step_group, step_valid, step_blocks, block_pos = _moe_step_tables(cnt[:, 0, :])
    ys = _moe_expert_call(hs, gs, w1, w3, w2, step_group, step_valid, step_blocks, layer)
    return _moe_unsort_call(ys, pos, x, mod, final_g, block_pos, is_last)


def _pad_cols(w):
    n = w.shape[-1]
    return jnp.pad(w, ((0, 0), (0, -n % LANES)))


def _pad_rows(w):
    n = w.shape[0]
    return jnp.pad(w, ((0, -n % LANES), (0, 0)))


def kernel(x, c, ada_w, ada_b, mix_w_in, mix_w_out, sgu_norm_g, sgu_w, sgu_b, rwkv_mu, rwkv_w_rkv, rwkv_w_o, rwkv_w0, rwkv_w1, rwkv_w2, rwkv_a0, rwkv_a1, rwkv_a2, rwkv_v0, rwkv_v1, rwkv_v2, rwkv_g1, rwkv_g2, rwkv_k_k, rwkv_k_a, rwkv_r_k, rwkv_ln_w, rwkv_ln_b, router_w, router_b, moe_w1, moe_w3, moe_w2, final_norm_g):
    depth = ada_w.shape[0]
    row = lambda t: t.reshape(1, -1)
    mods = _ada_call(c, ada_w, ada_b)
    rw_hi = router_w.astype(BF16)
    rw_lo = (router_w - rw_hi.astype(F32)).astype(BF16)
    router_w2 = jnp.concatenate([_pad_cols(rw_hi), _pad_cols(rw_lo)], axis=1)
    router_bc = router_b.reshape(N_EXPERTS, 1)
    final_g = row(final_norm_g)
    v_first = None
    for layer in range(depth):
        i = layer // 2
        mod = mods[2 * layer]
        if layer % 2 == 0:
            proj = _proj_call(x, mod, mix_w_in[i].astype(BF16))
            o_a = _attn_call(proj)
            bias_tile = jnp.repeat(sgu_b[i].T, HEAD_DIM, axis=1)
            x = _sgu_out_call(proj, o_a, x, mod, row(sgu_norm_g[i]), sgu_w[i], bias_tile,
                              mix_w_out[i].astype(BF16))
        else:
            weights = [
                rwkv_mu[i],
                rwkv_w_rkv[i, 0].astype(BF16), rwkv_w_rkv[i, 1].astype(BF16), rwkv_w_rkv[i, 2].astype(BF16),
                row(rwkv_w0[i]), _pad_cols(rwkv_w1[i]).astype(BF16), _pad_rows(rwkv_w2[i]).astype(BF16),
                row(rwkv_a0[i]), _pad_cols(rwkv_a1[i]).astype(BF16), _pad_rows(rwkv_a2[i]).astype(BF16),
                _pad_cols(rwkv_g1[i]).astype(BF16), _pad_rows(rwkv_g2[i]).astype(BF16),
                row(rwkv_k_k[i]), row(rwkv_k_a[i]),
            ]
            vres = None
            if i > 0:
                vres = [row(rwkv_v0[i - 1]), _pad_cols(rwkv_v1[i - 1]).astype(BF16),
                        _pad_rows(rwkv_v2[i - 1]).astype(BF16)]
            r, w_log, k, v, kk, a, g = _rwkv_pre_call(x, mod, weights, v_first, vres)
            if i == 0:
                v_first = v
            y = _wkv_call(r, w_log, k, v, kk, a, row(rwkv_ln_w[i]), row(rwkv_ln_b[i]), row(rwkv_r_k[i]))
            x = _rwkv_post_call(y, g, x, mod, rwkv_w_o[i].astype(BF16))
        x = _moe_layer(x, mods[2 * layer + 1], router_w2, router_bc, moe_w1, moe_w3, moe_w2, final_g, layer,
                       layer == depth - 1)
    return x
```

```python
import functools

import jax
import jax.numpy as jnp
from jax import lax
from jax.experimental import pallas as pl
from jax.experimental.pallas import tpu as pltpu

F32 = jnp.float32
BF16 = jnp.bfloat16

D_MODEL = 1024
HEAD_DIM = 64
WIDTH_A = 512
WIDTH_B = 512
MIX_IN_WIDTH = 2560
ATTN_SPAN = 128
ATTN_DILATIONS = (1, 4, 16)
SGU_CHUNK = 128
N_GROUPS_B = 8
RWKV_HEADS = 16
RWKV_GN_EPS = 64e-5
WKV_CHUNK = 64
N_EXPERTS = 16
N_EXPERT_GROUPS = 4
EXPERTS_PER_GROUP = 4
D_EXPERT = 256
RMS_EPS = 1e-6
LANES = 128
VMEM_LIMIT = 56 * 1024 * 1024

NT_DIMS = (((1,), (1,)), ((), ()))
TN_DIMS = (((0,), (0,)), ((), ()))


def _params(*sem):
    return pltpu.CompilerParams(dimension_semantics=sem, vmem_limit_bytes=VMEM_LIMIT)


def _norm_mod(x, mod_ref):
    ms = jnp.mean(x * x, axis=-1, keepdims=True)
    return (x * lax.rsqrt(ms + RMS_EPS)) * (1.0 + mod_ref[0, 1:2, :]) + mod_ref[0, 0:1, :]


def _gelu(x):
    return 0.5 * x * (1.0 + lax.erf(x * 0.7071067811865476))


def _ada_body(c_ref, w_ref, b_ref, o_ref):
    c = c_ref[...]
    c_act = c * jax.nn.sigmoid(c)
    n = c.shape[0]
    c_hi = c_act.astype(BF16)
    c_lo = (c_act - c_hi.astype(F32)).astype(BF16)
    c2 = jnp.concatenate([c_hi, c_lo], axis=0)
    w = w_ref[0]
    w_hi = w.astype(BF16)
    w_lo = (w - w_hi.astype(F32)).astype(BF16)
    by_hi = jnp.dot(c2, w_hi, preferred_element_type=F32)
    by_lo = jnp.dot(c2, w_lo, preferred_element_type=F32)
    o_ref[0] = (by_hi[:n] + by_hi[n:]) + (by_lo[:n] + by_lo[n:]) + b_ref[0]


def _ada_call(c, ada_w, ada_b):
    n_pair = ada_w.shape[0] * ada_w.shape[1]
    batch = c.shape[0]
    w = ada_w.reshape(n_pair, D_MODEL, 3 * D_MODEL)
    b = ada_b.reshape(n_pair, 1, 3 * D_MODEL)
    out = pl.pallas_call(
        _ada_body,
        grid=(n_pair, 3),
        in_specs=[
            pl.BlockSpec((batch, D_MODEL), lambda p, j: (0, 0)),
            pl.BlockSpec((1, D_MODEL, D_MODEL), lambda p, j: (p, 0, j)),
            pl.BlockSpec((1, 1, D_MODEL), lambda p, j: (p, 0, j)),
        ],
        out_specs=pl.BlockSpec((1, batch, D_MODEL), lambda p, j: (p, 0, j)),
        out_shape=jax.ShapeDtypeStruct((n_pair, batch, 3 * D_MODEL), F32),
        compiler_params=_params("arbitrary", "arbitrary"),
        name="ada_mod",
    )(c, w, b)
    return out.reshape(n_pair, batch, 3, D_MODEL)


def _proj_body(x_ref, mod_ref, w_ref, o_ref):
    h = _norm_mod(x_ref[0], mod_ref)
    o_ref[0] = jnp.dot(h.astype(BF16), w_ref[...], preferred_element_type=F32)


def _proj_call(x, mod, w_in, ts=1024):
    batch, seq, _ = x.shape
    return pl.pallas_call(
        _proj_body,
        grid=(batch, seq // ts),
        in_specs=[
            pl.BlockSpec((1, ts, D_MODEL), lambda b, i: (b, i, 0)),
            pl.BlockSpec((1, 3, D_MODEL), lambda b, i: (b, 0, 0)),
            pl.BlockSpec((D_MODEL, MIX_IN_WIDTH), lambda b, i: (0, 0)),
        ],
        out_specs=pl.BlockSpec((1, ts, MIX_IN_WIDTH), lambda b, i: (b, i, 0)),
        out_shape=jax.ShapeDtypeStruct((batch, seq, MIX_IN_WIDTH), F32),
        compiler_params=_params("arbitrary", "arbitrary"),
        name="mix_in_proj",
    )(x, mod, w_in)


def _attn_body(q_ref, k_ref, v_ref, o_ref, ob, lb):
    span = ATTN_SPAN
    W = LANES
    block_group = 8

    def iota(shape, dim):
        return lax.broadcasted_iota(jnp.int32, shape, dim)

    head0 = iota((span, W), 1) < HEAD_DIM
    mine = (head0, jnp.logical_not(head0))
    row2, col2 = iota((span, 2 * span), 0), iota((span, 2 * span), 1)
    mask_prev_cur = jnp.where(col2 < span, col2 - row2, row2 - (col2 - span)) >= 0
    mask_cur = iota((span, span), 1) <= iota((span, span), 0)

    def rows(start, n_rows, dil):
        return pl.ds(start, n_rows) if dil == 1 else pl.ds(start, n_rows, stride=dil)

    def group(p, dil, q_starts, with_prev):
        blocks = range(len(q_starts))
        heads = range(2)
        qb, kcat, vaug, masks = [], [], [], []
        for j in blocks:
            n_keys = 2 * span if with_prev[j] else span
            k_start = q_starts[j] - span * dil if with_prev[j] else q_starts[j]
            qb.append(q_ref[0, rows(q_starts[j], span, dil), :] * (HEAD_DIM ** -0.5))
            kcat.append(k_ref[0, rows(k_start, n_keys, dil), :].astype(BF16))
            vb = v_ref[0, rows(k_start, n_keys, dil), :].astype(BF16)
            vaug.append(jnp.concatenate([vb, jnp.ones((n_keys, W), BF16)], axis=1))
            masks.append(mask_prev_cur if with_prev[j] else mask_cur)
        qh = [[jnp.where(mine[h], qb[j], 0.0).astype(BF16) for h in heads] for j in blocks]
        s = [[lax.dot_general(qh[j][h], kcat[j], NT_DIMS, preferred_element_type=F32) for h in heads] for j in blocks]
        s = [[jnp.where(masks[j], s[j][h], -jnp.inf) for h in heads] for j in blocks]
        m = [[jnp.max(s[j][h], axis=-1, keepdims=True) for h in heads] for j in blocks]
        pr = [[jnp.exp(s[j][h] - m[j][h]).astype(BF16) for h in heads] for j in blocks]
        ad = [[jnp.dot(pr[j][h], vaug[j], preferred_element_type=F32) for h in heads] for j in blocks]
        for j in blocks:
            den = jnp.where(head0, ad[j][0][:, W:], ad[j][1][:, W:])
            o = jnp.where(head0, ad[j][0][:, :W], ad[j][1][:, :W]) / den
            lse = jnp.where(head0, m[j][0], m[j][1]) + jnp.log(den)
            qi = rows(q_starts[j], span, dil)
            ob[p, qi, :] = o
            lb[p, qi, :] = lse

    seq = q_ref.shape[1]
    for p, dil in enumerate(ATTN_DILATIONS):
        n_blk = seq // dil // span
        blocks = [(r + n * span * dil, n > 0) for r in range(dil) for n in range(n_blk)]
        for g0 in range(0, len(blocks), block_group):
            members = blocks[g0:g0 + block_group]
            group(p, dil, [b[0] for b in members], [b[1] for b in members])

    l0, l1, l2 = lb[0], lb[1], lb[2]
    m = jnp.maximum(jnp.maximum(l0, l1), l2)
    w0, w1, w2 = jnp.exp(l0 - m), jnp.exp(l1 - m), jnp.exp(l2 - m)
    o_ref[0] = (w0 * ob[0] + w1 * ob[1] + w2 * ob[2]) / (w0 + w1 + w2)


def _attn_call(proj):
    batch, seq, _ = proj.shape
    n_pair = WIDTH_A // LANES
    blk = lambda off: pl.BlockSpec((1, seq, LANES), lambda b, h: (b, 0, off + h))
    return pl.pallas_call(
        _attn_body,
        grid=(batch, n_pair),
        in_specs=[blk(0), blk(n_pair), blk(2 * n_pair)],
        out_specs=pl.BlockSpec((1, seq, LANES), lambda b, h: (b, 0, h)),
        out_shape=jax.ShapeDtypeStruct((batch, seq, WIDTH_A), F32),
        scratch_shapes=[
            pltpu.VMEM((3, seq, LANES), F32),
            pltpu.VMEM((3, seq, LANES), F32),
        ],
        compiler_params=_params("arbitrary", "arbitrary"),
        name="dilated_attn",
    )(proj, proj, proj)


def _sgu_out_body(u_ref, z_ref, oa_ref, x_ref, mod_ref, ng_ref, sw_ref, sb_ref, wo_ref, o_ref, mix):
    ts = u_ref.shape[1]
    u = _gelu(u_ref[0])
    z = _gelu(z_ref[0])
    z = z * lax.rsqrt(jnp.mean(z * z, axis=-1, keepdims=True) + RMS_EPS) * ng_ref[...]
    zb = z.astype(BF16)
    row = lax.broadcasted_iota(jnp.int32, (SGU_CHUNK, SGU_CHUNK), 0)
    col = lax.broadcasted_iota(jnp.int32, (SGU_CHUNK, SGU_CHUNK), 1)
    causal = col <= row
    for g in range(N_GROUPS_B):
        w_g = jnp.where(causal, sw_ref[g], 0.0).astype(BF16)
        for cc in range(ts // SGU_CHUNK):
            rs = slice(cc * SGU_CHUNK, (cc + 1) * SGU_CHUNK)
            ls = slice(g * HEAD_DIM, (g + 1) * HEAD_DIM)
            mix[rs, ls] = jnp.dot(w_g, zb[rs, ls], preferred_element_type=F32)
    bias = jnp.concatenate([sb_ref[...]] * (ts // SGU_CHUNK), axis=0)
    o_b = u * (mix[...] + bias)
    y = jnp.dot(oa_ref[0].astype(BF16), wo_ref[0:WIDTH_A, :], preferred_element_type=F32)
    y = y + jnp.dot(o_b.astype(BF16), wo_ref[WIDTH_A:, :], preferred_element_type=F32)
    o_ref[0] = x_ref[0] + mod_ref[0, 2:3, :] * y


def _sgu_out_call(proj, o_a, x, mod, norm_g, sgu_w, sgu_bias_tile, w_out, ts=1024):
    batch, seq, _ = x.shape
    u_blk = 3 * WIDTH_A // WIDTH_B
    full = lambda shape: pl.BlockSpec(shape, lambda b, i: (0,) * len(shape))
    return pl.pallas_call(
        _sgu_out_body,
        grid=(batch, seq // ts),
        in_specs=[
            pl.BlockSpec((1, ts, WIDTH_B), lambda b, i: (b, i, u_blk)),
            pl.BlockSpec((1, ts, WIDTH_B), lambda b, i: (b, i, u_blk + 1)),
            pl.BlockSpec((1, ts, WIDTH_A), lambda b, i: (b, i, 0)),
            pl.BlockSpec((1, ts, D_MODEL), lambda b, i: (b, i, 0)),
            pl.BlockSpec((1, 3, D_MODEL), lambda b, i: (b, 0, 0)),
            full((1, WIDTH_B)),
            full((N_GROUPS_B, SGU_CHUNK, SGU_CHUNK)),
            full((SGU_CHUNK, WIDTH_B)),
            full((WIDTH_A + WIDTH_B, D_MODEL)),
        ],
        out_specs=pl.BlockSpec((1, ts, D_MODEL), lambda b, i: (b, i, 0)),
        out_shape=jax.ShapeDtypeStruct(x.shape, F32),
        scratch_shapes=[pltpu.VMEM((ts, WIDTH_B), F32)],
        compiler_params=_params("arbitrary", "arbitrary"),
        name="sgu_out_proj",
    )(proj, proj, o_a, x, mod, norm_g, sgu_w, sgu_bias_tile, w_out)


def _rwkv_pre_body(has_vres, *refs):
    if has_vres:
        (x_ref, mod_ref, mu_ref, wr_ref, wk_ref, wv_ref, w0_ref, w1_ref, w2_ref, a0_ref, a1_ref, a2_ref,
         g1_ref, g2_ref, kk_ref, ka_ref, vf_ref, v0_ref, v1_ref, v2_ref,
         r_out, w_out, k_out, v_out, kk_out, a_out, g_out, carry) = refs
    else:
        (x_ref, mod_ref, mu_ref, wr_ref, wk_ref, wv_ref, w0_ref, w1_ref, w2_ref, a0_ref, a1_ref, a2_ref,
         g1_ref, g2_ref, kk_ref, ka_ref,
         r_out, w_out, k_out, v_out, kk_out, a_out, g_out, carry) = refs

    @pl.when(pl.program_id(1) == 0)
    def _():
        carry[...] = jnp.zeros_like(carry)

    h = _norm_mod(x_ref[0], mod_ref)
    ts = h.shape[0]
    first = lax.broadcasted_iota(jnp.int32, h.shape, 0) == 0
    h_prev = jnp.where(first, carry[0:1, :], pltpu.roll(h, 1, 0))
    carry[0:1, :] = h[ts - 1:ts, :]
    xx = h_prev - h

    def mixed(i):
        return (h + xx * mu_ref[i:i + 1, :]).astype(BF16)

    def mm(a, w_ref):
        return jnp.dot(a, w_ref[...], preferred_element_type=F32)

    xr, xw, xk, xv, xa, xg = [mixed(i) for i in range(6)]
    w_low = mm(xw, w1_ref)
    a_low = mm(xa, a1_ref)
    g_low = mm(xg, g1_ref)
    r = mm(xr, wr_ref)
    k = mm(xk, wk_ref)
    v = mm(xv, wv_ref)
    v_low = mm(xv, v1_ref) if has_vres else None
    z = w0_ref[...] + mm(jnp.tanh(w_low).astype(BF16), w2_ref)
    w_log = -(jnp.maximum(-z, 0.0) + jnp.log(1.0 + jnp.exp(-jnp.abs(z)))) - 0.5
    if has_vres:
        mix_v = jax.nn.sigmoid(v0_ref[...] + mm(v_low.astype(BF16), v2_ref))
        v = v + (vf_ref[0].astype(F32) - v) * mix_v
    a = jax.nn.sigmoid(a0_ref[...] + mm(a_low.astype(BF16), a2_ref))
    g = mm(jax.nn.sigmoid(g_low).astype(BF16), g2_ref)
    r_out[0] = r.astype(BF16)
    w_out[0] = w_log
    kk_out[0] = (k * kk_ref[...]).astype(BF16)
    k_out[0] = (k * (1.0 + (a - 1.0) * ka_ref[...])).astype(BF16)
    v_out[0] = v.astype(BF16)
    a_out[0] = a.astype(BF16)
    g_out[0] = g.astype(BF16)


def _rwkv_pre_call(x, mod, weights, v_first, vres, ts=512):
    batch, seq, _ = x.shape
    tok = pl.BlockSpec((1, ts, D_MODEL), lambda b, i: (b, i, 0))
    full = lambda a: pl.BlockSpec(a.shape, lambda b, i: (0,) * a.ndim, pipeline_mode=pl.Buffered(1))
    ins = [x, mod] + list(weights)
    specs = [tok, pl.BlockSpec((1, 3, D_MODEL), lambda b, i: (b, 0, 0))] + [full(a) for a in weights]
    if vres is not None:
        ins += [v_first] + list(vres)
        specs += [tok] + [full(a) for a in vres]
    out_dtypes = [BF16, F32, BF16, BF16, BF16, BF16, BF16]
    return pl.pallas_call(
        functools.partial(_rwkv_pre_body, vres is not None),
        grid=(batch, seq // ts),
        in_specs=specs,
        out_specs=[tok] * 7,
        out_shape=[jax.ShapeDtypeStruct(x.shape, dt) for dt in out_dtypes],
        scratch_shapes=[pltpu.VMEM((8, D_MODEL), F32)],
        compiler_params=_params("arbitrary", "arbitrary"),
        name="rwkv_pre",
    )(*ins)


def _wkv_body(r_ref, w_ref, k_ref, v_ref, kk_ref, a_ref, lnw_ref, lnb_ref, rk_ref, g_ref, x_ref, mod_ref, wo_ref,
              o_ref, state):
    C = WKV_CHUNK
    W = LANES
    n_chunk = r_ref.shape[1] // C
    n_pair = r_ref.shape[2] // W

    @pl.when(pl.program_id(1) == 0)
    def _():
        state[...] = jnp.zeros_like(state)

    def iota(shape, dim):
        return lax.broadcasted_iota(jnp.int32, shape, dim)

    def stack_heads(x):
        own = iota(x.shape, 1) % W < HEAD_DIM
        return jnp.concatenate([jnp.where(own, x, 0.0), jnp.where(own, 0.0, x)], axis=0).astype(BF16)

    def block_diag(x):
        left = iota(x.shape, 1) < C
        return jnp.concatenate([jnp.where(left, x, 0.0), jnp.where(left, 0.0, x)], axis=0).astype(BF16)

    row4, col4 = iota((C, 4 * C), 0), iota((C, 4 * C), 1) % C
    strict4 = col4 < row4
    lower4 = col4 <= row4
    tri = jnp.where(iota((C, C), 1) <= iota((C, C), 0), 1.0, 0.0).astype(BF16)
    eye2 = jnp.where(iota((C, 2 * C), 1) % C == iota((C, 2 * C), 0), 1.0, 0.0).astype(F32)
    rw, cw = iota((W, W), 0), iota((W, W), 1)
    same_head = (rw < HEAD_DIM) == (cw < HEAD_DIM)
    ones_bd = jnp.where(same_head, 1.0, 0.0).astype(BF16)
    diag_w = cw == rw
    zeros_cw = jnp.zeros((C, W), F32)

    def bdot(a, b):
        return jnp.dot(a.astype(BF16), b.astype(BF16), preferred_element_type=F32)

    chunks = range(n_chunk)
    pairs = range(n_pair)
    units = [(c, p) for c in chunks for p in pairs]
    rows = lambda t, c: t[c * C:(c + 1) * C]
    blk = lambda t, u: t[u[0] * C:(u[0] + 1) * C, u[1] * W:(u[1] + 1) * W]
    lanes = lambda t, p: t[:, p * W:(p + 1) * W]
    per_chunk = lambda vals: jnp.concatenate([jnp.broadcast_to(t, (C, n_pair * W)) for t in vals], axis=0)
    head_sums = lambda t: jnp.concatenate([bdot(lanes(t, p), ones_bd) for p in pairs], axis=1)

    r = r_ref[0].astype(F32)
    k = k_ref[0].astype(F32)
    v = v_ref[0].astype(F32)
    kk = kk_ref[0].astype(F32)
    kk = kk / jnp.maximum(jnp.sqrt(head_sums(kk * kk)), 1e-12)
    b = kk * a_ref[0].astype(F32)
    bonus = head_sums(r * k * rk_ref[...]) * v
    log_w = -jnp.exp(w_ref[0])
    hi = log_w.astype(BF16)
    lo = (log_w - hi.astype(F32)).astype(BF16)
    hilo = jnp.concatenate([hi, lo], axis=1)
    cum2 = [jnp.dot(tri, rows(hilo, c), preferred_element_type=F32) for c in chunks]
    cum = jnp.concatenate([c2[:, :n_pair * W] + c2[:, n_pair * W:] for c2 in cum2], axis=0)
    c_mid = [cum[c * C + C // 2 - 1:c * C + C // 2, :] for c in chunks]
    c_end = [cum[c * C + C - 1:(c + 1) * C, :] for c in chunks]
    g_last = [jnp.exp(t) for t in c_end]
    from_mid = jnp.exp(per_chunk(c_mid) - cum)
    to_mid = per_chunk([jnp.exp(-t) for t in c_mid])
    a_true = -kk * jnp.exp(cum - log_w)
    r_true = r * jnp.exp(cum)
    a_mid = a_true * to_mid
    r_mid = r_true * to_mid
    b_mid = b * from_mid
    k_mid = k * from_mid
    ar_mid = [jnp.concatenate([blk(a_mid, u), blk(r_mid, u)], axis=0).astype(BF16) for u in units]
    gram = [lax.dot_general(ar_mid[i], jnp.concatenate([stack_heads(blk(b_mid, u)), stack_heads(blk(k_mid, u))],
                                                        axis=0), NT_DIMS, preferred_element_type=F32)
            for i, u in enumerate(units)]
    top = [jnp.where(strict4, g[:C], 0.0) for g in gram]
    bot = [jnp.where(lower4, g[C:], 0.0).astype(BF16) for g in gram]
    n1 = [t[:, :2 * C] for t in top]
    inv = [eye2 + t for t in n1]
    pw = [bdot(t, block_diag(t)) for t in n1]
    levels = C.bit_length() - 2
    for lvl in range(levels):
        if lvl < levels - 1:
            st = [bdot(jnp.concatenate([pw[i], inv[i]], axis=0), block_diag(pw[i])) for i in range(len(units))]
            pw = [t[:C] for t in st]
            inv = [inv[i] + st[i][C:] for i in range(len(units))]
        else:
            inv = [inv[i] + bdot(inv[i], block_diag(pw[i])) for i in range(len(units))]
    x = [jnp.dot(top[i][:, 2 * C:].astype(BF16), stack_heads(blk(v, u)), preferred_element_type=F32)
         for i, u in enumerate(units)]
    pp = [jnp.dot(inv[i].astype(BF16), stack_heads(jnp.concatenate([blk(a_true, u), x[i]], axis=1)),
                  preferred_element_type=F32) for i, u in enumerate(units)]
    zv = [jnp.concatenate([zeros_cw, blk(v, u)], axis=1) for u in units]
    qq = [jnp.dot(bot[i], jnp.concatenate([stack_heads(pp[i]), stack_heads(zv[i])], axis=0),
                  preferred_element_type=F32) for i in range(len(units))]
    end_from_mid = [jnp.exp(c_end[c] - c_mid[c]) for c in chunks]
    bk_hat = [jnp.concatenate([blk(b_mid, u), blk(k_mid, u)], axis=0) * lanes(end_from_mid[u[0]], u[1])
              for u in units]
    mm = [lax.dot_general(bk_hat[i].astype(BF16), jnp.concatenate([pp[i], zv[i]], axis=0).astype(BF16), TN_DIMS,
                          preferred_element_type=F32) for i in range(len(units))]
    q1 = [blk(r_true, u) + qq[i][:, :W] for i, u in enumerate(units)]
    q2 = [t[:, W:] for t in qq]
    m1 = [(jnp.where(same_head, mm[i][:, :W], 0.0)
           + jnp.where(diag_w, jnp.broadcast_to(lanes(g_last[u[0]], u[1]), (W, W)), 0.0)).astype(BF16)
          for i, u in enumerate(units)]
    m2 = [jnp.where(same_head, t[:, W:], 0.0) for t in mm]
    h_cur = [state[p] for p in pairs]
    h_in = []
    for i, (c, p) in enumerate(units):
        h_in.append(h_cur[p])
        h_cur[p] = jnp.dot(m1[i], h_cur[p].astype(BF16), preferred_element_type=F32) + m2[i]
    for p in pairs:
        state[p] = h_cur[p]
    y_blk = [bdot(q1[i], h_in[i]) + q2[i] for i in range(len(units))]
    y = jnp.concatenate([jnp.concatenate([y_blk[c * n_pair + p] for p in pairs], axis=1) for c in chunks], axis=0)
    mean = head_sums(y) * (1.0 / HEAD_DIM)
    yc = y - mean
    var = head_sums(yc * yc) * (1.0 / HEAD_DIM)
    yn = yc * lax.rsqrt(var + RWKV_GN_EPS) * lnw_ref[...] + lnb_ref[...]
    yg = (yn + bonus).astype(BF16) * g_ref[0]
    o_ref[0] = x_ref[0] + mod_ref[0, 2:3, :] * jnp.dot(yg, wo_ref[...], preferred_element_type=F32)


def _wkv_call(r, w_log, k, v, kk, a, ln_w, ln_b, r_k, g, x, mod, w_o, tc=256):
    batch, seq, width = r.shape
    tok = pl.BlockSpec((1, tc, width), lambda b, t: (b, t, 0))
    vec = pl.BlockSpec((1, width), lambda b, t: (0, 0))
    return pl.pallas_call(
        _wkv_body,
        grid=(batch, seq // tc),
        in_specs=[tok] * 6 + [vec] * 3 + [
            tok, tok, pl.BlockSpec((1, 3, width), lambda b, t: (b, 0, 0)),
            pl.BlockSpec((width, width), lambda b, t: (0, 0), pipeline_mode=pl.Buffered(1))],
        out_specs=tok,
        out_shape=jax.ShapeDtypeStruct(x.shape, F32),
        scratch_shapes=[pltpu.VMEM((width // LANES, LANES, LANES), F32)],
        compiler_params=_params("arbitrary", "arbitrary"),
        name="wkv7_chunked",
    )(r, w_log, k, v, kk, a, ln_w, ln_b, r_k, g, x, mod, w_o)


MOE_TILE = 512
MOE_BLOCK = 64
MOE_SLOTS = MOE_TILE + N_EXPERT_GROUPS * MOE_BLOCK
MOE_BLOCKS_PER_TILE = MOE_SLOTS // MOE_BLOCK
MOE_STEP_BLOCKS = 16


def _route(logits_t, rb):
    n_g, epg = N_EXPERT_GROUPS, EXPERTS_PER_GROUP
    s = jax.nn.sigmoid(logits_t)
    sel = s + rb
    s_rows = [s[i:i + 1, :] for i in range(N_EXPERTS)]
    sel_rows = [sel[i:i + 1, :] for i in range(N_EXPERTS)]
    scores = []
    for g in range(n_g):
        members = sel_rows[g * epg:(g + 1) * epg]
        best = None
        for i in range(epg):
            for j in range(i + 1, epg):
                pair = members[i] + members[j]
                best = pair if best is None else jnp.maximum(best, pair)
        scores.append(best)
    g_idx = jnp.zeros_like(scores[0], dtype=jnp.int32)
    top = scores[0]
    for g in range(1, n_g):
        better = scores[g] > top
        g_idx = jnp.where(better, g, g_idx)
        top = jnp.where(better, scores[g], top)

    def pick(rows_, j):
        out = rows_[j]
        for g in range(1, n_g):
            out = jnp.where(g_idx == g, rows_[g * epg + j], out)
        return out

    in_sel = [pick(sel_rows, j) for j in range(epg)]
    in_s = [pick(s_rows, j) for j in range(epg)]
    chosen = []
    for j in range(epg):
        rank = jnp.zeros_like(g_idx)
        for i in range(epg):
            if i == j:
                continue
            ahead = (in_sel[i] >= in_sel[j]) if i < j else (in_sel[i] > in_sel[j])
            rank = rank + jnp.where(ahead, 1, 0)
        chosen.append(rank < 2)
    den = sum(jnp.where(chosen[j], in_s[j], 0.0) for j in range(epg))
    gate_rows = []
    for ex in range(N_EXPERTS):
        g, j = divmod(ex, epg)
        on = jnp.logical_and(chosen[j], g_idx == g)
        gate_rows.append(jnp.where(on, in_s[j] / den, 0.0))
    return g_idx, gate_rows


def _slot_one_hot(pos_row):
    slot = lax.broadcasted_iota(jnp.int32, (MOE_SLOTS, pos_row.shape[1]), 0)
    return jnp.where(slot == pos_row, 1.0, 0.0).astype(BF16)


def _moe_sort_body(x_ref, mod_ref, rw_ref, rb_ref, hs_ref, gs_ref, pos_ref, cnt_ref):
    tm = MOE_TILE
    n_g = N_EXPERT_GROUPS
    tiles = range(x_ref.shape[1] // tm)
    h = [_norm_mod(x_ref[0, i * tm:(i + 1) * tm, :], mod_ref) for i in tiles]
    hi = [t.astype(BF16) for t in h]
    lo = [(h[i] - hi[i].astype(F32)).astype(BF16) for i in tiles]
    parts = [jnp.dot(jnp.concatenate([hi[i], lo[i]], axis=0), rw_ref[...], preferred_element_type=F32)
             for i in tiles]
    logits = [(p[:tm, :LANES] + p[:tm, LANES:]) + (p[tm:, :LANES] + p[tm:, LANES:]) for p in parts]
    routed = [_route(t.T[:N_EXPERTS], rb_ref[...]) for t in logits]
    member = [[jnp.where(routed[i][0] == g, 1.0, 0.0) for g in range(n_g)] for i in tiles]
    earlier = jnp.where(lax.broadcasted_iota(jnp.int32, (tm, tm), 0) < lax.broadcasted_iota(jnp.int32, (tm, tm), 1),
                        1.0, 0.0).astype(BF16)
    rank = [jnp.dot(jnp.concatenate(member[i] + [jnp.zeros((8 - n_g, tm), F32)], axis=0).astype(BF16), earlier,
                    preferred_element_type=F32) for i in tiles]
    lane = lax.broadcasted_iota(jnp.int32, (1, LANES), 1)
    one_hot = []
    for i in tiles:
        pos = jnp.zeros((1, tm), F32)
        offset = jnp.zeros((1, 1), F32)
        counts = jnp.zeros((1, LANES), F32)
        for g in range(n_g):
            n_members = jnp.sum(member[i][g], axis=1, keepdims=True)
            n_blocks = jnp.floor((n_members + (MOE_BLOCK - 1)) * (1.0 / MOE_BLOCK))
            pos = pos + member[i][g] * (offset + rank[i][g:g + 1, :])
            counts = counts + jnp.where(lane == g, n_blocks, 0.0)
            offset = offset + n_blocks * MOE_BLOCK
        pos_i = pos.astype(jnp.int32)
        pos_ref[i] = pos_i
        cnt_ref[i] = counts.astype(jnp.int32)
        one_hot.append(_slot_one_hot(pos_i))
    pad = jnp.zeros((LANES - N_EXPERTS, tm), F32)
    gate_t = [jnp.concatenate(routed[i][1] + [pad], axis=0).T for i in tiles]
    g_hi = [t.astype(BF16) for t in gate_t]
    g_lo = [(gate_t[i] - g_hi[i].astype(F32)).astype(BF16) for i in tiles]
    hs = [jnp.dot(one_hot[i], hi[i], preferred_element_type=F32) for i in tiles]
    gs = [jnp.dot(one_hot[i], jnp.concatenate([g_hi[i], g_lo[i]], axis=1), preferred_element_type=F32)
          for i in tiles]
    for i in tiles:
        hs_ref[i * MOE_SLOTS:(i + 1) * MOE_SLOTS, :] = hs[i].astype(BF16)
        gs_ref[i * MOE_SLOTS:(i + 1) * MOE_SLOTS, :] = gs[i].astype(BF16)


def _moe_sort_call(x, mod, router_w2, router_b, tiles_per_step=4):
    batch, seq, _ = x.shape
    rows = tiles_per_step * MOE_TILE
    per_seq = seq // rows
    n_tiles = batch * seq // MOE_TILE
    step = lambda b, i: b * per_seq + i
    return pl.pallas_call(
        _moe_sort_body,
        grid=(batch, per_seq),
        in_specs=[
            pl.BlockSpec((1, rows, D_MODEL), lambda b, i: (b, i, 0)),
            pl.BlockSpec((1, 3, D_MODEL), lambda b, i: (b, 0, 0)),
            pl.BlockSpec((D_MODEL, 2 * LANES), lambda b, i: (0, 0)),
            pl.BlockSpec((N_EXPERTS, 1), lambda b, i: (0, 0)),
        ],
        out_specs=[
            pl.BlockSpec((tiles_per_step * MOE_SLOTS, D_MODEL), lambda b, i: (step(b, i), 0)),
            pl.BlockSpec((tiles_per_step * MOE_SLOTS, 2 * LANES), lambda b, i: (step(b, i), 0)),
            pl.BlockSpec((tiles_per_step, 1, MOE_TILE), lambda b, i: (step(b, i), 0, 0)),
            pl.BlockSpec((tiles_per_step, 1, LANES), lambda b, i: (step(b, i), 0, 0)),
        ],
        out_shape=[
            jax.ShapeDtypeStruct((n_tiles * MOE_SLOTS, D_MODEL), BF16),
            jax.ShapeDtypeStruct((n_tiles * MOE_SLOTS, 2 * LANES), BF16),
            jax.ShapeDtypeStruct((n_tiles, 1, MOE_TILE), jnp.int32),
            jax.ShapeDtypeStruct((n_tiles, 1, LANES), jnp.int32),
        ],
        compiler_params=_params("arbitrary", "arbitrary"),
        name="moe_sort",
    )(x, mod, router_w2, router_b)


def _moe_tables_body(cnt_ref, sg_ref, sv_ref, sb_ref, bp_ref):
    n_tiles = cnt_ref.shape[0]
    n_g, r, bpt = N_EXPERT_GROUPS, MOE_STEP_BLOCKS, MOE_BLOCKS_PER_TILE
    n_steps = sg_ref.shape[0]

    def fill(ref, n, value, unroll=4):
        def body(i, carry):
            for u in range(unroll):
                ref[i * unroll + u] = value
            return carry
        lax.fori_loop(0, n // unroll, body, 0)
        for i in range(n - n % unroll, n):
            ref[i] = value

    fill(sb_ref, n_steps * r, 0)
    fill(bp_ref, n_tiles * bpt, 0)
    fill(sg_ref, n_steps, n_g - 1)
    fill(sv_ref, n_steps, 0)
    slot = jnp.int32(0)
    for g in range(n_g):
        def tile_body(i, k, g=g):
            first = i * bpt
            for g_before in range(g):
                first = first + cnt_ref[i, g_before]

            def block_body(j, k):
                sb_ref[k] = first + j
                bp_ref[first + j] = k
                return k + 1
            return lax.fori_loop(0, cnt_ref[i, g], block_body, k)

        slot_end = lax.fori_loop(0, n_tiles, tile_body, slot)
        step_first = slot // r
        step_stop = (slot_end + r - 1) // r

        def step_body(s, carry, g=g, slot_end=slot_end):
            sg_ref[s] = g
            sv_ref[s] = jnp.minimum(slot_end - s * r, r)
            return carry
        lax.fori_loop(step_first, step_stop, step_body, 0)
        slot = step_stop * r


def _moe_step_tables(counts):
    n_tiles = counts.shape[0]
    n_blocks = n_tiles * MOE_BLOCKS_PER_TILE
    n_steps = n_blocks // MOE_STEP_BLOCKS + N_EXPERT_GROUPS
    smem = pl.BlockSpec(memory_space=pltpu.SMEM)
    return pl.pallas_call(
        _moe_tables_body,
        in_specs=[smem],
        out_specs=[smem] * 4,
        out_shape=[jax.ShapeDtypeStruct((n_steps,), jnp.int32), jax.ShapeDtypeStruct((n_steps,), jnp.int32),
                   jax.ShapeDtypeStruct((n_steps * MOE_STEP_BLOCKS,), jnp.int32),
                   jax.ShapeDtypeStruct((n_blocks,), jnp.int32)],
        name="moe_tables",
    )(counts)


def _moe_expert_body(sg_ref, sv_ref, sb_ref, *refs):
    r = MOE_STEP_BLOCKS
    h_refs, g_refs = refs[:r], refs[r:2 * r]
    w1_ref, w3_ref, w2_ref, o_ref, w13_scr, w2_scr, act_scr = refs[2 * r:]
    step = pl.program_id(0)
    epg = EXPERTS_PER_GROUP
    rows_total = r * MOE_BLOCK
    group = sg_ref[step]
    new_group = jnp.logical_or(step == 0, group != sg_ref[jnp.maximum(step - 1, 0)])

    @pl.when(sv_ref[step] == 0)
    def _():
        o_ref[...] = jnp.zeros_like(o_ref)

    @pl.when(jnp.logical_and(sv_ref[step] > 0, new_group))
    def _():
        w13_scr[:, :, :D_EXPERT] = w1_ref[0].astype(BF16)
        w13_scr[:, :, D_EXPERT:] = w3_ref[0].astype(BF16)
        w2_scr[...] = w2_ref[0].reshape(epg * D_EXPERT, D_MODEL).astype(BF16)

    @pl.when(sv_ref[step] > 0)
    def _():
        h = jnp.concatenate([ref[...] for ref in h_refs], axis=0)
        gates = jnp.concatenate([ref[...] for ref in g_refs], axis=0)
        real = lax.broadcasted_iota(jnp.int32, (rows_total, LANES), 0) < sv_ref[step] * MOE_BLOCK
        expert_row = lax.broadcasted_iota(jnp.int32, (2 * LANES, epg * LANES), 0) % LANES
        lane_block = lax.broadcasted_iota(jnp.int32, (2 * LANES, epg * LANES), 1) // LANES
        pick_rows = jnp.where(expert_row == group * epg + lane_block, 1.0, 0.0).astype(BF16)
        gate_all = jnp.dot(gates, pick_rows, preferred_element_type=F32)
        for j in range(epg):
            gate_b = jnp.where(real, gate_all[:, j * LANES:(j + 1) * LANES], 0.0)
            hid = jnp.dot(h, w13_scr[j], preferred_element_type=F32)
            h1 = hid[:, :D_EXPERT]
            act = h1 * jax.nn.sigmoid(h1) * hid[:, D_EXPERT:] * jnp.concatenate([gate_b, gate_b], axis=1)
            act_scr[:, j * D_EXPERT:(j + 1) * D_EXPERT] = act.astype(BF16)
        o_ref[...] = jnp.dot(act_scr[...], w2_scr[...], preferred_element_type=F32).astype(BF16)


def _moe_expert_call(hs, gs, w1, w3, w2, step_group, step_valid, step_blocks, layer):
    r = MOE_STEP_BLOCKS
    n_steps = step_group.shape[0]
    epg = EXPERTS_PER_GROUP
    rows_total = r * MOE_BLOCK

    def blk(width, j):
        return pl.BlockSpec((MOE_BLOCK, width), lambda s, sg, sv, sb, j=j: (sb[s * r + j], 0))

    def group_weights(shape):
        return pl.BlockSpec((1, epg) + shape, lambda s, sg, sv, sb: (layer, sg[s], 0, 0))

    grid_spec = pltpu.PrefetchScalarGridSpec(
        num_scalar_prefetch=3,
        grid=(n_steps,),
        in_specs=[blk(D_MODEL, j) for j in range(r)] + [blk(2 * LANES, j) for j in range(r)] + [
            group_weights((D_MODEL, D_EXPERT)), group_weights((D_MODEL, D_EXPERT)),
            group_weights((D_EXPERT, D_MODEL))],
        out_specs=pl.BlockSpec((rows_total, D_MODEL), lambda s, sg, sv, sb: (s, 0)),
        scratch_shapes=[
            pltpu.VMEM((epg, D_MODEL, 2 * D_EXPERT), BF16),
            pltpu.VMEM((epg * D_EXPERT, D_MODEL), BF16),
            pltpu.VMEM((rows_total, epg * D_EXPERT), BF16),
        ],
    )
    return pl.pallas_call(
        _moe_expert_body,
        grid_spec=grid_spec,
        out_shape=jax.ShapeDtypeStruct((n_steps * rows_total, D_MODEL), BF16),
        compiler_params=_params("arbitrary"),
        name="moe_experts",
    )(step_group, step_valid, step_blocks, *([hs] * r), *([gs] * r), w1, w3, w2)


def _moe_unsort_body(is_last, bp_ref, *refs):
    bpt = MOE_BLOCKS_PER_TILE
    pos_ref, x_ref, mod_ref, fg_ref, o_ref = refs[-5:]
    y_refs = refs[:-5]
    tm = MOE_TILE
    tiles = range(len(y_refs) // bpt)
    slot = lax.broadcasted_iota(jnp.int32, (tm, MOE_SLOTS), 1)
    one_hot = []
    for i in tiles:
        pos_col = jnp.broadcast_to(pos_ref[i].astype(F32), (LANES, tm)).T.astype(jnp.int32)
        pos_col = jnp.concatenate([pos_col] * (MOE_SLOTS // LANES), axis=1)
        one_hot.append(jnp.where(slot == pos_col, 1.0, 0.0).astype(BF16))
    ys = [jnp.concatenate([ref[...] for ref in y_refs[i * bpt:(i + 1) * bpt]], axis=0) for i in tiles]
    y = [jnp.dot(one_hot[i], ys[i], preferred_element_type=F32) for i in tiles]
    for i in tiles:
        out = x_ref[0, i * tm:(i + 1) * tm, :] + mod_ref[0, 2:3, :] * y[i]
        if is_last:
            out = out * lax.rsqrt(jnp.mean(out * out, axis=-1, keepdims=True) + RMS_EPS) * fg_ref[...]
        o_ref[0, i * tm:(i + 1) * tm, :] = out


def _moe_unsort_call(ys, pos, x, mod, final_g, block_pos, is_last, tiles_per_step=4):
    batch, seq, _ = x.shape
    rows = tiles_per_step * MOE_TILE
    per_seq = seq // rows
    bpt = MOE_BLOCKS_PER_TILE
    n_in = tiles_per_step * bpt

    def blk(j):
        return pl.BlockSpec((MOE_BLOCK, D_MODEL), lambda b, i, bp, j=j: (bp[(b * per_seq + i) * n_in + j], 0))

    grid_spec = pltpu.PrefetchScalarGridSpec(
        num_scalar_prefetch=1,
        grid=(batch, per_seq),
        in_specs=[blk(j) for j in range(n_in)] + [
            pl.BlockSpec((tiles_per_step, 1, MOE_TILE), lambda b, i, bp: (b * per_seq + i, 0, 0)),
            pl.BlockSpec((1, rows, D_MODEL), lambda b, i, bp: (b, i, 0)),
            pl.BlockSpec((1, 3, D_MODEL), lambda b, i, bp: (b, 0, 0)),
            pl.BlockSpec((1, D_MODEL), lambda b, i, bp: (0, 0)),
        ],
        out_specs=pl.BlockSpec((1, rows, D_MODEL), lambda b, i, bp: (b, i, 0)),
    )
    return pl.pallas_call(
        functools.partial(_moe_unsort_body, is_last),
        grid_spec=grid_spec,
        out_shape=jax.ShapeDtypeStruct(x.shape, F32),
        compiler_params=_params("arbitrary", "arbitrary"),
        name="moe_unsort",
    )(block_pos, *([ys] * n_in), pos, x, mod, final_g)


def _moe_layer(x, mod, router_w2, router_b, w1, w3, w2, final_g, layer, is_last):
    hs, gs, pos, cnt = _moe_sort_call(x, mod, router_w2, router_b)
    step_group, step_valid, step_blocks, block_pos = _moe_step_tables(cnt[:, 0, :])
    ys = _moe_expert_call(hs, gs, w1, w3, w2, step_group, step_valid, step_blocks, layer)
    return _moe_unsort_call(ys, pos, x, mod, final_g, block_pos, is_last)


def _pad_cols(w):
    n = w.shape[-1]
    return jnp.pad(w, ((0, 0), (0, -n % LANES)))


def _pad_rows(w):
    n = w.shape[0]
    return jnp.pad(w, ((0, -n % LANES), (0, 0)))


def kernel(x, c, ada_w, ada_b, mix_w_in, mix_w_out, sgu_norm_g, sgu_w, sgu_b, rwkv_mu, rwkv_w_rkv, rwkv_w_o, rwkv_w0, rwkv_w1, rwkv_w2, rwkv_a0, rwkv_a1, rwkv_a2, rwkv_v0, rwkv_v1, rwkv_v2, rwkv_g1, rwkv_g2, rwkv_k_k, rwkv_k_a, rwkv_r_k, rwkv_ln_w, rwkv_ln_b, router_w, router_b, moe_w1, moe_w3, moe_w2, final_norm_g):
    depth = ada_w.shape[0]
    row = lambda t: t.reshape(1, -1)
    mods = _ada_call(c, ada_w, ada_b)
    rw_hi = router_w.astype(BF16)
    rw_lo = (router_w - rw_hi.astype(F32)).astype(BF16)
    router_w2 = jnp.concatenate([_pad_cols(rw_hi), _pad_cols(rw_lo)], axis=1)
    router_bc = router_b.reshape(N_EXPERTS, 1)
    final_g = row(final_norm_g)
    v_first = None
    for layer in range(depth):
        i = layer // 2
        mod = mods[2 * layer]
        if layer % 2 == 0:
            proj = _proj_call(x, mod, mix_w_in[i].astype(BF16))
            o_a = _attn_call(proj)
            bias_tile = jnp.repeat(sgu_b[i].T, HEAD_DIM, axis=1)
            x = _sgu_out_call(proj, o_a, x, mod, row(sgu_norm_g[i]), sgu_w[i], bias_tile,
                              mix_w_out[i].astype(BF16))
        else:
            weights = [
                rwkv_mu[i],
                rwkv_w_rkv[i, 0].astype(BF16), rwkv_w_rkv[i, 1].astype(BF16), rwkv_w_rkv[i, 2].astype(BF16),
                row(rwkv_w0[i]), _pad_cols(rwkv_w1[i]).astype(BF16), _pad_rows(rwkv_w2[i]).astype(BF16),
                row(rwkv_a0[i]), _pad_cols(rwkv_a1[i]).astype(BF16), _pad_rows(rwkv_a2[i]).astype(BF16),
                _pad_cols(rwkv_g1[i]).astype(BF16), _pad_rows(rwkv_g2[i]).astype(BF16),
                row(rwkv_k_k[i]), row(rwkv_k_a[i]),
            ]
            vres = None
            if i > 0:
                vres = [row(rwkv_v0[i - 1]), _pad_cols(rwkv_v1[i - 1]).astype(BF16),
                        _pad_rows(rwkv_v2[i - 1]).astype(BF16)]
            r, w_log, k, v, kk, a, g = _rwkv_pre_call(x, mod, weights, v_first, vres)
            if i == 0:
                v_first = v
            x = _wkv_call(r, w_log, k, v, kk, a, row(rwkv_ln_w[i]), row(rwkv_ln_b[i]), row(rwkv_r_k[i]),
                          g, x, mod, rwkv_w_o[i].astype(BF16))
        x = _moe_layer(x, mods[2 * layer + 1], router_w2, router_bc, moe_w1, moe_w3, moe_w2, final_g, layer,
                       layer == depth - 1)
    return x
```

```python
import functools

import jax
import jax.numpy as jnp
from jax import lax
from jax.experimental import pallas as pl
from jax.experimental.pallas import tpu as pltpu

F32 = jnp.float32
BF16 = jnp.bfloat16

D_MODEL = 1024
HEAD_DIM = 64
WIDTH_A = 512
WIDTH_B = 512
MIX_IN_WIDTH = 2560
ATTN_SPAN = 128
ATTN_DILATIONS = (1, 4, 16)
SGU_CHUNK = 128
N_GROUPS_B = 8
RWKV_HEADS = 16
RWKV_GN_EPS = 64e-5
WKV_CHUNK = 64
N_EXPERTS = 16
N_EXPERT_GROUPS = 4
EXPERTS_PER_GROUP = 4
D_EXPERT = 256
RMS_EPS = 1e-6
LANES = 128
VMEM_LIMIT = 56 * 1024 * 1024

NT_DIMS = (((1,), (1,)), ((), ()))
TN_DIMS = (((0,), (0,)), ((), ()))


def _params(*sem):
    return pltpu.CompilerParams(dimension_semantics=sem, vmem_limit_bytes=VMEM_LIMIT)


def _norm_mod(x, mod_ref):
    ms = jnp.mean(x * x, axis=-1, keepdims=True)
    return (x * lax.rsqrt(ms + RMS_EPS)) * (1.0 + mod_ref[0, 1:2, :]) + mod_ref[0, 0:1, :]


def _gelu(x):
    return 0.5 * x * (1.0 + lax.erf(x * 0.7071067811865476))


def _ada_body(c_ref, w_ref, b_ref, o_ref):
    c = c_ref[...]
    c_act = c * jax.nn.sigmoid(c)
    n = c.shape[0]
    c_hi = c_act.astype(BF16)
    c_lo = (c_act - c_hi.astype(F32)).astype(BF16)
    c2 = jnp.concatenate([c_hi, c_lo], axis=0)
    w = w_ref[0]
    w_hi = w.astype(BF16)
    w_lo = (w - w_hi.astype(F32)).astype(BF16)
    by_hi = jnp.dot(c2, w_hi, preferred_element_type=F32)
    by_lo = jnp.dot(c2, w_lo, preferred_element_type=F32)
    o_ref[0] = (by_hi[:n] + by_hi[n:]) + (by_lo[:n] + by_lo[n:]) + b_ref[0]


def _ada_call(c, ada_w, ada_b):
    n_pair = ada_w.shape[0] * ada_w.shape[1]
    batch = c.shape[0]
    w = ada_w.reshape(n_pair, D_MODEL, 3 * D_MODEL)
    b = ada_b.reshape(n_pair, 1, 3 * D_MODEL)
    out = pl.pallas_call(
        _ada_body,
        grid=(n_pair, 3),
        in_specs=[
            pl.BlockSpec((batch, D_MODEL), lambda p, j: (0, 0)),
            pl.BlockSpec((1, D_MODEL, D_MODEL), lambda p, j: (p, 0, j)),
            pl.BlockSpec((1, 1, D_MODEL), lambda p, j: (p, 0, j)),
        ],
        out_specs=pl.BlockSpec((1, batch, D_MODEL), lambda p, j: (p, 0, j)),
        out_shape=jax.ShapeDtypeStruct((n_pair, batch, 3 * D_MODEL), F32),
        compiler_params=_params("arbitrary", "arbitrary"),
        name="ada_mod",
    )(c, w, b)
    return out.reshape(n_pair, batch, 3, D_MODEL)


def _proj_body(x_ref, mod_ref, w_ref, o_ref):
    h = _norm_mod(x_ref[0], mod_ref)
    o_ref[0] = jnp.dot(h.astype(BF16), w_ref[...], preferred_element_type=F32)


def _proj_call(x, mod, w_in, ts=1024):
    batch, seq, _ = x.shape
    return pl.pallas_call(
        _proj_body,
        grid=(batch, seq // ts),
        in_specs=[
            pl.BlockSpec((1, ts, D_MODEL), lambda b, i: (b, i, 0)),
            pl.BlockSpec((1, 3, D_MODEL), lambda b, i: (b, 0, 0)),
            pl.BlockSpec((D_MODEL, MIX_IN_WIDTH), lambda b, i: (0, 0)),
        ],
        out_specs=pl.BlockSpec((1, ts, MIX_IN_WIDTH), lambda b, i: (b, i, 0)),
        out_shape=jax.ShapeDtypeStruct((batch, seq, MIX_IN_WIDTH), F32),
        compiler_params=_params("arbitrary", "arbitrary"),
        name="mix_in_proj",
    )(x, mod, w_in)


def _attn_body(q_ref, k_ref, v_ref, o_ref, ob, lb):
    span = ATTN_SPAN
    W = LANES
    block_group = 8

    def iota(shape, dim):
        return lax.broadcasted_iota(jnp.int32, shape, dim)

    head0 = iota((span, W), 1) < HEAD_DIM
    mine = (head0, jnp.logical_not(head0))
    row2, col2 = iota((span, 2 * span), 0), iota((span, 2 * span), 1)
    mask_prev_cur = jnp.where(col2 < span, col2 - row2, row2 - (col2 - span)) >= 0
    mask_cur = iota((span, span), 1) <= iota((span, span), 0)

    def rows(start, n_rows, dil):
        return pl.ds(start, n_rows) if dil == 1 else pl.ds(start, n_rows, stride=dil)

    def group(p, dil, q_starts, with_prev):
        blocks = range(len(q_starts))
        heads = range(2)
        qb, kcat, vaug, masks = [], [], [], []
        for j in blocks:
            n_keys = 2 * span if with_prev[j] else span
            k_start = q_starts[j] - span * dil if with_prev[j] else q_starts[j]
            qb.append(q_ref[0, rows(q_starts[j], span, dil), :] * (HEAD_DIM ** -0.5))
            kcat.append(k_ref[0, rows(k_start, n_keys, dil), :].astype(BF16))
            vb = v_ref[0, rows(k_start, n_keys, dil), :].astype(BF16)
            vaug.append(jnp.concatenate([vb, jnp.ones((n_keys, W), BF16)], axis=1))
            masks.append(mask_prev_cur if with_prev[j] else mask_cur)
        qh = [[jnp.where(mine[h], qb[j], 0.0).astype(BF16) for h in heads] for j in blocks]
        s = [[lax.dot_general(qh[j][h], kcat[j], NT_DIMS, preferred_element_type=F32) for h in heads] for j in blocks]
        s = [[jnp.where(masks[j], s[j][h], -jnp.inf) for h in heads] for j in blocks]
        m = [[jnp.max(s[j][h], axis=-1, keepdims=True) for h in heads] for j in blocks]
        pr = [[jnp.exp(s[j][h] - m[j][h]).astype(BF16) for h in heads] for j in blocks]
        ad = [[jnp.dot(pr[j][h], vaug[j], preferred_element_type=F32) for h in heads] for j in blocks]
        for j in blocks:
            den = jnp.where(head0, ad[j][0][:, W:], ad[j][1][:, W:])
            o = jnp.where(head0, ad[j][0][:, :W], ad[j][1][:, :W]) / den
            lse = jnp.where(head0, m[j][0], m[j][1]) + jnp.log(den)
            qi = rows(q_starts[j], span, dil)
            ob[p, qi, :] = o
            lb[p, qi, :] = lse

    seq = q_ref.shape[1]
    for p, dil in enumerate(ATTN_DILATIONS):
        n_blk = seq // dil // span
        blocks = [(r + n * span * dil, n > 0) for r in range(dil) for n in range(n_blk)]
        for g0 in range(0, len(blocks), block_group):
            members = blocks[g0:g0 + block_group]
            group(p, dil, [b[0] for b in members], [b[1] for b in members])

    l0, l1, l2 = lb[0], lb[1], lb[2]
    m = jnp.maximum(jnp.maximum(l0, l1), l2)
    w0, w1, w2 = jnp.exp(l0 - m), jnp.exp(l1 - m), jnp.exp(l2 - m)
    o_ref[0] = (w0 * ob[0] + w1 * ob[1] + w2 * ob[2]) / (w0 + w1 + w2)


def _attn_call(proj):
    batch, seq, _ = proj.shape
    n_pair = WIDTH_A // LANES
    blk = lambda off: pl.BlockSpec((1, seq, LANES), lambda b, h: (b, 0, off + h))
    return pl.pallas_call(
        _attn_body,
        grid=(batch, n_pair),
        in_specs=[blk(0), blk(n_pair), blk(2 * n_pair)],
        out_specs=pl.BlockSpec((1, seq, LANES), lambda b, h: (b, 0, h)),
        out_shape=jax.ShapeDtypeStruct((batch, seq, WIDTH_A), F32),
        scratch_shapes=[
            pltpu.VMEM((3, seq, LANES), F32),
            pltpu.VMEM((3, seq, LANES), F32),
        ],
        compiler_params=_params("arbitrary", "arbitrary"),
        name="dilated_attn",
    )(proj, proj, proj)


def _sgu_out_body(u_ref, z_ref, oa_ref, x_ref, mod_ref, ng_ref, sw_ref, sb_ref, wo_ref, moe_mod_ref, rw_ref, rb_ref,
                  o_ref, hs_ref, gs_ref, pos_ref, cnt_ref, mix):
    ts = u_ref.shape[1]
    u = _gelu(u_ref[0])
    z = _gelu(z_ref[0])
    z = z * lax.rsqrt(jnp.mean(z * z, axis=-1, keepdims=True) + RMS_EPS) * ng_ref[...]
    zb = z.astype(BF16)
    row = lax.broadcasted_iota(jnp.int32, (SGU_CHUNK, SGU_CHUNK), 0)
    col = lax.broadcasted_iota(jnp.int32, (SGU_CHUNK, SGU_CHUNK), 1)
    causal = col <= row
    for g in range(N_GROUPS_B):
        w_g = jnp.where(causal, sw_ref[g], 0.0).astype(BF16)
        for cc in range(ts // SGU_CHUNK):
            rs = slice(cc * SGU_CHUNK, (cc + 1) * SGU_CHUNK)
            ls = slice(g * HEAD_DIM, (g + 1) * HEAD_DIM)
            mix[rs, ls] = jnp.dot(w_g, zb[rs, ls], preferred_element_type=F32)
    bias = jnp.concatenate([sb_ref[...]] * (ts // SGU_CHUNK), axis=0)
    o_b = u * (mix[...] + bias)
    y = jnp.dot(oa_ref[0].astype(BF16), wo_ref[0:WIDTH_A, :], preferred_element_type=F32)
    y = y + jnp.dot(o_b.astype(BF16), wo_ref[WIDTH_A:, :], preferred_element_type=F32)
    out = x_ref[0] + mod_ref[0, 2:3, :] * y
    o_ref[0] = out
    _moe_sort_tiles([out[i * MOE_TILE:(i + 1) * MOE_TILE] for i in range(ts // MOE_TILE)],
                    moe_mod_ref, rw_ref, rb_ref, hs_ref, gs_ref, pos_ref, cnt_ref)


def _sgu_out_call(proj, o_a, x, mod, norm_g, sgu_w, sgu_bias_tile, w_out, moe_mod, router_w2, router_b, ts=1024):
    batch, seq, _ = x.shape
    u_blk = 3 * WIDTH_A // WIDTH_B
    full = lambda shape: pl.BlockSpec(shape, lambda b, i: (0,) * len(shape))
    per_seq = seq // ts
    tiles = ts // MOE_TILE
    n_tiles = batch * seq // MOE_TILE
    step = lambda b, i: b * per_seq + i
    return pl.pallas_call(
        _sgu_out_body,
        grid=(batch, seq // ts),
        in_specs=[
            pl.BlockSpec((1, ts, WIDTH_B), lambda b, i: (b, i, u_blk)),
            pl.BlockSpec((1, ts, WIDTH_B), lambda b, i: (b, i, u_blk + 1)),
            pl.BlockSpec((1, ts, WIDTH_A), lambda b, i: (b, i, 0)),
            pl.BlockSpec((1, ts, D_MODEL), lambda b, i: (b, i, 0)),
            pl.BlockSpec((1, 3, D_MODEL), lambda b, i: (b, 0, 0)),
            full((1, WIDTH_B)),
            full((N_GROUPS_B, SGU_CHUNK, SGU_CHUNK)),
            full((SGU_CHUNK, WIDTH_B)),
            full((WIDTH_A + WIDTH_B, D_MODEL)),
            pl.BlockSpec((1, 3, D_MODEL), lambda b, i: (b, 0, 0)),
            full((D_MODEL, 2 * LANES)),
            full((N_EXPERTS, 1)),
        ],
        out_specs=[
            pl.BlockSpec((1, ts, D_MODEL), lambda b, i: (b, i, 0)),
            pl.BlockSpec((tiles * MOE_SLOTS, D_MODEL), lambda b, i: (step(b, i), 0)),
            pl.BlockSpec((tiles * MOE_SLOTS, 2 * LANES), lambda b, i: (step(b, i), 0)),
            pl.BlockSpec((tiles, 1, MOE_TILE), lambda b, i: (step(b, i), 0, 0)),
            pl.BlockSpec((tiles, 1, LANES), lambda b, i: (step(b, i), 0, 0)),
        ],
        out_shape=[
            jax.ShapeDtypeStruct(x.shape, F32),
            jax.ShapeDtypeStruct((n_tiles * MOE_SLOTS, D_MODEL), BF16),
            jax.ShapeDtypeStruct((n_tiles * MOE_SLOTS, 2 * LANES), BF16),
            jax.ShapeDtypeStruct((n_tiles, 1, MOE_TILE), jnp.int32),
            jax.ShapeDtypeStruct((n_tiles, 1, LANES), jnp.int32),
        ],
        scratch_shapes=[pltpu.VMEM((ts, WIDTH_B), F32)],
        compiler_params=_params("arbitrary", "arbitrary"),
        name="sgu_out_proj",
    )(proj, proj, o_a, x, mod, norm_g, sgu_w, sgu_bias_tile, w_out, moe_mod, router_w2, router_b)


def _rwkv_pre_body(has_vres, *refs):
    if has_vres:
        (x_ref, mod_ref, mu_ref, wr_ref, wk_ref, wv_ref, w0_ref, w1_ref, w2_ref, a0_ref, a1_ref, a2_ref,
         g1_ref, g2_ref, kk_ref, ka_ref, vf_ref, v0_ref, v1_ref, v2_ref,
         r_out, w_out, k_out, v_out, kk_out, a_out, g_out, carry) = refs
    else:
        (x_ref, mod_ref, mu_ref, wr_ref, wk_ref, wv_ref, w0_ref, w1_ref, w2_ref, a0_ref, a1_ref, a2_ref,
         g1_ref, g2_ref, kk_ref, ka_ref,
         r_out, w_out, k_out, v_out, kk_out, a_out, g_out, carry) = refs

    @pl.when(pl.program_id(1) == 0)
    def _():
        carry[...] = jnp.zeros_like(carry)

    h = _norm_mod(x_ref[0], mod_ref)
    ts = h.shape[0]
    first = lax.broadcasted_iota(jnp.int32, h.shape, 0) == 0
    h_prev = jnp.where(first, carry[0:1, :], pltpu.roll(h, 1, 0))
    carry[0:1, :] = h[ts - 1:ts, :]
    xx = h_prev - h

    def mixed(i):
        return (h + xx * mu_ref[i:i + 1, :]).astype(BF16)

    def mm(a, w_ref):
        return jnp.dot(a, w_ref[...], preferred_element_type=F32)

    xr, xw, xk, xv, xa, xg = [mixed(i) for i in range(6)]
    w_low = mm(xw, w1_ref)
    a_low = mm(xa, a1_ref)
    g_low = mm(xg, g1_ref)
    r = mm(xr, wr_ref)
    k = mm(xk, wk_ref)
    v = mm(xv, wv_ref)
    v_low = mm(xv, v1_ref) if has_vres else None
    z = w0_ref[...] + mm(jnp.tanh(w_low).astype(BF16), w2_ref)
    w_log = -(jnp.maximum(-z, 0.0) + jnp.log(1.0 + jnp.exp(-jnp.abs(z)))) - 0.5
    if has_vres:
        mix_v = jax.nn.sigmoid(v0_ref[...] + mm(v_low.astype(BF16), v2_ref))
        v = v + (vf_ref[0].astype(F32) - v) * mix_v
    a = jax.nn.sigmoid(a0_ref[...] + mm(a_low.astype(BF16), a2_ref))
    g = mm(jax.nn.sigmoid(g_low).astype(BF16), g2_ref)
    r_out[0] = r.astype(BF16)
    w_out[0] = w_log
    kk_out[0] = (k * kk_ref[...]).astype(BF16)
    k_out[0] = (k * (1.0 + (a - 1.0) * ka_ref[...])).astype(BF16)
    v_out[0] = v.astype(BF16)
    a_out[0] = a.astype(BF16)
    g_out[0] = g.astype(BF16)


def _rwkv_pre_call(x, mod, weights, v_first, vres, ts=512):
    batch, seq, _ = x.shape
    tok = pl.BlockSpec((1, ts, D_MODEL), lambda b, i: (b, i, 0))
    full = lambda a: pl.BlockSpec(a.shape, lambda b, i: (0,) * a.ndim, pipeline_mode=pl.Buffered(1))
    ins = [x, mod] + list(weights)
    specs = [tok, pl.BlockSpec((1, 3, D_MODEL), lambda b, i: (b, 0, 0))] + [full(a) for a in weights]
    if vres is not None:
        ins += [v_first] + list(vres)
        specs += [tok] + [full(a) for a in vres]
    out_dtypes = [BF16, F32, BF16, BF16, BF16, BF16, BF16]
    return pl.pallas_call(
        functools.partial(_rwkv_pre_body, vres is not None),
        grid=(batch, seq // ts),
        in_specs=specs,
        out_specs=[tok] * 7,
        out_shape=[jax.ShapeDtypeStruct(x.shape, dt) for dt in out_dtypes],
        scratch_shapes=[pltpu.VMEM((8, D_MODEL), F32)],
        compiler_params=_params("arbitrary", "arbitrary"),
        name="rwkv_pre",
    )(*ins)


def _wkv_body(r_ref, w_ref, k_ref, v_ref, kk_ref, a_ref, lnw_ref, lnb_ref, rk_ref, g_ref, x_ref, mod_ref, wo_ref,
              o_ref, state):
    C = WKV_CHUNK
    W = LANES
    n_chunk = r_ref.shape[1] // C
    n_pair = r_ref.shape[2] // W

    @pl.when(pl.program_id(1) == 0)
    def _():
        state[...] = jnp.zeros_like(state)

    def iota(shape, dim):
        return lax.broadcasted_iota(jnp.int32, shape, dim)

    def stack_heads(x):
        own = iota(x.shape, 1) % W < HEAD_DIM
        return jnp.concatenate([jnp.where(own, x, 0.0), jnp.where(own, 0.0, x)], axis=0).astype(BF16)

    def block_diag(x):
        left = iota(x.shape, 1) < C
        return jnp.concatenate([jnp.where(left, x, 0.0), jnp.where(left, 0.0, x)], axis=0).astype(BF16)

    row4, col4 = iota((C, 4 * C), 0), iota((C, 4 * C), 1) % C
    strict4 = col4 < row4
    lower4 = col4 <= row4
    tri = jnp.where(iota((C, C), 1) <= iota((C, C), 0), 1.0, 0.0).astype(BF16)
    eye2 = jnp.where(iota((C, 2 * C), 1) % C == iota((C, 2 * C), 0), 1.0, 0.0).astype(F32)
    rw, cw = iota((W, W), 0), iota((W, W), 1)
    same_head = (rw < HEAD_DIM) == (cw < HEAD_DIM)
    ones_bd = jnp.where(same_head, 1.0, 0.0).astype(BF16)
    diag_w = cw == rw
    zeros_cw = jnp.zeros((C, W), F32)

    def bdot(a, b):
        return jnp.dot(a.astype(BF16), b.astype(BF16), preferred_element_type=F32)

    chunks = range(n_chunk)
    pairs = range(n_pair)
    units = [(c, p) for c in chunks for p in pairs]
    rows = lambda t, c: t[c * C:(c + 1) * C]
    blk = lambda t, u: t[u[0] * C:(u[0] + 1) * C, u[1] * W:(u[1] + 1) * W]
    lanes = lambda t, p: t[:, p * W:(p + 1) * W]
    per_chunk = lambda vals: jnp.concatenate([jnp.broadcast_to(t, (C, n_pair * W)) for t in vals], axis=0)
    head_sums = lambda t: jnp.concatenate([bdot(lanes(t, p), ones_bd) for p in pairs], axis=1)

    r = r_ref[0].astype(F32)
    k = k_ref[0].astype(F32)
    v = v_ref[0].astype(F32)
    kk = kk_ref[0].astype(F32)
    kk = kk / jnp.maximum(jnp.sqrt(head_sums(kk * kk)), 1e-12)
    b = kk * a_ref[0].astype(F32)
    bonus = head_sums(r * k * rk_ref[...]) * v
    log_w = -jnp.exp(w_ref[0])
    hi = log_w.astype(BF16)
    lo = (log_w - hi.astype(F32)).astype(BF16)
    hilo = jnp.concatenate([hi, lo], axis=1)
    cum2 = [jnp.dot(tri, rows(hilo, c), preferred_element_type=F32) for c in chunks]
    cum = jnp.concatenate([c2[:, :n_pair * W] + c2[:, n_pair * W:] for c2 in cum2], axis=0)
    c_mid = [cum[c * C + C // 2 - 1:c * C + C // 2, :] for c in chunks]
    c_end = [cum[c * C + C - 1:(c + 1) * C, :] for c in chunks]
    g_last = [jnp.exp(t) for t in c_end]
    from_mid = jnp.exp(per_chunk(c_mid) - cum)
    to_mid = per_chunk([jnp.exp(-t) for t in c_mid])
    a_true = -kk * jnp.exp(cum - log_w)
    r_true = r * jnp.exp(cum)
    a_mid = a_true * to_mid
    r_mid = r_true * to_mid
    b_mid = b * from_mid
    k_mid = k * from_mid
    ar_mid = [jnp.concatenate([blk(a_mid, u), blk(r_mid, u)], axis=0).astype(BF16) for u in units]
    gram = [lax.dot_general(ar_mid[i], jnp.concatenate([stack_heads(blk(b_mid, u)), stack_heads(blk(k_mid, u))],
                                                        axis=0), NT_DIMS, preferred_element_type=F32)
            for i, u in enumerate(units)]
    top = [jnp.where(strict4, g[:C], 0.0) for g in gram]
    bot = [jnp.where(lower4, g[C:], 0.0).astype(BF16) for g in gram]
    n1 = [t[:, :2 * C] for t in top]
    inv = [eye2 + t for t in n1]
    pw = [bdot(t, block_diag(t)) for t in n1]
    levels = C.bit_length() - 2
    for lvl in range(levels):
        if lvl < levels - 1:
            st = [bdot(jnp.concatenate([pw[i], inv[i]], axis=0), block_diag(pw[i])) for i in range(len(units))]
            pw = [t[:C] for t in st]
            inv = [inv[i] + st[i][C:] for i in range(len(units))]
        else:
            inv = [inv[i] + bdot(inv[i], block_diag(pw[i])) for i in range(len(units))]
    x = [jnp.dot(top[i][:, 2 * C:].astype(BF16), stack_heads(blk(v, u)), preferred_element_type=F32)
         for i, u in enumerate(units)]
    pp = [jnp.dot(inv[i].astype(BF16), stack_heads(jnp.concatenate([blk(a_true, u), x[i]], axis=1)),
                  preferred_element_type=F32) for i, u in enumerate(units)]
    zv = [jnp.concatenate([zeros_cw, blk(v, u)], axis=1) for u in units]
    qq = [jnp.dot(bot[i], jnp.concatenate([stack_heads(pp[i]), stack_heads(zv[i])], axis=0),
                  preferred_element_type=F32) for i in range(len(units))]
    end_from_mid = [jnp.exp(c_end[c] - c_mid[c]) for c in chunks]
    bk_hat = [jnp.concatenate([blk(b_mid, u), blk(k_mid, u)], axis=0) * lanes(end_from_mid[u[0]], u[1])
              for u in units]
    mm = [lax.dot_general(bk_hat[i].astype(BF16), jnp.concatenate([pp[i], zv[i]], axis=0).astype(BF16), TN_DIMS,
                          preferred_element_type=F32) for i in range(len(units))]
    q1 = [blk(r_true, u) + qq[i][:, :W] for i, u in enumerate(units)]
    q2 = [t[:, W:] for t in qq]
    m1 = [(jnp.where(same_head, mm[i][:, :W], 0.0)
           + jnp.where(diag_w, jnp.broadcast_to(lanes(g_last[u[0]], u[1]), (W, W)), 0.0)).astype(BF16)
          for i, u in enumerate(units)]
    m2 = [jnp.where(same_head, t[:, W:], 0.0) for t in mm]
    h_cur = [state[p] for p in pairs]
    h_in = []
    for i, (c, p) in enumerate(units):
        h_in.append(h_cur[p])
        h_cur[p] = jnp.dot(m1[i], h_cur[p].astype(BF16), preferred_element_type=F32) + m2[i]
    for p in pairs:
        state[p] = h_cur[p]
    y_blk = [bdot(q1[i], h_in[i]) + q2[i] for i in range(len(units))]
    y = jnp.concatenate([jnp.concatenate([y_blk[c * n_pair + p] for p in pairs], axis=1) for c in chunks], axis=0)
    mean = head_sums(y) * (1.0 / HEAD_DIM)
    yc = y - mean
    var = head_sums(yc * yc) * (1.0 / HEAD_DIM)
    yn = yc * lax.rsqrt(var + RWKV_GN_EPS) * lnw_ref[...] + lnb_ref[...]
    yg = (yn + bonus).astype(BF16) * g_ref[0]
    o_ref[0] = x_ref[0] + mod_ref[0, 2:3, :] * jnp.dot(yg, wo_ref[...], preferred_element_type=F32)


def _wkv_call(r, w_log, k, v, kk, a, ln_w, ln_b, r_k, g, x, mod, w_o, tc=256):
    batch, seq, width = r.shape
    tok = pl.BlockSpec((1, tc, width), lambda b, t: (b, t, 0))
    vec = pl.BlockSpec((1, width), lambda b, t: (0, 0))
    return pl.pallas_call(
        _wkv_body,
        grid=(batch, seq // tc),
        in_specs=[tok] * 6 + [vec] * 3 + [
            tok, tok, pl.BlockSpec((1, 3, width), lambda b, t: (b, 0, 0)),
            pl.BlockSpec((width, width), lambda b, t: (0, 0), pipeline_mode=pl.Buffered(1))],
        out_specs=tok,
        out_shape=jax.ShapeDtypeStruct(x.shape, F32),
        scratch_shapes=[pltpu.VMEM((width // LANES, LANES, LANES), F32)],
        compiler_params=_params("arbitrary", "arbitrary"),
        name="wkv7_chunked",
    )(r, w_log, k, v, kk, a, ln_w, ln_b, r_k, g, x, mod, w_o)


MOE_TILE = 512
MOE_BLOCK = 64
MOE_SLOTS = MOE_TILE + N_EXPERT_GROUPS * MOE_BLOCK
MOE_BLOCKS_PER_TILE = MOE_SLOTS // MOE_BLOCK
MOE_STEP_BLOCKS = 16


def _route(logits_t, rb):
    n_g, epg = N_EXPERT_GROUPS, EXPERTS_PER_GROUP
    s = jax.nn.sigmoid(logits_t)
    sel = s + rb
    s_rows = [s[i:i + 1, :] for i in range(N_EXPERTS)]
    sel_rows = [sel[i:i + 1, :] for i in range(N_EXPERTS)]
    scores = []
    for g in range(n_g):
        members = sel_rows[g * epg:(g + 1) * epg]
        best = None
        for i in range(epg):
            for j in range(i + 1, epg):
                pair = members[i] + members[j]
                best = pair if best is None else jnp.maximum(best, pair)
        scores.append(best)
    g_idx = jnp.zeros_like(scores[0], dtype=jnp.int32)
    top = scores[0]
    for g in range(1, n_g):
        better = scores[g] > top
        g_idx = jnp.where(better, g, g_idx)
        top = jnp.where(better, scores[g], top)

    def pick(rows_, j):
        out = rows_[j]
        for g in range(1, n_g):
            out = jnp.where(g_idx == g, rows_[g * epg + j], out)
        return out

    in_sel = [pick(sel_rows, j) for j in range(epg)]
    in_s = [pick(s_rows, j) for j in range(epg)]
    chosen = []
    for j in range(epg):
        rank = jnp.zeros_like(g_idx)
        for i in range(epg):
            if i == j:
                continue
            ahead = (in_sel[i] >= in_sel[j]) if i < j else (in_sel[i] > in_sel[j])
            rank = rank + jnp.where(ahead, 1, 0)
        chosen.append(rank < 2)
    den = sum(jnp.where(chosen[j], in_s[j], 0.0) for j in range(epg))
    gate_rows = []
    for ex in range(N_EXPERTS):
        g, j = divmod(ex, epg)
        on = jnp.logical_and(chosen[j], g_idx == g)
        gate_rows.append(jnp.where(on, in_s[j] / den, 0.0))
    return g_idx, gate_rows


def _slot_one_hot(pos_row):
    slot = lax.broadcasted_iota(jnp.int32, (MOE_SLOTS, pos_row.shape[1]), 0)
    return jnp.where(slot == pos_row, 1.0, 0.0).astype(BF16)


def _moe_sort_body(x_ref, mod_ref, rw_ref, rb_ref, hs_ref, gs_ref, pos_ref, cnt_ref):
    tm = MOE_TILE
    _moe_sort_tiles([x_ref[0, i * tm:(i + 1) * tm, :] for i in range(x_ref.shape[1] // tm)],
                    mod_ref, rw_ref, rb_ref, hs_ref, gs_ref, pos_ref, cnt_ref)


def _moe_sort_tiles(x_tiles, mod_ref, rw_ref, rb_ref, hs_ref, gs_ref, pos_ref, cnt_ref):
    tm = MOE_TILE
    n_g = N_EXPERT_GROUPS
    tiles = range(len(x_tiles))
    h = [_norm_mod(t, mod_ref) for t in x_tiles]
    hi = [t.astype(BF16) for t in h]
    lo = [(h[i] - hi[i].astype(F32)).astype(BF16) for i in tiles]
    parts = [jnp.dot(jnp.concatenate([hi[i], lo[i]], axis=0), rw_ref[...], preferred_element_type=F32)
             for i in tiles]
    logits = [(p[:tm, :LANES] + p[:tm, LANES:]) + (p[tm:, :LANES] + p[tm:, LANES:]) for p in parts]
    routed = [_route(t.T[:N_EXPERTS], rb_ref[...]) for t in logits]
    member = [[jnp.where(routed[i][0] == g, 1.0, 0.0) for g in range(n_g)] for i in tiles]
    earlier = jnp.where(lax.broadcasted_iota(jnp.int32, (tm, tm), 0) < lax.broadcasted_iota(jnp.int32, (tm, tm), 1),
                        1.0, 0.0).astype(BF16)
    rank = [jnp.dot(jnp.concatenate(member[i] + [jnp.zeros((8 - n_g, tm), F32)], axis=0).astype(BF16), earlier,
                    preferred_element_type=F32) for i in tiles]
    lane = lax.broadcasted_iota(jnp.int32, (1, LANES), 1)
    one_hot = []
    for i in tiles:
        pos = jnp.zeros((1, tm), F32)
        offset = jnp.zeros((1, 1), F32)
        counts = jnp.zeros((1, LANES), F32)
        for g in range(n_g):
            n_members = jnp.sum(member[i][g], axis=1, keepdims=True)
            n_blocks = jnp.floor((n_members + (MOE_BLOCK - 1)) * (1.0 / MOE_BLOCK))
            pos = pos + member[i][g] * (offset + rank[i][g:g + 1, :])
            counts = counts + jnp.where(lane == g, n_blocks, 0.0)
            offset = offset + n_blocks * MOE_BLOCK
        pos_i = pos.astype(jnp.int32)
        pos_ref[i] = pos_i
        cnt_ref[i] = counts.astype(jnp.int32)
        one_hot.append(_slot_one_hot(pos_i))
    pad = jnp.zeros((LANES - N_EXPERTS, tm), F32)
    gate_t = [jnp.concatenate(routed[i][1] + [pad], axis=0).T for i in tiles]
    g_hi = [t.astype(BF16) for t in gate_t]
    g_lo = [(gate_t[i] - g_hi[i].astype(F32)).astype(BF16) for i in tiles]
    hs = [jnp.dot(one_hot[i], hi[i], preferred_element_type=F32) for i in tiles]
    gs = [jnp.dot(one_hot[i], jnp.concatenate([g_hi[i], g_lo[i]], axis=1), preferred_element_type=F32)
          for i in tiles]
    for i in tiles:
        hs_ref[i * MOE_SLOTS:(i + 1) * MOE_SLOTS, :] = hs[i].astype(BF16)
        gs_ref[i * MOE_SLOTS:(i + 1) * MOE_SLOTS, :] = gs[i].astype(BF16)


def _moe_sort_call(x, mod, router_w2, router_b, tiles_per_step=4):
    batch, seq, _ = x.shape
    rows = tiles_per_step * MOE_TILE
    per_seq = seq // rows
    n_tiles = batch * seq // MOE_TILE
    step = lambda b, i: b * per_seq + i
    return pl.pallas_call(
        _moe_sort_body,
        grid=(batch, per_seq),
        in_specs=[
            pl.BlockSpec((1, rows, D_MODEL), lambda b, i: (b, i, 0)),
            pl.BlockSpec((1, 3, D_MODEL), lambda b, i: (b, 0, 0)),
            pl.BlockSpec((D_MODEL, 2 * LANES), lambda b, i: (0, 0)),
            pl.BlockSpec((N_EXPERTS, 1), lambda b, i: (0, 0)),
        ],
        out_specs=[
            pl.BlockSpec((tiles_per_step * MOE_SLOTS, D_MODEL), lambda b, i: (step(b, i), 0)),
            pl.BlockSpec((tiles_per_step * MOE_SLOTS, 2 * LANES), lambda b, i: (step(b, i), 0)),
            pl.BlockSpec((tiles_per_step, 1, MOE_TILE), lambda b, i: (step(b, i), 0, 0)),
            pl.BlockSpec((tiles_per_step, 1, LANES), lambda b, i: (step(b, i), 0, 0)),
        ],
        out_shape=[
            jax.ShapeDtypeStruct((n_tiles * MOE_SLOTS, D_MODEL), BF16),
            jax.ShapeDtypeStruct((n_tiles * MOE_SLOTS, 2 * LANES), BF16),
            jax.ShapeDtypeStruct((n_tiles, 1, MOE_TILE), jnp.int32),
            jax.ShapeDtypeStruct((n_tiles, 1, LANES), jnp.int32),
        ],
        compiler_params=_params("arbitrary", "arbitrary"),
        name="moe_sort",
    )(x, mod, router_w2, router_b)


def _moe_tables_body(cnt_ref, sg_ref, sv_ref, sb_ref, bp_ref):
    n_tiles = cnt_ref.shape[0]
    n_g, r, bpt = N_EXPERT_GROUPS, MOE_STEP_BLOCKS, MOE_BLOCKS_PER_TILE
    n_steps = sg_ref.shape[0]

    def fill(ref, n, value, unroll=4):
        def body(i, carry):
            for u in range(unroll):
                ref[i * unroll + u] = value
            return carry
        lax.fori_loop(0, n // unroll, body, 0)
        for i in range(n - n % unroll, n):
            ref[i] = value

    fill(sb_ref, n_steps * r, 0)
    fill(bp_ref, n_tiles * bpt, 0)
    fill(sg_ref, n_steps, n_g - 1)
    fill(sv_ref, n_steps, 0)
    slot = jnp.int32(0)
    for g in range(n_g):
        def tile_body(i, k, g=g):
            first = i * bpt
            for g_before in range(g):
                first = first + cnt_ref[i, g_before]

            def block_body(j, k):
                sb_ref[k] = first + j
                bp_ref[first + j] = k
                return k + 1
            return lax.fori_loop(0, cnt_ref[i, g], block_body, k)

        slot_end = lax.fori_loop(0, n_tiles, tile_body, slot)
        step_first = slot // r
        step_stop = (slot_end + r - 1) // r

        def step_body(s, carry, g=g, slot_end=slot_end):
            sg_ref[s] = g
            sv_ref[s] = jnp.minimum(slot_end - s * r, r)
            return carry
        lax.fori_loop(step_first, step_stop, step_body, 0)
        slot = step_stop * r


def _moe_step_tables(counts):
    n_tiles = counts.shape[0]
    n_blocks = n_tiles * MOE_BLOCKS_PER_TILE
    n_steps = n_blocks // MOE_STEP_BLOCKS + N_EXPERT_GROUPS
    smem = pl.BlockSpec(memory_space=pltpu.SMEM)
    return pl.pallas_call(
        _moe_tables_body,
        in_specs=[smem],
        out_specs=[smem] * 4,
        out_shape=[jax.ShapeDtypeStruct((n_steps,), jnp.int32), jax.ShapeDtypeStruct((n_steps,), jnp.int32),
                   jax.ShapeDtypeStruct((n_steps * MOE_STEP_BLOCKS,), jnp.int32),
                   jax.ShapeDtypeStruct((n_blocks,), jnp.int32)],
        name="moe_tables",
    )(counts)


def _moe_expert_body(sg_ref, sv_ref, sb_ref, *refs):
    r = MOE_STEP_BLOCKS
    h_refs, g_refs = refs[:r], refs[r:2 * r]
    w1_ref, w3_ref, w2_ref, o_ref, w13_scr, w2_scr, act_scr = refs[2 * r:]
    step = pl.program_id(0)
    epg = EXPERTS_PER_GROUP
    rows_total = r * MOE_BLOCK
    group = sg_ref[step]
    new_group = jnp.logical_or(step == 0, group != sg_ref[jnp.maximum(step - 1, 0)])

    @pl.when(sv_ref[step] == 0)
    def _():
        o_ref[...] = jnp.zeros_like(o_ref)

    @pl.when(jnp.logical_and(sv_ref[step] > 0, new_group))
    def _():
        w13_scr[:, :, :D_EXPERT] = w1_ref[0].astype(BF16)
        w13_scr[:, :, D_EXPERT:] = w3_ref[0].astype(BF16)
        w2_scr[...] = w2_ref[0].reshape(epg * D_EXPERT, D_MODEL).astype(BF16)

    @pl.when(sv_ref[step] > 0)
    def _():
        h = jnp.concatenate([ref[...] for ref in h_refs], axis=0)
        gates = jnp.concatenate([ref[...] for ref in g_refs], axis=0)
        real = lax.broadcasted_iota(jnp.int32, (rows_total, LANES), 0) < sv_ref[step] * MOE_BLOCK
        expert_row = lax.broadcasted_iota(jnp.int32, (2 * LANES, epg * LANES), 0) % LANES
        lane_block = lax.broadcasted_iota(jnp.int32, (2 * LANES, epg * LANES), 1) // LANES
        pick_rows = jnp.where(expert_row == group * epg + lane_block, 1.0, 0.0).astype(BF16)
        gate_all = jnp.dot(gates, pick_rows, preferred_element_type=F32)
        for j in range(epg):
            gate_b = jnp.where(real, gate_all[:, j * LANES:(j + 1) * LANES], 0.0)
            hid = jnp.dot(h, w13_scr[j], preferred_element_type=F32)
            h1 = hid[:, :D_EXPERT]
            act = h1 * jax.nn.sigmoid(h1) * hid[:, D_EXPERT:] * jnp.concatenate([gate_b, gate_b], axis=1)
            act_scr[:, j * D_EXPERT:(j + 1) * D_EXPERT] = act.astype(BF16)
        o_ref[...] = jnp.dot(act_scr[...], w2_scr[...], preferred_element_type=F32).astype(BF16)


def _moe_expert_call(hs, gs, w1, w3, w2, step_group, step_valid, step_blocks, layer):
    r = MOE_STEP_BLOCKS
    n_steps = step_group.shape[0]
    epg = EXPERTS_PER_GROUP
    rows_total = r * MOE_BLOCK

    def blk(width, j):
        return pl.BlockSpec((MOE_BLOCK, width), lambda s, sg, sv, sb, j=j: (sb[s * r + j], 0))

    def group_weights(shape):
        return pl.BlockSpec((1, epg) + shape, lambda s, sg, sv, sb: (layer, sg[s], 0, 0))

    grid_spec = pltpu.PrefetchScalarGridSpec(
        num_scalar_prefetch=3,
        grid=(n_steps,),
        in_specs=[blk(D_MODEL, j) for j in range(r)] + [blk(2 * LANES, j) for j in range(r)] + [
            group_weights((D_MODEL, D_EXPERT)), group_weights((D_MODEL, D_EXPERT)),
            group_weights((D_EXPERT, D_MODEL))],
        out_specs=pl.BlockSpec((rows_total, D_MODEL), lambda s, sg, sv, sb: (s, 0)),
        scratch_shapes=[
            pltpu.VMEM((epg, D_MODEL, 2 * D_EXPERT), BF16),
            pltpu.VMEM((epg * D_EXPERT, D_MODEL), BF16),
            pltpu.VMEM((rows_total, epg * D_EXPERT), BF16),
        ],
    )
    return pl.pallas_call(
        _moe_expert_body,
        grid_spec=grid_spec,
        out_shape=jax.ShapeDtypeStruct((n_steps * rows_total, D_MODEL), BF16),
        compiler_params=_params("arbitrary"),
        name="moe_experts",
    )(step_group, step_valid, step_blocks, *([hs] * r), *([gs] * r), w1, w3, w2)


def _moe_unsort_body(is_last, bp_ref, *refs):
    bpt = MOE_BLOCKS_PER_TILE
    pos_ref, x_ref, mod_ref, fg_ref, o_ref = refs[-5:]
    y_refs = refs[:-5]
    tm = MOE_TILE
    tiles = range(len(y_refs) // bpt)
    slot = lax.broadcasted_iota(jnp.int32, (tm, MOE_SLOTS), 1)
    one_hot = []
    for i in tiles:
        pos_col = jnp.broadcast_to(pos_ref[i].astype(F32), (LANES, tm)).T.astype(jnp.int32)
        pos_col = jnp.concatenate([pos_col] * (MOE_SLOTS // LANES), axis=1)
        one_hot.append(jnp.where(slot == pos_col, 1.0, 0.0).astype(BF16))
    ys = [jnp.concatenate([ref[...] for ref in y_refs[i * bpt:(i + 1) * bpt]], axis=0) for i in tiles]
    y = [jnp.dot(one_hot[i], ys[i], preferred_element_type=F32) for i in tiles]
    for i in tiles:
        out = x_ref[0, i * tm:(i + 1) * tm, :] + mod_ref[0, 2:3, :] * y[i]
        if is_last:
            out = out * lax.rsqrt(jnp.mean(out * out, axis=-1, keepdims=True) + RMS_EPS) * fg_ref[...]
        o_ref[0, i * tm:(i + 1) * tm, :] = out


def _moe_unsort_call(ys, pos, x, mod, final_g, block_pos, is_last, tiles_per_step=4):
    batch, seq, _ = x.shape
    rows = tiles_per_step * MOE_TILE
    per_seq = seq // rows
    bpt = MOE_BLOCKS_PER_TILE
    n_in = tiles_per_step * bpt

    def blk(j):
        return pl.BlockSpec((MOE_BLOCK, D_MODEL), lambda b, i, bp, j=j: (bp[(b * per_seq + i) * n_in + j], 0))

    grid_spec = pltpu.PrefetchScalarGridSpec(
        num_scalar_prefetch=1,
        grid=(batch, per_seq),
        in_specs=[blk(j) for j in range(n_in)] + [
            pl.BlockSpec((tiles_per_step, 1, MOE_TILE), lambda b, i, bp: (b * per_seq + i, 0, 0)),
            pl.BlockSpec((1, rows, D_MODEL), lambda b, i, bp: (b, i, 0)),
            pl.BlockSpec((1, 3, D_MODEL), lambda b, i, bp: (b, 0, 0)),
            pl.BlockSpec((1, D_MODEL), lambda b, i, bp: (0, 0)),
        ],
        out_specs=pl.BlockSpec((1, rows, D_MODEL), lambda b, i, bp: (b, i, 0)),
    )
    return pl.pallas_call(
        functools.partial(_moe_unsort_body, is_last),
        grid_spec=grid_spec,
        out_shape=jax.ShapeDtypeStruct(x.shape, F32),
        compiler_params=_params("arbitrary", "arbitrary"),
        name="moe_unsort",
    )(block_pos, *([ys] * n_in), pos, x, mod, final_g)


def _moe_layer(x, mod, router_w2, router_b, w1, w3, w2, final_g, layer, is_last, routed=None):
    hs, gs, pos, cnt = routed if routed is not None else _moe_sort_call(x, mod, router_w2, router_b)
    step_group, step_valid, step_blocks, block_pos = _moe_step_tables(cnt[:, 0, :])
    ys = _moe_expert_call(hs, gs, w1, w3, w2, step_group, step_valid, step_blocks, layer)
    return _moe_unsort_call(ys, pos, x, mod, final_g, block_pos, is_last)


def _pad_cols(w):
    n = w.shape[-1]
    return jnp.pad(w, ((0, 0), (0, -n % LANES)))


def _pad_rows(w):
    n = w.shape[0]
    return jnp.pad(w, ((0, -n % LANES), (0, 0)))


def kernel(x, c, ada_w, ada_b, mix_w_in, mix_w_out, sgu_norm_g, sgu_w, sgu_b, rwkv_mu, rwkv_w_rkv, rwkv_w_o, rwkv_w0, rwkv_w1, rwkv_w2, rwkv_a0, rwkv_a1, rwkv_a2, rwkv_v0, rwkv_v1, rwkv_v2, rwkv_g1, rwkv_g2, rwkv_k_k, rwkv_k_a, rwkv_r_k, rwkv_ln_w, rwkv_ln_b, router_w, router_b, moe_w1, moe_w3, moe_w2, final_norm_g):
    depth = ada_w.shape[0]
    row = lambda t: t.reshape(1, -1)
    mods = _ada_call(c, ada_w, ada_b)
    rw_hi = router_w.astype(BF16)
    rw_lo = (router_w - rw_hi.astype(F32)).astype(BF16)
    router_w2 = jnp.concatenate([_pad_cols(rw_hi), _pad_cols(rw_lo)], axis=1)
    router_bc = router_b.reshape(N_EXPERTS, 1)
    final_g = row(final_norm_g)
    v_first = None
    for layer in range(depth):
        i = layer // 2
        mod = mods[2 * layer]
        if layer % 2 == 0:
            proj = _proj_call(x, mod, mix_w_in[i].astype(BF16))
            o_a = _attn_call(proj)
            bias_tile = jnp.repeat(sgu_b[i].T, HEAD_DIM, axis=1)
            x, *routed = _sgu_out_call(proj, o_a, x, mod, row(sgu_norm_g[i]), sgu_w[i], bias_tile,
                                       mix_w_out[i].astype(BF16), mods[2 * layer + 1], router_w2, router_bc)
        else:
            routed = None
            weights = [
                rwkv_mu[i],
                rwkv_w_rkv[i, 0].astype(BF16), rwkv_w_rkv[i, 1].astype(BF16), rwkv_w_rkv[i, 2].astype(BF16),
                row(rwkv_w0[i]), _pad_cols(rwkv_w1[i]).astype(BF16), _pad_rows(rwkv_w2[i]).astype(BF16),
                row(rwkv_a0[i]), _pad_cols(rwkv_a1[i]).astype(BF16), _pad_rows(rwkv_a2[i]).astype(BF16),
                _pad_cols(rwkv_g1[i]).astype(BF16), _pad_rows(rwkv_g2[i]).astype(BF16),
                row(rwkv_k_k[i]), row(rwkv_k_a[i]),
            ]
            vres = None
            if i > 0:
                vres = [row(rwkv_v0[i - 1]), _pad_cols(rwkv_v1[i - 1]).astype(BF16),
                        _pad_rows(rwkv_v2[i - 1]).astype(BF16)]
            r, w_log, k, v, kk, a, g = _rwkv_pre_call(x, mod, weights, v_first, vres)
            if i == 0:
                v_first = v
            x = _wkv_call(r, w_log, k, v, kk, a, row(rwkv_ln_w[i]), row(rwkv_ln_b[i]), row(rwkv_r_k[i]),
                          g, x, mod, rwkv_w_o[i].astype(BF16))
        x = _moe_layer(x, mods[2 * layer + 1], router_w2, router_bc, moe_w1, moe_w3, moe_w2, final_g, layer,
                       layer == depth - 1, routed)
    return x
```

```python
import functools

import jax
import jax.numpy as jnp
from jax import lax
from jax.experimental import pallas as pl
from jax.experimental.pallas import tpu as pltpu

F32 = jnp.float32
BF16 = jnp.bfloat16

D_MODEL = 1024
HEAD_DIM = 64
WIDTH_A = 512
WIDTH_B = 512
MIX_IN_WIDTH = 2560
ATTN_SPAN = 128
ATTN_DILATIONS = (1, 4, 16)
SGU_CHUNK = 128
N_GROUPS_B = 8
RWKV_HEADS = 16
RWKV_GN_EPS = 64e-5
WKV_CHUNK = 64
N_EXPERTS = 16
N_EXPERT_GROUPS = 4
EXPERTS_PER_GROUP = 4
D_EXPERT = 256
RMS_EPS = 1e-6
LANES = 128
VMEM_LIMIT = 56 * 1024 * 1024

NT_DIMS = (((1,), (1,)), ((), ()))
TN_DIMS = (((0,), (0,)), ((), ()))


def _params(*sem):
    return pltpu.CompilerParams(dimension_semantics=sem, vmem_limit_bytes=VMEM_LIMIT)


def _norm_mod(x, mod_ref):
    ms = jnp.mean(x * x, axis=-1, keepdims=True)
    return (x * lax.rsqrt(ms + RMS_EPS)) * (1.0 + mod_ref[0, 1:2, :]) + mod_ref[0, 0:1, :]


def _gelu(x):
    return 0.5 * x * (1.0 + lax.erf(x * 0.7071067811865476))


def _ada_body(c_ref, w_ref, b_ref, o_ref):
    c = c_ref[...]
    c_act = c * jax.nn.sigmoid(c)
    n = c.shape[0]
    c_hi = c_act.astype(BF16)
    c_lo = (c_act - c_hi.astype(F32)).astype(BF16)
    c2 = jnp.concatenate([c_hi, c_lo], axis=0)
    w = w_ref[0]
    w_hi = w.astype(BF16)
    w_lo = (w - w_hi.astype(F32)).astype(BF16)
    by_hi = jnp.dot(c2, w_hi, preferred_element_type=F32)
    by_lo = jnp.dot(c2, w_lo, preferred_element_type=F32)
    o_ref[0] = (by_hi[:n] + by_hi[n:]) + (by_lo[:n] + by_lo[n:]) + b_ref[0]


def _ada_call(c, ada_w, ada_b):
    n_pair = ada_w.shape[0] * ada_w.shape[1]
    batch = c.shape[0]
    w = ada_w.reshape(n_pair, D_MODEL, 3 * D_MODEL)
    b = ada_b.reshape(n_pair, 1, 3 * D_MODEL)
    out = pl.pallas_call(
        _ada_body,
        grid=(n_pair, 3),
        in_specs=[
            pl.BlockSpec((batch, D_MODEL), lambda p, j: (0, 0)),
            pl.BlockSpec((1, D_MODEL, D_MODEL), lambda p, j: (p, 0, j)),
            pl.BlockSpec((1, 1, D_MODEL), lambda p, j: (p, 0, j)),
        ],
        out_specs=pl.BlockSpec((1, batch, D_MODEL), lambda p, j: (p, 0, j)),
        out_shape=jax.ShapeDtypeStruct((n_pair, batch, 3 * D_MODEL), F32),
        compiler_params=_params("arbitrary", "arbitrary"),
        name="ada_mod",
    )(c, w, b)
    return out.reshape(n_pair, batch, 3, D_MODEL)


def _proj_body(x_ref, mod_ref, w_ref, qkv_ref, uz_ref):
    h = _norm_mod(x_ref[0], mod_ref)
    out = jnp.dot(h.astype(BF16), w_ref[...], preferred_element_type=F32)
    qkv_ref[0] = out[:, :3 * WIDTH_A]
    uz_ref[0] = out[:, 3 * WIDTH_A:].astype(BF16)


def _proj_call(x, mod, w_in, ts=1024):
    batch, seq, _ = x.shape
    return pl.pallas_call(
        _proj_body,
        grid=(batch, seq // ts),
        in_specs=[
            pl.BlockSpec((1, ts, D_MODEL), lambda b, i: (b, i, 0)),
            pl.BlockSpec((1, 3, D_MODEL), lambda b, i: (b, 0, 0)),
            pl.BlockSpec((D_MODEL, MIX_IN_WIDTH), lambda b, i: (0, 0)),
        ],
        out_specs=[pl.BlockSpec((1, ts, 3 * WIDTH_A), lambda b, i: (b, i, 0)),
                   pl.BlockSpec((1, ts, 2 * WIDTH_B), lambda b, i: (b, i, 0))],
        out_shape=[jax.ShapeDtypeStruct((batch, seq, 3 * WIDTH_A), F32),
                   jax.ShapeDtypeStruct((batch, seq, 2 * WIDTH_B), BF16)],
        compiler_params=_params("arbitrary", "arbitrary"),
        name="mix_in_proj",
    )(x, mod, w_in)


def _attn_body(q_ref, k_ref, v_ref, o_ref, ob, lb):
    span = ATTN_SPAN
    W = LANES
    block_group = 8

    def iota(shape, dim):
        return lax.broadcasted_iota(jnp.int32, shape, dim)

    head0 = iota((span, W), 1) < HEAD_DIM
    mine = (head0, jnp.logical_not(head0))
    row2, col2 = iota((span, 2 * span), 0), iota((span, 2 * span), 1)
    mask_prev_cur = jnp.where(col2 < span, col2 - row2, row2 - (col2 - span)) >= 0
    mask_cur = iota((span, span), 1) <= iota((span, span), 0)

    def rows(start, n_rows, dil):
        return pl.ds(start, n_rows) if dil == 1 else pl.ds(start, n_rows, stride=dil)

    def group(p, dil, q_starts, with_prev):
        blocks = range(len(q_starts))
        heads = range(2)
        qb, kcat, vaug, masks = [], [], [], []
        for j in blocks:
            n_keys = 2 * span if with_prev[j] else span
            k_start = q_starts[j] - span * dil if with_prev[j] else q_starts[j]
            qb.append(q_ref[0, rows(q_starts[j], span, dil), :] * (HEAD_DIM ** -0.5))
            kcat.append(k_ref[0, rows(k_start, n_keys, dil), :].astype(BF16))
            vb = v_ref[0, rows(k_start, n_keys, dil), :].astype(BF16)
            vaug.append(jnp.concatenate([vb, jnp.ones((n_keys, W), BF16)], axis=1))
            masks.append(mask_prev_cur if with_prev[j] else mask_cur)
        qh = [[jnp.where(mine[h], qb[j], 0.0).astype(BF16) for h in heads] for j in blocks]
        s = [[lax.dot_general(qh[j][h], kcat[j], NT_DIMS, preferred_element_type=F32) for h in heads] for j in blocks]
        s = [[jnp.where(masks[j], s[j][h], -jnp.inf) for h in heads] for j in blocks]
        m = [[jnp.max(s[j][h], axis=-1, keepdims=True) for h in heads] for j in blocks]
        pr = [[jnp.exp(s[j][h] - m[j][h]).astype(BF16) for h in heads] for j in blocks]
        ad = [[jnp.dot(pr[j][h], vaug[j], preferred_element_type=F32) for h in heads] for j in blocks]
        for j in blocks:
            den = jnp.where(head0, ad[j][0][:, W:], ad[j][1][:, W:])
            o = jnp.where(head0, ad[j][0][:, :W], ad[j][1][:, :W]) / den
            lse = jnp.where(head0, m[j][0], m[j][1]) + jnp.log(den)
            qi = rows(q_starts[j], span, dil)
            ob[p, qi, :] = o
            lb[p, qi, :] = lse

    seq = q_ref.shape[1]
    for p, dil in enumerate(ATTN_DILATIONS):
        n_blk = seq // dil // span
        blocks = [(r + n * span * dil, n > 0) for r in range(dil) for n in range(n_blk)]
        for g0 in range(0, len(blocks), block_group):
            members = blocks[g0:g0 + block_group]
            group(p, dil, [b[0] for b in members], [b[1] for b in members])

    l0, l1, l2 = lb[0], lb[1], lb[2]
    m = jnp.maximum(jnp.maximum(l0, l1), l2)
    w0, w1, w2 = jnp.exp(l0 - m), jnp.exp(l1 - m), jnp.exp(l2 - m)
    o_ref[0] = (w0 * ob[0] + w1 * ob[1] + w2 * ob[2]) / (w0 + w1 + w2)


def _attn_call(proj):
    batch, seq, _ = proj.shape
    n_pair = WIDTH_A // LANES
    blk = lambda off: pl.BlockSpec((1, seq, LANES), lambda b, h: (b, 0, off + h))
    return pl.pallas_call(
        _attn_body,
        grid=(batch, n_pair),
        in_specs=[blk(0), blk(n_pair), blk(2 * n_pair)],
        out_specs=pl.BlockSpec((1, seq, LANES), lambda b, h: (b, 0, h)),
        out_shape=jax.ShapeDtypeStruct((batch, seq, WIDTH_A), F32),
        scratch_shapes=[
            pltpu.VMEM((3, seq, LANES), F32),
            pltpu.VMEM((3, seq, LANES), F32),
        ],
        compiler_params=_params("arbitrary", "arbitrary"),
        name="dilated_attn",
    )(proj, proj, proj)


def _sgu_out_body(u_ref, z_ref, oa_ref, x_ref, mod_ref, ng_ref, sw_ref, sb_ref, wo_ref, moe_mod_ref, rw_ref, rb_ref,
                  o_ref, hs_ref, gs_ref, pos_ref, cnt_ref, mix):
    ts = u_ref.shape[1]
    u = _gelu(u_ref[0].astype(F32))
    z = _gelu(z_ref[0].astype(F32))
    z = z * lax.rsqrt(jnp.mean(z * z, axis=-1, keepdims=True) + RMS_EPS) * ng_ref[...]
    zb = z.astype(BF16)
    row = lax.broadcasted_iota(jnp.int32, (SGU_CHUNK, SGU_CHUNK), 0)
    col = lax.broadcasted_iota(jnp.int32, (SGU_CHUNK, SGU_CHUNK), 1)
    causal = col <= row
    for g in range(N_GROUPS_B):
        w_g = jnp.where(causal, sw_ref[g], 0.0).astype(BF16)
        for cc in range(ts // SGU_CHUNK):
            rs = slice(cc * SGU_CHUNK, (cc + 1) * SGU_CHUNK)
            ls = slice(g * HEAD_DIM, (g + 1) * HEAD_DIM)
            mix[rs, ls] = jnp.dot(w_g, zb[rs, ls], preferred_element_type=F32)
    bias = jnp.concatenate([sb_ref[...]] * (ts // SGU_CHUNK), axis=0)
    o_b = u * (mix[...] + bias)
    y = jnp.dot(oa_ref[0].astype(BF16), wo_ref[0:WIDTH_A, :], preferred_element_type=F32)
    y = y + jnp.dot(o_b.astype(BF16), wo_ref[WIDTH_A:, :], preferred_element_type=F32)
    out = x_ref[0] + mod_ref[0, 2:3, :] * y
    o_ref[0] = out
    _moe_sort_tiles([out[i * MOE_TILE:(i + 1) * MOE_TILE] for i in range(ts // MOE_TILE)],
                    moe_mod_ref, rw_ref, rb_ref, hs_ref, gs_ref, pos_ref, cnt_ref)


def _sgu_out_call(proj, o_a, x, mod, norm_g, sgu_w, sgu_bias_tile, w_out, moe_mod, router_w2, router_b, ts=1024):
    batch, seq, _ = x.shape
    u_blk = 0
    full = lambda shape: pl.BlockSpec(shape, lambda b, i: (0,) * len(shape))
    per_seq = seq // ts
    tiles = ts // MOE_TILE
    n_tiles = batch * seq // MOE_TILE
    step = lambda b, i: b * per_seq + i
    return pl.pallas_call(
        _sgu_out_body,
        grid=(batch, seq // ts),
        in_specs=[
            pl.BlockSpec((1, ts, WIDTH_B), lambda b, i: (b, i, u_blk)),
            pl.BlockSpec((1, ts, WIDTH_B), lambda b, i: (b, i, u_blk + 1)),
            pl.BlockSpec((1, ts, WIDTH_A), lambda b, i: (b, i, 0)),
            pl.BlockSpec((1, ts, D_MODEL), lambda b, i: (b, i, 0)),
            pl.BlockSpec((1, 3, D_MODEL), lambda b, i: (b, 0, 0)),
            full((1, WIDTH_B)),
            full((N_GROUPS_B, SGU_CHUNK, SGU_CHUNK)),
            full((SGU_CHUNK, WIDTH_B)),
            full((WIDTH_A + WIDTH_B, D_MODEL)),
            pl.BlockSpec((1, 3, D_MODEL), lambda b, i: (b, 0, 0)),
            full((D_MODEL, 2 * LANES)),
            full((N_EXPERTS, 1)),
        ],
        out_specs=[
            pl.BlockSpec((1, ts, D_MODEL), lambda b, i: (b, i, 0)),
            pl.BlockSpec((tiles * MOE_SLOTS, D_MODEL), lambda b, i: (step(b, i), 0)),
            pl.BlockSpec((tiles * MOE_SLOTS, 2 * LANES), lambda b, i: (step(b, i), 0)),
            pl.BlockSpec((tiles, 1, MOE_TILE), lambda b, i: (step(b, i), 0, 0)),
            pl.BlockSpec((tiles, 1, LANES), lambda b, i: (step(b, i), 0, 0)),
        ],
        out_shape=[
            jax.ShapeDtypeStruct(x.shape, F32),
            jax.ShapeDtypeStruct((n_tiles * MOE_SLOTS, D_MODEL), BF16),
            jax.ShapeDtypeStruct((n_tiles * MOE_SLOTS, 2 * LANES), BF16),
            jax.ShapeDtypeStruct((n_tiles, 1, MOE_TILE), jnp.int32),
            jax.ShapeDtypeStruct((n_tiles, 1, LANES), jnp.int32),
        ],
        scratch_shapes=[pltpu.VMEM((ts, WIDTH_B), F32)],
        compiler_params=_params("arbitrary", "arbitrary"),
        name="sgu_out_proj",
    )(proj, proj, o_a, x, mod, norm_g, sgu_w, sgu_bias_tile, w_out, moe_mod, router_w2, router_b)


def _rwkv_pre_body(has_vres, *refs):
    if has_vres:
        (x_ref, mod_ref, mu_ref, wr_ref, wk_ref, wv_ref, w0_ref, w1_ref, w2_ref, a0_ref, a1_ref, a2_ref,
         g1_ref, g2_ref, kk_ref, ka_ref, vf_ref, v0_ref, v1_ref, v2_ref,
         r_out, w_out, k_out, v_out, kk_out, a_out, g_out, carry) = refs
    else:
        (x_ref, mod_ref, mu_ref, wr_ref, wk_ref, wv_ref, w0_ref, w1_ref, w2_ref, a0_ref, a1_ref, a2_ref,
         g1_ref, g2_ref, kk_ref, ka_ref,
         r_out, w_out, k_out, v_out, kk_out, a_out, g_out, carry) = refs

    @pl.when(pl.program_id(1) == 0)
    def _():
        carry[...] = jnp.zeros_like(carry)

    h = _norm_mod(x_ref[0], mod_ref)
    ts = h.shape[0]
    first = lax.broadcasted_iota(jnp.int32, h.shape, 0) == 0
    h_prev = jnp.where(first, carry[0:1, :], pltpu.roll(h, 1, 0))
    carry[0:1, :] = h[ts - 1:ts, :]
    xx = h_prev - h

    def mixed(i):
        return (h + xx * mu_ref[i:i + 1, :]).astype(BF16)

    def mm(a, w_ref):
        return jnp.dot(a, w_ref[...], preferred_element_type=F32)

    xr, xw, xk, xv, xa, xg = [mixed(i) for i in range(6)]
    w_low = mm(xw, w1_ref)
    a_low = mm(xa, a1_ref)
    g_low = mm(xg, g1_ref)
    r = mm(xr, wr_ref)
    k = mm(xk, wk_ref)
    v = mm(xv, wv_ref)
    v_low = mm(xv, v1_ref) if has_vres else None
    z = w0_ref[...] + mm(jnp.tanh(w_low).astype(BF16), w2_ref)
    w_log = -(jnp.maximum(-z, 0.0) + jnp.log(1.0 + jnp.exp(-jnp.abs(z)))) - 0.5
    if has_vres:
        mix_v = jax.nn.sigmoid(v0_ref[...] + mm(v_low.astype(BF16), v2_ref))
        v = v + (vf_ref[0].astype(F32) - v) * mix_v
    a = jax.nn.sigmoid(a0_ref[...] + mm(a_low.astype(BF16), a2_ref))
    g = mm(jax.nn.sigmoid(g_low).astype(BF16), g2_ref)
    r_out[0] = r.astype(BF16)
    w_out[0] = w_log
    kk_out[0] = (k * kk_ref[...]).astype(BF16)
    k_out[0] = (k * (1.0 + (a - 1.0) * ka_ref[...])).astype(BF16)
    v_out[0] = v.astype(BF16)
    a_out[0] = a.astype(BF16)
    g_out[0] = g.astype(BF16)


def _rwkv_pre_call(x, mod, weights, v_first, vres, ts=512):
    batch, seq, _ = x.shape
    tok = pl.BlockSpec((1, ts, D_MODEL), lambda b, i: (b, i, 0))
    full = lambda a: pl.BlockSpec(a.shape, lambda b, i: (0,) * a.ndim, pipeline_mode=pl.Buffered(1))
    ins = [x, mod] + list(weights)
    specs = [tok, pl.BlockSpec((1, 3, D_MODEL), lambda b, i: (b, 0, 0))] + [full(a) for a in weights]
    if vres is not None:
        ins += [v_first] + list(vres)
        specs += [tok] + [full(a) for a in vres]
    out_dtypes = [BF16, F32, BF16, BF16, BF16, BF16, BF16]
    return pl.pallas_call(
        functools.partial(_rwkv_pre_body, vres is not None),
        grid=(batch, seq // ts),
        in_specs=specs,
        out_specs=[tok] * 7,
        out_shape=[jax.ShapeDtypeStruct(x.shape, dt) for dt in out_dtypes],
        scratch_shapes=[pltpu.VMEM((8, D_MODEL), F32)],
        compiler_params=_params("arbitrary", "arbitrary"),
        name="rwkv_pre",
    )(*ins)


def _wkv_body(r_ref, w_ref, k_ref, v_ref, kk_ref, a_ref, lnw_ref, lnb_ref, rk_ref, g_ref, x_ref, mod_ref, wo_ref,
              o_ref, state):
    C = WKV_CHUNK
    W = LANES
    n_chunk = r_ref.shape[1] // C
    n_pair = r_ref.shape[2] // W

    @pl.when(pl.program_id(1) == 0)
    def _():
        state[...] = jnp.zeros_like(state)

    def iota(shape, dim):
        return lax.broadcasted_iota(jnp.int32, shape, dim)

    def stack_heads(x):
        own = iota(x.shape, 1) % W < HEAD_DIM
        return jnp.concatenate([jnp.where(own, x, 0.0), jnp.where(own, 0.0, x)], axis=0).astype(BF16)

    def block_diag(x):
        left = iota(x.shape, 1) < C
        return jnp.concatenate([jnp.where(left, x, 0.0), jnp.where(left, 0.0, x)], axis=0).astype(BF16)

    row4, col4 = iota((C, 4 * C), 0), iota((C, 4 * C), 1) % C
    strict4 = col4 < row4
    lower4 = col4 <= row4
    tri = jnp.where(iota((C, C), 1) <= iota((C, C), 0), 1.0, 0.0).astype(BF16)
    eye2 = jnp.where(iota((C, 2 * C), 1) % C == iota((C, 2 * C), 0), 1.0, 0.0).astype(F32)
    rw, cw = iota((W, W), 0), iota((W, W), 1)
    same_head = (rw < HEAD_DIM) == (cw < HEAD_DIM)
    ones_bd = jnp.where(same_head, 1.0, 0.0).astype(BF16)
    diag_w = cw == rw
    zeros_cw = jnp.zeros((C, W), F32)

    def bdot(a, b):
        return jnp.dot(a.astype(BF16), b.astype(BF16), preferred_element_type=F32)

    chunks = range(n_chunk)
    pairs = range(n_pair)
    units = [(c, p) for c in chunks for p in pairs]
    rows = lambda t, c: t[c * C:(c + 1) * C]
    blk = lambda t, u: t[u[0] * C:(u[0] + 1) * C, u[1] * W:(u[1] + 1) * W]
    lanes = lambda t, p: t[:, p * W:(p + 1) * W]
    per_chunk = lambda vals: jnp.concatenate([jnp.broadcast_to(t, (C, n_pair * W)) for t in vals], axis=0)
    head_sums = lambda t: jnp.concatenate([bdot(lanes(t, p), ones_bd) for p in pairs], axis=1)

    r = r_ref[0].astype(F32)
    k = k_ref[0].astype(F32)
    v = v_ref[0].astype(F32)
    kk = kk_ref[0].astype(F32)
    kk = kk / jnp.maximum(jnp.sqrt(head_sums(kk * kk)), 1e-12)
    b = kk * a_ref[0].astype(F32)
    bonus = head_sums(r * k * rk_ref[...]) * v
    log_w = -jnp.exp(w_ref[0])
    hi = log_w.astype(BF16)
    lo = (log_w - hi.astype(F32)).astype(BF16)
    hilo = jnp.concatenate([hi, lo], axis=1)
    cum2 = [jnp.dot(tri, rows(hilo, c), preferred_element_type=F32) for c in chunks]
    cum = jnp.concatenate([c2[:, :n_pair * W] + c2[:, n_pair * W:] for c2 in cum2], axis=0)
    c_mid = [cum[c * C + C // 2 - 1:c * C + C // 2, :] for c in chunks]
    c_end = [cum[c * C + C - 1:(c + 1) * C, :] for c in chunks]
    g_last = [jnp.exp(t) for t in c_end]
    from_mid = jnp.exp(per_chunk(c_mid) - cum)
    to_mid = per_chunk([jnp.exp(-t) for t in c_mid])
    a_true = -kk * jnp.exp(cum - log_w)
    r_true = r * jnp.exp(cum)
    a_mid = a_true * to_mid
    r_mid = r_true * to_mid
    b_mid = b * from_mid
    k_mid = k * from_mid
    ar_mid = [jnp.concatenate([blk(a_mid, u), blk(r_mid, u)], axis=0).astype(BF16) for u in units]
    gram = [lax.dot_general(ar_mid[i], jnp.concatenate([stack_heads(blk(b_mid, u)), stack_heads(blk(k_mid, u))],
                                                        axis=0), NT_DIMS, preferred_element_type=F32)
            for i, u in enumerate(units)]
    top = [jnp.where(strict4, g[:C], 0.0) for g in gram]
    bot = [jnp.where(lower4, g[C:], 0.0).astype(BF16) for g in gram]
    n1 = [t[:, :2 * C] for t in top]
    inv = [eye2 + t for t in n1]
    pw = [bdot(t, block_diag(t)) for t in n1]
    levels = C.bit_length() - 2
    for lvl in range(levels):
        if lvl < levels - 1:
            st = [bdot(jnp.concatenate([pw[i], inv[i]], axis=0), block_diag(pw[i])) for i in range(len(units))]
            pw = [t[:C] for t in st]
            inv = [inv[i] + st[i][C:] for i in range(len(units))]
        else:
            inv = [inv[i] + bdot(inv[i], block_diag(pw[i])) for i in range(len(units))]
    x = [jnp.dot(top[i][:, 2 * C:].astype(BF16), stack_heads(blk(v, u)), preferred_element_type=F32)
         for i, u in enumerate(units)]
    pp = [jnp.dot(inv[i].astype(BF16), stack_heads(jnp.concatenate([blk(a_true, u), x[i]], axis=1)),
                  preferred_element_type=F32) for i, u in enumerate(units)]
    zv = [jnp.concatenate([zeros_cw, blk(v, u)], axis=1) for u in units]
    qq = [jnp.dot(bot[i], jnp.concatenate([stack_heads(pp[i]), stack_heads(zv[i])], axis=0),
                  preferred_element_type=F32) for i in range(len(units))]
    end_from_mid = [jnp.exp(c_end[c] - c_mid[c]) for c in chunks]
    bk_hat = [jnp.concatenate([blk(b_mid, u), blk(k_mid, u)], axis=0) * lanes(end_from_mid[u[0]], u[1])
              for u in units]
    mm = [lax.dot_general(bk_hat[i].astype(BF16), jnp.concatenate([pp[i], zv[i]], axis=0).astype(BF16), TN_DIMS,
                          preferred_element_type=F32) for i in range(len(units))]
    q1 = [blk(r_true, u) + qq[i][:, :W] for i, u in enumerate(units)]
    q2 = [t[:, W:] for t in qq]
    m1 = [(jnp.where(same_head, mm[i][:, :W], 0.0)
           + jnp.where(diag_w, jnp.broadcast_to(lanes(g_last[u[0]], u[1]), (W, W)), 0.0)).astype(BF16)
          for i, u in enumerate(units)]
    m2 = [jnp.where(same_head, t[:, W:], 0.0) for t in mm]
    h_cur = [state[p] for p in pairs]
    h_in = []
    for i, (c, p) in enumerate(units):
        h_in.append(h_cur[p])
        h_cur[p] = jnp.dot(m1[i], h_cur[p].astype(BF16), preferred_element_type=F32) + m2[i]
    for p in pairs:
        state[p] = h_cur[p]
    y_blk = [bdot(q1[i], h_in[i]) + q2[i] for i in range(len(units))]
    y = jnp.concatenate([jnp.concatenate([y_blk[c * n_pair + p] for p in pairs], axis=1) for c in chunks], axis=0)
    mean = head_sums(y) * (1.0 / HEAD_DIM)
    yc = y - mean
    var = head_sums(yc * yc) * (1.0 / HEAD_DIM)
    yn = yc * lax.rsqrt(var + RWKV_GN_EPS) * lnw_ref[...] + lnb_ref[...]
    yg = (yn + bonus).astype(BF16) * g_ref[0]
    o_ref[0] = x_ref[0] + mod_ref[0, 2:3, :] * jnp.dot(yg, wo_ref[...], preferred_element_type=F32)


def _wkv_call(r, w_log, k, v, kk, a, ln_w, ln_b, r_k, g, x, mod, w_o, tc=256):
    batch, seq, width = r.shape
    tok = pl.BlockSpec((1, tc, width), lambda b, t: (b, t, 0))
    vec = pl.BlockSpec((1, width), lambda b, t: (0, 0))
    return pl.pallas_call(
        _wkv_body,
        grid=(batch, seq // tc),
        in_specs=[tok] * 6 + [vec] * 3 + [
            tok, tok, pl.BlockSpec((1, 3, width), lambda b, t: (b, 0, 0)),
            pl.BlockSpec((width, width), lambda b, t: (0, 0), pipeline_mode=pl.Buffered(1))],
        out_specs=tok,
        out_shape=jax.ShapeDtypeStruct(x.shape, F32),
        scratch_shapes=[pltpu.VMEM((width // LANES, LANES, LANES), F32)],
        compiler_params=_params("arbitrary", "arbitrary"),
        name="wkv7_chunked",
    )(r, w_log, k, v, kk, a, ln_w, ln_b, r_k, g, x, mod, w_o)


MOE_TILE = 512
MOE_BLOCK = 64
MOE_SLOTS = MOE_TILE + N_EXPERT_GROUPS * MOE_BLOCK
MOE_BLOCKS_PER_TILE = MOE_SLOTS // MOE_BLOCK
MOE_STEP_BLOCKS = 16


def _route(logits_t, rb):
    n_g, epg = N_EXPERT_GROUPS, EXPERTS_PER_GROUP
    s = jax.nn.sigmoid(logits_t)
    sel = s + rb
    s_rows = [s[i:i + 1, :] for i in range(N_EXPERTS)]
    sel_rows = [sel[i:i + 1, :] for i in range(N_EXPERTS)]
    scores = []
    for g in range(n_g):
        members = sel_rows[g * epg:(g + 1) * epg]
        best = None
        for i in range(epg):
            for j in range(i + 1, epg):
                pair = members[i] + members[j]
                best = pair if best is None else jnp.maximum(best, pair)
        scores.append(best)
    g_idx = jnp.zeros_like(scores[0], dtype=jnp.int32)
    top = scores[0]
    for g in range(1, n_g):
        better = scores[g] > top
        g_idx = jnp.where(better, g, g_idx)
        top = jnp.where(better, scores[g], top)

    def pick(rows_, j):
        out = rows_[j]
        for g in range(1, n_g):
            out = jnp.where(g_idx == g, rows_[g * epg + j], out)
        return out

    in_sel = [pick(sel_rows, j) for j in range(epg)]
    in_s = [pick(s_rows, j) for j in range(epg)]
    chosen = []
    for j in range(epg):
        rank = jnp.zeros_like(g_idx)
        for i in range(epg):
            if i == j:
                continue
            ahead = (in_sel[i] >= in_sel[j]) if i < j else (in_sel[i] > in_sel[j])
            rank = rank + jnp.where(ahead, 1, 0)
        chosen.append(rank < 2)
    den = sum(jnp.where(chosen[j], in_s[j], 0.0) for j in range(epg))
    gate_rows = []
    for ex in range(N_EXPERTS):
        g, j = divmod(ex, epg)
        on = jnp.logical_and(chosen[j], g_idx == g)
        gate_rows.append(jnp.where(on, in_s[j] / den, 0.0))
    return g_idx, gate_rows


def _slot_one_hot(pos_row):
    slot = lax.broadcasted_iota(jnp.int32, (MOE_SLOTS, pos_row.shape[1]), 0)
    return jnp.where(slot == pos_row, 1.0, 0.0).astype(BF16)


def _moe_sort_body(x_ref, mod_ref, rw_ref, rb_ref, hs_ref, gs_ref, pos_ref, cnt_ref):
    tm = MOE_TILE
    _moe_sort_tiles([x_ref[0, i * tm:(i + 1) * tm, :] for i in range(x_ref.shape[1] // tm)],
                    mod_ref, rw_ref, rb_ref, hs_ref, gs_ref, pos_ref, cnt_ref)


def _moe_sort_tiles(x_tiles, mod_ref, rw_ref, rb_ref, hs_ref, gs_ref, pos_ref, cnt_ref):
    tm = MOE_TILE
    n_g = N_EXPERT_GROUPS
    tiles = range(len(x_tiles))
    h = [_norm_mod(t, mod_ref) for t in x_tiles]
    hi = [t.astype(BF16) for t in h]
    lo = [(h[i] - hi[i].astype(F32)).astype(BF16) for i in tiles]
    parts = [jnp.dot(jnp.concatenate([hi[i], lo[i]], axis=0), rw_ref[...], preferred_element_type=F32)
             for i in tiles]
    logits = [(p[:tm, :LANES] + p[:tm, LANES:]) + (p[tm:, :LANES] + p[tm:, LANES:]) for p in parts]
    routed = [_route(t.T[:N_EXPERTS], rb_ref[...]) for t in logits]
    member = [[jnp.where(routed[i][0] == g, 1.0, 0.0) for g in range(n_g)] for i in tiles]
    earlier = jnp.where(lax.broadcasted_iota(jnp.int32, (tm, tm), 0) < lax.broadcasted_iota(jnp.int32, (tm, tm), 1),
                        1.0, 0.0).astype(BF16)
    rank = [jnp.dot(jnp.concatenate(member[i] + [jnp.zeros((8 - n_g, tm), F32)], axis=0).astype(BF16), earlier,
                    preferred_element_type=F32) for i in tiles]
    lane = lax.broadcasted_iota(jnp.int32, (1, LANES), 1)
    one_hot = []
    for i in tiles:
        pos = jnp.zeros((1, tm), F32)
        offset = jnp.zeros((1, 1), F32)
        counts = jnp.zeros((1, LANES), F32)
        for g in range(n_g):
            n_members = jnp.sum(member[i][g], axis=1, keepdims=True)
            n_blocks = jnp.floor((n_members + (MOE_BLOCK - 1)) * (1.0 / MOE_BLOCK))
            pos = pos + member[i][g] * (offset + rank[i][g:g + 1, :])
            counts = counts + jnp.where(lane == g, n_blocks, 0.0)
            offset = offset + n_blocks * MOE_BLOCK
        pos_i = pos.astype(jnp.int32)
        pos_ref[i] = pos_i
        cnt_ref[i] = counts.astype(jnp.int32)
        one_hot.append(_slot_one_hot(pos_i))
    pad = jnp.zeros((LANES - N_EXPERTS, tm), F32)
    gate_t = [jnp.concatenate(routed[i][1] + [pad], axis=0).T for i in tiles]
    g_hi = [t.astype(BF16) for t in gate_t]
    g_lo = [(gate_t[i] - g_hi[i].astype(F32)).astype(BF16) for i in tiles]
    hs = [jnp.dot(one_hot[i], hi[i], preferred_element_type=F32) for i in tiles]
    gs = [jnp.dot(one_hot[i], jnp.concatenate([g_hi[i], g_lo[i]], axis=1), preferred_element_type=F32)
          for i in tiles]
    for i in tiles:
        hs_ref[i * MOE_SLOTS:(i + 1) * MOE_SLOTS, :] = hs[i].astype(BF16)
        gs_ref[i * MOE_SLOTS:(i + 1) * MOE_SLOTS, :] = gs[i].astype(BF16)


def _moe_sort_call(x, mod, router_w2, router_b, tiles_per_step=4):
    batch, seq, _ = x.shape
    rows = tiles_per_step * MOE_TILE
    per_seq = seq // rows
    n_tiles = batch * seq // MOE_TILE
    step = lambda b, i: b * per_seq + i
    return pl.pallas_call(
        _moe_sort_body,
        grid=(batch, per_seq),
        in_specs=[
            pl.BlockSpec((1, rows, D_MODEL), lambda b, i: (b, i, 0)),
            pl.BlockSpec((1, 3, D_MODEL), lambda b, i: (b, 0, 0)),
            pl.BlockSpec((D_MODEL, 2 * LANES), lambda b, i: (0, 0)),
            pl.BlockSpec((N_EXPERTS, 1), lambda b, i: (0, 0)),
        ],
        out_specs=[
            pl.BlockSpec((tiles_per_step * MOE_SLOTS, D_MODEL), lambda b, i: (step(b, i), 0)),
            pl.BlockSpec((tiles_per_step * MOE_SLOTS, 2 * LANES), lambda b, i: (step(b, i), 0)),
            pl.BlockSpec((tiles_per_step, 1, MOE_TILE), lambda b, i: (step(b, i), 0, 0)),
            pl.BlockSpec((tiles_per_step, 1, LANES), lambda b, i: (step(b, i), 0, 0)),
        ],
        out_shape=[
            jax.ShapeDtypeStruct((n_tiles * MOE_SLOTS, D_MODEL), BF16),
            jax.ShapeDtypeStruct((n_tiles * MOE_SLOTS, 2 * LANES), BF16),
            jax.ShapeDtypeStruct((n_tiles, 1, MOE_TILE), jnp.int32),
            jax.ShapeDtypeStruct((n_tiles, 1, LANES), jnp.int32),
        ],
        compiler_params=_params("arbitrary", "arbitrary"),
        name="moe_sort",
    )(x, mod, router_w2, router_b)


def _moe_tables_body(cnt_ref, sg_ref, sv_ref, sb_ref, bp_ref):
    n_tiles = cnt_ref.shape[0]
    n_g, r, bpt = N_EXPERT_GROUPS, MOE_STEP_BLOCKS, MOE_BLOCKS_PER_TILE
    n_steps = sg_ref.shape[0]

    def fill(ref, n, value, unroll=4):
        def body(i, carry):
            for u in range(unroll):
                ref[i * unroll + u] = value
            return carry
        lax.fori_loop(0, n // unroll, body, 0)
        for i in range(n - n % unroll, n):
            ref[i] = value

    fill(sb_ref, n_steps * r, 0)
    fill(bp_ref, n_tiles * bpt, 0)
    fill(sg_ref, n_steps, n_g - 1)
    fill(sv_ref, n_steps, 0)
    slot = jnp.int32(0)
    for g in range(n_g):
        def tile_body(i, k, g=g):
            first = i * bpt
            for g_before in range(g):
                first = first + cnt_ref[i, g_before]

            def block_body(j, k):
                sb_ref[k] = first + j
                bp_ref[first + j] = k
                return k + 1
            return lax.fori_loop(0, cnt_ref[i, g], block_body, k)

        slot_end = lax.fori_loop(0, n_tiles, tile_body, slot)
        step_first = slot // r
        step_stop = (slot_end + r - 1) // r

        def step_body(s, carry, g=g, slot_end=slot_end):
            sg_ref[s] = g
            sv_ref[s] = jnp.minimum(slot_end - s * r, r)
            return carry
        lax.fori_loop(step_first, step_stop, step_body, 0)
        slot = step_stop * r


def _moe_step_tables(counts):
    n_tiles = counts.shape[0]
    n_blocks = n_tiles * MOE_BLOCKS_PER_TILE
    n_steps = n_blocks // MOE_STEP_BLOCKS + N_EXPERT_GROUPS
    smem = pl.BlockSpec(memory_space=pltpu.SMEM)
    return pl.pallas_call(
        _moe_tables_body,
        in_specs=[smem],
        out_specs=[smem] * 4,
        out_shape=[jax.ShapeDtypeStruct((n_steps,), jnp.int32), jax.ShapeDtypeStruct((n_steps,), jnp.int32),
                   jax.ShapeDtypeStruct((n_steps * MOE_STEP_BLOCKS,), jnp.int32),
                   jax.ShapeDtypeStruct((n_blocks,), jnp.int32)],
        name="moe_tables",
    )(counts)


def _moe_expert_body(sg_ref, sv_ref, sb_ref, *refs):
    r = MOE_STEP_BLOCKS
    h_refs, g_refs = refs[:r], refs[r:2 * r]
    w1_ref, w3_ref, w2_ref, o_ref, w13_scr, w2_scr, act_scr = refs[2 * r:]
    step = pl.program_id(0)
    epg = EXPERTS_PER_GROUP
    rows_total = r * MOE_BLOCK
    group = sg_ref[step]
    new_group = jnp.logical_or(step == 0, group != sg_ref[jnp.maximum(step - 1, 0)])

    @pl.when(sv_ref[step] == 0)
    def _():
        o_ref[...] = jnp.zeros_like(o_ref)

    @pl.when(jnp.logical_and(sv_ref[step] > 0, new_group))
    def _():
        w13_scr[:, :, :D_EXPERT] = w1_ref[0].astype(BF16)
        w13_scr[:, :, D_EXPERT:] = w3_ref[0].astype(BF16)
        w2_scr[...] = w2_ref[0].reshape(epg * D_EXPERT, D_MODEL).astype(BF16)

    @pl.when(sv_ref[step] > 0)
    def _():
        h = jnp.concatenate([ref[...] for ref in h_refs], axis=0)
        gates = jnp.concatenate([ref[...] for ref in g_refs], axis=0)
        real = lax.broadcasted_iota(jnp.int32, (rows_total, LANES), 0) < sv_ref[step] * MOE_BLOCK
        expert_row = lax.broadcasted_iota(jnp.int32, (2 * LANES, epg * LANES), 0) % LANES
        lane_block = lax.broadcasted_iota(jnp.int32, (2 * LANES, epg * LANES), 1) // LANES
        pick_rows = jnp.where(expert_row == group * epg + lane_block, 1.0, 0.0).astype(BF16)
        gate_all = jnp.dot(gates, pick_rows, preferred_element_type=F32)
        for j in range(epg):
            gate_b = jnp.where(real, gate_all[:, j * LANES:(j + 1) * LANES], 0.0)
            hid = jnp.dot(h, w13_scr[j], preferred_element_type=F32)
            h1 = hid[:, :D_EXPERT]
            act = h1 * jax.nn.sigmoid(h1) * hid[:, D_EXPERT:] * jnp.concatenate([gate_b, gate_b], axis=1)
            act_scr[:, j * D_EXPERT:(j + 1) * D_EXPERT] = act.astype(BF16)
        o_ref[...] = jnp.dot(act_scr[...], w2_scr[...], preferred_element_type=F32).astype(BF16)


def _moe_expert_call(hs, gs, w1, w3, w2, step_group, step_valid, step_blocks, layer):
    r = MOE_STEP_BLOCKS
    n_steps = step_group.shape[0]
    epg = EXPERTS_PER_GROUP
    rows_total = r * MOE_BLOCK

    def blk(width, j):
        return pl.BlockSpec((MOE_BLOCK, width), lambda s, sg, sv, sb, j=j: (sb[s * r + j], 0))

    def group_weights(shape):
        return pl.BlockSpec((1, epg) + shape, lambda s, sg, sv, sb: (layer, sg[s], 0, 0))

    grid_spec = pltpu.PrefetchScalarGridSpec(
        num_scalar_prefetch=3,
        grid=(n_steps,),
        in_specs=[blk(D_MODEL, j) for j in range(r)] + [blk(2 * LANES, j) for j in range(r)] + [
            group_weights((D_MODEL, D_EXPERT)), group_weights((D_MODEL, D_EXPERT)),
            group_weights((D_EXPERT, D_MODEL))],
        out_specs=pl.BlockSpec((rows_total, D_MODEL), lambda s, sg, sv, sb: (s, 0)),
        scratch_shapes=[
            pltpu.VMEM((epg, D_MODEL, 2 * D_EXPERT), BF16),
            pltpu.VMEM((epg * D_EXPERT, D_MODEL), BF16),
            pltpu.VMEM((rows_total, epg * D_EXPERT), BF16),
        ],
    )
    return pl.pallas_call(
        _moe_expert_body,
        grid_spec=grid_spec,
        out_shape=jax.ShapeDtypeStruct((n_steps * rows_total, D_MODEL), BF16),
        compiler_params=_params("arbitrary"),
        name="moe_experts",
    )(step_group, step_valid, step_blocks, *([hs] * r), *([gs] * r), w1, w3, w2)


def _moe_unsort_body(is_last, bp_ref, *refs):
    bpt = MOE_BLOCKS_PER_TILE
    pos_ref, x_ref, mod_ref, fg_ref, o_ref = refs[-5:]
    y_refs = refs[:-5]
    tm = MOE_TILE
    tiles = range(len(y_refs) // bpt)
    slot = lax.broadcasted_iota(jnp.int32, (tm, MOE_SLOTS), 1)
    one_hot = []
    for i in tiles:
        pos_col = jnp.broadcast_to(pos_ref[i].astype(F32), (LANES, tm)).T.astype(jnp.int32)
        pos_col = jnp.concatenate([pos_col] * (MOE_SLOTS // LANES), axis=1)
        one_hot.append(jnp.where(slot == pos_col, 1.0, 0.0).astype(BF16))
    ys = [jnp.concatenate([ref[...] for ref in y_refs[i * bpt:(i + 1) * bpt]], axis=0) for i in tiles]
    y = [jnp.dot(one_hot[i], ys[i], preferred_element_type=F32) for i in tiles]
    for i in tiles:
        out = x_ref[0, i * tm:(i + 1) * tm, :] + mod_ref[0, 2:3, :] * y[i]
        if is_last:
            out = out * lax.rsqrt(jnp.mean(out * out, axis=-1, keepdims=True) + RMS_EPS) * fg_ref[...]
        o_ref[0, i * tm:(i + 1) * tm, :] = out


def _moe_unsort_call(ys, pos, x, mod, final_g, block_pos, is_last, tiles_per_step=4):
    batch, seq, _ = x.shape
    rows = tiles_per_step * MOE_TILE
    per_seq = seq // rows
    bpt = MOE_BLOCKS_PER_TILE
    n_in = tiles_per_step * bpt

    def blk(j):
        return pl.BlockSpec((MOE_BLOCK, D_MODEL), lambda b, i, bp, j=j: (bp[(b * per_seq + i) * n_in + j], 0))

    grid_spec = pltpu.PrefetchScalarGridSpec(
        num_scalar_prefetch=1,
        grid=(batch, per_seq),
        in_specs=[blk(j) for j in range(n_in)] + [
            pl.BlockSpec((tiles_per_step, 1, MOE_TILE), lambda b, i, bp: (b * per_seq + i, 0, 0)),
            pl.BlockSpec((1, rows, D_MODEL), lambda b, i, bp: (b, i, 0)),
            pl.BlockSpec((1, 3, D_MODEL), lambda b, i, bp: (b, 0, 0)),
            pl.BlockSpec((1, D_MODEL), lambda b, i, bp: (0, 0)),
        ],
        out_specs=pl.BlockSpec((1, rows, D_MODEL), lambda b, i, bp: (b, i, 0)),
    )
    return pl.pallas_call(
        functools.partial(_moe_unsort_body, is_last),
        grid_spec=grid_spec,
        out_shape=jax.ShapeDtypeStruct(x.shape, F32),
        compiler_params=_params("arbitrary", "arbitrary"),
        name="moe_unsort",
    )(block_pos, *([ys] * n_in), pos, x, mod, final_g)


def _moe_layer(x, mod, router_w2, router_b, w1, w3, w2, final_g, layer, is_last, routed=None):
    hs, gs, pos, cnt = routed if routed is not None else _moe_sort_call(x, mod, router_w2, router_b)
    step_group, step_valid, step_blocks, block_pos = _moe_step_tables(cnt[:, 0, :])
    ys = _moe_expert_call(hs, gs, w1, w3, w2, step_group, step_valid, step_blocks, layer)
    return _moe_unsort_call(ys, pos, x, mod, final_g, block_pos, is_last)


def _pad_cols(w):
    n = w.shape[-1]
    return jnp.pad(w, ((0, 0), (0, -n % LANES)))


def _pad_rows(w):
    n = w.shape[0]
    return jnp.pad(w, ((0, -n % LANES), (0, 0)))


def kernel(x, c, ada_w, ada_b, mix_w_in, mix_w_out, sgu_norm_g, sgu_w, sgu_b, rwkv_mu, rwkv_w_rkv, rwkv_w_o, rwkv_w0, rwkv_w1, rwkv_w2, rwkv_a0, rwkv_a1, rwkv_a2, rwkv_v0, rwkv_v1, rwkv_v2, rwkv_g1, rwkv_g2, rwkv_k_k, rwkv_k_a, rwkv_r_k, rwkv_ln_w, rwkv_ln_b, router_w, router_b, moe_w1, moe_w3, moe_w2, final_norm_g):
    depth = ada_w.shape[0]
    row = lambda t: t.reshape(1, -1)
    mods = _ada_call(c, ada_w, ada_b)
    rw_hi = router_w.astype(BF16)
    rw_lo = (router_w - rw_hi.astype(F32)).astype(BF16)
    router_w2 = jnp.concatenate([_pad_cols(rw_hi), _pad_cols(rw_lo)], axis=1)
    router_bc = router_b.reshape(N_EXPERTS, 1)
    final_g = row(final_norm_g)
    v_first = None
    for layer in range(depth):
        i = layer // 2
        mod = mods[2 * layer]
        if layer % 2 == 0:
            qkv, proj = _proj_call(x, mod, mix_w_in[i].astype(BF16))
            o_a = _attn_call(qkv)
            bias_tile = jnp.repeat(sgu_b[i].T, HEAD_DIM, axis=1)
            x, *routed = _sgu_out_call(proj, o_a, x, mod, row(sgu_norm_g[i]), sgu_w[i], bias_tile,
                                       mix_w_out[i].astype(BF16), mods[2 * layer + 1], router_w2, router_bc)
        else:
            routed = None
            weights = [
                rwkv_mu[i],
                rwkv_w_rkv[i, 0].astype(BF16), rwkv_w_rkv[i, 1].astype(BF16), rwkv_w_rkv[i, 2].astype(BF16),
                row(rwkv_w0[i]), _pad_cols(rwkv_w1[i]).astype(BF16), _pad_rows(rwkv_w2[i]).astype(BF16),
                row(rwkv_a0[i]), _pad_cols(rwkv_a1[i]).astype(BF16), _pad_rows(rwkv_a2[i]).astype(BF16),
                _pad_cols(rwkv_g1[i]).astype(BF16), _pad_rows(rwkv_g2[i]).astype(BF16),
                row(rwkv_k_k[i]), row(rwkv_k_a[i]),
            ]
            vres = None
            if i > 0:
                vres = [row(rwkv_v0[i - 1]), _pad_cols(rwkv_v1[i - 1]).astype(BF16),
                        _pad_rows(rwkv_v2[i - 1]).astype(BF16)]
            r, w_log, k, v, kk, a, g = _rwkv_pre_call(x, mod, weights, v_first, vres)
            if i == 0:
                v_first = v
            x = _wkv_call(r, w_log, k, v, kk, a, row(rwkv_ln_w[i]), row(rwkv_ln_b[i]), row(rwkv_r_k[i]),
                          g, x, mod, rwkv_w_o[i].astype(BF16))
        x = _moe_layer(x, mods[2 * layer + 1], router_w2, router_bc, moe_w1, moe_w3, moe_w2, final_g, layer,
                       layer == depth - 1, routed)
    return x
```
